```python
import jax, jax.numpy as jnp
from jax import lax
import numpy as np

D_MODEL = 1024
BATCH = 8
SEQ = 2048
DEPTH = 1
DEC_BATCH = 128
DEC_SEQ = 4
PAST_LEN = 16384
PAGE_SIZE = 128

N_META = 16
D_MIX = D_MODEL
NORM_EPS = 1e-6
GLA_WIDTH = D_MIX // 2
GLA_HEADS = 4
GLA_QK = GLA_WIDTH // 2
GLA_DK = GLA_QK // GLA_HEADS
GLA_DV = GLA_WIDTH // GLA_HEADS
GLA_GATE_RANK = 16
GLA_GATE_NORM = 16.0
GLA_CHUNK = 64
GLA_COLS = 2 * GLA_QK + 2 * GLA_WIDTH + GLA_GATE_RANK
RWKV_WIDTH = D_MIX - GLA_WIDTH
RWKV_HEAD = 64
RWKV_HEADS = RWKV_WIDTH // RWKV_HEAD
RWKV_DECAY_RANK = 64
RWKV_AAA_RANK = 64
RWKV_GATE_RANK = 128
RWKV_DECAY_SCALE = 0.606531
RWKV_GN_EPS = 64e-5
RWKV_COLS = 3 * RWKV_WIDTH + RWKV_DECAY_RANK + RWKV_AAA_RANK + RWKV_GATE_RANK
IN_COLS = GLA_COLS + RWKV_COLS
N_GROUPS = 4
EXPERTS_PER_GROUP = 8
N_EXPERTS = N_GROUPS * EXPERTS_PER_GROUP
TOP_K_INNER = 2
D_EXPERT = 512

kernel_name = 'hymba_gla_rwkv7_hmoe_step'

F32 = jnp.float32


def rmsnorm(x, g):
    xf = x.astype(F32)
    y = xf * lax.rsqrt(jnp.mean(xf * xf, axis=-1, keepdims=True) + NORM_EPS)
    return (y * g.astype(F32)).astype(x.dtype)


def gla_chunk(S, q, k, v, lg):
    L = q.shape[2]
    b = jnp.cumsum(lg, axis=2)
    causal = jnp.tril(jnp.ones((L, L), dtype=bool))[:, :, None]
    diff = b[:, :, :, None, :] - b[:, :, None, :, :]
    decay = jnp.where(causal, jnp.exp(jnp.where(causal, diff, 0.0)), 0.0)
    scores = jnp.einsum('bhid,bhjd,bhijd->bhij', q, k, decay)
    o = jnp.einsum('bhij,bhje->bhie', scores, v) + jnp.einsum('bhid,bhde->bhie', q * jnp.exp(b), S)
    b_last = b[:, :, -1:, :]
    S_new = jnp.exp(b_last[:, :, 0, :])[..., None] * S + jnp.einsum('bhjd,bhje->bhde', k * jnp.exp(b_last - b), v)
    return o, S_new


def gla_scan(S0, q, k, v, lg, n_lead):
    B, H, T, _ = q.shape
    outs = []
    S = S0
    if n_lead > 0:
        o, S = gla_chunk(S, q[:, :, :n_lead], k[:, :, :n_lead], v[:, :, :n_lead], lg[:, :, :n_lead])
        outs.append(o)
    rest = T - n_lead
    C = GLA_CHUNK if rest % GLA_CHUNK == 0 else rest
    nc = rest // C

    def to_chunks(a):
        return a[:, :, n_lead:].reshape(B, H, nc, C, a.shape[-1]).transpose(2, 0, 1, 3, 4)

    def step(S_c, xs):
        o_c, S_c = gla_chunk(S_c, *xs)
        return S_c, o_c

    S, oc = lax.scan(step, S, (to_chunks(q), to_chunks(k), to_chunks(v), to_chunks(lg)))
    outs.append(oc.transpose(1, 2, 0, 3, 4).reshape(B, H, rest, v.shape[-1]))
    return jnp.concatenate(outs, axis=2), S


def rwkv_scan(S0, r, w, k, v, kk, a):
    def step(S, xs):
        r_t, w_t, k_t, v_t, kk_t, a_t = xs
        sab = jnp.einsum('bhij,bhj->bhi', S, kk_t)
        S = S * w_t[:, :, None, :] - sab[..., None] * (kk_t * a_t)[:, :, None, :] + v_t[..., None] * k_t[:, :, None, :]
        y = jnp.einsum('bhij,bhj->bhi', S, r_t)
        return S, y
    xs = (jnp.moveaxis(r, 1, 0), jnp.moveaxis(w, 1, 0), jnp.moveaxis(k, 1, 0),
          jnp.moveaxis(v, 1, 0), jnp.moveaxis(kk, 1, 0), jnp.moveaxis(a, 1, 0))
    S, ys = lax.scan(step, S0, xs)
    return jnp.moveaxis(ys, 0, 1), S


def mixer(n, S_gla, S_rwkv, shift, n_lead, w_in, gla_gate_w2, gla_gate_b, gla_norm,
          mu, w0, w2, a0, a2, g2, k_k, k_a, r_k, ln_w, ln_b, w_out):
    B, T, _ = n.shape
    p = n @ w_in
    pg, pr = p[..., :GLA_COLS], p[..., GLA_COLS:]
    gq, gk, gv, gg, gl = jnp.split(pg, [GLA_QK, 2 * GLA_QK, 2 * GLA_QK + GLA_WIDTH, 2 * GLA_QK + 2 * GLA_WIDTH], axis=-1)
    lg = jax.nn.log_sigmoid((gl @ gla_gate_w2 + gla_gate_b).astype(F32)) / GLA_GATE_NORM

    def heads(t, d):
        return t.reshape(B, T, GLA_HEADS, d).transpose(0, 2, 1, 3).astype(F32)

    o, S_gla_new = gla_scan(S_gla.astype(F32), heads(gq, GLA_DK) * (GLA_DK ** -0.5), heads(gk, GLA_DK),
                            heads(gv, GLA_DV), heads(lg, GLA_DK), n_lead)
    o = o.transpose(0, 2, 1, 3)
    o = o * lax.rsqrt(jnp.mean(o * o, axis=-1, keepdims=True) + NORM_EPS) * gla_norm.astype(F32)
    o_gla = o.reshape(B, T, GLA_WIDTH) * jax.nn.silu(gg.astype(F32))
    prev = jnp.concatenate([shift[:, None, :].astype(pr.dtype), pr[:, :-1]], axis=1)
    xm = pr + (prev - pr) * mu
    rr, rk, rv, wl, al, gl2 = jnp.split(xm, [RWKV_WIDTH, 2 * RWKV_WIDTH, 3 * RWKV_WIDTH,
                                            3 * RWKV_WIDTH + RWKV_DECAY_RANK,
                                            3 * RWKV_WIDTH + RWKV_DECAY_RANK + RWKV_AAA_RANK], axis=-1)
    logw = -RWKV_DECAY_SCALE * jax.nn.sigmoid((w0 + jnp.tanh(wl) @ w2).astype(F32))
    aa = jax.nn.sigmoid((a0 + al @ a2).astype(F32))
    gate = (jax.nn.sigmoid(gl2) @ g2).astype(F32)

    def rh(t):
        return t.reshape(B, T, RWKV_HEADS, RWKV_HEAD).astype(F32)

    r_h, k_h, v_h, a_h, w_h = rh(rr), rh(rk), rh(rv), rh(aa), jnp.exp(rh(logw))
    kk = k_h * k_k.astype(F32).reshape(RWKV_HEADS, RWKV_HEAD)
    kk = kk / jnp.maximum(jnp.sqrt(jnp.sum(kk * kk, axis=-1, keepdims=True)), 1e-12)
    k_h = k_h * (1.0 + (a_h - 1.0) * k_a.astype(F32).reshape(RWKV_HEADS, RWKV_HEAD))
    y, S_rwkv_new = rwkv_scan(S_rwkv.astype(F32), r_h, w_h, k_h, v_h, kk, a_h)
    mean = jnp.mean(y, axis=-1, keepdims=True)
    var = jnp.mean(jnp.square(y - mean), axis=-1, keepdims=True)
    y = (y - mean) * lax.rsqrt(var + RWKV_GN_EPS) * ln_w.astype(F32).reshape(RWKV_HEADS, RWKV_HEAD) \
        + ln_b.astype(F32).reshape(RWKV_HEADS, RWKV_HEAD)
    y = y + jnp.sum(r_h * k_h * r_k.astype(F32), axis=-1, keepdims=True) * v_h
    o_rwkv = y.reshape(B, T, RWKV_WIDTH) * gate
    out = jnp.concatenate([o_gla, o_rwkv], axis=-1).astype(n.dtype) @ w_out
    return out, S_gla_new, S_rwkv_new, pr[:, -1]


def hier_moe(x, rg_w, rg_b, re_w, re_b, w1, w3, w2):
    pg = jax.nn.softmax((x @ rg_w + rg_b).astype(F32), axis=-1)
    p_top, g_idx = lax.top_k(pg, 1)
    el = (jnp.einsum('td,gde->tge', x, re_w) + re_b).astype(F32)
    el_sel = jnp.take_along_axis(el, g_idx[:, :, None], axis=1)[:, 0]
    pe = jax.nn.softmax(el_sel, axis=-1)
    w_top, e_idx = lax.top_k(pe, TOP_K_INNER)
    w_top = w_top / jnp.sum(w_top, axis=-1, keepdims=True) * p_top
    eidx = g_idx * EXPERTS_PER_GROUP + e_idx
    combine = jnp.sum(jax.nn.one_hot(eidx, N_EXPERTS, dtype=F32) * w_top[..., None], axis=1)
    y = jnp.zeros(x.shape, F32)
    for e in range(N_EXPERTS):
        h = jax.nn.silu(x @ w1[e]) * (x @ w3[e])
        y = y + combine[:, e:e + 1] * (h @ w2[e]).astype(F32)
    return y.astype(x.dtype)


def setup_inputs(seed: int = 0) -> dict:
    key = jax.random.key(seed)
    ks = jax.random.split(key, 32)
    L = DEPTH

    def nrm(i, shape, s):
        return s * jax.random.normal(ks[i], shape, F32)

    def gain(i, shape):
        return 1.0 + 0.02 * jax.random.normal(ks[i], shape, F32)

    return {
        'x_prompt': nrm(0, (BATCH, SEQ, D_MODEL), 1.0),
        'x_sample': nrm(1, (DEC_BATCH, DEC_SEQ, D_MODEL), 1.0),
        'state_gla': nrm(2, (L, DEC_BATCH, GLA_HEADS, GLA_DK, GLA_DV), 0.5),
        'state_rwkv': nrm(3, (L, DEC_BATCH, RWKV_HEADS, RWKV_HEAD, RWKV_HEAD), 0.5),
        'state_shift': nrm(4, (L, DEC_BATCH, RWKV_COLS), 1.0),
        'meta_tokens': nrm(5, (N_META, D_MODEL), 1.0),
        'norm_mix': gain(6, (L, D_MODEL)),
        'w_in': nrm(7, (L, D_MODEL, IN_COLS), D_MODEL ** -0.5),
        'gla_gate_w2': nrm(8, (L, GLA_GATE_RANK, GLA_QK), GLA_GATE_RANK ** -0.5),
        'gla_gate_b': nrm(9, (L, GLA_QK), 0.5),
        'gla_norm': gain(10, (L, GLA_DV)),
        'rwkv_mu': jax.random.uniform(ks[11], (L, RWKV_COLS), F32),
        'rwkv_w0': nrm(12, (L, RWKV_WIDTH), 0.5),
        'rwkv_w2': nrm(13, (L, RWKV_DECAY_RANK, RWKV_WIDTH), RWKV_DECAY_RANK ** -0.5),
        'rwkv_a0': nrm(14, (L, RWKV_WIDTH), 0.1),
        'rwkv_a2': nrm(15, (L, RWKV_AAA_RANK, RWKV_WIDTH), RWKV_AAA_RANK ** -0.5),
        'rwkv_g2': nrm(16, (L, RWKV_GATE_RANK, RWKV_WIDTH), RWKV_GATE_RANK ** -0.5),
        'rwkv_kk': 0.85 + nrm(17, (L, RWKV_WIDTH), 0.1),
        'rwkv_ka': 1.0 + nrm(18, (L, RWKV_WIDTH), 0.1),
        'rwkv_rk': nrm(19, (L, RWKV_HEADS, RWKV_HEAD), 0.1),
        'rwkv_ln_w': gain(20, (L, RWKV_WIDTH)),
        'rwkv_ln_b': nrm(21, (L, RWKV_WIDTH), 0.01),
        'w_out': nrm(22, (L, D_MIX, D_MODEL), D_MIX ** -0.5),
        'norm_ffn': gain(23, (L, D_MODEL)),
        'router_group_w': nrm(24, (L, D_MODEL, N_GROUPS), D_MODEL ** -0.5),
        'router_group_b': nrm(25, (L, N_GROUPS), 0.01),
        'router_expert_w': nrm(26, (L, N_GROUPS, D_MODEL, EXPERTS_PER_GROUP), D_MODEL ** -0.5),
        'router_expert_b': nrm(27, (L, N_GROUPS, EXPERTS_PER_GROUP), 0.01),
        'moe_w1': nrm(28, (L, N_EXPERTS, D_MODEL, D_EXPERT), D_MODEL ** -0.5),
        'moe_w3': nrm(29, (L, N_EXPERTS, D_MODEL, D_EXPERT), D_MODEL ** -0.5),
        'moe_w2': nrm(30, (L, N_EXPERTS, D_EXPERT, D_MODEL), D_EXPERT ** -0.5),
        'norm_final': gain(31, (D_MODEL,)),
    }


def reference(x_prompt, x_sample, state_gla, state_rwkv, state_shift, meta_tokens, norm_mix, w_in,
              gla_gate_w2, gla_gate_b, gla_norm, rwkv_mu, rwkv_w0, rwkv_w2, rwkv_a0, rwkv_a2, rwkv_g2,
              rwkv_kk, rwkv_ka, rwkv_rk, rwkv_ln_w, rwkv_ln_b, w_out, norm_ffn, router_group_w,
              router_group_b, router_expert_w, router_expert_b, moe_w1, moe_w3, moe_w2, norm_final):
    b_p = x_prompt.shape[0]
    meta = jnp.broadcast_to(meta_tokens.astype(x_prompt.dtype)[None], (b_p, N_META, D_MODEL))
    hp = jnp.concatenate([meta, x_prompt], axis=1)
    hs = x_sample
    gla_p, rwkv_p, shift_p, gla_s, rwkv_s, shift_s = [], [], [], [], [], []
    for l in range(DEPTH):
        mix_w = (w_in[l], gla_gate_w2[l], gla_gate_b[l], gla_norm[l], rwkv_mu[l], rwkv_w0[l], rwkv_w2[l],
                 rwkv_a0[l], rwkv_a2[l], rwkv_g2[l], rwkv_kk[l], rwkv_ka[l], rwkv_rk[l], rwkv_ln_w[l],
                 rwkv_ln_b[l], w_out[l])
        z_gla = jnp.zeros((b_p, GLA_HEADS, GLA_DK, GLA_DV), F32)
        z_rwkv = jnp.zeros((b_p, RWKV_HEADS, RWKV_HEAD, RWKV_HEAD), F32)
        z_shift = jnp.zeros((b_p, RWKV_COLS), hp.dtype)
        mp, sg_p, sr_p, sh_p = mixer(rmsnorm(hp, norm_mix[l]), z_gla, z_rwkv, z_shift, N_META, *mix_w)
        ms, sg_s, sr_s, sh_s = mixer(rmsnorm(hs, norm_mix[l]), state_gla[l], state_rwkv[l], state_shift[l], 0, *mix_w)
        hp = hp + mp
        hs = hs + ms
        n_tok_p = hp.shape[0] * hp.shape[1]
        tok = jnp.concatenate([hp.reshape(n_tok_p, D_MODEL), hs.reshape(-1, D_MODEL)], axis=0)
        f = hier_moe(rmsnorm(tok, norm_ffn[l]), router_group_w[l], router_group_b[l], router_expert_w[l],
                     router_expert_b[l], moe_w1[l], moe_w3[l], moe_w2[l])
        hp = hp + f[:n_tok_p].reshape(hp.shape)
        hs = hs + f[n_tok_p:].reshape(hs.shape)
        gla_p.append(sg_p.astype(state_gla.dtype))
        rwkv_p.append(sr_p.astype(state_rwkv.dtype))
        shift_p.append(sh_p.astype(state_shift.dtype))
        gla_s.append(sg_s.astype(state_gla.dtype))
        rwkv_s.append(sr_s.astype(state_rwkv.dtype))
        shift_s.append(sh_s.astype(state_shift.dtype))
    y_prompt = rmsnorm(hp[:, N_META:], norm_final)
    y_sample = rmsnorm(hs, norm_final)
    return (y_prompt, y_sample, jnp.stack(gla_p), jnp.stack(rwkv_p), jnp.stack(shift_p),
            jnp.stack(gla_s), jnp.stack(rwkv_s), jnp.stack(shift_s))
```

```python
import functools

import jax
import jax.numpy as jnp
from jax import lax
from jax.experimental import pallas as pl
from jax.experimental.pallas import tpu as pltpu

F32 = jnp.float32
BF16 = jnp.bfloat16
HIGHEST = lax.Precision.HIGHEST

D_MODEL = 1024
N_META = 16
NORM_EPS = 1e-6
GLA_HEADS = 4
GLA_DK = 64
GLA_DV = 128
GLA_QK = GLA_HEADS * GLA_DK
GLA_WIDTH = GLA_HEADS * GLA_DV
GLA_GATE_RANK = 16
GLA_GATE_NORM = 16.0
GLA_CHUNK = 64
GLA_SUB = 16
GLA_COLS = 2 * GLA_QK + 2 * GLA_WIDTH + GLA_GATE_RANK
GLA_PCOLS = 2 * GLA_QK + 2 * GLA_WIDTH + 128
RWKV_WIDTH = 512
RWKV_HEAD = 64
RWKV_HEADS = 8
RWKV_PAIRS = RWKV_HEADS // 2
RWKV_DECAY_SCALE = 0.606531
RWKV_GN_EPS = 64e-5
RWKV_COLS = 3 * RWKV_WIDTH + 64 + 64 + 128
REC_TB = 64
N_GROUPS = 4
EXPERTS_PER_GROUP = 8
N_EXPERTS = 32
D_EXPERT = 512
ROUTER_LANES = 128
EXPERT_LANE0 = N_GROUPS

LANE = 128
VMEM_LIMIT = 56 * 1024 * 1024


def _cparams(*sem):
    return pltpu.CompilerParams(dimension_semantics=sem, vmem_limit_bytes=VMEM_LIMIT)


def _block_ones(n, blk):
    i = jnp.arange(n)
    return (i[:, None] // blk == i[None, :] // blk).astype(BF16)


def _sigmoid(x):
    return 1.0 / (1.0 + jnp.exp(-x))


def _dot(a, b):
    return jnp.dot(a, b, preferred_element_type=F32)


def _dot_nt(a, b):
    return lax.dot_general(a, b, (((1,), (1,)), ((), ())), preferred_element_type=F32)


def _dot_tn(a, b):
    return lax.dot_general(a, b, (((0,), (0,)), ((), ())), preferred_element_type=F32)


def _split2(x):
    hi = x.astype(BF16)
    lo = (x - hi.astype(F32)).astype(BF16)
    return hi, lo


def _group_sum(x, bo):
    hi, lo = _split2(x)
    return _dot(hi, bo) + _dot(lo, bo)


def _inproj_kernel(x_ref, g_ref, wg_ref, wr_ref, pg_ref, pr_ref):
    x = x_ref[...]
    n = x * lax.rsqrt(jnp.mean(x * x, axis=-1, keepdims=True) + NORM_EPS) * g_ref[...]
    nb = n.astype(BF16)
    pg_ref[...] = _dot(nb, wg_ref[...])
    pr_ref[...] = _dot(nb, wr_ref[...])


def _inproj(x, g, wg, wr, tm):
    t = x.shape[0]
    return pl.pallas_call(
        _inproj_kernel,
        grid=(t // tm,),
        in_specs=[
            pl.BlockSpec((tm, D_MODEL), lambda i: (i, 0)),
            pl.BlockSpec((1, D_MODEL), lambda i: (0, 0)),
            pl.BlockSpec((D_MODEL, GLA_PCOLS), lambda i: (0, 0)),
            pl.BlockSpec((D_MODEL, RWKV_COLS), lambda i: (0, 0)),
        ],
        out_specs=[
            pl.BlockSpec((tm, GLA_PCOLS), lambda i: (i, 0)),
            pl.BlockSpec((tm, RWKV_COLS), lambda i: (i, 0)),
        ],
        out_shape=[jax.ShapeDtypeStruct((t, GLA_PCOLS), F32), jax.ShapeDtypeStruct((t, RWKV_COLS), F32)],
        compiler_params=_cparams("parallel"),
        name="inproj",
    )(x, g, wg, wr)


def _gla_kernel(pg_ref, s0_ref, gw2_ref, gb_ref, gn_ref, bo_ref, tril_ref, o_ref, sout_ref, s_scr,
                *, bb, chunk, sub, t_valid):
    ci = pl.program_id(1)

    @pl.when(ci == 0)
    def _():
        s_scr[...] = s0_ref[...]

    bo = bo_ref[...]
    tril = tril_ref[...]
    lane = lax.broadcasted_iota(jnp.int32, (sub, LANE), 1) & (GLA_DK - 1)
    rowi = lax.broadcasted_iota(jnp.int32, (sub, LANE), 0)
    rowc = lax.broadcasted_iota(jnp.int32, (chunk, GLA_DK), 0)
    ones_cv = jnp.ones((chunk, GLA_DV), BF16)

    for bi in range(bb):
        pg = pg_ref[bi]
        q = pg[:, 0:GLA_QK] * (GLA_DK ** -0.5)
        k = pg[:, GLA_QK:2 * GLA_QK]
        v = pg[:, 2 * GLA_QK:2 * GLA_QK + GLA_WIDTH]
        g = pg[:, 2 * GLA_QK + GLA_WIDTH:2 * GLA_QK + 2 * GLA_WIDTH]
        gl = pg[:, 2 * GLA_QK + 2 * GLA_WIDTH:]
        z = jnp.dot(gl, gw2_ref[...], precision=HIGHEST, preferred_element_type=F32) + gb_ref[...]
        lg = (jnp.minimum(z, 0.0) - jnp.log1p(jnp.exp(-jnp.abs(z)))) * (1.0 / GLA_GATE_NORM)
        if t_valid < chunk:
            rows = lax.broadcasted_iota(jnp.int32, lg.shape, 0)
            lg = jnp.where(rows < t_valid, lg, 0.0)
        b = jnp.dot(tril, lg, precision=HIGHEST, preferred_element_type=F32)
        eb = jnp.exp(b)
        blast = b[chunk - 1:chunk, :]
        kl = k * jnp.exp(blast - b)
        qe = q * eb

        o_heads = []
        for hp in range(GLA_HEADS // 2):
            sl = slice(hp * LANE, (hp + 1) * LANE)
            qp, kp, bp = q[:, sl], k[:, sl], b[:, sl]
            row_blocks = []
            for blk in range(chunk // sub):
                rs = slice(blk * sub, (blk + 1) * sub)
                qb, kb, bbk = qp[rs], kp[rs], bp[rs]
                ps = []
                for j in range(sub):
                    ps.append(qb * (kb[j:j + 1] * jnp.exp(jnp.minimum(bbk - bbk[j:j + 1], 0.0))))
                red = _dot(jnp.concatenate(ps, axis=0).astype(BF16), bo)
                a = jnp.zeros((sub, LANE), F32)
                for j in range(sub):
                    a = jnp.where((lane == blk * sub + j) & (rowi >= j), red[j * sub:(j + 1) * sub], a)
                if blk > 0:
                    bref = bp[blk * sub - 1:blk * sub]
                    qt = (qb * jnp.exp(bbk - bref)).astype(BF16)
                    kt = (kp * jnp.exp(jnp.minimum(bref - bp, 0.0))).astype(BF16)
                    off = jnp.concatenate(
                        [_dot_nt(qt[:, :GLA_DK], kt[:, :GLA_DK]), _dot_nt(qt[:, GLA_DK:], kt[:, GLA_DK:])], axis=1)
                    a = jnp.where(lane < blk * sub, off, a)
                row_blocks.append(a)
            a_pair = row_blocks[0] if len(row_blocks) == 1 else jnp.concatenate(row_blocks, axis=0)
            for h2 in range(2):
                h = 2 * hp + h2
                a_h = a_pair[:, h2 * GLA_DK:h2 * GLA_DK + chunk].astype(BF16)
                vhb = v[:, h * GLA_DV:(h + 1) * GLA_DV].astype(BF16)
                s_h = s_scr[bi, h]
                hs = slice(h * GLA_DK, (h + 1) * GLA_DK)
                o_h = _dot(a_h, vhb) + _dot(qe[:, hs].astype(BF16), s_h.astype(BF16))
                e_hi, e_lo = _split2(jnp.where(rowc == chunk - 1, eb[:, hs], 0.0))
                dcol = _dot_tn(e_hi, ones_cv) + _dot_tn(e_lo, ones_cv)
                s_scr[bi, h] = dcol * s_h + _dot_tn(kl[:, hs].astype(BF16), vhb)
                o_heads.append(o_h * lax.rsqrt(jnp.mean(o_h * o_h, axis=-1, keepdims=True) + NORM_EPS) * gn_ref[...])
        o = jnp.concatenate(o_heads, axis=1)
        o_ref[bi] = o * (g * _sigmoid(g))

    @pl.when(ci == pl.num_programs(1) - 1)
    def _():
        sout_ref[...] = s_scr[...]


def _gla(pg, s0, gw2p, gb, gn, *, bb, chunk, sub, t_valid):
    b, t, _ = pg.shape
    tril = jnp.tril(jnp.ones((chunk, chunk), F32))
    kern = functools.partial(_gla_kernel, bb=bb, chunk=chunk, sub=sub, t_valid=t_valid)
    return pl.pallas_call(
        kern,
        grid=(b // bb, t // chunk),
        in_specs=[
            pl.BlockSpec((bb, chunk, GLA_PCOLS), lambda i, j: (i, j, 0)),
            pl.BlockSpec((bb, GLA_HEADS, GLA_DK, GLA_DV), lambda i, j: (i, 0, 0, 0)),
            pl.BlockSpec((LANE, GLA_QK), lambda i, j: (0, 0)),
            pl.BlockSpec((1, GLA_QK), lambda i, j: (0, 0)),
            pl.BlockSpec((1, GLA_DV), lambda i, j: (0, 0)),
            pl.BlockSpec((LANE, LANE), lambda i, j: (0, 0)),
            pl.BlockSpec((chunk, chunk), lambda i, j: (0, 0)),
        ],
        out_specs=[
            pl.BlockSpec((bb, chunk, GLA_WIDTH), lambda i, j: (i, j, 0)),
            pl.BlockSpec((bb, GLA_HEADS, GLA_DK, GLA_DV), lambda i, j: (i, 0, 0, 0)),
        ],
        out_shape=[jax.ShapeDtypeStruct((b, t, GLA_WIDTH), F32),
                   jax.ShapeDtypeStruct((b, GLA_HEADS, GLA_DK, GLA_DV), F32)],
        scratch_shapes=[pltpu.VMEM((bb, GLA_HEADS, GLA_DK, GLA_DV), F32)],
        compiler_params=_cparams("parallel", "arbitrary"),
        name="gla_chunk",
    )(pg, s0, gw2p, gb, gn, _block_ones(LANE, GLA_DK), tril)


def _rwkv_pre_kernel(pr_ref, aux_ref, mu_ref, w0_ref, w2_ref, a0_ref, a2_ref, g2_ref, kk_ref, ka_ref, rk_ref, bo_ref,
                     r_out, w_out, k_out, kkn_out, kka_out, v_out, bv_out, gate_out, carry_scr,
                     *, tm, explicit_prev):
    pr = pr_ref[0]
    if explicit_prev:
        prev = aux_ref[0]
    else:
        j = pl.program_id(1)
        row0 = jnp.where(j == 0, aux_ref[0], carry_scr[...])
        rows = lax.broadcasted_iota(jnp.int32, pr.shape, 0)
        prev = jnp.where(rows == 0, row0, pltpu.roll(pr, 1, 0))
        carry_scr[...] = pr[tm - 1:tm, :]
    xm = pr + (prev - pr) * mu_ref[...]
    wd = RWKV_WIDTH
    rr, rk, rv = xm[:, 0:wd], xm[:, wd:2 * wd], xm[:, 2 * wd:3 * wd]
    wa = xm[:, 3 * wd:3 * wd + LANE]
    gl2 = xm[:, 3 * wd + LANE:3 * wd + 2 * LANE]
    logw = -RWKV_DECAY_SCALE * _sigmoid(w0_ref[...] + _dot(jnp.tanh(wa).astype(BF16), w2_ref[...]))
    aa = _sigmoid(a0_ref[...] + _dot(wa.astype(BF16), a2_ref[...]))
    gate = _dot(_sigmoid(gl2).astype(BF16), g2_ref[...])
    bo = bo_ref[...]
    kk = rk * kk_ref[...]
    kk = kk / jnp.maximum(jnp.sqrt(_group_sum(kk * kk, bo)), 1e-12)
    k = rk * (1.0 + (aa - 1.0) * ka_ref[...])
    bv = _group_sum(rr * k * rk_ref[...], bo) * rv
    w = jnp.exp(logw)
    kka = kk * aa
    for hp in range(RWKV_PAIRS):
        sl = slice(hp * LANE, (hp + 1) * LANE)
        r_out[0, hp] = rr[:, sl]
        w_out[0, hp] = w[:, sl]
        k_out[0, hp] = k[:, sl]
        kkn_out[0, hp] = kk[:, sl]
        kka_out[0, hp] = kka[:, sl]
    v_out[0] = rv
    bv_out[0] = bv
    gate_out[0] = gate


def _rwkv_pre(pr, aux, params, *, tm, explicit_prev):
    b, t, _ = pr.shape
    kern = functools.partial(_rwkv_pre_kernel, tm=tm, explicit_prev=explicit_prev)
    aux_spec = (pl.BlockSpec((1, tm, RWKV_COLS), lambda i, j: (i, j, 0)) if explicit_prev
                else pl.BlockSpec((1, 1, RWKV_COLS), lambda i, j: (i, 0, 0)))
    const = lambda shape: pl.BlockSpec(shape, lambda i, j: (0,) * len(shape))
    pair_spec = pl.BlockSpec((1, RWKV_PAIRS, tm, LANE), lambda i, j: (i, 0, j, 0))
    row_spec = pl.BlockSpec((1, tm, RWKV_WIDTH), lambda i, j: (i, j, 0))
    pair_shape = jax.ShapeDtypeStruct((b, RWKV_PAIRS, t, LANE), F32)
    row_shape = jax.ShapeDtypeStruct((b, t, RWKV_WIDTH), F32)
    return pl.pallas_call(
        kern,
        grid=(b, t // tm),
        in_specs=[
            pl.BlockSpec((1, tm, RWKV_COLS), lambda i, j: (i, j, 0)),
            aux_spec,
            const((1, RWKV_COLS)), const((1, RWKV_WIDTH)), const((LANE, RWKV_WIDTH)), const((1, RWKV_WIDTH)),
            const((LANE, RWKV_WIDTH)), const((LANE, RWKV_WIDTH)), const((1, RWKV_WIDTH)), const((1, RWKV_WIDTH)),
            const((1, RWKV_WIDTH)), const((RWKV_WIDTH, RWKV_WIDTH)),
        ],
        out_specs=[pair_spec] * 5 + [row_spec] * 3,
        out_shape=[pair_shape] * 5 + [row_shape] * 3,
        scratch_shapes=[pltpu.VMEM((1, RWKV_COLS), F32)],
        compiler_params=_cparams("parallel", "arbitrary"),
        name="rwkv_pre",
    )(pr, aux, *params)


def _rwkv_rec_kernel(r_ref, w_ref, k_ref, kk_ref, kka_ref, vt_ref, s0_ref, bo_ref, vsel_ref, ysel_ref,
                     y_ref, sout_ref, s_scr, t1_scr, t3_scr, *, bb, n_steps):
    tb = pl.program_id(1)
    nc = bb * RWKV_PAIRS
    hd = RWKV_HEAD

    @pl.when(tb == 0)
    def _():
        for c in range(nc):
            s_scr[c] = s0_ref[c // RWKV_PAIRS, c % RWKV_PAIRS]

    y_ref[...] = jnp.zeros(y_ref.shape, F32)
    bo = bo_ref[...]

    def step(t, carry):
        for c in range(nc):
            bi, hp = divmod(c, RWKV_PAIRS)
            t1_scr[c * hd:(c + 1) * hd, :] = (s_scr[c] * kk_ref[bi, hp, pl.ds(t, 1), :]).astype(BF16)
        sab = _dot(t1_scr[...], bo)
        vb = _dot(vt_ref[...].reshape(nc * hd, LANE), vsel_ref[t])
        for c in range(nc):
            bi, hp = divmod(c, RWKV_PAIRS)
            rs = slice(c * hd, (c + 1) * hd)
            row = pl.ds(t, 1)
            s2 = (s_scr[c] * w_ref[bi, hp, row, :] - sab[rs] * kka_ref[bi, hp, row, :]
                  + vb[rs] * k_ref[bi, hp, row, :])
            s_scr[c] = s2
            t3_scr[rs, :] = (s2 * r_ref[bi, hp, row, :]).astype(BF16)
        yc = _dot_nt(ysel_ref[t & 3], t3_scr[...])
        y_ref[0, t >> 2] += yc
        return carry

    lax.fori_loop(0, n_steps, step, 0)

    @pl.when(tb == pl.num_programs(1) - 1)
    def _():
        for c in range(nc):
            sout_ref[c // RWKV_PAIRS, c % RWKV_PAIRS] = s_scr[c]


def _rwkv_rec(r, w, k, kk, kka, vt, s0, *, bb, n_steps):
    b, _, t, _ = r.shape
    tblk = min(REC_TB, t)
    ntb = t // tblk
    nc = bb * RWKV_PAIRS
    lane = jnp.arange(LANE)
    vsel = ((lane[None, :, None] // RWKV_HEAD == lane[None, None, :] // RWKV_HEAD)
            & (lane[None, :, None] % RWKV_HEAD == jnp.arange(RWKV_HEAD)[:, None, None])).astype(BF16)
    ysel = (jnp.arange(8)[None, :, None]
            == 2 * jnp.arange(4)[:, None, None] + lane[None, None, :] // RWKV_HEAD).astype(BF16)
    kern = functools.partial(_rwkv_rec_kernel, bb=bb, n_steps=n_steps)
    pair_spec = pl.BlockSpec((bb, RWKV_PAIRS, tblk, LANE), lambda i, j: (i, 0, j, 0))
    state_spec = pl.BlockSpec((bb, RWKV_PAIRS, RWKV_HEAD, LANE), lambda i, j: (i, 0, 0, 0))
    ytb = max(tblk // 4, 1)
    return pl.pallas_call(
        kern,
        grid=(b // bb, ntb),
        in_specs=[pair_spec] * 5 + [
            pl.BlockSpec((bb, 1, RWKV_PAIRS, RWKV_HEAD, LANE), lambda i, j: (i, j, 0, 0, 0)),
            state_spec,
            pl.BlockSpec((LANE, LANE), lambda i, j: (0, 0)),
            pl.BlockSpec((RWKV_HEAD, LANE, LANE), lambda i, j: (0, 0, 0)),
            pl.BlockSpec((4, 8, LANE), lambda i, j: (0, 0, 0)),
        ],
        out_specs=[
            pl.BlockSpec((1, ytb, 8, nc * RWKV_HEAD), lambda i, j: (i, j, 0, 0)),
            state_spec,
        ],
        out_shape=[jax.ShapeDtypeStruct((b // bb, ntb * ytb, 8, nc * RWKV_HEAD), F32),
                   jax.ShapeDtypeStruct((b, RWKV_PAIRS, RWKV_HEAD, LANE), F32)],
        scratch_shapes=[pltpu.VMEM((nc, RWKV_HEAD, LANE), F32),
                        pltpu.VMEM((nc * RWKV_HEAD, LANE), BF16),
                        pltpu.VMEM((nc * RWKV_HEAD, LANE), BF16)],
        compiler_params=_cparams("parallel", "arbitrary"),
        name="rwkv_rec",
    )(r, w, k, kk, kka, vt, s0, _block_ones(LANE, RWKV_HEAD), vsel, ysel)


def _mix_router_kernel(x_ref, og_ref, y_ref, bv_ref, gate_ref, lnw_ref, lnb_ref, wo_ref, gffn_ref, wr_ref, br_ref,
                       bo_ref, h_ref, n2_ref, comb_ref):
    bo = bo_ref[...]
    y = y_ref[...]
    inv_n = 1.0 / RWKV_HEAD
    d = y - _group_sum(y, bo) * inv_n
    var = _group_sum(d * d, bo) * inv_n
    yn = d * lax.rsqrt(var + RWKV_GN_EPS) * lnw_ref[...] + lnb_ref[...] + bv_ref[...]
    o_rwkv = yn * gate_ref[...]
    mix = (_dot(og_ref[...].astype(BF16), wo_ref[0:GLA_WIDTH, :])
           + _dot(o_rwkv.astype(BF16), wo_ref[GLA_WIDTH:, :]))
    h = x_ref[...] + mix
    h_ref[...] = h
    n2 = h * lax.rsqrt(jnp.mean(h * h, axis=-1, keepdims=True) + NORM_EPS) * gffn_ref[...]
    n2_ref[...] = n2.astype(BF16)
    lg = jnp.dot(n2, wr_ref[...], precision=HIGHEST, preferred_element_type=F32) + br_ref[...]
    neg = jnp.float32(-3.0e38)
    big = jnp.float32(1.0e9)
    lane = lax.broadcasted_iota(jnp.int32, lg.shape, 1).astype(F32)
    gmask = lane < N_GROUPS
    gmax = jnp.max(jnp.where(gmask, lg, neg), axis=1, keepdims=True)
    p_top = 1.0 / jnp.sum(jnp.where(gmask, jnp.exp(jnp.minimum(lg - gmax, 0.0)), 0.0), axis=1, keepdims=True)
    gidx = jnp.min(jnp.where(gmask & (lg == gmax), lane, big), axis=1, keepdims=True)
    e_lo = EXPERT_LANE0 + gidx * EXPERTS_PER_GROUP
    emask = (lane >= e_lo) & (lane < e_lo + EXPERTS_PER_GROUP)
    m1 = jnp.max(jnp.where(emask, lg, neg), axis=1, keepdims=True)
    e1 = jnp.min(jnp.where(emask & (lg == m1), lane, big), axis=1, keepdims=True)
    emask2 = emask & (lane != e1)
    m2 = jnp.max(jnp.where(emask2, lg, neg), axis=1, keepdims=True)
    e2 = jnp.min(jnp.where(emask2 & (lg == m2), lane, big), axis=1, keepdims=True)
    r21 = jnp.exp(m2 - m1)
    w1 = p_top / (1.0 + r21)
    w2 = p_top * r21 / (1.0 + r21)
    comb_ref[...] = jnp.where(lane == e1, w1, 0.0) + jnp.where(lane == e2, w2, 0.0)


def _mix_router(x, og, y, bv, gate, lnw, lnb, wo, gffn, wr, br, *, tm):
    t = x.shape[0]
    row = lambda n: pl.BlockSpec((tm, n), lambda i: (i, 0))
    const = lambda shape: pl.BlockSpec(shape, lambda i: (0,) * len(shape))
    return pl.pallas_call(
        _mix_router_kernel,
        grid=(t // tm,),
        in_specs=[row(D_MODEL), row(GLA_WIDTH), row(RWKV_WIDTH), row(RWKV_WIDTH), row(RWKV_WIDTH),
                  const((1, RWKV_WIDTH)), const((1, RWKV_WIDTH)), const((D_MODEL, D_MODEL)), const((1, D_MODEL)),
                  const((D_MODEL, ROUTER_LANES)), const((1, ROUTER_LANES)), const((RWKV_WIDTH, RWKV_WIDTH))],
        out_specs=[row(D_MODEL), row(D_MODEL), row(ROUTER_LANES)],
        out_shape=[jax.ShapeDtypeStruct((t, D_MODEL), F32), jax.ShapeDtypeStruct((t, D_MODEL), BF16),
                   jax.ShapeDtypeStruct((t, ROUTER_LANES), F32)],
        compiler_params=_cparams("parallel"),
        name="mix_router",
    )(x, og, y, bv, gate, lnw, lnb, wo, gffn, wr, br, _block_ones(RWKV_WIDTH, RWKV_HEAD))


def _moe_kernel(n2_ref, h_ref, comb_ref, w1_ref, w3_ref, w2_ref, gfin_ref, y_ref, acc_scr):
    e = pl.program_id(1)

    @pl.when(e == 0)
    def _():
        acc_scr[...] = jnp.zeros(acc_scr.shape, F32)

    x = n2_ref[...]
    a = _dot(x, w1_ref[0])
    b = _dot(x, w3_ref[0])
    comb = comb_ref[...]
    lane = lax.broadcasted_iota(jnp.int32, comb.shape, 1)
    cw = jnp.sum(jnp.where(lane == e + EXPERT_LANE0, comb, 0.0), axis=1, keepdims=True)
    hh = (a * _sigmoid(a)) * b * cw
    acc_scr[...] += _dot(hh.astype(BF16), w2_ref[0])

    @pl.when(e == pl.num_programs(1) - 1)
    def _():
        hf = h_ref[...] + acc_scr[...]
        y_ref[...] = hf * lax.rsqrt(jnp.mean(hf * hf, axis=-1, keepdims=True) + NORM_EPS) * gfin_ref[...]


def _moe(n2, h, comb, w1, w3, w2, gfin, *, tm):
    t = n2.shape[0]
    return pl.pallas_call(
        _moe_kernel,
        grid=(t // tm, N_EXPERTS),
        in_specs=[
            pl.BlockSpec((tm, D_MODEL), lambda i, e: (i, 0)),
            pl.BlockSpec((tm, D_MODEL), lambda i, e: (i, 0)),
            pl.BlockSpec((tm, ROUTER_LANES), lambda i, e: (i, 0)),
            pl.BlockSpec((1, D_MODEL, D_EXPERT), lambda i, e: (e, 0, 0)),
            pl.BlockSpec((1, D_MODEL, D_EXPERT), lambda i, e: (e, 0, 0)),
            pl.BlockSpec((1, D_EXPERT, D_MODEL), lambda i, e: (e, 0, 0)),
            pl.BlockSpec((1, D_MODEL), lambda i, e: (0, 0)),
        ],
        out_specs=pl.BlockSpec((tm, D_MODEL), lambda i, e: (i, 0)),
        out_shape=jax.ShapeDtypeStruct((t, D_MODEL), F32),
        scratch_shapes=[pltpu.VMEM((tm, D_MODEL), F32)],
        compiler_params=_cparams("parallel", "arbitrary"),
        name="moe_experts",
    )(n2, h, comb, w1, w3, w2, gfin)


def _pair_state(s):
    b = s.shape[0]
    return s.reshape(b, RWKV_PAIRS, 2, RWKV_HEAD, RWKV_HEAD).transpose(0, 1, 3, 2, 4).reshape(
        b, RWKV_PAIRS, RWKV_HEAD, LANE)


def _unpair_state(s):
    b = s.shape[0]
    return s.reshape(b, RWKV_PAIRS, RWKV_HEAD, 2, RWKV_HEAD).transpose(0, 1, 3, 2, 4).reshape(
        b, RWKV_HEADS, RWKV_HEAD, RWKV_HEAD)


def _v_tiles(v, tblk):
    b, t, _ = v.shape
    x = v.reshape(b, t // tblk, tblk, RWKV_PAIRS, 2, RWKV_HEAD).transpose(0, 1, 3, 5, 4, 2)
    x = jnp.pad(x, ((0, 0),) * 5 + ((0, RWKV_HEAD - tblk),))
    return x.reshape(b, t // tblk, RWKV_PAIRS, RWKV_HEAD, LANE).astype(BF16)


def _y_rows(y, bb, t):
    nb = y.shape[0]
    x = y.reshape(nb, -1, 4, 2, bb, RWKV_PAIRS, RWKV_HEAD).transpose(0, 4, 1, 2, 5, 3, 6)
    return x.reshape(nb * bb, -1, RWKV_WIDTH)[:, :t]


def _rwkv_branch(pr, aux, s0_pairs, pre_params, *, tm, explicit_prev, bb):
    b, t, _ = pr.shape
    r, w, k, kk, kka, v, bv, gate = _rwkv_pre(pr, aux, pre_params, tm=tm, explicit_prev=explicit_prev)
    tblk = min(REC_TB, t)
    y, s_new = _rwkv_rec(r, w, k, kk, kka, _v_tiles(v, tblk), s0_pairs, bb=bb, n_steps=tblk)
    return _y_rows(y, bb, t), bv, gate, s_new


def kernel(x_prompt, x_sample, state_gla, state_rwkv, state_shift, meta_tokens, norm_mix, w_in, gla_gate_w2,
           gla_gate_b, gla_norm, rwkv_mu, rwkv_w0, rwkv_w2, rwkv_a0, rwkv_a2, rwkv_g2, rwkv_kk, rwkv_ka, rwkv_rk,
           rwkv_ln_w, rwkv_ln_b, w_out, norm_ffn, router_group_w, router_group_b, router_expert_w,
           router_expert_b, moe_w1, moe_w3, moe_w2, norm_final):
    bp, tp, _ = x_prompt.shape
    bs, ts, _ = x_sample.shape
    assert state_gla.shape[0] == 1, "one layer"
    lyr = 0

    w_in_l = w_in[lyr]
    wg = jnp.pad(w_in_l[:, :GLA_COLS], ((0, 0), (0, GLA_PCOLS - GLA_COLS))).astype(BF16)
    wr = w_in_l[:, GLA_COLS:].astype(BF16)
    g_mix = norm_mix[lyr][None, :]
    gw2p = jnp.pad(gla_gate_w2[lyr], ((0, LANE - GLA_GATE_RANK), (0, 0)))
    gb = gla_gate_b[lyr][None, :]
    gn = gla_norm[lyr][None, :]
    w2p = jnp.pad(rwkv_w2[lyr], ((0, 64), (0, 0))).astype(BF16)
    a2p = jnp.pad(rwkv_a2[lyr], ((64, 0), (0, 0))).astype(BF16)
    pre_params = (rwkv_mu[lyr][None, :], rwkv_w0[lyr][None, :], w2p, rwkv_a0[lyr][None, :], a2p,
                  rwkv_g2[lyr].astype(BF16), rwkv_kk[lyr][None, :], rwkv_ka[lyr][None, :],
                  rwkv_rk[lyr].reshape(1, RWKV_WIDTH), _block_ones(RWKV_WIDTH, RWKV_HEAD))
    lnw = rwkv_ln_w[lyr][None, :]
    lnb = rwkv_ln_b[lyr][None, :]
    wo = w_out[lyr].astype(BF16)
    gffn = norm_ffn[lyr][None, :]
    n_used = N_GROUPS + N_EXPERTS
    w_router = jnp.pad(
        jnp.concatenate([router_group_w[lyr],
                         router_expert_w[lyr].transpose(1, 0, 2).reshape(D_MODEL, N_EXPERTS)], axis=1),
        ((0, 0), (0, ROUTER_LANES - n_used)))
    b_router = jnp.pad(jnp.concatenate([router_group_b[lyr], router_expert_b[lyr].reshape(N_EXPERTS)]),
                       (0, ROUTER_LANES - n_used))[None, :]
    w1b, w3b, w2b = moe_w1[lyr].astype(BF16), moe_w3[lyr].astype(BF16), moe_w2[lyr].astype(BF16)
    gfin = norm_final[None, :]

    pg_m, pr_m = _inproj(meta_tokens, g_mix, wg, wr, N_META)
    _, sg_m = _gla(pg_m[None], jnp.zeros((1, GLA_HEADS, GLA_DK, GLA_DV), F32), gw2p, gb, gn,
                   bb=1, chunk=N_META, sub=N_META, t_valid=N_META)
    _, _, _, sr_m = _rwkv_branch(pr_m[None], jnp.zeros((1, 1, RWKV_COLS), F32),
                                 jnp.zeros((1, RWKV_PAIRS, RWKV_HEAD, LANE), F32), pre_params,
                                 tm=N_META, explicit_prev=False, bb=1)

    xp = x_prompt.reshape(bp * tp, D_MODEL)
    pg_p, pr_p = _inproj(xp, g_mix, wg, wr, 512)
    og_p, sg_p = _gla(pg_p.reshape(bp, tp, GLA_PCOLS), jnp.broadcast_to(sg_m, (bp,) + sg_m.shape[1:]), gw2p, gb, gn,
                      bb=1, chunk=GLA_CHUNK, sub=GLA_SUB, t_valid=GLA_CHUNK)
    pr_p3 = pr_p.reshape(bp, tp, RWKV_COLS)
    first_prev = jnp.broadcast_to(pr_m[N_META - 1][None, None, :], (bp, 1, RWKV_COLS))
    y_p, bv_p, gate_p, sr_p = _rwkv_branch(pr_p3, first_prev, jnp.broadcast_to(sr_m, (bp,) + sr_m.shape[1:]),
                                           pre_params, tm=256, explicit_prev=False, bb=bp)
    h_p, n2_p, comb_p = _mix_router(xp, og_p.reshape(bp * tp, GLA_WIDTH), y_p.reshape(bp * tp, RWKV_WIDTH),
                                    bv_p.reshape(bp * tp, RWKV_WIDTH), gate_p.reshape(bp * tp, RWKV_WIDTH),
                                    lnw, lnb, wo, gffn, w_router, b_router, tm=512)
    y_prompt = _moe(n2_p, h_p, comb_p, w1b, w3b, w2b, gfin, tm=1024).reshape(bp, tp, D_MODEL)

    xs = x_sample.reshape(bs * ts, D_MODEL)
    pg_s, pr_s = _inproj(xs, g_mix, wg, wr, bs * ts)
    ts_pad = 8
    pg_s3 = jnp.pad(pg_s.reshape(bs, ts, GLA_PCOLS), ((0, 0), (0, ts_pad - ts), (0, 0)))
    og_s, sg_s = _gla(pg_s3, state_gla[lyr], gw2p, gb, gn, bb=8, chunk=ts_pad, sub=ts_pad, t_valid=ts)
    og_s = og_s[:, :ts]
    pr_s3 = pr_s.reshape(bs, ts, RWKV_COLS)
    prev_s = jnp.concatenate([state_shift[lyr][:, None, :], pr_s3[:, :-1]], axis=1)
    r, w, k, kk, kka, v_s, bv_s, gate_s = _rwkv_pre(pr_s3.reshape(1, bs * ts, RWKV_COLS),
                                                     prev_s.reshape(1, bs * ts, RWKV_COLS), pre_params,
                                                     tm=bs * ts, explicit_prev=True)
    unflat = lambda a: a.reshape(RWKV_PAIRS, bs, ts, LANE).transpose(1, 0, 2, 3)
    y_s, sr_s = _rwkv_rec(unflat(r), unflat(w), unflat(k), unflat(kk), unflat(kka),
                          _v_tiles(v_s.reshape(bs, ts, RWKV_WIDTH), ts), _pair_state(state_rwkv[lyr]),
                          bb=8, n_steps=ts)
    y_s = _y_rows(y_s, 8, ts)
    h_s, n2_s, comb_s = _mix_router(xs, og_s.reshape(bs * ts, GLA_WIDTH), y_s.reshape(bs * ts, RWKV_WIDTH),
                                    bv_s.reshape(bs * ts, RWKV_WIDTH), gate_s.reshape(bs * ts, RWKV_WIDTH),
                                    lnw, lnb, wo, gffn, w_router, b_router, tm=bs * ts)
    y_sample = _moe(n2_s, h_s, comb_s, w1b, w3b, w2b, gfin, tm=bs * ts).reshape(bs, ts, D_MODEL)

    return (y_prompt, y_sample,
            sg_p[None], _unpair_state(sr_p)[None], pr_p3[:, -1][None],
            sg_s[None], _unpair_state(sr_s)[None], pr_s3[:, -1][None])
```

```python
import functools

import jax
import jax.numpy as jnp
from jax import lax
from jax.experimental import pallas as pl
from jax.experimental.pallas import tpu as pltpu

F32 = jnp.float32
BF16 = jnp.bfloat16
HIGHEST = lax.Precision.HIGHEST

D_MODEL = 1024
N_META = 16
NORM_EPS = 1e-6
GLA_HEADS = 4
GLA_DK = 64
GLA_DV = 128
GLA_QK = GLA_HEADS * GLA_DK
GLA_WIDTH = GLA_HEADS * GLA_DV
GLA_GATE_RANK = 16
GLA_GATE_NORM = 16.0
GLA_CHUNK = 64
GLA_SUB = 16
GLA_COLS = 2 * GLA_QK + 2 * GLA_WIDTH + GLA_GATE_RANK
GLA_PCOLS = 2 * GLA_QK + 2 * GLA_WIDTH + 128
RWKV_WIDTH = 512
RWKV_HEAD = 64
RWKV_HEADS = 8
RWKV_PAIRS = RWKV_HEADS // 2
RWKV_DECAY_SCALE = 0.606531
RWKV_GN_EPS = 64e-5
RWKV_COLS = 3 * RWKV_WIDTH + 64 + 64 + 128
REC_TB = 64
N_GROUPS = 4
EXPERTS_PER_GROUP = 8
N_EXPERTS = 32
D_EXPERT = 512
ROUTER_LANES = 128
EXPERT_LANE0 = N_GROUPS
ROUTE_E1, ROUTE_E2, ROUTE_W1, ROUTE_W2, ROUTE_P1, ROUTE_P2 = range(6)
MOE_TM = 512
MOE_TS = 256

LANE = 128
VMEM_LIMIT = 56 * 1024 * 1024


def _cparams(*sem):
    return pltpu.CompilerParams(dimension_semantics=sem, vmem_limit_bytes=VMEM_LIMIT)


def _block_ones(n, blk):
    i = jnp.arange(n)
    return (i[:, None] // blk == i[None, :] // blk).astype(BF16)


def _sigmoid(x):
    return 1.0 / (1.0 + jnp.exp(-x))


def _dot(a, b):
    return jnp.dot(a, b, preferred_element_type=F32)


def _dot_nt(a, b):
    return lax.dot_general(a, b, (((1,), (1,)), ((), ())), preferred_element_type=F32)


def _dot_tn(a, b):
    return lax.dot_general(a, b, (((0,), (0,)), ((), ())), preferred_element_type=F32)


def _split2(x):
    hi = x.astype(BF16)
    lo = (x - hi.astype(F32)).astype(BF16)
    return hi, lo


def _group_sum(x, bo):
    hi, lo = _split2(x)
    return _dot(hi, bo) + _dot(lo, bo)


def _inproj_kernel(x_ref, g_ref, wg_ref, wr_ref, pg_ref, pr_ref):
    x = x_ref[...]
    n = x * lax.rsqrt(jnp.mean(x * x, axis=-1, keepdims=True) + NORM_EPS) * g_ref[...]
    nb = n.astype(BF16)
    pg_ref[...] = _dot(nb, wg_ref[...])
    pr_ref[...] = _dot(nb, wr_ref[...])


def _inproj(x, g, wg, wr, tm):
    t = x.shape[0]
    return pl.pallas_call(
        _inproj_kernel,
        grid=(t // tm,),
        in_specs=[
            pl.BlockSpec((tm, D_MODEL), lambda i: (i, 0)),
            pl.BlockSpec((1, D_MODEL), lambda i: (0, 0)),
            pl.BlockSpec((D_MODEL, GLA_PCOLS), lambda i: (0, 0)),
            pl.BlockSpec((D_MODEL, RWKV_COLS), lambda i: (0, 0)),
        ],
        out_specs=[
            pl.BlockSpec((tm, GLA_PCOLS), lambda i: (i, 0)),
            pl.BlockSpec((tm, RWKV_COLS), lambda i: (i, 0)),
        ],
        out_shape=[jax.ShapeDtypeStruct((t, GLA_PCOLS), F32), jax.ShapeDtypeStruct((t, RWKV_COLS), F32)],
        compiler_params=_cparams("parallel"),
        name="inproj",
    )(x, g, wg, wr)


def _gla_kernel(pg_ref, s0_ref, gw2_ref, gb_ref, gn_ref, bo_ref, tril_ref, o_ref, sout_ref, s_scr,
                *, bb, chunk, sub, t_valid):
    ci = pl.program_id(1)

    @pl.when(ci == 0)
    def _():
        s_scr[...] = s0_ref[...]

    bo = bo_ref[...]
    tril = tril_ref[...]
    lane = lax.broadcasted_iota(jnp.int32, (sub, LANE), 1) & (GLA_DK - 1)
    rowi = lax.broadcasted_iota(jnp.int32, (sub, LANE), 0)
    rowc = lax.broadcasted_iota(jnp.int32, (chunk, GLA_DK), 0)
    ones_cv = jnp.ones((chunk, GLA_DV), BF16)

    for bi in range(bb):
        pg = pg_ref[bi]
        q = pg[:, 0:GLA_QK] * (GLA_DK ** -0.5)
        k = pg[:, GLA_QK:2 * GLA_QK]
        v = pg[:, 2 * GLA_QK:2 * GLA_QK + GLA_WIDTH]
        g = pg[:, 2 * GLA_QK + GLA_WIDTH:2 * GLA_QK + 2 * GLA_WIDTH]
        gl = pg[:, 2 * GLA_QK + 2 * GLA_WIDTH:]
        z = jnp.dot(gl, gw2_ref[...], precision=HIGHEST, preferred_element_type=F32) + gb_ref[...]
        lg = (jnp.minimum(z, 0.0) - jnp.log1p(jnp.exp(-jnp.abs(z)))) * (1.0 / GLA_GATE_NORM)
        if t_valid < chunk:
            rows = lax.broadcasted_iota(jnp.int32, lg.shape, 0)
            lg = jnp.where(rows < t_valid, lg, 0.0)
        b = jnp.dot(tril, lg, precision=HIGHEST, preferred_element_type=F32)
        eb = jnp.exp(b)
        blast = b[chunk - 1:chunk, :]
        kl = k * jnp.exp(blast - b)
        qe = q * eb

        o_heads = []
        for hp in range(GLA_HEADS // 2):
            sl = slice(hp * LANE, (hp + 1) * LANE)
            qp, kp, bp = q[:, sl], k[:, sl], b[:, sl]
            row_blocks = []
            for blk in range(chunk // sub):
                rs = slice(blk * sub, (blk + 1) * sub)
                qb, kb, bbk = qp[rs], kp[rs], bp[rs]
                ps = []
                for j in range(sub):
                    ps.append(qb * (kb[j:j + 1] * jnp.exp(jnp.minimum(bbk - bbk[j:j + 1], 0.0))))
                red = _dot(jnp.concatenate(ps, axis=0).astype(BF16), bo)
                a = jnp.zeros((sub, LANE), F32)
                for j in range(sub):
                    a = jnp.where((lane == blk * sub + j) & (rowi >= j), red[j * sub:(j + 1) * sub], a)
                if blk > 0:
                    bref = bp[blk * sub - 1:blk * sub]
                    qt = (qb * jnp.exp(bbk - bref)).astype(BF16)
                    kt = (kp * jnp.exp(jnp.minimum(bref - bp, 0.0))).astype(BF16)
                    off = jnp.concatenate(
                        [_dot_nt(qt[:, :GLA_DK], kt[:, :GLA_DK]), _dot_nt(qt[:, GLA_DK:], kt[:, GLA_DK:])], axis=1)
                    a = jnp.where(lane < blk * sub, off, a)
                row_blocks.append(a)
            a_pair = row_blocks[0] if len(row_blocks) == 1 else jnp.concatenate(row_blocks, axis=0)
            for h2 in range(2):
                h = 2 * hp + h2
                a_h = a_pair[:, h2 * GLA_DK:h2 * GLA_DK + chunk].astype(BF16)
                vhb = v[:, h * GLA_DV:(h + 1) * GLA_DV].astype(BF16)
                s_h = s_scr[bi, h]
                hs = slice(h * GLA_DK, (h + 1) * GLA_DK)
                o_h = _dot(a_h, vhb) + _dot(qe[:, hs].astype(BF16), s_h.astype(BF16))
                e_hi, e_lo = _split2(jnp.where(rowc == chunk - 1, eb[:, hs], 0.0))
                dcol = _dot_tn(e_hi, ones_cv) + _dot_tn(e_lo, ones_cv)
                s_scr[bi, h] = dcol * s_h + _dot_tn(kl[:, hs].astype(BF16), vhb)
                o_heads.append(o_h * lax.rsqrt(jnp.mean(o_h * o_h, axis=-1, keepdims=True) + NORM_EPS) * gn_ref[...])
        o = jnp.concatenate(o_heads, axis=1)
        o_ref[bi] = o * (g * _sigmoid(g))

    @pl.when(ci == pl.num_programs(1) - 1)
    def _():
        sout_ref[...] = s_scr[...]


def _gla(pg, s0, gw2p, gb, gn, *, bb, chunk, sub, t_valid):
    b, t, _ = pg.shape
    tril = jnp.tril(jnp.ones((chunk, chunk), F32))
    kern = functools.partial(_gla_kernel, bb=bb, chunk=chunk, sub=sub, t_valid=t_valid)
    return pl.pallas_call(
        kern,
        grid=(b // bb, t // chunk),
        in_specs=[
            pl.BlockSpec((bb, chunk, GLA_PCOLS), lambda i, j: (i, j, 0)),
            pl.BlockSpec((bb, GLA_HEADS, GLA_DK, GLA_DV), lambda i, j: (i, 0, 0, 0)),
            pl.BlockSpec((LANE, GLA_QK), lambda i, j: (0, 0)),
            pl.BlockSpec((1, GLA_QK), lambda i, j: (0, 0)),
            pl.BlockSpec((1, GLA_DV), lambda i, j: (0, 0)),
            pl.BlockSpec((LANE, LANE), lambda i, j: (0, 0)),
            pl.BlockSpec((chunk, chunk), lambda i, j: (0, 0)),
        ],
        out_specs=[
            pl.BlockSpec((bb, chunk, GLA_WIDTH), lambda i, j: (i, j, 0)),
            pl.BlockSpec((bb, GLA_HEADS, GLA_DK, GLA_DV), lambda i, j: (i, 0, 0, 0)),
        ],
        out_shape=[jax.ShapeDtypeStruct((b, t, GLA_WIDTH), F32),
                   jax.ShapeDtypeStruct((b, GLA_HEADS, GLA_DK, GLA_DV), F32)],
        scratch_shapes=[pltpu.VMEM((bb, GLA_HEADS, GLA_DK, GLA_DV), F32)],
        compiler_params=_cparams("parallel", "arbitrary"),
        name="gla_chunk",
    )(pg, s0, gw2p, gb, gn, _block_ones(LANE, GLA_DK), tril)


def _rwkv_pre_kernel(pr_ref, aux_ref, mu_ref, w0_ref, w2_ref, a0_ref, a2_ref, g2_ref, kk_ref, ka_ref, rk_ref, bo_ref,
                     r_out, w_out, k_out, kkn_out, kka_out, v_out, bv_out, gate_out, carry_scr,
                     *, tm, explicit_prev):
    pr = pr_ref[0]
    if explicit_prev:
        prev = aux_ref[0]
    else:
        j = pl.program_id(1)
        row0 = jnp.where(j == 0, aux_ref[0], carry_scr[...])
        rows = lax.broadcasted_iota(jnp.int32, pr.shape, 0)
        prev = jnp.where(rows == 0, row0, pltpu.roll(pr, 1, 0))
        carry_scr[...] = pr[tm - 1:tm, :]
    xm = pr + (prev - pr) * mu_ref[...]
    wd = RWKV_WIDTH
    rr, rk, rv = xm[:, 0:wd], xm[:, wd:2 * wd], xm[:, 2 * wd:3 * wd]
    wa = xm[:, 3 * wd:3 * wd + LANE]
    gl2 = xm[:, 3 * wd + LANE:3 * wd + 2 * LANE]
    logw = -RWKV_DECAY_SCALE * _sigmoid(w0_ref[...] + _dot(jnp.tanh(wa).astype(BF16), w2_ref[...]))
    aa = _sigmoid(a0_ref[...] + _dot(wa.astype(BF16), a2_ref[...]))
    gate = _dot(_sigmoid(gl2).astype(BF16), g2_ref[...])
    bo = bo_ref[...]
    kk = rk * kk_ref[...]
    kk = kk / jnp.maximum(jnp.sqrt(_group_sum(kk * kk, bo)), 1e-12)
    k = rk * (1.0 + (aa - 1.0) * ka_ref[...])
    bv = _group_sum(rr * k * rk_ref[...], bo) * rv
    w = jnp.exp(logw)
    kka = kk * aa
    for hp in range(RWKV_PAIRS):
        sl = slice(hp * LANE, (hp + 1) * LANE)
        r_out[0, hp] = rr[:, sl]
        w_out[0, hp] = w[:, sl]
        k_out[0, hp] = k[:, sl]
        kkn_out[0, hp] = kk[:, sl]
        kka_out[0, hp] = kka[:, sl]
    v_out[0] = rv
    bv_out[0] = bv
    gate_out[0] = gate


def _rwkv_pre(pr, aux, params, *, tm, explicit_prev):
    b, t, _ = pr.shape
    kern = functools.partial(_rwkv_pre_kernel, tm=tm, explicit_prev=explicit_prev)
    aux_spec = (pl.BlockSpec((1, tm, RWKV_COLS), lambda i, j: (i, j, 0)) if explicit_prev
                else pl.BlockSpec((1, 1, RWKV_COLS), lambda i, j: (i, 0, 0)))
    const = lambda shape: pl.BlockSpec(shape, lambda i, j: (0,) * len(shape))
    pair_spec = pl.BlockSpec((1, RWKV_PAIRS, tm, LANE), lambda i, j: (i, 0, j, 0))
    row_spec = pl.BlockSpec((1, tm, RWKV_WIDTH), lambda i, j: (i, j, 0))
    pair_shape = jax.ShapeDtypeStruct((b, RWKV_PAIRS, t, LANE), F32)
    row_shape = jax.ShapeDtypeStruct((b, t, RWKV_WIDTH), F32)
    return pl.pallas_call(
        kern,
        grid=(b, t // tm),
        in_specs=[
            pl.BlockSpec((1, tm, RWKV_COLS), lambda i, j: (i, j, 0)),
            aux_spec,
            const((1, RWKV_COLS)), const((1, RWKV_WIDTH)), const((LANE, RWKV_WIDTH)), const((1, RWKV_WIDTH)),
            const((LANE, RWKV_WIDTH)), const((LANE, RWKV_WIDTH)), const((1, RWKV_WIDTH)), const((1, RWKV_WIDTH)),
            const((1, RWKV_WIDTH)), const((RWKV_WIDTH, RWKV_WIDTH)),
        ],
        out_specs=[pair_spec] * 5 + [row_spec] * 3,
        out_shape=[pair_shape] * 5 + [row_shape] * 3,
        scratch_shapes=[pltpu.VMEM((1, RWKV_COLS), F32)],
        compiler_params=_cparams("parallel", "arbitrary"),
        name="rwkv_pre",
    )(pr, aux, *params)


def _rwkv_rec_kernel(r_ref, w_ref, k_ref, kk_ref, kka_ref, vt_ref, s0_ref, bo_ref, vsel_ref, ysel_ref,
                     y_ref, sout_ref, s_scr, t1_scr, t3_scr, *, bb, n_steps):
    tb = pl.program_id(1)
    nc = bb * RWKV_PAIRS
    hd = RWKV_HEAD

    @pl.when(tb == 0)
    def _():
        for c in range(nc):
            s_scr[c] = s0_ref[c // RWKV_PAIRS, c % RWKV_PAIRS]

    y_ref[...] = jnp.zeros(y_ref.shape, F32)
    bo = bo_ref[...]

    def step(t, carry):
        for c in range(nc):
            bi, hp = divmod(c, RWKV_PAIRS)
            t1_scr[c * hd:(c + 1) * hd, :] = (s_scr[c] * kk_ref[bi, hp, pl.ds(t, 1), :]).astype(BF16)
        sab = _dot(t1_scr[...], bo)
        vb = _dot(vt_ref[...].reshape(nc * hd, LANE), vsel_ref[t])
        for c in range(nc):
            bi, hp = divmod(c, RWKV_PAIRS)
            rs = slice(c * hd, (c + 1) * hd)
            row = pl.ds(t, 1)
            s2 = (s_scr[c] * w_ref[bi, hp, row, :] - sab[rs] * kka_ref[bi, hp, row, :]
                  + vb[rs] * k_ref[bi, hp, row, :])
            s_scr[c] = s2
            t3_scr[rs, :] = (s2 * r_ref[bi, hp, row, :]).astype(BF16)
        yc = _dot_nt(ysel_ref[t & 3], t3_scr[...])
        y_ref[0, t >> 2] += yc
        return carry

    lax.fori_loop(0, n_steps, step, 0)

    @pl.when(tb == pl.num_programs(1) - 1)
    def _():
        for c in range(nc):
            sout_ref[c // RWKV_PAIRS, c % RWKV_PAIRS] = s_scr[c]


def _rwkv_rec(r, w, k, kk, kka, vt, s0, *, bb, n_steps):
    b, _, t, _ = r.shape
    tblk = min(REC_TB, t)
    ntb = t // tblk
    nc = bb * RWKV_PAIRS
    lane = jnp.arange(LANE)
    vsel = ((lane[None, :, None] // RWKV_HEAD == lane[None, None, :] // RWKV_HEAD)
            & (lane[None, :, None] % RWKV_HEAD == jnp.arange(RWKV_HEAD)[:, None, None])).astype(BF16)
    ysel = (jnp.arange(8)[None, :, None]
            == 2 * jnp.arange(4)[:, None, None] + lane[None, None, :] // RWKV_HEAD).astype(BF16)
    kern = functools.partial(_rwkv_rec_kernel, bb=bb, n_steps=n_steps)
    pair_spec = pl.BlockSpec((bb, RWKV_PAIRS, tblk, LANE), lambda i, j: (i, 0, j, 0))
    state_spec = pl.BlockSpec((bb, RWKV_PAIRS, RWKV_HEAD, LANE), lambda i, j: (i, 0, 0, 0))
    ytb = max(tblk // 4, 1)
    return pl.pallas_call(
        kern,
        grid=(b // bb, ntb),
        in_specs=[pair_spec] * 5 + [
            pl.BlockSpec((bb, 1, RWKV_PAIRS, RWKV_HEAD, LANE), lambda i, j: (i, j, 0, 0, 0)),
            state_spec,
            pl.BlockSpec((LANE, LANE), lambda i, j: (0, 0)),
            pl.BlockSpec((RWKV_HEAD, LANE, LANE), lambda i, j: (0, 0, 0)),
            pl.BlockSpec((4, 8, LANE), lambda i, j: (0, 0, 0)),
        ],
        out_specs=[
            pl.BlockSpec((1, ytb, 8, nc * RWKV_HEAD), lambda i, j: (i, j, 0, 0)),
            state_spec,
        ],
        out_shape=[jax.ShapeDtypeStruct((b // bb, ntb * ytb, 8, nc * RWKV_HEAD), F32),
                   jax.ShapeDtypeStruct((b, RWKV_PAIRS, RWKV_HEAD, LANE), F32)],
        scratch_shapes=[pltpu.VMEM((nc, RWKV_HEAD, LANE), F32),
                        pltpu.VMEM((nc * RWKV_HEAD, LANE), BF16),
                        pltpu.VMEM((nc * RWKV_HEAD, LANE), BF16)],
        compiler_params=_cparams("parallel", "arbitrary"),
        name="rwkv_rec",
    )(r, w, k, kk, kka, vt, s0, _block_ones(LANE, RWKV_HEAD), vsel, ysel)


def _mix_router_body(x_ref, og_ref, y_ref, bv_ref, gate_ref, lnw_ref, lnb_ref, wo_ref, gffn_ref, wr_ref, br_ref,
                     bo_ref, tril_ref, h_ref, n2_ref, route_ref, cnt_scr):
    bo = bo_ref[...]
    y = y_ref[...]
    inv_n = 1.0 / RWKV_HEAD
    d = y - _group_sum(y, bo) * inv_n
    var = _group_sum(d * d, bo) * inv_n
    yn = d * lax.rsqrt(var + RWKV_GN_EPS) * lnw_ref[...] + lnb_ref[...] + bv_ref[...]
    o_rwkv = yn * gate_ref[...]
    mix = (_dot(og_ref[...].astype(BF16), wo_ref[0:GLA_WIDTH, :])
           + _dot(o_rwkv.astype(BF16), wo_ref[GLA_WIDTH:, :]))
    h = x_ref[...] + mix
    h_ref[...] = h
    n2 = h * lax.rsqrt(jnp.mean(h * h, axis=-1, keepdims=True) + NORM_EPS) * gffn_ref[...]
    n2_ref[...] = n2
    lg = jnp.dot(n2, wr_ref[...], precision=HIGHEST, preferred_element_type=F32) + br_ref[...]
    neg = jnp.float32(-3.0e38)
    big = jnp.float32(1.0e9)
    lane = lax.broadcasted_iota(jnp.int32, lg.shape, 1).astype(F32)
    gmask = lane < N_GROUPS
    gmax = jnp.max(jnp.where(gmask, lg, neg), axis=1, keepdims=True)
    p_top = 1.0 / jnp.sum(jnp.where(gmask, jnp.exp(jnp.minimum(lg - gmax, 0.0)), 0.0), axis=1, keepdims=True)
    gidx = jnp.min(jnp.where(gmask & (lg == gmax), lane, big), axis=1, keepdims=True)
    e_lo = EXPERT_LANE0 + gidx * EXPERTS_PER_GROUP
    emask = (lane >= e_lo) & (lane < e_lo + EXPERTS_PER_GROUP)
    m1 = jnp.max(jnp.where(emask, lg, neg), axis=1, keepdims=True)
    e1 = jnp.min(jnp.where(emask & (lg == m1), lane, big), axis=1, keepdims=True)
    emask2 = emask & (lane != e1)
    m2 = jnp.max(jnp.where(emask2, lg, neg), axis=1, keepdims=True)
    e2 = jnp.min(jnp.where(emask2 & (lg == m2), lane, big), axis=1, keepdims=True)
    r21 = jnp.exp(m2 - m1)
    w1 = p_top / (1.0 + r21)
    w2 = p_top * r21 / (1.0 + r21)
    o1 = lane == e1
    o2 = lane == e2
    onehot = jnp.where(o1 | o2, 1.0, 0.0)
    rank = _dot(tril_ref[...], onehot.astype(BF16)) + cnt_scr[...]
    pos1 = jnp.sum(jnp.where(o1, rank, 0.0), axis=1, keepdims=True)
    pos2 = jnp.sum(jnp.where(o2, rank, 0.0), axis=1, keepdims=True)
    cnt_scr[...] += jnp.sum(onehot, axis=0, keepdims=True)
    route = jnp.where(lane == ROUTE_E1, e1 - EXPERT_LANE0, 0.0)
    route = jnp.where(lane == ROUTE_E2, e2 - EXPERT_LANE0, route)
    route = jnp.where(lane == ROUTE_W1, w1, route)
    route = jnp.where(lane == ROUTE_W2, w2, route)
    route = jnp.where(lane == ROUTE_P1, pos1, route)
    route_ref[...] = jnp.where(lane == ROUTE_P2, pos2, route)


def _mix_router_kernel(*refs, n_prompt_tiles):
    prompt_rows, sample_rows, rest = refs[0:5], refs[5:10], refs[10:]
    consts, (h_ref, n2_ref, route_ref, cnt_ref, cnt_scr) = rest[:8], rest[8:]
    i = pl.program_id(0)

    @pl.when(i == 0)
    def _():
        cnt_scr[...] = jnp.zeros(cnt_scr.shape, F32)

    @pl.when(i < n_prompt_tiles)
    def _():
        _mix_router_body(*prompt_rows, *consts, h_ref, n2_ref, route_ref, cnt_scr)

    @pl.when(i >= n_prompt_tiles)
    def _():
        _mix_router_body(*sample_rows, *consts, h_ref, n2_ref, route_ref, cnt_scr)

    cnt_ref[...] = cnt_scr[...]


def _mix_router(prompt_rows, sample_rows, lnw, lnb, wo, gffn, wr, br):
    tm = MOE_TM
    n_p = prompt_rows[0].shape[0] // tm
    assert sample_rows[0].shape[0] == tm
    t = (n_p + 1) * tm
    widths = (D_MODEL, GLA_WIDTH, RWKV_WIDTH, RWKV_WIDTH, RWKV_WIDTH)
    p_specs = [pl.BlockSpec((tm, n), lambda i: (jnp.minimum(i, n_p - 1), 0)) for n in widths]
    s_specs = [pl.BlockSpec((tm, n), lambda i: (0, 0)) for n in widths]
    const = lambda shape: pl.BlockSpec(shape, lambda i: (0,) * len(shape))
    row = lambda n: pl.BlockSpec((tm, n), lambda i: (i, 0))
    tril = jnp.tril(jnp.ones((tm, tm), F32), -1).astype(BF16)
    return pl.pallas_call(
        functools.partial(_mix_router_kernel, n_prompt_tiles=n_p),
        grid=(n_p + 1,),
        in_specs=p_specs + s_specs + [
            const((1, RWKV_WIDTH)), const((1, RWKV_WIDTH)), const((D_MODEL, D_MODEL)), const((1, D_MODEL)),
            const((D_MODEL, ROUTER_LANES)), const((1, ROUTER_LANES)), const((RWKV_WIDTH, RWKV_WIDTH)),
            const((tm, tm))],
        out_specs=[row(D_MODEL), row(D_MODEL), row(ROUTER_LANES), const((1, ROUTER_LANES))],
        out_shape=[jax.ShapeDtypeStruct((t, D_MODEL), F32), jax.ShapeDtypeStruct((t, D_MODEL), F32),
                   jax.ShapeDtypeStruct((t, ROUTER_LANES), F32), jax.ShapeDtypeStruct((1, ROUTER_LANES), F32)],
        scratch_shapes=[pltpu.VMEM((1, ROUTER_LANES), F32)],
        compiler_params=_cparams("arbitrary"),
        name="mix_router",
    )(*prompt_rows, *sample_rows, lnw, lnb, wo, gffn, wr, br, _block_ones(RWKV_WIDTH, RWKV_HEAD), tril)


def _row_copy_wait(src_row, dst_row, sem, n):
    def body(_, c):
        pltpu.make_async_copy(src_row, dst_row, sem).wait()
        return c
    lax.fori_loop(0, n, body, 0)


def _dispatch_kernel(slots_ref, x_ref, xs_hbm, sem):
    tm = x_ref.shape[0]

    def issue(r, c):
        src = x_ref.at[pl.ds(r, 1)]
        pltpu.make_async_copy(src, xs_hbm.at[pl.ds(slots_ref[0, 0, 2 * r], 1)], sem).start()
        pltpu.make_async_copy(src, xs_hbm.at[pl.ds(slots_ref[0, 0, 2 * r + 1], 1)], sem).start()
        return c

    lax.fori_loop(0, tm, issue, 0)
    _row_copy_wait(x_ref.at[pl.ds(0, 1)], xs_hbm.at[pl.ds(0, 1)], sem, 2 * tm)


def _dispatch(n2, slots):
    t = n2.shape[0]
    tm = MOE_TM
    return pl.pallas_call(
        _dispatch_kernel,
        grid_spec=pltpu.PrefetchScalarGridSpec(
            num_scalar_prefetch=0,
            grid=(t // tm,),
            in_specs=[pl.BlockSpec((1, 1, 2 * tm), lambda i: (i, 0, 0), memory_space=pltpu.SMEM),
                      pl.BlockSpec((tm, D_MODEL), lambda i: (i, 0))],
            out_specs=pl.BlockSpec(memory_space=pl.ANY),
            scratch_shapes=[pltpu.SemaphoreType.DMA(())],
        ),
        out_shape=jax.ShapeDtypeStruct((2 * t, D_MODEL), F32),
        compiler_params=_cparams("arbitrary"),
        name="moe_dispatch",
    )(slots, n2)


def _experts_kernel(wt_ref, we_ref, wlo_ref, whi_ref, wfirst_ref, nw_ref,
                    xs_ref, w1_ref, w3_ref, w2_ref, ys_ref, wb1, wb3, wb2):
    w = pl.program_id(0)

    @pl.when(w < nw_ref[0])
    def _():
        new_expert = jnp.logical_or(w == 0, we_ref[w] != we_ref[jnp.maximum(w - 1, 0)])

        @pl.when(new_expert)
        def _():
            wb1[...] = w1_ref[0].astype(BF16)
            wb3[...] = w3_ref[0].astype(BF16)
            wb2[...] = w2_ref[0].astype(BF16)

        x = xs_ref[...].astype(BF16)
        a = _dot(x, wb1[...])
        b = _dot(x, wb3[...])
        o = _dot(((a * _sigmoid(a)) * b).astype(BF16), wb2[...])

        @pl.when(wfirst_ref[w] == 1)
        def _():
            ys_ref[...] = o

        @pl.when(wfirst_ref[w] == 0)
        def _():
            rows = lax.broadcasted_iota(jnp.int32, o.shape, 0)
            ys_ref[...] = jnp.where((rows >= wlo_ref[w]) & (rows < whi_ref[w]), o, ys_ref[...])


def _experts(xs, work, w1, w3, w2):
    s = xs.shape[0]
    ts = MOE_TS
    n_work = work[0].shape[0]
    return pl.pallas_call(
        _experts_kernel,
        grid_spec=pltpu.PrefetchScalarGridSpec(
            num_scalar_prefetch=6,
            grid=(n_work,),
            in_specs=[
                pl.BlockSpec((ts, D_MODEL), lambda w, wt, we, *_: (wt[w], 0)),
                pl.BlockSpec((1, D_MODEL, D_EXPERT), lambda w, wt, we, *_: (we[w], 0, 0)),
                pl.BlockSpec((1, D_MODEL, D_EXPERT), lambda w, wt, we, *_: (we[w], 0, 0)),
                pl.BlockSpec((1, D_EXPERT, D_MODEL), lambda w, wt, we, *_: (we[w], 0, 0)),
            ],
            out_specs=pl.BlockSpec((ts, D_MODEL), lambda w, wt, we, *_: (wt[w], 0)),
            scratch_shapes=[pltpu.VMEM((D_MODEL, D_EXPERT), BF16), pltpu.VMEM((D_MODEL, D_EXPERT), BF16),
                            pltpu.VMEM((D_EXPERT, D_MODEL), BF16)],
        ),
        out_shape=jax.ShapeDtypeStruct((s, D_MODEL), F32),
        compiler_params=_cparams("arbitrary"),
        name="moe_experts",
    )(*work, xs, w1, w3, w2)


def _expert_work_items(counts, total):
    ts = MOE_TS
    n_tiles = total // ts
    n_work = n_tiles + N_EXPERTS - 1
    offs = jnp.cumsum(counts) - counts
    t0 = (jnp.arange(n_tiles, dtype=jnp.int32) * ts)[:, None]
    lo = jnp.maximum(t0, offs[None, :])
    hi = jnp.minimum(t0 + ts, (offs + counts)[None, :])
    nonempty = (hi > lo).reshape(-1)
    nw = jnp.sum(nonempty.astype(jnp.int32))
    idx = jnp.nonzero(nonempty, size=n_work, fill_value=0)[0].astype(jnp.int32)
    idx = jnp.where(jnp.arange(n_work) < nw, idx, idx[jnp.maximum(nw - 1, 0)])
    wt = idx // N_EXPERTS
    we = idx % N_EXPERTS
    wlo = lo.reshape(-1)[idx] - wt * ts
    whi = hi.reshape(-1)[idx] - wt * ts
    wfirst = jnp.concatenate([jnp.ones((1,), jnp.int32), (wt[1:] != wt[:-1]).astype(jnp.int32)])
    return wt, we, wlo, whi, wfirst, nw.reshape(1)


def _combine_kernel(slots_ref, h_ref, route_ref, gfin_ref, ys_hbm, yp_ref, ysm_ref, gbuf, sem, *, n_prompt_tiles):
    i = pl.program_id(0)
    tm = h_ref.shape[0]

    def issue(r, c):
        pltpu.make_async_copy(ys_hbm.at[pl.ds(slots_ref[0, 0, 2 * r], 1)], gbuf.at[0, pl.ds(r, 1)], sem).start()
        pltpu.make_async_copy(ys_hbm.at[pl.ds(slots_ref[0, 0, 2 * r + 1], 1)], gbuf.at[1, pl.ds(r, 1)], sem).start()
        return c

    lax.fori_loop(0, tm, issue, 0)
    _row_copy_wait(ys_hbm.at[pl.ds(0, 1)], gbuf.at[0, pl.ds(0, 1)], sem, 2 * tm)
    route = route_ref[...]
    lane = lax.broadcasted_iota(jnp.int32, route.shape, 1)
    w1 = jnp.sum(jnp.where(lane == ROUTE_W1, route, 0.0), axis=1, keepdims=True)
    w2 = jnp.sum(jnp.where(lane == ROUTE_W2, route, 0.0), axis=1, keepdims=True)
    hf = h_ref[...] + (w1 * gbuf[0] + w2 * gbuf[1])
    y = hf * lax.rsqrt(jnp.mean(hf * hf, axis=-1, keepdims=True) + NORM_EPS) * gfin_ref[...]

    @pl.when(i < n_prompt_tiles)
    def _():
        yp_ref[...] = y

    @pl.when(i >= n_prompt_tiles)
    def _():
        ysm_ref[...] = y


def _combine(h, route, slots, ys, gfin, n_prompt_tiles):
    t = h.shape[0]
    tm = MOE_TM
    n_p = n_prompt_tiles
    return pl.pallas_call(
        functools.partial(_combine_kernel, n_prompt_tiles=n_p),
        grid_spec=pltpu.PrefetchScalarGridSpec(
            num_scalar_prefetch=0,
            grid=(t // tm,),
            in_specs=[pl.BlockSpec((1, 1, 2 * tm), lambda i: (i, 0, 0), memory_space=pltpu.SMEM),
                      pl.BlockSpec((tm, D_MODEL), lambda i: (i, 0)),
                      pl.BlockSpec((tm, ROUTER_LANES), lambda i: (i, 0)),
                      pl.BlockSpec((1, D_MODEL), lambda i: (0, 0)),
                      pl.BlockSpec(memory_space=pl.ANY)],
            out_specs=[pl.BlockSpec((tm, D_MODEL), lambda i: (jnp.minimum(i, n_p - 1), 0)),
                       pl.BlockSpec((tm, D_MODEL), lambda i: (0, 0))],
            scratch_shapes=[pltpu.VMEM((2, tm, D_MODEL), F32), pltpu.SemaphoreType.DMA(())],
        ),
        out_shape=[jax.ShapeDtypeStruct((n_p * tm, D_MODEL), F32), jax.ShapeDtypeStruct((tm, D_MODEL), F32)],
        compiler_params=_cparams("arbitrary"),
        name="moe_combine",
    )(slots, h, route, gfin, ys)


def _pair_state(s):
    b = s.shape[0]
    return s.reshape(b, RWKV_PAIRS, 2, RWKV_HEAD, RWKV_HEAD).transpose(0, 1, 3, 2, 4).reshape(
        b, RWKV_PAIRS, RWKV_HEAD, LANE)


def _unpair_state(s):
    b = s.shape[0]
    return s.reshape(b, RWKV_PAIRS, RWKV_HEAD, 2, RWKV_HEAD).transpose(0, 1, 3, 2, 4).reshape(
        b, RWKV_HEADS, RWKV_HEAD, RWKV_HEAD)


def _v_tiles(v, tblk):
    b, t, _ = v.shape
    x = v.reshape(b, t // tblk, tblk, RWKV_PAIRS, 2, RWKV_HEAD).transpose(0, 1, 3, 5, 4, 2)
    x = jnp.pad(x, ((0, 0),) * 5 + ((0, RWKV_HEAD - tblk),))
    return x.reshape(b, t // tblk, RWKV_PAIRS, RWKV_HEAD, LANE).astype(BF16)


def _y_rows(y, bb, t):
    nb = y.shape[0]
    x = y.reshape(nb, -1, 4, 2, bb, RWKV_PAIRS, RWKV_HEAD).transpose(0, 4, 1, 2, 5, 3, 6)
    return x.reshape(nb * bb, -1, RWKV_WIDTH)[:, :t]


def _rwkv_branch(pr, aux, s0_pairs, pre_params, *, tm, explicit_prev, bb):
    b, t, _ = pr.shape
    r, w, k, kk, kka, v, bv, gate = _rwkv_pre(pr, aux, pre_params, tm=tm, explicit_prev=explicit_prev)
    tblk = min(REC_TB, t)
    y, s_new = _rwkv_rec(r, w, k, kk, kka, _v_tiles(v, tblk), s0_pairs, bb=bb, n_steps=tblk)
    return _y_rows(y, bb, t), bv, gate, s_new


def kernel(x_prompt, x_sample, state_gla, state_rwkv, state_shift, meta_tokens, norm_mix, w_in, gla_gate_w2,
           gla_gate_b, gla_norm, rwkv_mu, rwkv_w0, rwkv_w2, rwkv_a0, rwkv_a2, rwkv_g2, rwkv_kk, rwkv_ka, rwkv_rk,
           rwkv_ln_w, rwkv_ln_b, w_out, norm_ffn, router_group_w, router_group_b, router_expert_w,
           router_expert_b, moe_w1, moe_w3, moe_w2, norm_final):
    bp, tp, _ = x_prompt.shape
    bs, ts, _ = x_sample.shape
    assert state_gla.shape[0] == 1, "one layer"
    lyr = 0

    w_in_l = w_in[lyr]
    wg = jnp.pad(w_in_l[:, :GLA_COLS], ((0, 0), (0, GLA_PCOLS - GLA_COLS))).astype(BF16)
    wr = w_in_l[:, GLA_COLS:].astype(BF16)
    g_mix = norm_mix[lyr][None, :]
    gw2p = jnp.pad(gla_gate_w2[lyr], ((0, LANE - GLA_GATE_RANK), (0, 0)))
    gb = gla_gate_b[lyr][None, :]
    gn = gla_norm[lyr][None, :]
    w2p = jnp.pad(rwkv_w2[lyr], ((0, 64), (0, 0))).astype(BF16)
    a2p = jnp.pad(rwkv_a2[lyr], ((64, 0), (0, 0))).astype(BF16)
    pre_params = (rwkv_mu[lyr][None, :], rwkv_w0[lyr][None, :], w2p, rwkv_a0[lyr][None, :], a2p,
                  rwkv_g2[lyr].astype(BF16), rwkv_kk[lyr][None, :], rwkv_ka[lyr][None, :],
                  rwkv_rk[lyr].reshape(1, RWKV_WIDTH), _block_ones(RWKV_WIDTH, RWKV_HEAD))
    lnw = rwkv_ln_w[lyr][None, :]
    lnb = rwkv_ln_b[lyr][None, :]
    wo = w_out[lyr].astype(BF16)
    gffn = norm_ffn[lyr][None, :]
    n_used = N_GROUPS + N_EXPERTS
    w_router = jnp.pad(
        jnp.concatenate([router_group_w[lyr],
                         router_expert_w[lyr].transpose(1, 0, 2).reshape(D_MODEL, N_EXPERTS)], axis=1),
        ((0, 0), (0, ROUTER_LANES - n_used)))
    b_router = jnp.pad(jnp.concatenate([router_group_b[lyr], router_expert_b[lyr].reshape(N_EXPERTS)]),
                       (0, ROUTER_LANES - n_used))[None, :]
    gfin = norm_final[None, :]

    pg_m, pr_m = _inproj(meta_tokens, g_mix, wg, wr, N_META)
    _, sg_m = _gla(pg_m[None], jnp.zeros((1, GLA_HEADS, GLA_DK, GLA_DV), F32), gw2p, gb, gn,
                   bb=1, chunk=N_META, sub=N_META, t_valid=N_META)
    _, _, _, sr_m = _rwkv_branch(pr_m[None], jnp.zeros((1, 1, RWKV_COLS), F32),
                                 jnp.zeros((1, RWKV_PAIRS, RWKV_HEAD, LANE), F32), pre_params,
                                 tm=N_META, explicit_prev=False, bb=1)

    xp = x_prompt.reshape(bp * tp, D_MODEL)
    pg_p, pr_p = _inproj(xp, g_mix, wg, wr, 512)
    og_p, sg_p = _gla(pg_p.reshape(bp, tp, GLA_PCOLS), jnp.broadcast_to(sg_m, (bp,) + sg_m.shape[1:]), gw2p, gb, gn,
                      bb=1, chunk=GLA_CHUNK, sub=GLA_SUB, t_valid=GLA_CHUNK)
    pr_p3 = pr_p.reshape(bp, tp, RWKV_COLS)
    first_prev = jnp.broadcast_to(pr_m[N_META - 1][None, None, :], (bp, 1, RWKV_COLS))
    y_p, bv_p, gate_p, sr_p = _rwkv_branch(pr_p3, first_prev, jnp.broadcast_to(sr_m, (bp,) + sr_m.shape[1:]),
                                           pre_params, tm=256, explicit_prev=False, bb=bp)
    prompt_rows = (xp, og_p.reshape(bp * tp, GLA_WIDTH), y_p.reshape(bp * tp, RWKV_WIDTH),
                   bv_p.reshape(bp * tp, RWKV_WIDTH), gate_p.reshape(bp * tp, RWKV_WIDTH))

    xs = x_sample.reshape(bs * ts, D_MODEL)
    pg_s, pr_s = _inproj(xs, g_mix, wg, wr, bs * ts)
    ts_pad = 8
    pg_s3 = jnp.pad(pg_s.reshape(bs, ts, GLA_PCOLS), ((0, 0), (0, ts_pad - ts), (0, 0)))
    og_s, sg_s = _gla(pg_s3, state_gla[lyr], gw2p, gb, gn, bb=8, chunk=ts_pad, sub=ts_pad, t_valid=ts)
    og_s = og_s[:, :ts]
    pr_s3 = pr_s.reshape(bs, ts, RWKV_COLS)
    prev_s = jnp.concatenate([state_shift[lyr][:, None, :], pr_s3[:, :-1]], axis=1)
    r, w, k, kk, kka, v_s, bv_s, gate_s = _rwkv_pre(pr_s3.reshape(1, bs * ts, RWKV_COLS),
                                                     prev_s.reshape(1, bs * ts, RWKV_COLS), pre_params,
                                                     tm=bs * ts, explicit_prev=True)
    unflat = lambda a: a.reshape(RWKV_PAIRS, bs, ts, LANE).transpose(1, 0, 2, 3)
    y_s, sr_s = _rwkv_rec(unflat(r), unflat(w), unflat(k), unflat(kk), unflat(kka),
                          _v_tiles(v_s.reshape(bs, ts, RWKV_WIDTH), ts), _pair_state(state_rwkv[lyr]),
                          bb=8, n_steps=ts)
    y_s = _y_rows(y_s, 8, ts)
    sample_rows = (xs, og_s.reshape(bs * ts, GLA_WIDTH), y_s.reshape(bs * ts, RWKV_WIDTH),
                   bv_s.reshape(bs * ts, RWKV_WIDTH), gate_s.reshape(bs * ts, RWKV_WIDTH))

    h_all, n2_all, route, counts = _mix_router(prompt_rows, sample_rows, lnw, lnb, wo, gffn, w_router, b_router)
    n_tok = h_all.shape[0]
    n_p_tiles = (bp * tp) // MOE_TM
    cnt = counts[0, EXPERT_LANE0:EXPERT_LANE0 + N_EXPERTS].astype(jnp.int32)
    offs = jnp.cumsum(cnt) - cnt
    eid = route[:, ROUTE_E1:ROUTE_E2 + 1].astype(jnp.int32)
    pos = route[:, ROUTE_P1:ROUTE_P2 + 1].astype(jnp.int32)
    slots = (offs[eid] + pos).reshape(n_tok // MOE_TM, 1, 2 * MOE_TM)
    xs_sorted = _dispatch(n2_all, slots)
    ys_sorted = _experts(xs_sorted, _expert_work_items(cnt, 2 * n_tok), moe_w1[lyr], moe_w3[lyr], moe_w2[lyr])
    y_prompt, y_sample = _combine(h_all, route, slots, ys_sorted, gfin, n_p_tiles)
    y_prompt = y_prompt.reshape(bp, tp, D_MODEL)
    y_sample = y_sample.reshape(bs, ts, D_MODEL)

    return (y_prompt, y_sample,
            sg_p[None], _unpair_state(sr_p)[None], pr_p3[:, -1][None],
            sg_s[None], _unpair_state(sr_s)[None], pr_s3[:, -1][None])
```

```python
import functools

import jax
import jax.numpy as jnp
from jax import lax
from jax.experimental import pallas as pl
from jax.experimental.pallas import tpu as pltpu

F32 = jnp.float32
BF16 = jnp.bfloat16
HIGHEST = lax.Precision.HIGHEST

D_MODEL = 1024
N_META = 16
NORM_EPS = 1e-6
GLA_HEADS = 4
GLA_DK = 64
GLA_DV = 128
GLA_QK = GLA_HEADS * GLA_DK
GLA_WIDTH = GLA_HEADS * GLA_DV
GLA_GATE_RANK = 16
GLA_GATE_NORM = 16.0
GLA_CHUNK = 64
GLA_SUB = 16
GLA_COLS = 2 * GLA_QK + 2 * GLA_WIDTH + GLA_GATE_RANK
GLA_PCOLS = 2 * GLA_QK + 2 * GLA_WIDTH + 128
RWKV_WIDTH = 512
RWKV_HEAD = 64
RWKV_HEADS = 8
RWKV_PAIRS = RWKV_HEADS // 2
RWKV_DECAY_SCALE = 0.606531
RWKV_GN_EPS = 64e-5
RWKV_COLS = 3 * RWKV_WIDTH + 64 + 64 + 128
REC_TB = 64
N_GROUPS = 4
EXPERTS_PER_GROUP = 8
N_EXPERTS = 32
D_EXPERT = 512
ROUTER_LANES = 128
EXPERT_LANE0 = N_GROUPS
ROUTE_E1, ROUTE_E2, ROUTE_W1, ROUTE_W2, ROUTE_P1, ROUTE_P2 = range(6)
MOE_TM = 512
MOE_TS = 256

LANE = 128
VMEM_LIMIT = 56 * 1024 * 1024


def _cparams(*sem):
    return pltpu.CompilerParams(dimension_semantics=sem, vmem_limit_bytes=VMEM_LIMIT)


def _block_ones(n, blk):
    i = jnp.arange(n)
    return (i[:, None] // blk == i[None, :] // blk).astype(BF16)


def _sigmoid(x):
    return 1.0 / (1.0 + jnp.exp(-x))


def _dot(a, b):
    return jnp.dot(a, b, preferred_element_type=F32)


def _dot_nt(a, b):
    return lax.dot_general(a, b, (((1,), (1,)), ((), ())), preferred_element_type=F32)


def _dot_tn(a, b):
    return lax.dot_general(a, b, (((0,), (0,)), ((), ())), preferred_element_type=F32)


def _split2(x):
    hi = x.astype(BF16)
    lo = (x - hi.astype(F32)).astype(BF16)
    return hi, lo


def _group_sum(x, bo):
    hi, lo = _split2(x)
    return _dot(hi, bo) + _dot(lo, bo)


def _inproj_kernel(x_ref, g_ref, wg_ref, wr_ref, pg_ref, pr_ref):
    x = x_ref[...]
    n = x * lax.rsqrt(jnp.mean(x * x, axis=-1, keepdims=True) + NORM_EPS) * g_ref[...]
    nb = n.astype(BF16)
    pg_ref[...] = _dot(nb, wg_ref[...])
    pr_ref[...] = _dot(nb, wr_ref[...])


def _inproj(x, g, wg, wr, tm):
    t = x.shape[0]
    return pl.pallas_call(
        _inproj_kernel,
        grid=(t // tm,),
        in_specs=[
            pl.BlockSpec((tm, D_MODEL), lambda i: (i, 0)),
            pl.BlockSpec((1, D_MODEL), lambda i: (0, 0)),
            pl.BlockSpec((D_MODEL, GLA_PCOLS), lambda i: (0, 0)),
            pl.BlockSpec((D_MODEL, RWKV_COLS), lambda i: (0, 0)),
        ],
        out_specs=[
            pl.BlockSpec((tm, GLA_PCOLS), lambda i: (i, 0)),
            pl.BlockSpec((tm, RWKV_COLS), lambda i: (i, 0)),
        ],
        out_shape=[jax.ShapeDtypeStruct((t, GLA_PCOLS), F32), jax.ShapeDtypeStruct((t, RWKV_COLS), F32)],
        compiler_params=_cparams("parallel"),
        name="inproj",
    )(x, g, wg, wr)


def _gla_kernel(pg_ref, s0_ref, gw2_ref, gb_ref, gn_ref, bo_ref, tril_ref, o_ref, sout_ref, s_scr,
                *, bb, chunk, sub, t_valid):
    ci = pl.program_id(1)

    @pl.when(ci == 0)
    def _():
        s_scr[...] = s0_ref[...]

    bo = bo_ref[...]
    tril = tril_ref[...]
    lane = lax.broadcasted_iota(jnp.int32, (sub, LANE), 1) & (GLA_DK - 1)
    rowi = lax.broadcasted_iota(jnp.int32, (sub, LANE), 0)
    rowc = lax.broadcasted_iota(jnp.int32, (chunk, GLA_DK), 0)
    ones_cv = jnp.ones((chunk, GLA_DV), BF16)

    for bi in range(bb):
        pg = pg_ref[bi]
        q = pg[:, 0:GLA_QK] * (GLA_DK ** -0.5)
        k = pg[:, GLA_QK:2 * GLA_QK]
        v = pg[:, 2 * GLA_QK:2 * GLA_QK + GLA_WIDTH]
        g = pg[:, 2 * GLA_QK + GLA_WIDTH:2 * GLA_QK + 2 * GLA_WIDTH]
        gl = pg[:, 2 * GLA_QK + 2 * GLA_WIDTH:]
        z = jnp.dot(gl, gw2_ref[...], precision=HIGHEST, preferred_element_type=F32) + gb_ref[...]
        lg = (jnp.minimum(z, 0.0) - jnp.log1p(jnp.exp(-jnp.abs(z)))) * (1.0 / GLA_GATE_NORM)
        if t_valid < chunk:
            rows = lax.broadcasted_iota(jnp.int32, lg.shape, 0)
            lg = jnp.where(rows < t_valid, lg, 0.0)
        b = jnp.dot(tril, lg, precision=HIGHEST, preferred_element_type=F32)
        eb = jnp.exp(b)
        blast = b[chunk - 1:chunk, :]
        kl = k * jnp.exp(blast - b)
        qe = q * eb

        o_heads = []
        for hp in range(GLA_HEADS // 2):
            sl = slice(hp * LANE, (hp + 1) * LANE)
            qp, kp, bp = q[:, sl], k[:, sl], b[:, sl]
            row_blocks = []
            for blk in range(chunk // sub):
                rs = slice(blk * sub, (blk + 1) * sub)
                qb, kb, bbk = qp[rs], kp[rs], bp[rs]
                ps = []
                for j in range(sub):
                    ps.append(qb * (kb[j:j + 1] * jnp.exp(jnp.minimum(bbk - bbk[j:j + 1], 0.0))))
                red = _dot(jnp.concatenate(ps, axis=0).astype(BF16), bo)
                a = jnp.zeros((sub, LANE), F32)
                for j in range(sub):
                    a = jnp.where((lane == blk * sub + j) & (rowi >= j), red[j * sub:(j + 1) * sub], a)
                if blk > 0:
                    bref = bp[blk * sub - 1:blk * sub]
                    qt = (qb * jnp.exp(bbk - bref)).astype(BF16)
                    kt = (kp * jnp.exp(jnp.minimum(bref - bp, 0.0))).astype(BF16)
                    off = jnp.concatenate(
                        [_dot_nt(qt[:, :GLA_DK], kt[:, :GLA_DK]), _dot_nt(qt[:, GLA_DK:], kt[:, GLA_DK:])], axis=1)
                    a = jnp.where(lane < blk * sub, off, a)
                row_blocks.append(a)
            a_pair = row_blocks[0] if len(row_blocks) == 1 else jnp.concatenate(row_blocks, axis=0)
            for h2 in range(2):
                h = 2 * hp + h2
                a_h = a_pair[:, h2 * GLA_DK:h2 * GLA_DK + chunk].astype(BF16)
                vhb = v[:, h * GLA_DV:(h + 1) * GLA_DV].astype(BF16)
                s_h = s_scr[bi, h]
                hs = slice(h * GLA_DK, (h + 1) * GLA_DK)
                o_h = _dot(a_h, vhb) + _dot(qe[:, hs].astype(BF16), s_h.astype(BF16))
                e_hi, e_lo = _split2(jnp.where(rowc == chunk - 1, eb[:, hs], 0.0))
                dcol = _dot_tn(e_hi, ones_cv) + _dot_tn(e_lo, ones_cv)
                s_scr[bi, h] = dcol * s_h + _dot_tn(kl[:, hs].astype(BF16), vhb)
                o_heads.append(o_h * lax.rsqrt(jnp.mean(o_h * o_h, axis=-1, keepdims=True) + NORM_EPS) * gn_ref[...])
        o = jnp.concatenate(o_heads, axis=1)
        o_ref[bi] = o * (g * _sigmoid(g))

    @pl.when(ci == pl.num_programs(1) - 1)
    def _():
        sout_ref[...] = s_scr[...]


def _gla(pg, s0, gw2p, gb, gn, *, bb, chunk, sub, t_valid):
    b, t, _ = pg.shape
    tril = jnp.tril(jnp.ones((chunk, chunk), F32))
    kern = functools.partial(_gla_kernel, bb=bb, chunk=chunk, sub=sub, t_valid=t_valid)
    return pl.pallas_call(
        kern,
        grid=(b // bb, t // chunk),
        in_specs=[
            pl.BlockSpec((bb, chunk, GLA_PCOLS), lambda i, j: (i, j, 0)),
            pl.BlockSpec((bb, GLA_HEADS, GLA_DK, GLA_DV), lambda i, j: (i, 0, 0, 0)),
            pl.BlockSpec((LANE, GLA_QK), lambda i, j: (0, 0)),
            pl.BlockSpec((1, GLA_QK), lambda i, j: (0, 0)),
            pl.BlockSpec((1, GLA_DV), lambda i, j: (0, 0)),
            pl.BlockSpec((LANE, LANE), lambda i, j: (0, 0)),
            pl.BlockSpec((chunk, chunk), lambda i, j: (0, 0)),
        ],
        out_specs=[
            pl.BlockSpec((bb, chunk, GLA_WIDTH), lambda i, j: (i, j, 0)),
            pl.BlockSpec((bb, GLA_HEADS, GLA_DK, GLA_DV), lambda i, j: (i, 0, 0, 0)),
        ],
        out_shape=[jax.ShapeDtypeStruct((b, t, GLA_WIDTH), F32),
                   jax.ShapeDtypeStruct((b, GLA_HEADS, GLA_DK, GLA_DV), F32)],
        scratch_shapes=[pltpu.VMEM((bb, GLA_HEADS, GLA_DK, GLA_DV), F32)],
        compiler_params=_cparams("parallel", "arbitrary"),
        name="gla_chunk",
    )(pg, s0, gw2p, gb, gn, _block_ones(LANE, GLA_DK), tril)


def _rwkv_pre_kernel(pr_ref, aux_ref, mu_ref, w0_ref, w2_ref, a0_ref, a2_ref, g2_ref, kk_ref, ka_ref, rk_ref, bo_ref,
                     r_out, w_out, k_out, kkn_out, kka_out, v_out, bv_out, gate_out, carry_scr,
                     *, tm, explicit_prev, emit_vt):
    pr = pr_ref[0]
    if explicit_prev:
        prev = aux_ref[0]
    else:
        j = pl.program_id(1)
        row0 = jnp.where(j == 0, aux_ref[0], carry_scr[...])
        rows = lax.broadcasted_iota(jnp.int32, pr.shape, 0)
        prev = jnp.where(rows == 0, row0, pltpu.roll(pr, 1, 0))
        carry_scr[...] = pr[tm - 1:tm, :]
    xm = pr + (prev - pr) * mu_ref[...]
    wd = RWKV_WIDTH
    rr, rk, rv = xm[:, 0:wd], xm[:, wd:2 * wd], xm[:, 2 * wd:3 * wd]
    wa = xm[:, 3 * wd:3 * wd + LANE]
    gl2 = xm[:, 3 * wd + LANE:3 * wd + 2 * LANE]
    logw = -RWKV_DECAY_SCALE * _sigmoid(w0_ref[...] + _dot(jnp.tanh(wa).astype(BF16), w2_ref[...]))
    aa = _sigmoid(a0_ref[...] + _dot(wa.astype(BF16), a2_ref[...]))
    gate = _dot(_sigmoid(gl2).astype(BF16), g2_ref[...])
    bo = bo_ref[...]
    kk = rk * kk_ref[...]
    kk = kk / jnp.maximum(jnp.sqrt(_group_sum(kk * kk, bo)), 1e-12)
    k = rk * (1.0 + (aa - 1.0) * ka_ref[...])
    bv = _group_sum(rr * k * rk_ref[...], bo) * rv
    w = jnp.exp(logw)
    kka = kk * aa
    hd = RWKV_HEAD

    def pair(x, hp):
        return jnp.concatenate([x[:, hp * hd:(hp + 1) * hd], x[:, (hp + RWKV_PAIRS) * hd:(hp + RWKV_PAIRS + 1) * hd]],
                               axis=1)

    for hp in range(RWKV_PAIRS):
        r_out[0, hp] = pair(rr, hp)
        w_out[0, hp] = pair(w, hp)
        k_out[0, hp] = pair(k, hp)
        kkn_out[0, hp] = pair(kk, hp)
        kka_out[0, hp] = pair(kka, hp)
    if emit_vt:
        vt = rv.T
        for tb in range(tm // REC_TB):
            ts = slice(tb * REC_TB, (tb + 1) * REC_TB)
            for hp in range(RWKV_PAIRS):
                lo, hi = hp * hd, (hp + RWKV_PAIRS) * hd
                v_out[0, tb, hp] = jnp.concatenate([vt[lo:lo + hd, ts], vt[hi:hi + hd, ts]], axis=1).astype(BF16)
    else:
        v_out[0] = rv
    bv_out[0] = bv
    gate_out[0] = gate


def _rwkv_pre(pr, aux, params, *, tm, explicit_prev, emit_vt):
    b, t, _ = pr.shape
    kern = functools.partial(_rwkv_pre_kernel, tm=tm, explicit_prev=explicit_prev, emit_vt=emit_vt)
    aux_spec = (pl.BlockSpec((1, tm, RWKV_COLS), lambda i, j: (i, j, 0)) if explicit_prev
                else pl.BlockSpec((1, 1, RWKV_COLS), lambda i, j: (i, 0, 0)))
    const = lambda shape: pl.BlockSpec(shape, lambda i, j: (0,) * len(shape))
    pair_spec = pl.BlockSpec((1, RWKV_PAIRS, tm, LANE), lambda i, j: (i, 0, j, 0))
    row_spec = pl.BlockSpec((1, tm, RWKV_WIDTH), lambda i, j: (i, j, 0))
    pair_shape = jax.ShapeDtypeStruct((b, RWKV_PAIRS, t, LANE), F32)
    row_shape = jax.ShapeDtypeStruct((b, t, RWKV_WIDTH), F32)
    if emit_vt:
        v_spec = pl.BlockSpec((1, tm // REC_TB, RWKV_PAIRS, RWKV_HEAD, LANE), lambda i, j: (i, j, 0, 0, 0))
        v_shape = jax.ShapeDtypeStruct((b, t // REC_TB, RWKV_PAIRS, RWKV_HEAD, LANE), BF16)
    else:
        v_spec, v_shape = row_spec, row_shape
    return pl.pallas_call(
        kern,
        grid=(b, t // tm),
        in_specs=[
            pl.BlockSpec((1, tm, RWKV_COLS), lambda i, j: (i, j, 0)),
            aux_spec,
            const((1, RWKV_COLS)), const((1, RWKV_WIDTH)), const((LANE, RWKV_WIDTH)), const((1, RWKV_WIDTH)),
            const((LANE, RWKV_WIDTH)), const((LANE, RWKV_WIDTH)), const((1, RWKV_WIDTH)), const((1, RWKV_WIDTH)),
            const((1, RWKV_WIDTH)), const((RWKV_WIDTH, RWKV_WIDTH)),
        ],
        out_specs=[pair_spec] * 5 + [v_spec, row_spec, row_spec],
        out_shape=[pair_shape] * 5 + [v_shape, row_shape, row_shape],
        scratch_shapes=[pltpu.VMEM((1, RWKV_COLS), F32)],
        compiler_params=_cparams("parallel", "arbitrary"),
        name="rwkv_pre",
    )(pr, aux, *params)


def _rwkv_rec_kernel(r_ref, w_ref, k_ref, kk_ref, kka_ref, vt_ref, s0_ref, bo_ref, vsel_ref, ysel_ref,
                     y_ref, sout_ref, s_scr, t1_scr, t3_scr, yt_scr, *, bb, n_steps):
    tb = pl.program_id(1)
    nc = bb * RWKV_PAIRS
    hd = RWKV_HEAD

    @pl.when(tb == 0)
    def _():
        for c in range(nc):
            s_scr[c] = s0_ref[c // RWKV_PAIRS, c % RWKV_PAIRS]

    bo = bo_ref[...]

    def step(t, u):
        row = pl.ds(t, 1)
        for c in range(nc):
            bi, hp = divmod(c, RWKV_PAIRS)
            t1_scr[c * hd:(c + 1) * hd, :] = (s_scr[c] * kk_ref[bi, hp, row, :]).astype(BF16)
        sab = _dot(t1_scr[...], bo)
        vb = _dot(vt_ref[...].reshape(nc * hd, LANE), vsel_ref[t])
        for c in range(nc):
            bi, hp = divmod(c, RWKV_PAIRS)
            rs = slice(c * hd, (c + 1) * hd)
            s2 = (s_scr[c] * w_ref[bi, hp, row, :] - sab[rs] * kka_ref[bi, hp, row, :]
                  + vb[rs] * k_ref[bi, hp, row, :])
            s_scr[c] = s2
            t3_scr[rs, :] = (s2 * r_ref[bi, hp, row, :]).astype(BF16)
        yt_scr[...] += _dot_nt(ysel_ref[u], t3_scr[...])

    n_inner = min(8, n_steps)

    def block8(t8, carry):
        yt_scr[...] = jnp.zeros(yt_scr.shape, F32)

        def inner(u, c2):
            step(t8 * 8 + u, u)
            return c2

        lax.fori_loop(0, n_inner, inner, 0)
        t0 = pl.multiple_of(t8 * 8, 8)
        blk = RWKV_PAIRS * hd
        for bi in range(bb):
            for h2 in range(2):
                y_ref[0, pl.ds(t0, 8), bi * RWKV_WIDTH + h2 * blk:bi * RWKV_WIDTH + (h2 + 1) * blk] = (
                    yt_scr[h2 * 8:(h2 + 1) * 8, bi * blk:(bi + 1) * blk])
        return carry

    lax.fori_loop(0, (n_steps + 7) // 8, block8, 0)

    @pl.when(tb == pl.num_programs(1) - 1)
    def _():
        for c in range(nc):
            sout_ref[c // RWKV_PAIRS, c % RWKV_PAIRS] = s_scr[c]


def _rwkv_rec(r, w, k, kk, kka, vt, s0, *, bb, n_steps):
    b, _, t, _ = r.shape
    tblk = min(REC_TB, t)
    ntb = t // tblk
    nc = bb * RWKV_PAIRS
    lane = jnp.arange(LANE)
    vsel = ((lane[None, :, None] // RWKV_HEAD == lane[None, None, :] // RWKV_HEAD)
            & (lane[None, :, None] % RWKV_HEAD == jnp.arange(RWKV_HEAD)[:, None, None])).astype(BF16)
    ysel = (jnp.arange(16)[None, :, None]
            == 8 * (lane[None, None, :] // RWKV_HEAD) + jnp.arange(8)[:, None, None]).astype(BF16)
    kern = functools.partial(_rwkv_rec_kernel, bb=bb, n_steps=n_steps)
    pair_spec = pl.BlockSpec((bb, RWKV_PAIRS, tblk, LANE), lambda i, j: (i, 0, j, 0))
    state_spec = pl.BlockSpec((bb, RWKV_PAIRS, RWKV_HEAD, LANE), lambda i, j: (i, 0, 0, 0))
    ytb = max(tblk, 8)
    return pl.pallas_call(
        kern,
        grid=(b // bb, ntb),
        in_specs=[pair_spec] * 5 + [
            pl.BlockSpec((bb, 1, RWKV_PAIRS, RWKV_HEAD, LANE), lambda i, j: (i, j, 0, 0, 0)),
            state_spec,
            pl.BlockSpec((LANE, LANE), lambda i, j: (0, 0)),
            pl.BlockSpec((RWKV_HEAD, LANE, LANE), lambda i, j: (0, 0, 0)),
            pl.BlockSpec((8, 16, LANE), lambda i, j: (0, 0, 0)),
        ],
        out_specs=[
            pl.BlockSpec((1, ytb, bb * RWKV_WIDTH), lambda i, j: (i, j, 0)),
            state_spec,
        ],
        out_shape=[jax.ShapeDtypeStruct((b // bb, ntb * ytb, bb * RWKV_WIDTH), F32),
                   jax.ShapeDtypeStruct((b, RWKV_PAIRS, RWKV_HEAD, LANE), F32)],
        scratch_shapes=[pltpu.VMEM((nc, RWKV_HEAD, LANE), F32),
                        pltpu.VMEM((nc * RWKV_HEAD, LANE), BF16),
                        pltpu.VMEM((nc * RWKV_HEAD, LANE), BF16),
                        pltpu.VMEM((16, nc * RWKV_HEAD), F32)],
        compiler_params=_cparams("parallel", "arbitrary"),
        name="rwkv_rec",
    )(r, w, k, kk, kka, vt, s0, _block_ones(LANE, RWKV_HEAD), vsel, ysel)


def _mix_router_body(x_ref, og_ref, y_ref, bv_ref, gate_ref, lnw_ref, lnb_ref, wo_ref, gffn_ref, wr_ref, br_ref,
                     bo_ref, tril_ref, h_ref, n2_ref, route_ref, cnt_scr):
    bo = bo_ref[...]
    y = y_ref[...]
    inv_n = 1.0 / RWKV_HEAD
    d = y - _group_sum(y, bo) * inv_n
    var = _group_sum(d * d, bo) * inv_n
    yn = d * lax.rsqrt(var + RWKV_GN_EPS) * lnw_ref[...] + lnb_ref[...] + bv_ref[...]
    o_rwkv = yn * gate_ref[...]
    mix = (_dot(og_ref[...].astype(BF16), wo_ref[0:GLA_WIDTH, :])
           + _dot(o_rwkv.astype(BF16), wo_ref[GLA_WIDTH:, :]))
    h = x_ref[...] + mix
    h_ref[...] = h
    n2 = h * lax.rsqrt(jnp.mean(h * h, axis=-1, keepdims=True) + NORM_EPS) * gffn_ref[...]
    n2_ref[...] = n2
    lg = jnp.dot(n2, wr_ref[...], precision=HIGHEST, preferred_element_type=F32) + br_ref[...]
    neg = jnp.float32(-3.0e38)
    big = jnp.float32(1.0e9)
    lane = lax.broadcasted_iota(jnp.int32, lg.shape, 1).astype(F32)
    gmask = lane < N_GROUPS
    gmax = jnp.max(jnp.where(gmask, lg, neg), axis=1, keepdims=True)
    p_top = 1.0 / jnp.sum(jnp.where(gmask, jnp.exp(jnp.minimum(lg - gmax, 0.0)), 0.0), axis=1, keepdims=True)
    gidx = jnp.min(jnp.where(gmask & (lg == gmax), lane, big), axis=1, keepdims=True)
    e_lo = EXPERT_LANE0 + gidx * EXPERTS_PER_GROUP
    emask = (lane >= e_lo) & (lane < e_lo + EXPERTS_PER_GROUP)
    m1 = jnp.max(jnp.where(emask, lg, neg), axis=1, keepdims=True)
    e1 = jnp.min(jnp.where(emask & (lg == m1), lane, big), axis=1, keepdims=True)
    emask2 = emask & (lane != e1)
    m2 = jnp.max(jnp.where(emask2, lg, neg), axis=1, keepdims=True)
    e2 = jnp.min(jnp.where(emask2 & (lg == m2), lane, big), axis=1, keepdims=True)
    r21 = jnp.exp(m2 - m1)
    w1 = p_top / (1.0 + r21)
    w2 = p_top * r21 / (1.0 + r21)
    o1 = lane == e1
    o2 = lane == e2
    onehot = jnp.where(o1 | o2, 1.0, 0.0)
    rank = _dot(tril_ref[...], onehot.astype(BF16)) + cnt_scr[...]
    pos1 = jnp.sum(jnp.where(o1, rank, 0.0), axis=1, keepdims=True)
    pos2 = jnp.sum(jnp.where(o2, rank, 0.0), axis=1, keepdims=True)
    cnt_scr[...] += jnp.sum(onehot, axis=0, keepdims=True)
    route = jnp.where(lane == ROUTE_E1, e1 - EXPERT_LANE0, 0.0)
    route = jnp.where(lane == ROUTE_E2, e2 - EXPERT_LANE0, route)
    route = jnp.where(lane == ROUTE_W1, w1, route)
    route = jnp.where(lane == ROUTE_W2, w2, route)
    route = jnp.where(lane == ROUTE_P1, pos1, route)
    route_ref[...] = jnp.where(lane == ROUTE_P2, pos2, route)


def _mix_router_kernel(*refs, n_prompt_tiles):
    prompt_rows, sample_rows, rest = refs[0:5], refs[5:10], refs[10:]
    consts, (h_ref, n2_ref, route_ref, cnt_ref, cnt_scr) = rest[:8], rest[8:]
    i = pl.program_id(0)

    @pl.when(i == 0)
    def _():
        cnt_scr[...] = jnp.zeros(cnt_scr.shape, F32)

    @pl.when(i < n_prompt_tiles)
    def _():
        _mix_router_body(*prompt_rows, *consts, h_ref, n2_ref, route_ref, cnt_scr)

    @pl.when(i >= n_prompt_tiles)
    def _():
        _mix_router_body(*sample_rows, *consts, h_ref, n2_ref, route_ref, cnt_scr)

    cnt_ref[...] = cnt_scr[...]


def _mix_router(prompt_rows, sample_rows, lnw, lnb, wo, gffn, wr, br, *, seq_tiles):
    tm = MOE_TM
    n_p = prompt_rows[0].shape[0] // tm
    assert sample_rows[0].shape[0] == tm
    t = (n_p + 1) * tm
    widths = (D_MODEL, GLA_WIDTH, RWKV_WIDTH, RWKV_WIDTH, RWKV_WIDTH)
    p_specs = [pl.BlockSpec((tm, n), lambda i: (jnp.minimum(i, n_p - 1), 0)) for n in widths]
    p_specs[2] = pl.BlockSpec(
        (tm, RWKV_WIDTH), lambda i: (jnp.minimum(i, n_p - 1) % seq_tiles, jnp.minimum(i, n_p - 1) // seq_tiles))
    s_specs = [pl.BlockSpec((tm, n), lambda i: (0, 0)) for n in widths]
    const = lambda shape: pl.BlockSpec(shape, lambda i: (0,) * len(shape))
    row = lambda n: pl.BlockSpec((tm, n), lambda i: (i, 0))
    tril = jnp.tril(jnp.ones((tm, tm), F32), -1).astype(BF16)
    return pl.pallas_call(
        functools.partial(_mix_router_kernel, n_prompt_tiles=n_p),
        grid=(n_p + 1,),
        in_specs=p_specs + s_specs + [
            const((1, RWKV_WIDTH)), const((1, RWKV_WIDTH)), const((D_MODEL, D_MODEL)), const((1, D_MODEL)),
            const((D_MODEL, ROUTER_LANES)), const((1, ROUTER_LANES)), const((RWKV_WIDTH, RWKV_WIDTH)),
            const((tm, tm))],
        out_specs=[row(D_MODEL), row(D_MODEL), row(ROUTER_LANES), const((1, ROUTER_LANES))],
        out_shape=[jax.ShapeDtypeStruct((t, D_MODEL), F32), jax.ShapeDtypeStruct((t, D_MODEL), F32),
                   jax.ShapeDtypeStruct((t, ROUTER_LANES), F32), jax.ShapeDtypeStruct((1, ROUTER_LANES), F32)],
        scratch_shapes=[pltpu.VMEM((1, ROUTER_LANES), F32)],
        compiler_params=_cparams("arbitrary"),
        name="mix_router",
    )(*prompt_rows, *sample_rows, lnw, lnb, wo, gffn, wr, br, _block_ones(RWKV_WIDTH, RWKV_HEAD), tril)


ROW_DMA_UNROLL = 8


def _dispatch_kernel(slots_ref, x_ref, xs_hbm, sem):
    tm = x_ref.shape[0]

    def issue(r, c):
        src = x_ref.at[pl.ds(r, 1)]
        pltpu.make_async_copy(src, xs_hbm.at[pl.ds(slots_ref[0, 0, 2 * r], 1)], sem).start()
        pltpu.make_async_copy(src, xs_hbm.at[pl.ds(slots_ref[0, 0, 2 * r + 1], 1)], sem).start()
        return c

    lax.fori_loop(0, tm, issue, 0, unroll=ROW_DMA_UNROLL)
    for _ in range(2):
        pltpu.make_async_copy(x_ref, xs_hbm.at[pl.ds(0, tm)], sem).wait()


def _dispatch(n2, slots):
    t = n2.shape[0]
    tm = MOE_TM
    return pl.pallas_call(
        _dispatch_kernel,
        grid_spec=pltpu.PrefetchScalarGridSpec(
            num_scalar_prefetch=0,
            grid=(t // tm,),
            in_specs=[pl.BlockSpec((1, 1, 2 * tm), lambda i: (i, 0, 0), memory_space=pltpu.SMEM),
                      pl.BlockSpec((tm, D_MODEL), lambda i: (i, 0))],
            out_specs=pl.BlockSpec(memory_space=pl.ANY),
            scratch_shapes=[pltpu.SemaphoreType.DMA(())],
        ),
        out_shape=jax.ShapeDtypeStruct((2 * t, D_MODEL), F32),
        compiler_params=_cparams("arbitrary"),
        name="moe_dispatch",
    )(slots, n2)


def _experts_kernel(wt_ref, we_ref, wlo_ref, whi_ref, wfirst_ref, nw_ref,
                    xs_ref, w1_ref, w3_ref, w2_ref, ys_ref, wb1, wb3, wb2):
    w = pl.program_id(0)

    @pl.when(w < nw_ref[0])
    def _():
        new_expert = jnp.logical_or(w == 0, we_ref[w] != we_ref[jnp.maximum(w - 1, 0)])

        @pl.when(new_expert)
        def _():
            wb1[...] = w1_ref[0].astype(BF16)
            wb3[...] = w3_ref[0].astype(BF16)
            wb2[...] = w2_ref[0].astype(BF16)

        x = xs_ref[...].astype(BF16)
        a = _dot(x, wb1[...])
        b = _dot(x, wb3[...])
        o = _dot(((a * _sigmoid(a)) * b).astype(BF16), wb2[...])

        @pl.when(wfirst_ref[w] == 1)
        def _():
            ys_ref[...] = o

        @pl.when(wfirst_ref[w] == 0)
        def _():
            rows = lax.broadcasted_iota(jnp.int32, o.shape, 0)
            ys_ref[...] = jnp.where((rows >= wlo_ref[w]) & (rows < whi_ref[w]), o, ys_ref[...])


def _experts(xs, work, w1, w3, w2):
    s = xs.shape[0]
    ts = MOE_TS
    n_work = work[0].shape[0]
    return pl.pallas_call(
        _experts_kernel,
        grid_spec=pltpu.PrefetchScalarGridSpec(
            num_scalar_prefetch=6,
            grid=(n_work,),
            in_specs=[
                pl.BlockSpec((ts, D_MODEL), lambda w, wt, we, *_: (wt[w], 0)),
                pl.BlockSpec((1, D_MODEL, D_EXPERT), lambda w, wt, we, *_: (we[w], 0, 0)),
                pl.BlockSpec((1, D_MODEL, D_EXPERT), lambda w, wt, we, *_: (we[w], 0, 0)),
                pl.BlockSpec((1, D_EXPERT, D_MODEL), lambda w, wt, we, *_: (we[w], 0, 0)),
            ],
            out_specs=pl.BlockSpec((ts, D_MODEL), lambda w, wt, we, *_: (wt[w], 0)),
            scratch_shapes=[pltpu.VMEM((D_MODEL, D_EXPERT), BF16), pltpu.VMEM((D_MODEL, D_EXPERT), BF16),
                            pltpu.VMEM((D_EXPERT, D_MODEL), BF16)],
        ),
        out_shape=jax.ShapeDtypeStruct((s, D_MODEL), F32),
        compiler_params=_cparams("arbitrary"),
        name="moe_experts",
    )(*work, xs, w1, w3, w2)


def _expert_work_items(counts, total):
    ts = MOE_TS
    n_tiles = total // ts
    n_work = n_tiles + N_EXPERTS - 1
    offs = jnp.cumsum(counts) - counts
    t0 = (jnp.arange(n_tiles, dtype=jnp.int32) * ts)[:, None]
    lo = jnp.maximum(t0, offs[None, :])
    hi = jnp.minimum(t0 + ts, (offs + counts)[None, :])
    nonempty = (hi > lo).reshape(-1)
    nw = jnp.sum(nonempty.astype(jnp.int32))
    idx = jnp.nonzero(nonempty, size=n_work, fill_value=0)[0].astype(jnp.int32)
    idx = jnp.where(jnp.arange(n_work) < nw, idx, idx[jnp.maximum(nw - 1, 0)])
    wt = idx // N_EXPERTS
    we = idx % N_EXPERTS
    wlo = lo.reshape(-1)[idx] - wt * ts
    whi = hi.reshape(-1)[idx] - wt * ts
    wfirst = jnp.concatenate([jnp.ones((1,), jnp.int32), (wt[1:] != wt[:-1]).astype(jnp.int32)])
    return wt, we, wlo, whi, wfirst, nw.reshape(1)


def _combine_kernel(slots_ref, slots_next_ref, h_ref, route_ref, gfin_ref, ys_hbm, yp_ref, ysm_ref, gbuf, sems,
                    *, n_prompt_tiles):
    i = pl.program_id(0)
    n = pl.num_programs(0)
    tm = h_ref.shape[0]

    def gather(s_ref, buf):
        def issue(r, c):
            pltpu.make_async_copy(ys_hbm.at[pl.ds(s_ref[0, 0, 2 * r], 1)], gbuf.at[buf, 0, pl.ds(r, 1)],
                                  sems.at[buf]).start()
            pltpu.make_async_copy(ys_hbm.at[pl.ds(s_ref[0, 0, 2 * r + 1], 1)], gbuf.at[buf, 1, pl.ds(r, 1)],
                                  sems.at[buf]).start()
            return c
        lax.fori_loop(0, tm, issue, 0, unroll=ROW_DMA_UNROLL)

    cur = i % 2

    @pl.when(i == 0)
    def _():
        gather(slots_ref, 0)

    @pl.when(i + 1 < n)
    def _():
        gather(slots_next_ref, 1 - cur)

    for k in range(2):
        pltpu.make_async_copy(ys_hbm.at[pl.ds(0, tm)], gbuf.at[cur, k], sems.at[cur]).wait()
    route = route_ref[...]
    lane = lax.broadcasted_iota(jnp.int32, route.shape, 1)
    w1 = jnp.sum(jnp.where(lane == ROUTE_W1, route, 0.0), axis=1, keepdims=True)
    w2 = jnp.sum(jnp.where(lane == ROUTE_W2, route, 0.0), axis=1, keepdims=True)
    hf = h_ref[...] + (w1 * gbuf[cur, 0] + w2 * gbuf[cur, 1])
    y = hf * lax.rsqrt(jnp.mean(hf * hf, axis=-1, keepdims=True) + NORM_EPS) * gfin_ref[...]

    @pl.when(i < n_prompt_tiles)
    def _():
        yp_ref[...] = y

    @pl.when(i >= n_prompt_tiles)
    def _():
        ysm_ref[...] = y


def _combine(h, route, slots, ys, gfin, n_prompt_tiles):
    t = h.shape[0]
    tm = MOE_TM
    n_p = n_prompt_tiles
    return pl.pallas_call(
        functools.partial(_combine_kernel, n_prompt_tiles=n_p),
        grid_spec=pltpu.PrefetchScalarGridSpec(
            num_scalar_prefetch=0,
            grid=(t // tm,),
            in_specs=[pl.BlockSpec((1, 1, 2 * tm), lambda i: (i, 0, 0), memory_space=pltpu.SMEM),
                      pl.BlockSpec((1, 1, 2 * tm), lambda i: (jnp.minimum(i + 1, t // tm - 1), 0, 0),
                                   memory_space=pltpu.SMEM),
                      pl.BlockSpec((tm, D_MODEL), lambda i: (i, 0)),
                      pl.BlockSpec((tm, ROUTER_LANES), lambda i: (i, 0)),
                      pl.BlockSpec((1, D_MODEL), lambda i: (0, 0)),
                      pl.BlockSpec(memory_space=pl.ANY)],
            out_specs=[pl.BlockSpec((tm, D_MODEL), lambda i: (jnp.minimum(i, n_p - 1), 0)),
                       pl.BlockSpec((tm, D_MODEL), lambda i: (0, 0))],
            scratch_shapes=[pltpu.VMEM((2, 2, tm, D_MODEL), F32), pltpu.SemaphoreType.DMA((2,))],
        ),
        out_shape=[jax.ShapeDtypeStruct((n_p * tm, D_MODEL), F32), jax.ShapeDtypeStruct((tm, D_MODEL), F32)],
        compiler_params=_cparams("arbitrary"),
        name="moe_combine",
    )(slots, slots, h, route, gfin, ys)


def _pair_state(s):
    b = s.shape[0]
    return s.reshape(b, 2, RWKV_PAIRS, RWKV_HEAD, RWKV_HEAD).transpose(0, 2, 3, 1, 4).reshape(
        b, RWKV_PAIRS, RWKV_HEAD, LANE)


def _unpair_state(s):
    b = s.shape[0]
    return s.reshape(b, RWKV_PAIRS, RWKV_HEAD, 2, RWKV_HEAD).transpose(0, 3, 1, 2, 4).reshape(
        b, RWKV_HEADS, RWKV_HEAD, RWKV_HEAD)


def _v_tiles(v, tblk):
    b, t, _ = v.shape
    x = v.reshape(b, t // tblk, tblk, 2, RWKV_PAIRS, RWKV_HEAD).transpose(0, 1, 4, 5, 3, 2)
    x = jnp.pad(x, ((0, 0),) * 5 + ((0, RWKV_HEAD - tblk),))
    return x.reshape(b, t // tblk, RWKV_PAIRS, RWKV_HEAD, LANE).astype(BF16)


def _y_rows(y, bb, t):
    nb, tpad, _ = y.shape
    return y.reshape(nb, tpad, bb, RWKV_WIDTH).transpose(0, 2, 1, 3).reshape(nb * bb, tpad, RWKV_WIDTH)[:, :t]


def kernel(x_prompt, x_sample, state_gla, state_rwkv, state_shift, meta_tokens, norm_mix, w_in, gla_gate_w2,
           gla_gate_b, gla_norm, rwkv_mu, rwkv_w0, rwkv_w2, rwkv_a0, rwkv_a2, rwkv_g2, rwkv_kk, rwkv_ka, rwkv_rk,
           rwkv_ln_w, rwkv_ln_b, w_out, norm_ffn, router_group_w, router_group_b, router_expert_w,
           router_expert_b, moe_w1, moe_w3, moe_w2, norm_final):
    bp, tp, _ = x_prompt.shape
    bs, ts, _ = x_sample.shape
    assert state_gla.shape[0] == 1, "one layer"
    lyr = 0

    w_in_l = w_in[lyr]
    wg = jnp.pad(w_in_l[:, :GLA_COLS], ((0, 0), (0, GLA_PCOLS - GLA_COLS))).astype(BF16)
    wr = w_in_l[:, GLA_COLS:].astype(BF16)
    g_mix = norm_mix[lyr][None, :]
    gw2p = jnp.pad(gla_gate_w2[lyr], ((0, LANE - GLA_GATE_RANK), (0, 0)))
    gb = gla_gate_b[lyr][None, :]
    gn = gla_norm[lyr][None, :]
    w2p = jnp.pad(rwkv_w2[lyr], ((0, 64), (0, 0))).astype(BF16)
    a2p = jnp.pad(rwkv_a2[lyr], ((64, 0), (0, 0))).astype(BF16)
    pre_params = (rwkv_mu[lyr][None, :], rwkv_w0[lyr][None, :], w2p, rwkv_a0[lyr][None, :], a2p,
                  rwkv_g2[lyr].astype(BF16), rwkv_kk[lyr][None, :], rwkv_ka[lyr][None, :],
                  rwkv_rk[lyr].reshape(1, RWKV_WIDTH), _block_ones(RWKV_WIDTH, RWKV_HEAD))
    lnw = rwkv_ln_w[lyr][None, :]
    lnb = rwkv_ln_b[lyr][None, :]
    wo = w_out[lyr].astype(BF16)
    gffn = norm_ffn[lyr][None, :]
    n_used = N_GROUPS + N_EXPERTS
    w_router = jnp.pad(
        jnp.concatenate([router_group_w[lyr],
                         router_expert_w[lyr].transpose(1, 0, 2).reshape(D_MODEL, N_EXPERTS)], axis=1),
        ((0, 0), (0, ROUTER_LANES - n_used)))
    b_router = jnp.pad(jnp.concatenate([router_group_b[lyr], router_expert_b[lyr].reshape(N_EXPERTS)]),
                       (0, ROUTER_LANES - n_used))[None, :]
    gfin = norm_final[None, :]

    pg_m, pr_m = _inproj(meta_tokens, g_mix, wg, wr, N_META)
    _, sg_m = _gla(pg_m[None], jnp.zeros((1, GLA_HEADS, GLA_DK, GLA_DV), F32), gw2p, gb, gn,
                   bb=1, chunk=N_META, sub=N_META, t_valid=N_META)
    r, w, k, kk, kka, v_m, _, _ = _rwkv_pre(pr_m[None], jnp.zeros((1, 1, RWKV_COLS), F32), pre_params,
                                            tm=N_META, explicit_prev=False, emit_vt=False)
    _, sr_m = _rwkv_rec(r, w, k, kk, kka, _v_tiles(v_m, N_META), jnp.zeros((1, RWKV_PAIRS, RWKV_HEAD, LANE), F32),
                        bb=1, n_steps=N_META)

    xp = x_prompt.reshape(bp * tp, D_MODEL)
    pg_p, pr_p = _inproj(xp, g_mix, wg, wr, 512)
    og_p, sg_p = _gla(pg_p.reshape(bp, tp, GLA_PCOLS), jnp.broadcast_to(sg_m, (bp,) + sg_m.shape[1:]), gw2p, gb, gn,
                      bb=1, chunk=GLA_CHUNK, sub=GLA_SUB, t_valid=GLA_CHUNK)
    pr_p3 = pr_p.reshape(bp, tp, RWKV_COLS)
    first_prev = jnp.broadcast_to(pr_m[N_META - 1][None, None, :], (bp, 1, RWKV_COLS))
    r, w, k, kk, kka, vt_p, bv_p, gate_p = _rwkv_pre(pr_p3, first_prev, pre_params, tm=256, explicit_prev=False,
                                                     emit_vt=True)
    y_p, sr_p = _rwkv_rec(r, w, k, kk, kka, vt_p, jnp.broadcast_to(sr_m, (bp,) + sr_m.shape[1:]),
                          bb=bp, n_steps=REC_TB)
    prompt_rows = (xp, og_p.reshape(bp * tp, GLA_WIDTH), y_p.reshape(tp, bp * RWKV_WIDTH),
                   bv_p.reshape(bp * tp, RWKV_WIDTH), gate_p.reshape(bp * tp, RWKV_WIDTH))

    xs = x_sample.reshape(bs * ts, D_MODEL)
    pg_s, pr_s = _inproj(xs, g_mix, wg, wr, bs * ts)
    ts_pad = 8
    pg_s3 = jnp.pad(pg_s.reshape(bs, ts, GLA_PCOLS), ((0, 0), (0, ts_pad - ts), (0, 0)))
    og_s, sg_s = _gla(pg_s3, state_gla[lyr], gw2p, gb, gn, bb=8, chunk=ts_pad, sub=ts_pad, t_valid=ts)
    og_s = og_s[:, :ts]
    pr_s3 = pr_s.reshape(bs, ts, RWKV_COLS)
    prev_s = jnp.concatenate([state_shift[lyr][:, None, :], pr_s3[:, :-1]], axis=1)
    r, w, k, kk, kka, v_s, bv_s, gate_s = _rwkv_pre(pr_s3.reshape(1, bs * ts, RWKV_COLS),
                                                     prev_s.reshape(1, bs * ts, RWKV_COLS), pre_params,
                                                     tm=bs * ts, explicit_prev=True, emit_vt=False)
    unflat = lambda a: a.reshape(RWKV_PAIRS, bs, ts, LANE).transpose(1, 0, 2, 3)
    y_s, sr_s = _rwkv_rec(unflat(r), unflat(w), unflat(k), unflat(kk), unflat(kka),
                          _v_tiles(v_s.reshape(bs, ts, RWKV_WIDTH), ts), _pair_state(state_rwkv[lyr]),
                          bb=8, n_steps=ts)
    y_s = _y_rows(y_s, 8, ts)
    sample_rows = (xs, og_s.reshape(bs * ts, GLA_WIDTH), y_s.reshape(bs * ts, RWKV_WIDTH),
                   bv_s.reshape(bs * ts, RWKV_WIDTH), gate_s.reshape(bs * ts, RWKV_WIDTH))

    h_all, n2_all, route, counts = _mix_router(prompt_rows, sample_rows, lnw, lnb, wo, gffn, w_router, b_router,
                                               seq_tiles=tp // MOE_TM)
    n_tok = h_all.shape[0]
    n_p_tiles = (bp * tp) // MOE_TM
    cnt = counts[0, EXPERT_LANE0:EXPERT_LANE0 + N_EXPERTS].astype(jnp.int32)
    offs = jnp.cumsum(cnt) - cnt
    eid = route[:, ROUTE_E1:ROUTE_E2 + 1].astype(jnp.int32)
    pos = route[:, ROUTE_P1:ROUTE_P2 + 1].astype(jnp.int32)
    slots = (offs[eid] + pos).reshape(n_tok // MOE_TM, 1, 2 * MOE_TM)
    xs_sorted = _dispatch(n2_all, slots)
    ys_sorted = _experts(xs_sorted, _expert_work_items(cnt, 2 * n_tok), moe_w1[lyr], moe_w3[lyr], moe_w2[lyr])
    y_prompt, y_sample = _combine(h_all, route, slots, ys_sorted, gfin, n_p_tiles)
    y_prompt = y_prompt.reshape(bp, tp, D_MODEL)
    y_sample = y_sample.reshape(bs, ts, D_MODEL)

    return (y_prompt, y_sample,
            sg_p[None], _unpair_state(sr_p)[None], pr_p3[:, -1][None],
            sg_s[None], _unpair_state(sr_s)[None], pr_s3[:, -1][None])
```

```python
import functools

import jax
import jax.numpy as jnp
from jax import lax
from jax.experimental import pallas as pl
from jax.experimental.pallas import tpu as pltpu

F32 = jnp.float32
BF16 = jnp.bfloat16
HIGHEST = lax.Precision.HIGHEST

D_MODEL = 1024
N_META = 16
NORM_EPS = 1e-6
GLA_HEADS = 4
GLA_DK = 64
GLA_DV = 128
GLA_QK = GLA_HEADS * GLA_DK
GLA_WIDTH = GLA_HEADS * GLA_DV
GLA_GATE_RANK = 16
GLA_GATE_NORM = 16.0
GLA_CHUNK = 64
GLA_SUB = 16
GLA_COLS = 2 * GLA_QK + 2 * GLA_WIDTH + GLA_GATE_RANK
GLA_PCOLS = 2 * GLA_QK + 2 * GLA_WIDTH + 128
RWKV_WIDTH = 512
RWKV_HEAD = 64
RWKV_HEADS = 8
RWKV_PAIRS = RWKV_HEADS // 2
RWKV_DECAY_SCALE = 0.606531
RWKV_GN_EPS = 64e-5
RWKV_COLS = 3 * RWKV_WIDTH + 64 + 64 + 128
REC_TB = 64
REC_UNROLL = 8
N_GROUPS = 4
EXPERTS_PER_GROUP = 8
N_EXPERTS = 32
D_EXPERT = 512
ROUTER_LANES = 128
EXPERT_LANE0 = N_GROUPS
ROUTE_E1, ROUTE_E2, ROUTE_W1, ROUTE_W2, ROUTE_P1, ROUTE_P2 = range(6)
MOE_TM = 512
MOE_TS = 512

LANE = 128
VMEM_LIMIT = 56 * 1024 * 1024


def _cparams(*sem):
    return pltpu.CompilerParams(dimension_semantics=sem, vmem_limit_bytes=VMEM_LIMIT)


def _block_ones(n, blk):
    i = jnp.arange(n)
    return (i[:, None] // blk == i[None, :] // blk).astype(BF16)


def _sigmoid(x):
    return 1.0 / (1.0 + jnp.exp(-x))


def _dot(a, b):
    return jnp.dot(a, b, preferred_element_type=F32)


def _dot_nt(a, b):
    return lax.dot_general(a, b, (((1,), (1,)), ((), ())), preferred_element_type=F32)


def _dot_tn(a, b):
    return lax.dot_general(a, b, (((0,), (0,)), ((), ())), preferred_element_type=F32)


def _split2(x):
    hi = x.astype(BF16)
    lo = (x - hi.astype(F32)).astype(BF16)
    return hi, lo


def _group_sum(x, bo):
    hi, lo = _split2(x)
    return _dot(hi, bo) + _dot(lo, bo)


def _inproj_kernel(x_ref, g_ref, wg_ref, wr_ref, pg_ref, pr_ref):
    x = x_ref[...]
    n = x * lax.rsqrt(jnp.mean(x * x, axis=-1, keepdims=True) + NORM_EPS) * g_ref[...]
    nb = n.astype(BF16)
    pg_ref[...] = _dot(nb, wg_ref[...])
    pr_ref[...] = _dot(nb, wr_ref[...])


def _inproj(x, g, wg, wr, tm):
    t = x.shape[0]
    return pl.pallas_call(
        _inproj_kernel,
        grid=(t // tm,),
        in_specs=[
            pl.BlockSpec((tm, D_MODEL), lambda i: (i, 0)),
            pl.BlockSpec((1, D_MODEL), lambda i: (0, 0)),
            pl.BlockSpec((D_MODEL, GLA_PCOLS), lambda i: (0, 0)),
            pl.BlockSpec((D_MODEL, RWKV_COLS), lambda i: (0, 0)),
        ],
        out_specs=[
            pl.BlockSpec((tm, GLA_PCOLS), lambda i: (i, 0)),
            pl.BlockSpec((tm, RWKV_COLS), lambda i: (i, 0)),
        ],
        out_shape=[jax.ShapeDtypeStruct((t, GLA_PCOLS), F32), jax.ShapeDtypeStruct((t, RWKV_COLS), F32)],
        compiler_params=_cparams("parallel"),
        name="inproj",
    )(x, g, wg, wr)


def _gla_kernel(pg_ref, s0_ref, gw2_ref, gb_ref, gn_ref, bo_ref, tril_ref, o_ref, sout_ref, s_scr,
                *, bb, chunk, sub, t_valid):
    ci = pl.program_id(1)

    @pl.when(ci == 0)
    def _():
        s_scr[...] = s0_ref[...]

    bo = bo_ref[...]
    tril = tril_ref[...]
    lane = lax.broadcasted_iota(jnp.int32, (sub, LANE), 1) & (GLA_DK - 1)
    rowi = lax.broadcasted_iota(jnp.int32, (sub, LANE), 0)
    head0_s = lax.broadcasted_iota(jnp.int32, (sub, LANE), 1) < GLA_DK
    head0_c = lax.broadcasted_iota(jnp.int32, (chunk, LANE), 1) < GLA_DK

    for bi in range(bb):
        pg = pg_ref[bi]
        q = pg[:, 0:GLA_QK] * (GLA_DK ** -0.5)
        k = pg[:, GLA_QK:2 * GLA_QK]
        v = pg[:, 2 * GLA_QK:2 * GLA_QK + GLA_WIDTH]
        g = pg[:, 2 * GLA_QK + GLA_WIDTH:2 * GLA_QK + 2 * GLA_WIDTH]
        gl = pg[:, 2 * GLA_QK + 2 * GLA_WIDTH:]
        z = jnp.dot(gl, gw2_ref[...], precision=HIGHEST, preferred_element_type=F32) + gb_ref[...]
        lg = (jnp.minimum(z, 0.0) - jnp.log1p(jnp.exp(-jnp.abs(z)))) * (1.0 / GLA_GATE_NORM)
        if t_valid < chunk:
            rows = lax.broadcasted_iota(jnp.int32, lg.shape, 0)
            lg = jnp.where(rows < t_valid, lg, 0.0)
        b = jnp.dot(tril, lg, precision=HIGHEST, preferred_element_type=F32)
        eb = jnp.exp(b)
        blast = b[chunk - 1:chunk, :]
        kl = k * jnp.exp(blast - b)
        qe = q * eb

        n_blk = chunk // sub
        n_pairs = GLA_HEADS // 2
        ps = []
        for hp in range(n_pairs):
            sl = slice(hp * LANE, (hp + 1) * LANE)
            for blk in range(n_blk):
                rs = slice(blk * sub, (blk + 1) * sub)
                qb, kb, bbk = q[rs, sl], k[rs, sl], b[rs, sl]
                for j in range(sub):
                    ps.append(qb * (kb[j:j + 1] * jnp.exp(jnp.minimum(bbk - bbk[j:j + 1], 0.0))))
        red = _dot(jnp.concatenate(ps, axis=0).astype(BF16), bo)

        o_heads = []
        for hp in range(n_pairs):
            sl = slice(hp * LANE, (hp + 1) * LANE)
            kp, bp = k[:, sl], b[:, sl]
            row_blocks = []
            for blk in range(n_blk):
                rs = slice(blk * sub, (blk + 1) * sub)
                base = (hp * n_blk + blk) * sub * sub
                a = jnp.zeros((sub, LANE), F32)
                for j in range(sub):
                    a = jnp.where((lane == blk * sub + j) & (rowi >= j), red[base + j * sub:base + (j + 1) * sub], a)
                if blk > 0:
                    bref = bp[blk * sub - 1:blk * sub]
                    qt = q[rs, sl] * jnp.exp(bp[rs] - bref)
                    kt = (kp * jnp.exp(jnp.minimum(bref - bp, 0.0))).astype(BF16)
                    qt2 = jnp.concatenate([jnp.where(head0_s, qt, 0.0), jnp.where(head0_s, 0.0, qt)], axis=0)
                    off2 = _dot_nt(qt2.astype(BF16), kt)
                    a = jnp.where(lane < blk * sub, jnp.concatenate([off2[:sub], off2[sub:]], axis=1), a)
                row_blocks.append(a)
            a_pair = row_blocks[0] if n_blk == 1 else jnp.concatenate(row_blocks, axis=0)
            v0 = v[:, 2 * hp * GLA_DV:(2 * hp + 1) * GLA_DV]
            v1 = v[:, (2 * hp + 1) * GLA_DV:(2 * hp + 2) * GLA_DV]
            s_pair = s_scr[bi, 2 * hp:2 * hp + 2].reshape(2 * GLA_DK, GLA_DV)
            qe_p, kl_p = qe[:, sl], kl[:, sl]

            def by_head(x):
                return jnp.concatenate([jnp.where(head0_c, x, 0.0), jnp.where(head0_c, 0.0, x)], axis=0)

            if chunk == GLA_DK:
                v_rows = jnp.concatenate([v0, v1], axis=0)
            else:
                zpad = jnp.zeros((GLA_DK - chunk, GLA_DV), F32)
                v_rows = jnp.concatenate([v0, zpad, v1, zpad], axis=0)
            lhs = jnp.concatenate([by_head(a_pair), by_head(qe_p)], axis=1).astype(BF16)
            rhs = jnp.concatenate([v_rows, s_pair], axis=0).astype(BF16)
            o2 = _dot(lhs, rhs)
            upd = _dot_tn(by_head(kl_p).astype(BF16), jnp.concatenate([v0, v1], axis=0).astype(BF16))
            dcol = jnp.broadcast_to(jnp.exp(blast[:, sl]), (8, LANE)).T[:, 0:1]
            s_new = dcol * s_pair + upd
            s_scr[bi, 2 * hp] = s_new[:GLA_DK]
            s_scr[bi, 2 * hp + 1] = s_new[GLA_DK:]
            for h2 in range(2):
                o_h = o2[h2 * chunk:(h2 + 1) * chunk]
                o_heads.append(o_h * lax.rsqrt(jnp.mean(o_h * o_h, axis=-1, keepdims=True) + NORM_EPS) * gn_ref[...])
        o = jnp.concatenate(o_heads, axis=1)
        o_ref[bi] = o * (g * _sigmoid(g))

    @pl.when(ci == pl.num_programs(1) - 1)
    def _():
        sout_ref[...] = s_scr[...]


def _gla(pg, s0, gw2p, gb, gn, *, bb, chunk, sub, t_valid):
    b, t, _ = pg.shape
    tril = jnp.tril(jnp.ones((chunk, chunk), F32))
    kern = functools.partial(_gla_kernel, bb=bb, chunk=chunk, sub=sub, t_valid=t_valid)
    return pl.pallas_call(
        kern,
        grid=(b // bb, t // chunk),
        in_specs=[
            pl.BlockSpec((bb, chunk, GLA_PCOLS), lambda i, j: (i, j, 0)),
            pl.BlockSpec((bb, GLA_HEADS, GLA_DK, GLA_DV), lambda i, j: (i, 0, 0, 0)),
            pl.BlockSpec((LANE, GLA_QK), lambda i, j: (0, 0)),
            pl.BlockSpec((1, GLA_QK), lambda i, j: (0, 0)),
            pl.BlockSpec((1, GLA_DV), lambda i, j: (0, 0)),
            pl.BlockSpec((LANE, LANE), lambda i, j: (0, 0)),
            pl.BlockSpec((chunk, chunk), lambda i, j: (0, 0)),
        ],
        out_specs=[
            pl.BlockSpec((bb, chunk, GLA_WIDTH), lambda i, j: (i, j, 0)),
            pl.BlockSpec((bb, GLA_HEADS, GLA_DK, GLA_DV), lambda i, j: (i, 0, 0, 0)),
        ],
        out_shape=[jax.ShapeDtypeStruct((b, t, GLA_WIDTH), F32),
                   jax.ShapeDtypeStruct((b, GLA_HEADS, GLA_DK, GLA_DV), F32)],
        scratch_shapes=[pltpu.VMEM((bb, GLA_HEADS, GLA_DK, GLA_DV), F32)],
        compiler_params=_cparams("parallel", "arbitrary"),
        name="gla_chunk",
    )(pg, s0, gw2p, gb, gn, _block_ones(LANE, GLA_DK), tril)


def _rwkv_pre_kernel(pr_ref, aux_ref, mu_ref, w0_ref, w2_ref, a0_ref, a2_ref, g2_ref, kk_ref, ka_ref, rk_ref, bo_ref,
                     r_out, w_out, k_out, kkn_out, kka_out, v_out, bv_out, gate_out, carry_scr,
                     *, tm, explicit_prev, emit_vt):
    pr = pr_ref[0]
    if explicit_prev:
        prev = aux_ref[0]
    else:
        j = pl.program_id(1)
        row0 = jnp.where(j == 0, aux_ref[0], carry_scr[...])
        rows = lax.broadcasted_iota(jnp.int32, pr.shape, 0)
        prev = jnp.where(rows == 0, row0, pltpu.roll(pr, 1, 0))
        carry_scr[...] = pr[tm - 1:tm, :]
    xm = pr + (prev - pr) * mu_ref[...]
    wd = RWKV_WIDTH
    rr, rk, rv = xm[:, 0:wd], xm[:, wd:2 * wd], xm[:, 2 * wd:3 * wd]
    wa = xm[:, 3 * wd:3 * wd + LANE]
    gl2 = xm[:, 3 * wd + LANE:3 * wd + 2 * LANE]
    logw = -RWKV_DECAY_SCALE * _sigmoid(w0_ref[...] + _dot(jnp.tanh(wa).astype(BF16), w2_ref[...]))
    aa = _sigmoid(a0_ref[...] + _dot(wa.astype(BF16), a2_ref[...]))
    gate = _dot(_sigmoid(gl2).astype(BF16), g2_ref[...])
    bo = bo_ref[...]
    kk = rk * kk_ref[...]
    kk = kk / jnp.maximum(jnp.sqrt(_group_sum(kk * kk, bo)), 1e-12)
    k = rk * (1.0 + (aa - 1.0) * ka_ref[...])
    bv = _group_sum(rr * k * rk_ref[...], bo) * rv
    w = jnp.exp(logw)
    kka = kk * aa
    hd = RWKV_HEAD

    def pair(x, hp):
        return jnp.concatenate([x[:, hp * hd:(hp + 1) * hd], x[:, (hp + RWKV_PAIRS) * hd:(hp + RWKV_PAIRS + 1) * hd]],
                               axis=1)

    for hp in range(RWKV_PAIRS):
        r_out[0, hp] = pair(rr, hp)
        w_out[0, hp] = pair(w, hp)
        k_out[0, hp] = pair(k, hp)
        kkn_out[0, hp] = pair(kk, hp)
        kka_out[0, hp] = pair(kka, hp)
    if emit_vt:
        vt = rv.T
        for tb in range(tm // REC_TB):
            ts = slice(tb * REC_TB, (tb + 1) * REC_TB)
            for hp in range(RWKV_PAIRS):
                lo, hi = hp * hd, (hp + RWKV_PAIRS) * hd
                v_out[0, tb, hp] = jnp.concatenate([vt[lo:lo + hd, ts], vt[hi:hi + hd, ts]], axis=1).astype(BF16)
    else:
        v_out[0] = rv
    bv_out[0] = bv
    gate_out[0] = gate


def _rwkv_pre(pr, aux, params, *, tm, explicit_prev, emit_vt):
    b, t, _ = pr.shape
    kern = functools.partial(_rwkv_pre_kernel, tm=tm, explicit_prev=explicit_prev, emit_vt=emit_vt)
    aux_spec = (pl.BlockSpec((1, tm, RWKV_COLS), lambda i, j: (i, j, 0)) if explicit_prev
                else pl.BlockSpec((1, 1, RWKV_COLS), lambda i, j: (i, 0, 0)))
    const = lambda shape: pl.BlockSpec(shape, lambda i, j: (0,) * len(shape))
    pair_spec = pl.BlockSpec((1, RWKV_PAIRS, tm, LANE), lambda i, j: (i, 0, j, 0))
    row_spec = pl.BlockSpec((1, tm, RWKV_WIDTH), lambda i, j: (i, j, 0))
    pair_shape = jax.ShapeDtypeStruct((b, RWKV_PAIRS, t, LANE), F32)
    row_shape = jax.ShapeDtypeStruct((b, t, RWKV_WIDTH), F32)
    if emit_vt:
        v_spec = pl.BlockSpec((1, tm // REC_TB, RWKV_PAIRS, RWKV_HEAD, LANE), lambda i, j: (i, j, 0, 0, 0))
        v_shape = jax.ShapeDtypeStruct((b, t // REC_TB, RWKV_PAIRS, RWKV_HEAD, LANE), BF16)
    else:
        v_spec, v_shape = row_spec, row_shape
    return pl.pallas_call(
        kern,
        grid=(b, t // tm),
        in_specs=[
            pl.BlockSpec((1, tm, RWKV_COLS), lambda i, j: (i, j, 0)),
            aux_spec,
            const((1, RWKV_COLS)), const((1, RWKV_WIDTH)), const((LANE, RWKV_WIDTH)), const((1, RWKV_WIDTH)),
            const((LANE, RWKV_WIDTH)), const((LANE, RWKV_WIDTH)), const((1, RWKV_WIDTH)), const((1, RWKV_WIDTH)),
            const((1, RWKV_WIDTH)), const((RWKV_WIDTH, RWKV_WIDTH)),
        ],
        out_specs=[pair_spec] * 5 + [v_spec, row_spec, row_spec],
        out_shape=[pair_shape] * 5 + [v_shape, row_shape, row_shape],
        scratch_shapes=[pltpu.VMEM((1, RWKV_COLS), F32)],
        compiler_params=_cparams("parallel", "arbitrary"),
        name="rwkv_pre",
    )(pr, aux, *params)


def _rwkv_rec_kernel(r_ref, w_ref, k_ref, kk_ref, kka_ref, vt_ref, s0_ref, bo_ref, vsel_ref, ysel_ref,
                     y_ref, sout_ref, s_scr, t1_scr, t3_scr, yt_scr, *, bb, n_steps):
    tb = pl.program_id(1)
    nc = bb * RWKV_PAIRS
    hd = RWKV_HEAD

    @pl.when(tb == 0)
    def _():
        for c in range(nc):
            s_scr[c] = s0_ref[c // RWKV_PAIRS, c % RWKV_PAIRS]

    bo = bo_ref[...]

    def step(t, u):
        row = pl.ds(t, 1)
        for c in range(nc):
            bi, hp = divmod(c, RWKV_PAIRS)
            t1_scr[c * hd:(c + 1) * hd, :] = (s_scr[c] * kk_ref[bi, hp, row, :]).astype(BF16)
        sab = _dot(t1_scr[...], bo)
        vb = _dot(vt_ref[...].reshape(nc * hd, LANE), vsel_ref[t])
        for c in range(nc):
            bi, hp = divmod(c, RWKV_PAIRS)
            rs = slice(c * hd, (c + 1) * hd)
            s2 = (s_scr[c] * w_ref[bi, hp, row, :] - sab[rs] * kka_ref[bi, hp, row, :]
                  + vb[rs] * k_ref[bi, hp, row, :])
            s_scr[c] = s2
            t3_scr[rs, :] = (s2 * r_ref[bi, hp, row, :]).astype(BF16)
        yt_scr[...] += _dot_nt(ysel_ref[u], t3_scr[...])

    n_inner = min(8, n_steps)

    def block8(t8, carry):
        yt_scr[...] = jnp.zeros(yt_scr.shape, F32)

        def inner(u, c2):
            step(t8 * 8 + u, u)
            return c2

        lax.fori_loop(0, n_inner, inner, 0, unroll=REC_UNROLL)
        t0 = pl.multiple_of(t8 * 8, 8)
        blk = RWKV_PAIRS * hd
        for bi in range(bb):
            for h2 in range(2):
                y_ref[0, pl.ds(t0, 8), bi * RWKV_WIDTH + h2 * blk:bi * RWKV_WIDTH + (h2 + 1) * blk] = (
                    yt_scr[h2 * 8:(h2 + 1) * 8, bi * blk:(bi + 1) * blk])
        return carry

    lax.fori_loop(0, (n_steps + 7) // 8, block8, 0)

    @pl.when(tb == pl.num_programs(1) - 1)
    def _():
        for c in range(nc):
            sout_ref[c // RWKV_PAIRS, c % RWKV_PAIRS] = s_scr[c]


def _rwkv_rec(r, w, k, kk, kka, vt, s0, *, bb, n_steps):
    b, _, t, _ = r.shape
    tblk = min(REC_TB, t)
    ntb = t // tblk
    nc = bb * RWKV_PAIRS
    lane = jnp.arange(LANE)
    vsel = ((lane[None, :, None] // RWKV_HEAD == lane[None, None, :] // RWKV_HEAD)
            & (lane[None, :, None] % RWKV_HEAD == jnp.arange(RWKV_HEAD)[:, None, None])).astype(BF16)
    ysel = (jnp.arange(16)[None, :, None]
            == 8 * (lane[None, None, :] // RWKV_HEAD) + jnp.arange(8)[:, None, None]).astype(BF16)
    kern = functools.partial(_rwkv_rec_kernel, bb=bb, n_steps=n_steps)
    pair_spec = pl.BlockSpec((bb, RWKV_PAIRS, tblk, LANE), lambda i, j: (i, 0, j, 0))
    state_spec = pl.BlockSpec((bb, RWKV_PAIRS, RWKV_HEAD, LANE), lambda i, j: (i, 0, 0, 0))
    ytb = max(tblk, 8)
    return pl.pallas_call(
        kern,
        grid=(b // bb, ntb),
        in_specs=[pair_spec] * 5 + [
            pl.BlockSpec((bb, 1, RWKV_PAIRS, RWKV_HEAD, LANE), lambda i, j: (i, j, 0, 0, 0)),
            state_spec,
            pl.BlockSpec((LANE, LANE), lambda i, j: (0, 0)),
            pl.BlockSpec((RWKV_HEAD, LANE, LANE), lambda i, j: (0, 0, 0)),
            pl.BlockSpec((8, 16, LANE), lambda i, j: (0, 0, 0)),
        ],
        out_specs=[
            pl.BlockSpec((1, ytb, bb * RWKV_WIDTH), lambda i, j: (i, j, 0)),
            state_spec,
        ],
        out_shape=[jax.ShapeDtypeStruct((b // bb, ntb * ytb, bb * RWKV_WIDTH), F32),
                   jax.ShapeDtypeStruct((b, RWKV_PAIRS, RWKV_HEAD, LANE), F32)],
        scratch_shapes=[pltpu.VMEM((nc, RWKV_HEAD, LANE), F32),
                        pltpu.VMEM((nc * RWKV_HEAD, LANE), BF16),
                        pltpu.VMEM((nc * RWKV_HEAD, LANE), BF16),
                        pltpu.VMEM((16, nc * RWKV_HEAD), F32)],
        compiler_params=_cparams("parallel", "arbitrary"),
        name="rwkv_rec",
    )(r, w, k, kk, kka, vt, s0, _block_ones(LANE, RWKV_HEAD), vsel, ysel)


def _mix_router_body(x_ref, og_ref, y_ref, bv_ref, gate_ref, lnw_ref, lnb_ref, wo_ref, gffn_ref, wr_ref, br_ref,
                     bo_ref, tril_ref, h_ref, n2_ref, route_ref, cnt_scr):
    bo = bo_ref[...]
    y = y_ref[...]
    inv_n = 1.0 / RWKV_HEAD
    d = y - _group_sum(y, bo) * inv_n
    var = _group_sum(d * d, bo) * inv_n
    yn = d * lax.rsqrt(var + RWKV_GN_EPS) * lnw_ref[...] + lnb_ref[...] + bv_ref[...]
    o_rwkv = yn * gate_ref[...]
    mix = (_dot(og_ref[...].astype(BF16), wo_ref[0:GLA_WIDTH, :])
           + _dot(o_rwkv.astype(BF16), wo_ref[GLA_WIDTH:, :]))
    h = x_ref[...] + mix
    h_ref[...] = h
    n2 = h * lax.rsqrt(jnp.mean(h * h, axis=-1, keepdims=True) + NORM_EPS) * gffn_ref[...]
    n2_ref[...] = n2
    lg = jnp.dot(n2, wr_ref[...], precision=HIGHEST, preferred_element_type=F32) + br_ref[...]
    neg = jnp.float32(-3.0e38)
    big = jnp.float32(1.0e9)
    lane = lax.broadcasted_iota(jnp.int32, lg.shape, 1).astype(F32)
    gmask = lane < N_GROUPS
    gmax = jnp.max(jnp.where(gmask, lg, neg), axis=1, keepdims=True)
    p_top = 1.0 / jnp.sum(jnp.where(gmask, jnp.exp(jnp.minimum(lg - gmax, 0.0)), 0.0), axis=1, keepdims=True)
    gidx = jnp.min(jnp.where(gmask & (lg == gmax), lane, big), axis=1, keepdims=True)
    e_lo = EXPERT_LANE0 + gidx * EXPERTS_PER_GROUP
    emask = (lane >= e_lo) & (lane < e_lo + EXPERTS_PER_GROUP)
    m1 = jnp.max(jnp.where(emask, lg, neg), axis=1, keepdims=True)
    e1 = jnp.min(jnp.where(emask & (lg == m1), lane, big), axis=1, keepdims=True)
    emask2 = emask & (lane != e1)
    m2 = jnp.max(jnp.where(emask2, lg, neg), axis=1, keepdims=True)
    e2 = jnp.min(jnp.where(emask2 & (lg == m2), lane, big), axis=1, keepdims=True)
    r21 = jnp.exp(m2 - m1)
    w1 = p_top / (1.0 + r21)
    w2 = p_top * r21 / (1.0 + r21)
    o1 = lane == e1
    o2 = lane == e2
    onehot = jnp.where(o1 | o2, 1.0, 0.0)
    rank = _dot(tril_ref[...], onehot.astype(BF16)) + cnt_scr[...]
    pos1 = jnp.sum(jnp.where(o1, rank, 0.0), axis=1, keepdims=True)
    pos2 = jnp.sum(jnp.where(o2, rank, 0.0), axis=1, keepdims=True)
    cnt_scr[...] += jnp.sum(onehot, axis=0, keepdims=True)
    route = jnp.where(lane == ROUTE_E1, e1 - EXPERT_LANE0, 0.0)
    route = jnp.where(lane == ROUTE_E2, e2 - EXPERT_LANE0, route)
    route = jnp.where(lane == ROUTE_W1, w1, route)
    route = jnp.where(lane == ROUTE_W2, w2, route)
    route = jnp.where(lane == ROUTE_P1, pos1, route)
    route_ref[...] = jnp.where(lane == ROUTE_P2, pos2, route)


def _mix_router_kernel(*refs, n_prompt_tiles):
    prompt_rows, sample_rows, rest = refs[0:5], refs[5:10], refs[10:]
    consts, (h_ref, n2_ref, route_ref, cnt_ref, cnt_scr) = rest[:8], rest[8:]
    i = pl.program_id(0)

    @pl.when(i == 0)
    def _():
        cnt_scr[...] = jnp.zeros(cnt_scr.shape, F32)

    @pl.when(i < n_prompt_tiles)
    def _():
        _mix_router_body(*prompt_rows, *consts, h_ref, n2_ref, route_ref, cnt_scr)

    @pl.when(i >= n_prompt_tiles)
    def _():
        _mix_router_body(*sample_rows, *consts, h_ref, n2_ref, route_ref, cnt_scr)

    cnt_ref[...] = cnt_scr[...]


def _mix_router(prompt_rows, sample_rows, lnw, lnb, wo, gffn, wr, br, *, seq_tiles):
    tm = MOE_TM
    n_p = prompt_rows[0].shape[0] // tm
    assert sample_rows[0].shape[0] == tm
    t = (n_p + 1) * tm
    widths = (D_MODEL, GLA_WIDTH, RWKV_WIDTH, RWKV_WIDTH, RWKV_WIDTH)
    p_specs = [pl.BlockSpec((tm, n), lambda i: (jnp.minimum(i, n_p - 1), 0)) for n in widths]
    p_specs[2] = pl.BlockSpec(
        (tm, RWKV_WIDTH), lambda i: (jnp.minimum(i, n_p - 1) % seq_tiles, jnp.minimum(i, n_p - 1) // seq_tiles))
    s_specs = [pl.BlockSpec((tm, n), lambda i: (0, 0)) for n in widths]
    const = lambda shape: pl.BlockSpec(shape, lambda i: (0,) * len(shape))
    row = lambda n: pl.BlockSpec((tm, n), lambda i: (i, 0))
    tril = jnp.tril(jnp.ones((tm, tm), F32), -1).astype(BF16)
    return pl.pallas_call(
        functools.partial(_mix_router_kernel, n_prompt_tiles=n_p),
        grid=(n_p + 1,),
        in_specs=p_specs + s_specs + [
            const((1, RWKV_WIDTH)), const((1, RWKV_WIDTH)), const((D_MODEL, D_MODEL)), const((1, D_MODEL)),
            const((D_MODEL, ROUTER_LANES)), const((1, ROUTER_LANES)), const((RWKV_WIDTH, RWKV_WIDTH)),
            const((tm, tm))],
        out_specs=[row(D_MODEL), row(D_MODEL), row(ROUTER_LANES), const((1, ROUTER_LANES))],
        out_shape=[jax.ShapeDtypeStruct((t, D_MODEL), F32), jax.ShapeDtypeStruct((t, D_MODEL), F32),
                   jax.ShapeDtypeStruct((t, ROUTER_LANES), F32), jax.ShapeDtypeStruct((1, ROUTER_LANES), F32)],
        scratch_shapes=[pltpu.VMEM((1, ROUTER_LANES), F32)],
        compiler_params=_cparams("arbitrary"),
        name="mix_router",
    )(*prompt_rows, *sample_rows, lnw, lnb, wo, gffn, wr, br, _block_ones(RWKV_WIDTH, RWKV_HEAD), tril)


ROW_DMA_UNROLL = 8


def _dispatch_kernel(slots_ref, x_ref, xs_hbm, sem):
    tm = x_ref.shape[0]

    def issue(r, c):
        src = x_ref.at[pl.ds(r, 1)]
        pltpu.make_async_copy(src, xs_hbm.at[pl.ds(slots_ref[0, 0, 2 * r], 1)], sem).start()
        pltpu.make_async_copy(src, xs_hbm.at[pl.ds(slots_ref[0, 0, 2 * r + 1], 1)], sem).start()
        return c

    lax.fori_loop(0, tm, issue, 0, unroll=ROW_DMA_UNROLL)
    for _ in range(2):
        pltpu.make_async_copy(x_ref, xs_hbm.at[pl.ds(0, tm)], sem).wait()


def _dispatch(n2, slots):
    t = n2.shape[0]
    tm = MOE_TM
    return pl.pallas_call(
        _dispatch_kernel,
        grid_spec=pltpu.PrefetchScalarGridSpec(
            num_scalar_prefetch=0,
            grid=(t // tm,),
            in_specs=[pl.BlockSpec((1, 1, 2 * tm), lambda i: (i, 0, 0), memory_space=pltpu.SMEM),
                      pl.BlockSpec((tm, D_MODEL), lambda i: (i, 0))],
            out_specs=pl.BlockSpec(memory_space=pl.ANY),
            scratch_shapes=[pltpu.SemaphoreType.DMA(())],
        ),
        out_shape=jax.ShapeDtypeStruct((2 * t, D_MODEL), F32),
        compiler_params=_cparams("arbitrary"),
        name="moe_dispatch",
    )(slots, n2)


def _experts_kernel(wt_ref, we_ref, wlo_ref, whi_ref, wfirst_ref, nw_ref,
                    xs_ref, w1_ref, w3_ref, w2_ref, ys_ref, wb1, wb3, wb2):
    w = pl.program_id(0)

    @pl.when(w < nw_ref[0])
    def _():
        new_expert = jnp.logical_or(w == 0, we_ref[w] != we_ref[jnp.maximum(w - 1, 0)])

        @pl.when(new_expert)
        def _():
            wb1[...] = w1_ref[0].astype(BF16)
            wb3[...] = w3_ref[0].astype(BF16)
            wb2[...] = w2_ref[0].astype(BF16)

        x = xs_ref[...].astype(BF16)
        a = _dot(x, wb1[...])
        b = _dot(x, wb3[...])
        o = _dot(((a * _sigmoid(a)) * b).astype(BF16), wb2[...])

        @pl.when(wfirst_ref[w] == 1)
        def _():
            ys_ref[...] = o

        @pl.when(wfirst_ref[w] == 0)
        def _():
            rows = lax.broadcasted_iota(jnp.int32, o.shape, 0)
            ys_ref[...] = jnp.where((rows >= wlo_ref[w]) & (rows < whi_ref[w]), o, ys_ref[...])


def _experts(xs, work, w1, w3, w2):
    s = xs.shape[0]
    ts = MOE_TS
    n_work = work[0].shape[0]
    return pl.pallas_call(
        _experts_kernel,
        grid_spec=pltpu.PrefetchScalarGridSpec(
            num_scalar_prefetch=6,
            grid=(n_work,),
            in_specs=[
                pl.BlockSpec((ts, D_MODEL), lambda w, wt, we, *_: (wt[w], 0)),
                pl.BlockSpec((1, D_MODEL, D_EXPERT), lambda w, wt, we, *_: (we[w], 0, 0)),
                pl.BlockSpec((1, D_MODEL, D_EXPERT), lambda w, wt, we, *_: (we[w], 0, 0)),
                pl.BlockSpec((1, D_EXPERT, D_MODEL), lambda w, wt, we, *_: (we[w], 0, 0)),
            ],
            out_specs=pl.BlockSpec((ts, D_MODEL), lambda w, wt, we, *_: (wt[w], 0)),
            scratch_shapes=[pltpu.VMEM((D_MODEL, D_EXPERT), BF16), pltpu.VMEM((D_MODEL, D_EXPERT), BF16),
                            pltpu.VMEM((D_EXPERT, D_MODEL), BF16)],
        ),
        out_shape=jax.ShapeDtypeStruct((s, D_MODEL), F32),
        compiler_params=_cparams("arbitrary"),
        name="moe_experts",
    )(*work, xs, w1, w3, w2)


def _expert_work_items(counts, total):
    ts = MOE_TS
    n_tiles = total // ts
    n_work = n_tiles + N_EXPERTS - 1
    offs = jnp.cumsum(counts) - counts
    t0 = (jnp.arange(n_tiles, dtype=jnp.int32) * ts)[:, None]
    lo = jnp.maximum(t0, offs[None, :])
    hi = jnp.minimum(t0 + ts, (offs + counts)[None, :])
    nonempty = (hi > lo).reshape(-1)
    nw = jnp.sum(nonempty.astype(jnp.int32))
    idx = jnp.nonzero(nonempty, size=n_work, fill_value=0)[0].astype(jnp.int32)
    idx = jnp.where(jnp.arange(n_work) < nw, idx, idx[jnp.maximum(nw - 1, 0)])
    wt = idx // N_EXPERTS
    we = idx % N_EXPERTS
    wlo = lo.reshape(-1)[idx] - wt * ts
    whi = hi.reshape(-1)[idx] - wt * ts
    wfirst = jnp.concatenate([jnp.ones((1,), jnp.int32), (wt[1:] != wt[:-1]).astype(jnp.int32)])
    return wt, we, wlo, whi, wfirst, nw.reshape(1)


def _combine_kernel(slots_ref, slots_next_ref, h_ref, route_ref, gfin_ref, ys_hbm, yp_ref, ysm_ref, gbuf, sems,
                    *, n_prompt_tiles):
    i = pl.program_id(0)
    n = pl.num_programs(0)
    tm = h_ref.shape[0]

    def gather(s_ref, buf):
        def issue(r, c):
            pltpu.make_async_copy(ys_hbm.at[pl.ds(s_ref[0, 0, 2 * r], 1)], gbuf.at[buf, 0, pl.ds(r, 1)],
                                  sems.at[buf]).start()
            pltpu.make_async_copy(ys_hbm.at[pl.ds(s_ref[0, 0, 2 * r + 1], 1)], gbuf.at[buf, 1, pl.ds(r, 1)],
                                  sems.at[buf]).start()
            return c
        lax.fori_loop(0, tm, issue, 0, unroll=ROW_DMA_UNROLL)

    cur = i % 2

    @pl.when(i == 0)
    def _():
        gather(slots_ref, 0)

    @pl.when(i + 1 < n)
    def _():
        gather(slots_next_ref, 1 - cur)

    for k in range(2):
        pltpu.make_async_copy(ys_hbm.at[pl.ds(0, tm)], gbuf.at[cur, k], sems.at[cur]).wait()
    route = route_ref[...]
    lane = lax.broadcasted_iota(jnp.int32, route.shape, 1)
    w1 = jnp.sum(jnp.where(lane == ROUTE_W1, route, 0.0), axis=1, keepdims=True)
    w2 = jnp.sum(jnp.where(lane == ROUTE_W2, route, 0.0), axis=1, keepdims=True)
    hf = h_ref[...] + (w1 * gbuf[cur, 0] + w2 * gbuf[cur, 1])
    y = hf * lax.rsqrt(jnp.mean(hf * hf, axis=-1, keepdims=True) + NORM_EPS) * gfin_ref[...]

    @pl.when(i < n_prompt_tiles)
    def _():
        yp_ref[...] = y

    @pl.when(i >= n_prompt_tiles)
    def _():
        ysm_ref[...] = y


def _combine(h, route, slots, ys, gfin, n_prompt_tiles):
    t = h.shape[0]
    tm = MOE_TM
    n_p = n_prompt_tiles
    return pl.pallas_call(
        functools.partial(_combine_kernel, n_prompt_tiles=n_p),
        grid_spec=pltpu.PrefetchScalarGridSpec(
            num_scalar_prefetch=0,
            grid=(t // tm,),
            in_specs=[pl.BlockSpec((1, 1, 2 * tm), lambda i: (i, 0, 0), memory_space=pltpu.SMEM),
                      pl.BlockSpec((1, 1, 2 * tm), lambda i: (jnp.minimum(i + 1, t // tm - 1), 0, 0),
                                   memory_space=pltpu.SMEM),
                      pl.BlockSpec((tm, D_MODEL), lambda i: (i, 0)),
                      pl.BlockSpec((tm, ROUTER_LANES), lambda i: (i, 0)),
                      pl.BlockSpec((1, D_MODEL), lambda i: (0, 0)),
                      pl.BlockSpec(memory_space=pl.ANY)],
            out_specs=[pl.BlockSpec((tm, D_MODEL), lambda i: (jnp.minimum(i, n_p - 1), 0)),
                       pl.BlockSpec((tm, D_MODEL), lambda i: (0, 0))],
            scratch_shapes=[pltpu.VMEM((2, 2, tm, D_MODEL), F32), pltpu.SemaphoreType.DMA((2,))],
        ),
        out_shape=[jax.ShapeDtypeStruct((n_p * tm, D_MODEL), F32), jax.ShapeDtypeStruct((tm, D_MODEL), F32)],
        compiler_params=_cparams("arbitrary"),
        name="moe_combine",
    )(slots, slots, h, route, gfin, ys)


def _pair_state(s):
    b = s.shape[0]
    return s.reshape(b, 2, RWKV_PAIRS, RWKV_HEAD, RWKV_HEAD).transpose(0, 2, 3, 1, 4).reshape(
        b, RWKV_PAIRS, RWKV_HEAD, LANE)


def _unpair_state(s):
    b = s.shape[0]
    return s.reshape(b, RWKV_PAIRS, RWKV_HEAD, 2, RWKV_HEAD).transpose(0, 3, 1, 2, 4).reshape(
        b, RWKV_HEADS, RWKV_HEAD, RWKV_HEAD)


def _v_tiles(v, tblk):
    b, t, _ = v.shape
    x = v.reshape(b, t // tblk, tblk, 2, RWKV_PAIRS, RWKV_HEAD).transpose(0, 1, 4, 5, 3, 2)
    x = jnp.pad(x, ((0, 0),) * 5 + ((0, RWKV_HEAD - tblk),))
    return x.reshape(b, t // tblk, RWKV_PAIRS, RWKV_HEAD, LANE).astype(BF16)


def _y_rows(y, bb, t):
    nb, tpad, _ = y.shape
    return y.reshape(nb, tpad, bb, RWKV_WIDTH).transpose(0, 2, 1, 3).reshape(nb * bb, tpad, RWKV_WIDTH)[:, :t]


def kernel(x_prompt, x_sample, state_gla, state_rwkv, state_shift, meta_tokens, norm_mix, w_in, gla_gate_w2,
           gla_gate_b, gla_norm, rwkv_mu, rwkv_w0, rwkv_w2, rwkv_a0, rwkv_a2, rwkv_g2, rwkv_kk, rwkv_ka, rwkv_rk,
           rwkv_ln_w, rwkv_ln_b, w_out, norm_ffn, router_group_w, router_group_b, router_expert_w,
           router_expert_b, moe_w1, moe_w3, moe_w2, norm_final):
    bp, tp, _ = x_prompt.shape
    bs, ts, _ = x_sample.shape
    assert state_gla.shape[0] == 1, "one layer"
    lyr = 0

    w_in_l = w_in[lyr]
    wg = jnp.pad(w_in_l[:, :GLA_COLS], ((0, 0), (0, GLA_PCOLS - GLA_COLS))).astype(BF16)
    wr = w_in_l[:, GLA_COLS:].astype(BF16)
    g_mix = norm_mix[lyr][None, :]
    gw2p = jnp.pad(gla_gate_w2[lyr], ((0, LANE - GLA_GATE_RANK), (0, 0)))
    gb = gla_gate_b[lyr][None, :]
    gn = gla_norm[lyr][None, :]
    w2p = jnp.pad(rwkv_w2[lyr], ((0, 64), (0, 0))).astype(BF16)
    a2p = jnp.pad(rwkv_a2[lyr], ((64, 0), (0, 0))).astype(BF16)
    pre_params = (rwkv_mu[lyr][None, :], rwkv_w0[lyr][None, :], w2p, rwkv_a0[lyr][None, :], a2p,
                  rwkv_g2[lyr].astype(BF16), rwkv_kk[lyr][None, :], rwkv_ka[lyr][None, :],
                  rwkv_rk[lyr].reshape(1, RWKV_WIDTH), _block_ones(RWKV_WIDTH, RWKV_HEAD))
    lnw = rwkv_ln_w[lyr][None, :]
    lnb = rwkv_ln_b[lyr][None, :]
    wo = w_out[lyr].astype(BF16)
    gffn = norm_ffn[lyr][None, :]
    n_used = N_GROUPS + N_EXPERTS
    w_router = jnp.pad(
        jnp.concatenate([router_group_w[lyr],
                         router_expert_w[lyr].transpose(1, 0, 2).reshape(D_MODEL, N_EXPERTS)], axis=1),
        ((0, 0), (0, ROUTER_LANES - n_used)))
    b_router = jnp.pad(jnp.concatenate([router_group_b[lyr], router_expert_b[lyr].reshape(N_EXPERTS)]),
                       (0, ROUTER_LANES - n_used))[None, :]
    gfin = norm_final[None, :]

    pg_m, pr_m = _inproj(meta_tokens, g_mix, wg, wr, N_META)
    _, sg_m = _gla(pg_m[None], jnp.zeros((1, GLA_HEADS, GLA_DK, GLA_DV), F32), gw2p, gb, gn,
                   bb=1, chunk=N_META, sub=N_META, t_valid=N_META)
    r, w, k, kk, kka, v_m, _, _ = _rwkv_pre(pr_m[None], jnp.zeros((1, 1, RWKV_COLS), F32), pre_params,
                                            tm=N_META, explicit_prev=False, emit_vt=False)
    _, sr_m = _rwkv_rec(r, w, k, kk, kka, _v_tiles(v_m, N_META), jnp.zeros((1, RWKV_PAIRS, RWKV_HEAD, LANE), F32),
                        bb=1, n_steps=N_META)

    xp = x_prompt.reshape(bp * tp, D_MODEL)
    pg_p, pr_p = _inproj(xp, g_mix, wg, wr, 512)
    og_p, sg_p = _gla(pg_p.reshape(bp, tp, GLA_PCOLS), jnp.broadcast_to(sg_m, (bp,) + sg_m.shape[1:]), gw2p, gb, gn,
                      bb=4, chunk=GLA_CHUNK, sub=GLA_SUB, t_valid=GLA_CHUNK)
    pr_p3 = pr_p.reshape(bp, tp, RWKV_COLS)
    first_prev = jnp.broadcast_to(pr_m[N_META - 1][None, None, :], (bp, 1, RWKV_COLS))
    r, w, k, kk, kka, vt_p, bv_p, gate_p = _rwkv_pre(pr_p3, first_prev, pre_params, tm=256, explicit_prev=False,
                                                     emit_vt=True)
    y_p, sr_p = _rwkv_rec(r, w, k, kk, kka, vt_p, jnp.broadcast_to(sr_m, (bp,) + sr_m.shape[1:]),
                          bb=bp, n_steps=REC_TB)
    prompt_rows = (xp, og_p.reshape(bp * tp, GLA_WIDTH), y_p.reshape(tp, bp * RWKV_WIDTH),
                   bv_p.reshape(bp * tp, RWKV_WIDTH), gate_p.reshape(bp * tp, RWKV_WIDTH))

    xs = x_sample.reshape(bs * ts, D_MODEL)
    pg_s, pr_s = _inproj(xs, g_mix, wg, wr, bs * ts)
    ts_pad = 8
    pg_s3 = jnp.pad(pg_s.reshape(bs, ts, GLA_PCOLS), ((0, 0), (0, ts_pad - ts), (0, 0)))
    og_s, sg_s = _gla(pg_s3, state_gla[lyr], gw2p, gb, gn, bb=8, chunk=ts_pad, sub=ts_pad, t_valid=ts)
    og_s = og_s[:, :ts]
    pr_s3 = pr_s.reshape(bs, ts, RWKV_COLS)
    prev_s = jnp.concatenate([state_shift[lyr][:, None, :], pr_s3[:, :-1]], axis=1)
    r, w, k, kk, kka, v_s, bv_s, gate_s = _rwkv_pre(pr_s3.reshape(1, bs * ts, RWKV_COLS),
                                                     prev_s.reshape(1, bs * ts, RWKV_COLS), pre_params,
                                                     tm=bs * ts, explicit_prev=True, emit_vt=False)
    unflat = lambda a: a.reshape(RWKV_PAIRS, bs, ts, LANE).transpose(1, 0, 2, 3)
    y_s, sr_s = _rwkv_rec(unflat(r), unflat(w), unflat(k), unflat(kk), unflat(kka),
                          _v_tiles(v_s.reshape(bs, ts, RWKV_WIDTH), ts), _pair_state(state_rwkv[lyr]),
                          bb=8, n_steps=ts)
    y_s = _y_rows(y_s, 8, ts)
    sample_rows = (xs, og_s.reshape(bs * ts, GLA_WIDTH), y_s.reshape(bs * ts, RWKV_WIDTH),
                   bv_s.reshape(bs * ts, RWKV_WIDTH), gate_s.reshape(bs * ts, RWKV_WIDTH))

    h_all, n2_all, route, counts = _mix_router(prompt_rows, sample_rows, lnw, lnb, wo, gffn, w_router, b_router,
                                               seq_tiles=tp // MOE_TM)
    n_tok = h_all.shape[0]
    n_p_tiles = (bp * tp) // MOE_TM
    cnt = counts[0, EXPERT_LANE0:EXPERT_LANE0 + N_EXPERTS].astype(jnp.int32)
    offs = jnp.cumsum(cnt) - cnt
    eid = route[:, ROUTE_E1:ROUTE_E2 + 1].astype(jnp.int32)
    pos = route[:, ROUTE_P1:ROUTE_P2 + 1].astype(jnp.int32)
    slots = (offs[eid] + pos).reshape(n_tok // MOE_TM, 1, 2 * MOE_TM)
    xs_sorted = _dispatch(n2_all, slots)
    ys_sorted = _experts(xs_sorted, _expert_work_items(cnt, 2 * n_tok), moe_w1[lyr], moe_w3[lyr], moe_w2[lyr])
    y_prompt, y_sample = _combine(h_all, route, slots, ys_sorted, gfin, n_p_tiles)
    y_prompt = y_prompt.reshape(bp, tp, D_MODEL)
    y_sample = y_sample.reshape(bs, ts, D_MODEL)

    return (y_prompt, y_sample,
            sg_p[None], _unpair_state(sr_p)[None], pr_p3[:, -1][None],
            sg_s[None], _unpair_state(sr_s)[None], pr_s3[:, -1][None])
```

```python
import functools

import jax
import jax.numpy as jnp
from jax import lax
from jax.experimental import pallas as pl
from jax.experimental.pallas import tpu as pltpu

F32 = jnp.float32
BF16 = jnp.bfloat16
HIGHEST = lax.Precision.HIGHEST

D_MODEL = 1024
N_META = 16
NORM_EPS = 1e-6
GLA_HEADS = 4
GLA_DK = 64
GLA_DV = 128
GLA_QK = GLA_HEADS * GLA_DK
GLA_WIDTH = GLA_HEADS * GLA_DV
GLA_GATE_RANK = 16
GLA_GATE_NORM = 16.0
GLA_CHUNK = 64
GLA_SUB = 16
GLA_COLS = 2 * GLA_QK + 2 * GLA_WIDTH + GLA_GATE_RANK
GLA_PCOLS = 2 * GLA_QK + 2 * GLA_WIDTH + 128
RWKV_WIDTH = 512
RWKV_HEAD = 64
RWKV_HEADS = 8
RWKV_PAIRS = RWKV_HEADS // 2
RWKV_DECAY_SCALE = 0.606531
RWKV_GN_EPS = 64e-5
RWKV_COLS = 3 * RWKV_WIDTH + 64 + 64 + 128
REC_TB = 64
REC_UNROLL = 8
N_GROUPS = 4
EXPERTS_PER_GROUP = 8
N_EXPERTS = 32
D_EXPERT = 512
ROUTER_LANES = 128
EXPERT_LANE0 = N_GROUPS
ROUTE_E1, ROUTE_E2, ROUTE_W1, ROUTE_W2, ROUTE_P1, ROUTE_P2 = range(6)
MOE_TM = 512
MOE_TS = 512

LANE = 128
VMEM_LIMIT = 56 * 1024 * 1024


def _cparams(*sem):
    return pltpu.CompilerParams(dimension_semantics=sem, vmem_limit_bytes=VMEM_LIMIT)


def _block_ones(n, blk):
    i = jnp.arange(n)
    return (i[:, None] // blk == i[None, :] // blk).astype(BF16)


def _sigmoid(x):
    return 1.0 / (1.0 + jnp.exp(-x))


def _dot(a, b):
    return jnp.dot(a, b, preferred_element_type=F32)


def _dot_nt(a, b):
    return lax.dot_general(a, b, (((1,), (1,)), ((), ())), preferred_element_type=F32)


def _dot_tn(a, b):
    return lax.dot_general(a, b, (((0,), (0,)), ((), ())), preferred_element_type=F32)


def _split2(x):
    hi = x.astype(BF16)
    lo = (x - hi.astype(F32)).astype(BF16)
    return hi, lo


def _head_selector():
    return (jnp.arange(RWKV_WIDTH)[:, None] // RWKV_HEAD == jnp.arange(LANE)[None, :]).astype(BF16)


def _group_sum(x, sel):
    hi, lo = _split2(x)
    s_hi, s_lo = _split2(_dot(hi, sel) + _dot(lo, sel))
    return _dot_nt(s_hi, sel) + _dot_nt(s_lo, sel)


def _inproj_kernel(x_ref, g_ref, wg_ref, wr_ref, pg_ref, pr_ref):
    x = x_ref[...]
    n = x * lax.rsqrt(jnp.mean(x * x, axis=-1, keepdims=True) + NORM_EPS) * g_ref[...]
    nb = n.astype(BF16)
    pg_ref[...] = _dot(nb, wg_ref[...])
    pr_ref[...] = _dot(nb, wr_ref[...])


def _inproj(x, g, wg, wr, tm):
    t = x.shape[0]
    return pl.pallas_call(
        _inproj_kernel,
        grid=(t // tm,),
        in_specs=[
            pl.BlockSpec((tm, D_MODEL), lambda i: (i, 0)),
            pl.BlockSpec((1, D_MODEL), lambda i: (0, 0)),
            pl.BlockSpec((D_MODEL, GLA_PCOLS), lambda i: (0, 0)),
            pl.BlockSpec((D_MODEL, RWKV_COLS), lambda i: (0, 0)),
        ],
        out_specs=[
            pl.BlockSpec((tm, GLA_PCOLS), lambda i: (i, 0)),
            pl.BlockSpec((tm, RWKV_COLS), lambda i: (i, 0)),
        ],
        out_shape=[jax.ShapeDtypeStruct((t, GLA_PCOLS), F32), jax.ShapeDtypeStruct((t, RWKV_COLS), F32)],
        compiler_params=_cparams("parallel"),
        name="inproj",
    )(x, g, wg, wr)


def _gla_kernel(pg_ref, s0_ref, gw2_ref, gb_ref, gn_ref, bo_ref, tril_ref, o_ref, sout_ref, s_scr,
                *, bb, chunk, sub, t_valid):
    ci = pl.program_id(1)

    @pl.when(ci == 0)
    def _():
        s_scr[...] = s0_ref[...]

    bo = bo_ref[...]
    tril = tril_ref[...]
    lane = lax.broadcasted_iota(jnp.int32, (sub, LANE), 1) & (GLA_DK - 1)
    rowi = lax.broadcasted_iota(jnp.int32, (sub, LANE), 0)
    head0_s = lax.broadcasted_iota(jnp.int32, (sub, LANE), 1) < GLA_DK
    head0_c = lax.broadcasted_iota(jnp.int32, (chunk, LANE), 1) < GLA_DK

    for bi in range(bb):
        pg = pg_ref[bi]
        q = pg[:, 0:GLA_QK] * (GLA_DK ** -0.5)
        k = pg[:, GLA_QK:2 * GLA_QK]
        v = pg[:, 2 * GLA_QK:2 * GLA_QK + GLA_WIDTH]
        g = pg[:, 2 * GLA_QK + GLA_WIDTH:2 * GLA_QK + 2 * GLA_WIDTH]
        gl = pg[:, 2 * GLA_QK + 2 * GLA_WIDTH:]
        z = jnp.dot(gl, gw2_ref[...], precision=HIGHEST, preferred_element_type=F32) + gb_ref[...]
        lg = (jnp.minimum(z, 0.0) - jnp.log1p(jnp.exp(-jnp.abs(z)))) * (1.0 / GLA_GATE_NORM)
        if t_valid < chunk:
            rows = lax.broadcasted_iota(jnp.int32, lg.shape, 0)
            lg = jnp.where(rows < t_valid, lg, 0.0)
        b = jnp.dot(tril, lg, precision=HIGHEST, preferred_element_type=F32)
        eb = jnp.exp(b)
        blast = b[chunk - 1:chunk, :]
        kl = k * jnp.exp(blast - b)
        qe = q * eb

        n_blk = chunk // sub
        n_pairs = GLA_HEADS // 2
        ps = []
        for hp in range(n_pairs):
            sl = slice(hp * LANE, (hp + 1) * LANE)
            for blk in range(n_blk):
                rs = slice(blk * sub, (blk + 1) * sub)
                qb, kb, bbk = q[rs, sl], k[rs, sl], b[rs, sl]
                for j in range(sub):
                    ps.append(qb * (kb[j:j + 1] * jnp.exp(jnp.minimum(bbk - bbk[j:j + 1], 0.0))))
        red = _dot(jnp.concatenate(ps, axis=0).astype(BF16), bo)

        o_heads = []
        for hp in range(n_pairs):
            sl = slice(hp * LANE, (hp + 1) * LANE)
            kp, bp = k[:, sl], b[:, sl]
            row_blocks = []
            for blk in range(n_blk):
                rs = slice(blk * sub, (blk + 1) * sub)
                base = (hp * n_blk + blk) * sub * sub
                a = jnp.zeros((sub, LANE), F32)
                for j in range(sub):
                    a = jnp.where((lane == blk * sub + j) & (rowi >= j), red[base + j * sub:base + (j + 1) * sub], a)
                if blk > 0:
                    bref = bp[blk * sub - 1:blk * sub]
                    qt = q[rs, sl] * jnp.exp(bp[rs] - bref)
                    kt = (kp * jnp.exp(jnp.minimum(bref - bp, 0.0))).astype(BF16)
                    qt2 = jnp.concatenate([jnp.where(head0_s, qt, 0.0), jnp.where(head0_s, 0.0, qt)], axis=0)
                    off2 = _dot_nt(qt2.astype(BF16), kt)
                    a = jnp.where(lane < blk * sub, jnp.concatenate([off2[:sub], off2[sub:]], axis=1), a)
                row_blocks.append(a)
            a_pair = row_blocks[0] if n_blk == 1 else jnp.concatenate(row_blocks, axis=0)
            v0 = v[:, 2 * hp * GLA_DV:(2 * hp + 1) * GLA_DV]
            v1 = v[:, (2 * hp + 1) * GLA_DV:(2 * hp + 2) * GLA_DV]
            s_pair = s_scr[bi, 2 * hp:2 * hp + 2].reshape(2 * GLA_DK, GLA_DV)
            qe_p, kl_p = qe[:, sl], kl[:, sl]

            def by_head(x):
                return jnp.concatenate([jnp.where(head0_c, x, 0.0), jnp.where(head0_c, 0.0, x)], axis=0)

            if chunk == GLA_DK:
                v_rows = jnp.concatenate([v0, v1], axis=0)
            else:
                zpad = jnp.zeros((GLA_DK - chunk, GLA_DV), F32)
                v_rows = jnp.concatenate([v0, zpad, v1, zpad], axis=0)
            lhs = jnp.concatenate([by_head(a_pair), by_head(qe_p)], axis=1).astype(BF16)
            rhs = jnp.concatenate([v_rows, s_pair], axis=0).astype(BF16)
            o2 = _dot(lhs, rhs)
            upd = _dot_tn(by_head(kl_p).astype(BF16), jnp.concatenate([v0, v1], axis=0).astype(BF16))
            dcol = jnp.broadcast_to(jnp.exp(blast[:, sl]), (8, LANE)).T[:, 0:1]
            s_new = dcol * s_pair + upd
            s_scr[bi, 2 * hp] = s_new[:GLA_DK]
            s_scr[bi, 2 * hp + 1] = s_new[GLA_DK:]
            for h2 in range(2):
                o_h = o2[h2 * chunk:(h2 + 1) * chunk]
                o_heads.append(o_h * lax.rsqrt(jnp.mean(o_h * o_h, axis=-1, keepdims=True) + NORM_EPS) * gn_ref[...])
        o = jnp.concatenate(o_heads, axis=1)
        o_ref[bi] = o * (g * _sigmoid(g))

    @pl.when(ci == pl.num_programs(1) - 1)
    def _():
        sout_ref[...] = s_scr[...]


def _gla(pg, s0, gw2p, gb, gn, *, bb, chunk, sub, t_valid):
    b, t, _ = pg.shape
    tril = jnp.tril(jnp.ones((chunk, chunk), F32))
    kern = functools.partial(_gla_kernel, bb=bb, chunk=chunk, sub=sub, t_valid=t_valid)
    return pl.pallas_call(
        kern,
        grid=(b // bb, t // chunk),
        in_specs=[
            pl.BlockSpec((bb, chunk, GLA_PCOLS), lambda i, j: (i, j, 0)),
            pl.BlockSpec((bb, GLA_HEADS, GLA_DK, GLA_DV), lambda i, j: (i, 0, 0, 0)),
            pl.BlockSpec((LANE, GLA_QK), lambda i, j: (0, 0)),
            pl.BlockSpec((1, GLA_QK), lambda i, j: (0, 0)),
            pl.BlockSpec((1, GLA_DV), lambda i, j: (0, 0)),
            pl.BlockSpec((LANE, LANE), lambda i, j: (0, 0)),
            pl.BlockSpec((chunk, chunk), lambda i, j: (0, 0)),
        ],
        out_specs=[
            pl.BlockSpec((bb, chunk, GLA_WIDTH), lambda i, j: (i, j, 0)),
            pl.BlockSpec((bb, GLA_HEADS, GLA_DK, GLA_DV), lambda i, j: (i, 0, 0, 0)),
        ],
        out_shape=[jax.ShapeDtypeStruct((b, t, GLA_WIDTH), F32),
                   jax.ShapeDtypeStruct((b, GLA_HEADS, GLA_DK, GLA_DV), F32)],
        scratch_shapes=[pltpu.VMEM((bb, GLA_HEADS, GLA_DK, GLA_DV), F32)],
        compiler_params=_cparams("parallel", "arbitrary"),
        name="gla_chunk",
    )(pg, s0, gw2p, gb, gn, _block_ones(LANE, GLA_DK), tril)


def _rwkv_pre_kernel(pr_ref, aux_ref, mu_ref, w0_ref, w2_ref, a0_ref, a2_ref, g2_ref, kk_ref, ka_ref, rk_ref, bo_ref,
                     r_out, w_out, k_out, kkn_out, kka_out, v_out, bv_out, gate_out, carry_scr,
                     *, tm, explicit_prev, emit_vt):
    pr = pr_ref[0]
    if explicit_prev:
        prev = aux_ref[0]
    else:
        j = pl.program_id(1)
        row0 = jnp.where(j == 0, aux_ref[0], carry_scr[...])
        rows = lax.broadcasted_iota(jnp.int32, pr.shape, 0)
        prev = jnp.where(rows == 0, row0, pltpu.roll(pr, 1, 0))
        carry_scr[...] = pr[tm - 1:tm, :]
    xm = pr + (prev - pr) * mu_ref[...]
    wd = RWKV_WIDTH
    rr, rk, rv = xm[:, 0:wd], xm[:, wd:2 * wd], xm[:, 2 * wd:3 * wd]
    wa = xm[:, 3 * wd:3 * wd + LANE]
    gl2 = xm[:, 3 * wd + LANE:3 * wd + 2 * LANE]
    logw = -RWKV_DECAY_SCALE * _sigmoid(w0_ref[...] + _dot(jnp.tanh(wa).astype(BF16), w2_ref[...]))
    aa = _sigmoid(a0_ref[...] + _dot(wa.astype(BF16), a2_ref[...]))
    gate = _dot(_sigmoid(gl2).astype(BF16), g2_ref[...])
    bo = bo_ref[...]
    kk = rk * kk_ref[...]
    kk = kk / jnp.maximum(jnp.sqrt(_group_sum(kk * kk, bo)), 1e-12)
    k = rk * (1.0 + (aa - 1.0) * ka_ref[...])
    bv = _group_sum(rr * k * rk_ref[...], bo) * rv
    w = jnp.exp(logw)
    kka = kk * aa
    hd = RWKV_HEAD

    def pair(x, hp):
        return jnp.concatenate([x[:, hp * hd:(hp + 1) * hd], x[:, (hp + RWKV_PAIRS) * hd:(hp + RWKV_PAIRS + 1) * hd]],
                               axis=1)

    for hp in range(RWKV_PAIRS):
        r_out[0, hp] = pair(rr, hp)
        w_out[0, hp] = pair(w, hp)
        k_out[0, hp] = pair(k, hp)
        kkn_out[0, hp] = pair(kk, hp)
        kka_out[0, hp] = pair(kka, hp)
    if emit_vt:
        vt = rv.T
        for tb in range(tm // REC_TB):
            ts = slice(tb * REC_TB, (tb + 1) * REC_TB)
            for hp in range(RWKV_PAIRS):
                lo, hi = hp * hd, (hp + RWKV_PAIRS) * hd
                v_out[0, tb, hp] = jnp.concatenate([vt[lo:lo + hd, ts], vt[hi:hi + hd, ts]], axis=1).astype(BF16)
    else:
        v_out[0] = rv
    bv_out[0] = bv
    gate_out[0] = gate


def _rwkv_pre(pr, aux, params, *, tm, explicit_prev, emit_vt):
    b, t, _ = pr.shape
    kern = functools.partial(_rwkv_pre_kernel, tm=tm, explicit_prev=explicit_prev, emit_vt=emit_vt)
    aux_spec = (pl.BlockSpec((1, tm, RWKV_COLS), lambda i, j: (i, j, 0)) if explicit_prev
                else pl.BlockSpec((1, 1, RWKV_COLS), lambda i, j: (i, 0, 0)))
    const = lambda shape: pl.BlockSpec(shape, lambda i, j: (0,) * len(shape))
    pair_spec = pl.BlockSpec((1, RWKV_PAIRS, tm, LANE), lambda i, j: (i, 0, j, 0))
    row_spec = pl.BlockSpec((1, tm, RWKV_WIDTH), lambda i, j: (i, j, 0))
    pair_shape = jax.ShapeDtypeStruct((b, RWKV_PAIRS, t, LANE), F32)
    row_shape = jax.ShapeDtypeStruct((b, t, RWKV_WIDTH), F32)
    if emit_vt:
        v_spec = pl.BlockSpec((1, tm // REC_TB, RWKV_PAIRS, RWKV_HEAD, LANE), lambda i, j: (i, j, 0, 0, 0))
        v_shape = jax.ShapeDtypeStruct((b, t // REC_TB, RWKV_PAIRS, RWKV_HEAD, LANE), BF16)
    else:
        v_spec, v_shape = row_spec, row_shape
    return pl.pallas_call(
        kern,
        grid=(b, t // tm),
        in_specs=[
            pl.BlockSpec((1, tm, RWKV_COLS), lambda i, j: (i, j, 0)),
            aux_spec,
            const((1, RWKV_COLS)), const((1, RWKV_WIDTH)), const((LANE, RWKV_WIDTH)), const((1, RWKV_WIDTH)),
            const((LANE, RWKV_WIDTH)), const((LANE, RWKV_WIDTH)), const((1, RWKV_WIDTH)), const((1, RWKV_WIDTH)),
            const((1, RWKV_WIDTH)), const((RWKV_WIDTH, LANE)),
        ],
        out_specs=[pair_spec] * 5 + [v_spec, row_spec, row_spec],
        out_shape=[pair_shape] * 5 + [v_shape, row_shape, row_shape],
        scratch_shapes=[pltpu.VMEM((1, RWKV_COLS), F32)],
        compiler_params=_cparams("parallel", "arbitrary"),
        name="rwkv_pre",
    )(pr, aux, *params)


def _rwkv_rec_kernel(r_ref, w_ref, k_ref, kk_ref, kka_ref, vt_ref, s0_ref, bo_ref, vsel_ref, ysel_ref,
                     y_ref, sout_ref, s_scr, t1_scr, t3_scr, yt_scr, *, bb, n_steps):
    tb = pl.program_id(1)
    nc = bb * RWKV_PAIRS
    hd = RWKV_HEAD

    @pl.when(tb == 0)
    def _():
        for c in range(nc):
            bi, hp = divmod(c, RWKV_PAIRS)
            s_scr[c] = jnp.concatenate([s0_ref[bi, hp], s0_ref[bi, hp + RWKV_PAIRS]], axis=1)

    bo = bo_ref[...]

    def step(t, u):
        row = pl.ds(t, 1)
        for c in range(nc):
            bi, hp = divmod(c, RWKV_PAIRS)
            t1_scr[c * hd:(c + 1) * hd, :] = (s_scr[c] * kk_ref[bi, hp, row, :]).astype(BF16)
        sab = _dot(t1_scr[...], bo)
        vb = _dot(vt_ref[...].reshape(nc * hd, LANE), vsel_ref[t])
        for c in range(nc):
            bi, hp = divmod(c, RWKV_PAIRS)
            rs = slice(c * hd, (c + 1) * hd)
            s2 = (s_scr[c] * w_ref[bi, hp, row, :] - sab[rs] * kka_ref[bi, hp, row, :]
                  + vb[rs] * k_ref[bi, hp, row, :])
            s_scr[c] = s2
            t3_scr[rs, :] = (s2 * r_ref[bi, hp, row, :]).astype(BF16)
        yt_scr[...] += _dot_nt(ysel_ref[u], t3_scr[...])

    n_inner = min(8, n_steps)

    def block8(t8, carry):
        yt_scr[...] = jnp.zeros(yt_scr.shape, F32)

        def inner(u, c2):
            step(t8 * 8 + u, u)
            return c2

        lax.fori_loop(0, n_inner, inner, 0, unroll=REC_UNROLL)
        t0 = pl.multiple_of(t8 * 8, 8)
        blk = RWKV_PAIRS * hd
        for bi in range(bb):
            for h2 in range(2):
                y_ref[0, pl.ds(t0, 8), bi * RWKV_WIDTH + h2 * blk:bi * RWKV_WIDTH + (h2 + 1) * blk] = (
                    yt_scr[h2 * 8:(h2 + 1) * 8, bi * blk:(bi + 1) * blk])
        return carry

    lax.fori_loop(0, (n_steps + 7) // 8, block8, 0)

    @pl.when(tb == pl.num_programs(1) - 1)
    def _():
        for c in range(nc):
            bi, hp = divmod(c, RWKV_PAIRS)
            s_c = s_scr[c]
            sout_ref[bi, hp] = s_c[:, :RWKV_HEAD]
            sout_ref[bi, hp + RWKV_PAIRS] = s_c[:, RWKV_HEAD:]


def _rwkv_rec(r, w, k, kk, kka, vt, s0, *, bb, n_steps):
    b, _, t, _ = r.shape
    tblk = min(REC_TB, t)
    ntb = t // tblk
    nc = bb * RWKV_PAIRS
    lane = jnp.arange(LANE)
    vsel = ((lane[None, :, None] // RWKV_HEAD == lane[None, None, :] // RWKV_HEAD)
            & (lane[None, :, None] % RWKV_HEAD == jnp.arange(RWKV_HEAD)[:, None, None])).astype(BF16)
    ysel = (jnp.arange(16)[None, :, None]
            == 8 * (lane[None, None, :] // RWKV_HEAD) + jnp.arange(8)[:, None, None]).astype(BF16)
    kern = functools.partial(_rwkv_rec_kernel, bb=bb, n_steps=n_steps)
    pair_spec = pl.BlockSpec((bb, RWKV_PAIRS, tblk, LANE), lambda i, j: (i, 0, j, 0))
    state_spec = pl.BlockSpec((bb, RWKV_HEADS, RWKV_HEAD, RWKV_HEAD), lambda i, j: (i, 0, 0, 0))
    ytb = max(tblk, 8)
    return pl.pallas_call(
        kern,
        grid=(b // bb, ntb),
        in_specs=[pair_spec] * 5 + [
            pl.BlockSpec((bb, 1, RWKV_PAIRS, RWKV_HEAD, LANE), lambda i, j: (i, j, 0, 0, 0)),
            state_spec,
            pl.BlockSpec((LANE, LANE), lambda i, j: (0, 0)),
            pl.BlockSpec((RWKV_HEAD, LANE, LANE), lambda i, j: (0, 0, 0)),
            pl.BlockSpec((8, 16, LANE), lambda i, j: (0, 0, 0)),
        ],
        out_specs=[
            pl.BlockSpec((1, ytb, bb * RWKV_WIDTH), lambda i, j: (i, j, 0)),
            state_spec,
        ],
        out_shape=[jax.ShapeDtypeStruct((b // bb, ntb * ytb, bb * RWKV_WIDTH), F32),
                   jax.ShapeDtypeStruct((b, RWKV_HEADS, RWKV_HEAD, RWKV_HEAD), F32)],
        scratch_shapes=[pltpu.VMEM((nc, RWKV_HEAD, LANE), F32),
                        pltpu.VMEM((nc * RWKV_HEAD, LANE), BF16),
                        pltpu.VMEM((nc * RWKV_HEAD, LANE), BF16),
                        pltpu.VMEM((16, nc * RWKV_HEAD), F32)],
        compiler_params=_cparams("parallel", "arbitrary"),
        name="rwkv_rec",
    )(r, w, k, kk, kka, vt, s0, _block_ones(LANE, RWKV_HEAD), vsel, ysel)


def _mix_router_body(x_ref, og_ref, y_ref, bv_ref, gate_ref, lnw_ref, lnb_ref, wo_ref, gffn_ref, wr_hi_ref, wr_lo_ref,
                     br_ref, bo_ref, tril_ref, h_ref, n2_ref, route_ref, cnt_scr):
    bo = bo_ref[...]
    y = y_ref[...]
    inv_n = 1.0 / RWKV_HEAD
    d = y - _group_sum(y, bo) * inv_n
    var = _group_sum(d * d, bo) * inv_n
    yn = d * lax.rsqrt(var + RWKV_GN_EPS) * lnw_ref[...] + lnb_ref[...] + bv_ref[...]
    o_rwkv = yn * gate_ref[...]
    mix = (_dot(og_ref[...].astype(BF16), wo_ref[0:GLA_WIDTH, :])
           + _dot(o_rwkv.astype(BF16), wo_ref[GLA_WIDTH:, :]))
    h = x_ref[...] + mix
    h_ref[...] = h
    n2 = h * lax.rsqrt(jnp.mean(h * h, axis=-1, keepdims=True) + NORM_EPS) * gffn_ref[...]
    n2_ref[...] = n2
    n2_hi, n2_lo = _split2(n2)
    lg = (_dot(n2_hi, wr_hi_ref[...]) + _dot(n2_hi, wr_lo_ref[...]) + _dot(n2_lo, wr_hi_ref[...])) + br_ref[...]
    neg = jnp.float32(-3.0e38)
    big = jnp.float32(1.0e9)
    lane = lax.broadcasted_iota(jnp.int32, lg.shape, 1).astype(F32)
    gmask = lane < N_GROUPS
    gmax = jnp.max(jnp.where(gmask, lg, neg), axis=1, keepdims=True)
    p_top = 1.0 / jnp.sum(jnp.where(gmask, jnp.exp(jnp.minimum(lg - gmax, 0.0)), 0.0), axis=1, keepdims=True)
    gidx = jnp.min(jnp.where(gmask & (lg == gmax), lane, big), axis=1, keepdims=True)
    e_lo = EXPERT_LANE0 + gidx * EXPERTS_PER_GROUP
    emask = (lane >= e_lo) & (lane < e_lo + EXPERTS_PER_GROUP)
    m1 = jnp.max(jnp.where(emask, lg, neg), axis=1, keepdims=True)
    e1 = jnp.min(jnp.where(emask & (lg == m1), lane, big), axis=1, keepdims=True)
    emask2 = emask & (lane != e1)
    m2 = jnp.max(jnp.where(emask2, lg, neg), axis=1, keepdims=True)
    e2 = jnp.min(jnp.where(emask2 & (lg == m2), lane, big), axis=1, keepdims=True)
    r21 = jnp.exp(m2 - m1)
    w1 = p_top / (1.0 + r21)
    w2 = p_top * r21 / (1.0 + r21)
    o1 = lane == e1
    o2 = lane == e2
    onehot = jnp.where(o1 | o2, 1.0, 0.0)
    rank = _dot(tril_ref[...], onehot.astype(BF16)) + cnt_scr[...]
    pos1 = jnp.sum(jnp.where(o1, rank, 0.0), axis=1, keepdims=True)
    pos2 = jnp.sum(jnp.where(o2, rank, 0.0), axis=1, keepdims=True)
    cnt_scr[...] += jnp.sum(onehot, axis=0, keepdims=True)
    route = jnp.where(lane == ROUTE_E1, e1 - EXPERT_LANE0, 0.0)
    route = jnp.where(lane == ROUTE_E2, e2 - EXPERT_LANE0, route)
    route = jnp.where(lane == ROUTE_W1, w1, route)
    route = jnp.where(lane == ROUTE_W2, w2, route)
    route = jnp.where(lane == ROUTE_P1, pos1, route)
    route_ref[...] = jnp.where(lane == ROUTE_P2, pos2, route)


def _mix_router_kernel(*refs, n_prompt_tiles):
    prompt_rows, sample_rows, rest = refs[0:5], refs[5:10], refs[10:]
    consts, (h_ref, n2_ref, route_ref, cnt_ref, cnt_scr) = rest[:9], rest[9:]
    i = pl.program_id(0)

    @pl.when(i == 0)
    def _():
        cnt_scr[...] = jnp.zeros(cnt_scr.shape, F32)

    @pl.when(i < n_prompt_tiles)
    def _():
        _mix_router_body(*prompt_rows, *consts, h_ref, n2_ref, route_ref, cnt_scr)

    @pl.when(i >= n_prompt_tiles)
    def _():
        _mix_router_body(*sample_rows, *consts, h_ref, n2_ref, route_ref, cnt_scr)

    cnt_ref[...] = cnt_scr[...]


def _mix_router(prompt_rows, sample_rows, lnw, lnb, wo, gffn, wr, br, *, seq_tiles):
    tm = MOE_TM
    n_p = prompt_rows[0].shape[0] // tm
    assert sample_rows[0].shape[0] == tm
    t = (n_p + 1) * tm
    widths = (D_MODEL, GLA_WIDTH, RWKV_WIDTH, RWKV_WIDTH, RWKV_WIDTH)
    p_specs = [pl.BlockSpec((tm, n), lambda i: (jnp.minimum(i, n_p - 1), 0)) for n in widths]
    p_specs[2] = pl.BlockSpec(
        (tm, RWKV_WIDTH), lambda i: (jnp.minimum(i, n_p - 1) % seq_tiles, jnp.minimum(i, n_p - 1) // seq_tiles))
    s_specs = [pl.BlockSpec((tm, n), lambda i: (0, 0)) for n in widths]
    const = lambda shape: pl.BlockSpec(shape, lambda i: (0,) * len(shape))
    row = lambda n: pl.BlockSpec((tm, n), lambda i: (i, 0))
    tril = jnp.tril(jnp.ones((tm, tm), F32), -1).astype(BF16)
    return pl.pallas_call(
        functools.partial(_mix_router_kernel, n_prompt_tiles=n_p),
        grid=(n_p + 1,),
        in_specs=p_specs + s_specs + [
            const((1, RWKV_WIDTH)), const((1, RWKV_WIDTH)), const((D_MODEL, D_MODEL)), const((1, D_MODEL)),
            const((D_MODEL, ROUTER_LANES)), const((D_MODEL, ROUTER_LANES)), const((1, ROUTER_LANES)),
            const((RWKV_WIDTH, LANE)),
            const((tm, tm))],
        out_specs=[row(D_MODEL), row(D_MODEL), row(ROUTER_LANES), const((1, ROUTER_LANES))],
        out_shape=[jax.ShapeDtypeStruct((t, D_MODEL), F32), jax.ShapeDtypeStruct((t, D_MODEL), F32),
                   jax.ShapeDtypeStruct((t, ROUTER_LANES), F32), jax.ShapeDtypeStruct((1, ROUTER_LANES), F32)],
        scratch_shapes=[pltpu.VMEM((1, ROUTER_LANES), F32)],
        compiler_params=_cparams("arbitrary"),
        name="mix_router",
    )(*prompt_rows, *sample_rows, lnw, lnb, wo, gffn, *_split2(wr), br, _head_selector(), tril)


ROW_DMA_UNROLL = 8


def _dispatch_kernel(slots_ref, x_ref, xs_hbm, sem):
    tm = x_ref.shape[0]

    def issue(r, c):
        src = x_ref.at[pl.ds(r, 1)]
        pltpu.make_async_copy(src, xs_hbm.at[pl.ds(slots_ref[0, 0, 2 * r], 1)], sem).start()
        pltpu.make_async_copy(src, xs_hbm.at[pl.ds(slots_ref[0, 0, 2 * r + 1], 1)], sem).start()
        return c

    lax.fori_loop(0, tm, issue, 0, unroll=ROW_DMA_UNROLL)
    for _ in range(2):
        pltpu.make_async_copy(x_ref, xs_hbm.at[pl.ds(0, tm)], sem).wait()


def _dispatch(n2, slots):
    t = n2.shape[0]
    tm = MOE_TM
    return pl.pallas_call(
        _dispatch_kernel,
        grid_spec=pltpu.PrefetchScalarGridSpec(
            num_scalar_prefetch=0,
            grid=(t // tm,),
            in_specs=[pl.BlockSpec((1, 1, 2 * tm), lambda i: (i, 0, 0), memory_space=pltpu.SMEM),
                      pl.BlockSpec((tm, D_MODEL), lambda i: (i, 0))],
            out_specs=pl.BlockSpec(memory_space=pl.ANY),
            scratch_shapes=[pltpu.SemaphoreType.DMA(())],
        ),
        out_shape=jax.ShapeDtypeStruct((2 * t, D_MODEL), F32),
        compiler_params=_cparams("arbitrary"),
        name="moe_dispatch",
    )(slots, n2)


def _experts_kernel(wt_ref, we_ref, wlo_ref, whi_ref, wfirst_ref, nw_ref,
                    xs_ref, w1_ref, w3_ref, w2_ref, ys_ref, wb1, wb3, wb2):
    w = pl.program_id(0)

    @pl.when(w < nw_ref[0])
    def _():
        new_expert = jnp.logical_or(w == 0, we_ref[w] != we_ref[jnp.maximum(w - 1, 0)])

        @pl.when(new_expert)
        def _():
            wb1[...] = w1_ref[0].astype(BF16)
            wb3[...] = w3_ref[0].astype(BF16)
            wb2[...] = w2_ref[0].astype(BF16)

        x = xs_ref[...].astype(BF16)
        a = _dot(x, wb1[...])
        b = _dot(x, wb3[...])
        o = _dot(((a * _sigmoid(a)) * b).astype(BF16), wb2[...])

        @pl.when(wfirst_ref[w] == 1)
        def _():
            ys_ref[...] = o

        @pl.when(wfirst_ref[w] == 0)
        def _():
            rows = lax.broadcasted_iota(jnp.int32, o.shape, 0)
            ys_ref[...] = jnp.where((rows >= wlo_ref[w]) & (rows < whi_ref[w]), o, ys_ref[...])


def _experts(xs, work, w1, w3, w2):
    s = xs.shape[0]
    ts = MOE_TS
    n_work = work[0].shape[0]
    return pl.pallas_call(
        _experts_kernel,
        grid_spec=pltpu.PrefetchScalarGridSpec(
            num_scalar_prefetch=6,
            grid=(n_work,),
            in_specs=[
                pl.BlockSpec((ts, D_MODEL), lambda w, wt, we, *_: (wt[w], 0)),
                pl.BlockSpec((1, D_MODEL, D_EXPERT), lambda w, wt, we, *_: (we[w], 0, 0)),
                pl.BlockSpec((1, D_MODEL, D_EXPERT), lambda w, wt, we, *_: (we[w], 0, 0)),
                pl.BlockSpec((1, D_EXPERT, D_MODEL), lambda w, wt, we, *_: (we[w], 0, 0)),
            ],
            out_specs=pl.BlockSpec((ts, D_MODEL), lambda w, wt, we, *_: (wt[w], 0)),
            scratch_shapes=[pltpu.VMEM((D_MODEL, D_EXPERT), BF16), pltpu.VMEM((D_MODEL, D_EXPERT), BF16),
                            pltpu.VMEM((D_EXPERT, D_MODEL), BF16)],
        ),
        out_shape=jax.ShapeDtypeStruct((s, D_MODEL), F32),
        compiler_params=_cparams("arbitrary"),
        name="moe_experts",
    )(*work, xs, w1, w3, w2)


def _expert_work_items(counts, total):
    ts = MOE_TS
    n_tiles = total // ts
    n_work = n_tiles + N_EXPERTS - 1
    offs = jnp.cumsum(counts) - counts
    t0 = (jnp.arange(n_tiles, dtype=jnp.int32) * ts)[:, None]
    lo = jnp.maximum(t0, offs[None, :])
    hi = jnp.minimum(t0 + ts, (offs + counts)[None, :])
    nonempty = (hi > lo).reshape(-1)
    nw = jnp.sum(nonempty.astype(jnp.int32))
    idx = jnp.nonzero(nonempty, size=n_work, fill_value=0)[0].astype(jnp.int32)
    idx = jnp.where(jnp.arange(n_work) < nw, idx, idx[jnp.maximum(nw - 1, 0)])
    wt = idx // N_EXPERTS
    we = idx % N_EXPERTS
    wlo = lo.reshape(-1)[idx] - wt * ts
    whi = hi.reshape(-1)[idx] - wt * ts
    wfirst = jnp.concatenate([jnp.ones((1,), jnp.int32), (wt[1:] != wt[:-1]).astype(jnp.int32)])
    return wt, we, wlo, whi, wfirst, nw.reshape(1)


def _combine_kernel(slots_ref, slots_next_ref, h_ref, route_ref, gfin_ref, ys_hbm, yp_ref, ysm_ref, gbuf, sems,
                    *, n_prompt_tiles):
    i = pl.program_id(0)
    n = pl.num_programs(0)
    tm = h_ref.shape[0]

    def gather(s_ref, buf):
        def issue(r, c):
            pltpu.make_async_copy(ys_hbm.at[pl.ds(s_ref[0, 0, 2 * r], 1)], gbuf.at[buf, 0, pl.ds(r, 1)],
                                  sems.at[buf]).start()
            pltpu.make_async_copy(ys_hbm.at[pl.ds(s_ref[0, 0, 2 * r + 1], 1)], gbuf.at[buf, 1, pl.ds(r, 1)],
                                  sems.at[buf]).start()
            return c
        lax.fori_loop(0, tm, issue, 0, unroll=ROW_DMA_UNROLL)

    cur = i % 2

    @pl.when(i == 0)
    def _():
        gather(slots_ref, 0)

    @pl.when(i + 1 < n)
    def _():
        gather(slots_next_ref, 1 - cur)

    for k in range(2):
        pltpu.make_async_copy(ys_hbm.at[pl.ds(0, tm)], gbuf.at[cur, k], sems.at[cur]).wait()
    route = route_ref[...]
    lane = lax.broadcasted_iota(jnp.int32, route.shape, 1)
    w1 = jnp.sum(jnp.where(lane == ROUTE_W1, route, 0.0), axis=1, keepdims=True)
    w2 = jnp.sum(jnp.where(lane == ROUTE_W2, route, 0.0), axis=1, keepdims=True)
    hf = h_ref[...] + (w1 * gbuf[cur, 0] + w2 * gbuf[cur, 1])
    y = hf * lax.rsqrt(jnp.mean(hf * hf, axis=-1, keepdims=True) + NORM_EPS) * gfin_ref[...]

    @pl.when(i < n_prompt_tiles)
    def _():
        yp_ref[...] = y

    @pl.when(i >= n_prompt_tiles)
    def _():
        ysm_ref[...] = y


def _combine(h, route, slots, ys, gfin, n_prompt_tiles):
    t = h.shape[0]
    tm = MOE_TM
    n_p = n_prompt_tiles
    return pl.pallas_call(
        functools.partial(_combine_kernel, n_prompt_tiles=n_p),
        grid_spec=pltpu.PrefetchScalarGridSpec(
            num_scalar_prefetch=0,
            grid=(t // tm,),
            in_specs=[pl.BlockSpec((1, 1, 2 * tm), lambda i: (i, 0, 0), memory_space=pltpu.SMEM),
                      pl.BlockSpec((1, 1, 2 * tm), lambda i: (jnp.minimum(i + 1, t // tm - 1), 0, 0),
                                   memory_space=pltpu.SMEM),
                      pl.BlockSpec((tm, D_MODEL), lambda i: (i, 0)),
                      pl.BlockSpec((tm, ROUTER_LANES), lambda i: (i, 0)),
                      pl.BlockSpec((1, D_MODEL), lambda i: (0, 0)),
                      pl.BlockSpec(memory_space=pl.ANY)],
            out_specs=[pl.BlockSpec((tm, D_MODEL), lambda i: (jnp.minimum(i, n_p - 1), 0)),
                       pl.BlockSpec((tm, D_MODEL), lambda i: (0, 0))],
            scratch_shapes=[pltpu.VMEM((2, 2, tm, D_MODEL), F32), pltpu.SemaphoreType.DMA((2,))],
        ),
        out_shape=[jax.ShapeDtypeStruct((n_p * tm, D_MODEL), F32), jax.ShapeDtypeStruct((tm, D_MODEL), F32)],
        compiler_params=_cparams("arbitrary"),
        name="moe_combine",
    )(slots, slots, h, route, gfin, ys)


def _v_tiles(v, tblk):
    b, t, _ = v.shape
    x = v.reshape(b, t // tblk, tblk, 2, RWKV_PAIRS, RWKV_HEAD).transpose(0, 1, 4, 5, 3, 2)
    x = jnp.pad(x, ((0, 0),) * 5 + ((0, RWKV_HEAD - tblk),))
    return x.reshape(b, t // tblk, RWKV_PAIRS, RWKV_HEAD, LANE).astype(BF16)


def _y_rows(y, bb, t):
    nb, tpad, _ = y.shape
    return y.reshape(nb, tpad, bb, RWKV_WIDTH).transpose(0, 2, 1, 3).reshape(nb * bb, tpad, RWKV_WIDTH)[:, :t]


def kernel(x_prompt, x_sample, state_gla, state_rwkv, state_shift, meta_tokens, norm_mix, w_in, gla_gate_w2,
           gla_gate_b, gla_norm, rwkv_mu, rwkv_w0, rwkv_w2, rwkv_a0, rwkv_a2, rwkv_g2, rwkv_kk, rwkv_ka, rwkv_rk,
           rwkv_ln_w, rwkv_ln_b, w_out, norm_ffn, router_group_w, router_group_b, router_expert_w,
           router_expert_b, moe_w1, moe_w3, moe_w2, norm_final):
    bp, tp, _ = x_prompt.shape
    bs, ts, _ = x_sample.shape
    assert state_gla.shape[0] == 1, "one layer"
    lyr = 0

    w_in_l = w_in[lyr]
    wg = jnp.pad(w_in_l[:, :GLA_COLS], ((0, 0), (0, GLA_PCOLS - GLA_COLS))).astype(BF16)
    wr = w_in_l[:, GLA_COLS:].astype(BF16)
    g_mix = norm_mix[lyr][None, :]
    gw2p = jnp.pad(gla_gate_w2[lyr], ((0, LANE - GLA_GATE_RANK), (0, 0)))
    gb = gla_gate_b[lyr][None, :]
    gn = gla_norm[lyr][None, :]
    w2p = jnp.pad(rwkv_w2[lyr], ((0, 64), (0, 0))).astype(BF16)
    a2p = jnp.pad(rwkv_a2[lyr], ((64, 0), (0, 0))).astype(BF16)
    pre_params = (rwkv_mu[lyr][None, :], rwkv_w0[lyr][None, :], w2p, rwkv_a0[lyr][None, :], a2p,
                  rwkv_g2[lyr].astype(BF16), rwkv_kk[lyr][None, :], rwkv_ka[lyr][None, :],
                  rwkv_rk[lyr].reshape(1, RWKV_WIDTH), _head_selector())
    lnw = rwkv_ln_w[lyr][None, :]
    lnb = rwkv_ln_b[lyr][None, :]
    wo = w_out[lyr].astype(BF16)
    gffn = norm_ffn[lyr][None, :]
    n_used = N_GROUPS + N_EXPERTS
    w_router = jnp.pad(
        jnp.concatenate([router_group_w[lyr],
                         router_expert_w[lyr].transpose(1, 0, 2).reshape(D_MODEL, N_EXPERTS)], axis=1),
        ((0, 0), (0, ROUTER_LANES - n_used)))
    b_router = jnp.pad(jnp.concatenate([router_group_b[lyr], router_expert_b[lyr].reshape(N_EXPERTS)]),
                       (0, ROUTER_LANES - n_used))[None, :]
    gfin = norm_final[None, :]

    pg_m, pr_m = _inproj(meta_tokens, g_mix, wg, wr, N_META)
    _, sg_m = _gla(pg_m[None], jnp.zeros((1, GLA_HEADS, GLA_DK, GLA_DV), F32), gw2p, gb, gn,
                   bb=1, chunk=N_META, sub=N_META, t_valid=N_META)
    r, w, k, kk, kka, v_m, _, _ = _rwkv_pre(pr_m[None], jnp.zeros((1, 1, RWKV_COLS), F32), pre_params,
                                            tm=N_META, explicit_prev=False, emit_vt=False)
    _, sr_m = _rwkv_rec(r, w, k, kk, kka, _v_tiles(v_m, N_META),
                        jnp.zeros((1, RWKV_HEADS, RWKV_HEAD, RWKV_HEAD), F32), bb=1, n_steps=N_META)

    xp = x_prompt.reshape(bp * tp, D_MODEL)
    pg_p, pr_p = _inproj(xp, g_mix, wg, wr, 512)
    og_p, sg_p = _gla(pg_p.reshape(bp, tp, GLA_PCOLS), jnp.broadcast_to(sg_m, (bp,) + sg_m.shape[1:]), gw2p, gb, gn,
                      bb=4, chunk=GLA_CHUNK, sub=GLA_SUB, t_valid=GLA_CHUNK)
    pr_p3 = pr_p.reshape(bp, tp, RWKV_COLS)
    first_prev = jnp.broadcast_to(pr_m[N_META - 1][None, None, :], (bp, 1, RWKV_COLS))
    r, w, k, kk, kka, vt_p, bv_p, gate_p = _rwkv_pre(pr_p3, first_prev, pre_params, tm=256, explicit_prev=False,
                                                     emit_vt=True)
    y_p, sr_p = _rwkv_rec(r, w, k, kk, kka, vt_p, jnp.broadcast_to(sr_m, (bp,) + sr_m.shape[1:]),
                          bb=bp, n_steps=REC_TB)
    prompt_rows = (xp, og_p.reshape(bp * tp, GLA_WIDTH), y_p.reshape(tp, bp * RWKV_WIDTH),
                   bv_p.reshape(bp * tp, RWKV_WIDTH), gate_p.reshape(bp * tp, RWKV_WIDTH))

    xs = x_sample.reshape(bs * ts, D_MODEL)
    pg_s, pr_s = _inproj(xs, g_mix, wg, wr, bs * ts)
    ts_pad = 8
    pg_s3 = jnp.pad(pg_s.reshape(bs, ts, GLA_PCOLS), ((0, 0), (0, ts_pad - ts), (0, 0)))
    og_s, sg_s = _gla(pg_s3, state_gla[lyr], gw2p, gb, gn, bb=8, chunk=ts_pad, sub=ts_pad, t_valid=ts)
    og_s = og_s[:, :ts]
    pr_s3 = pr_s.reshape(bs, ts, RWKV_COLS)
    prev_s = jnp.concatenate([state_shift[lyr][:, None, :], pr_s3[:, :-1]], axis=1)
    r, w, k, kk, kka, v_s, bv_s, gate_s = _rwkv_pre(pr_s3.reshape(1, bs * ts, RWKV_COLS),
                                                     prev_s.reshape(1, bs * ts, RWKV_COLS), pre_params,
                                                     tm=bs * ts, explicit_prev=True, emit_vt=False)
    unflat = lambda a: a.reshape(RWKV_PAIRS, bs, ts, LANE).transpose(1, 0, 2, 3)
    y_s, sr_s = _rwkv_rec(unflat(r), unflat(w), unflat(k), unflat(kk), unflat(kka),
                          _v_tiles(v_s.reshape(bs, ts, RWKV_WIDTH), ts), state_rwkv[lyr], bb=8, n_steps=ts)
    y_s = _y_rows(y_s, 8, ts)
    sample_rows = (xs, og_s.reshape(bs * ts, GLA_WIDTH), y_s.reshape(bs * ts, RWKV_WIDTH),
                   bv_s.reshape(bs * ts, RWKV_WIDTH), gate_s.reshape(bs * ts, RWKV_WIDTH))

    h_all, n2_all, route, counts = _mix_router(prompt_rows, sample_rows, lnw, lnb, wo, gffn, w_router, b_router,
                                               seq_tiles=tp // MOE_TM)
    n_tok = h_all.shape[0]
    n_p_tiles = (bp * tp) // MOE_TM
    cnt = counts[0, EXPERT_LANE0:EXPERT_LANE0 + N_EXPERTS].astype(jnp.int32)
    offs = jnp.cumsum(cnt) - cnt
    eid = route[:, ROUTE_E1:ROUTE_E2 + 1].astype(jnp.int32)
    pos = route[:, ROUTE_P1:ROUTE_P2 + 1].astype(jnp.int32)
    onehot = (eid[..., None] == jnp.arange(N_EXPERTS, dtype=jnp.int32)).astype(F32)
    off = jnp.einsum("tke,e->tk", onehot, offs.astype(F32), precision=HIGHEST).astype(jnp.int32)
    slots = (off + pos).reshape(n_tok // MOE_TM, 1, 2 * MOE_TM)
    xs_sorted = _dispatch(n2_all, slots)
    ys_sorted = _experts(xs_sorted, _expert_work_items(cnt, 2 * n_tok), moe_w1[lyr], moe_w3[lyr], moe_w2[lyr])
    y_prompt, y_sample = _combine(h_all, route, slots, ys_sorted, gfin, n_p_tiles)
    y_prompt = y_prompt.reshape(bp, tp, D_MODEL)
    y_sample = y_sample.reshape(bs, ts, D_MODEL)

    return (y_prompt, y_sample,
            sg_p[None], sr_p[None], pr_p3[:, -1][None],
            sg_s[None], sr_s[None], pr_s3[:, -1][None])
```

```python
import functools

import jax
import jax.numpy as jnp
from jax import lax
from jax.experimental import pallas as pl
from jax.experimental.pallas import tpu as pltpu

F32 = jnp.float32
BF16 = jnp.bfloat16
HIGHEST = lax.Precision.HIGHEST

D_MODEL = 1024
N_META = 16
NORM_EPS = 1e-6
LOG2E = 1.4426950408889634
GLA_HEADS = 4
GLA_DK = 64
GLA_DV = 128
GLA_QK = GLA_HEADS * GLA_DK
GLA_WIDTH = GLA_HEADS * GLA_DV
GLA_GATE_RANK = 16
GLA_GATE_NORM = 16.0
GLA_CHUNK = 64
GLA_SUB = 16
GLA_COLS = 2 * GLA_QK + 2 * GLA_WIDTH + GLA_GATE_RANK
GLA_PCOLS = 2 * GLA_QK + 2 * GLA_WIDTH + 128
RWKV_WIDTH = 512
RWKV_HEAD = 64
RWKV_HEADS = 8
RWKV_PAIRS = RWKV_HEADS // 2
RWKV_DECAY_SCALE = 0.606531
RWKV_GN_EPS = 64e-5
RWKV_COLS = 3 * RWKV_WIDTH + 64 + 64 + 128
REC_TB = 64
REC_UNROLL = 8
N_GROUPS = 4
EXPERTS_PER_GROUP = 8
N_EXPERTS = 32
D_EXPERT = 512
ROUTER_LANES = 128
EXPERT_LANE0 = N_GROUPS
ROUTE_E1, ROUTE_E2, ROUTE_W1, ROUTE_W2, ROUTE_P1, ROUTE_P2 = range(6)
MOE_TM = 512
MOE_TS = 512

LANE = 128
VMEM_LIMIT = 56 * 1024 * 1024


def _cparams(*sem):
    return pltpu.CompilerParams(dimension_semantics=sem, vmem_limit_bytes=VMEM_LIMIT)


def _block_ones(n, blk):
    i = jnp.arange(n)
    return (i[:, None] // blk == i[None, :] // blk).astype(BF16)


def _sigmoid(x):
    return 1.0 / (1.0 + jnp.exp(-x))


def _dot(a, b):
    return jnp.dot(a, b, preferred_element_type=F32)


def _dot_nt(a, b):
    return lax.dot_general(a, b, (((1,), (1,)), ((), ())), preferred_element_type=F32)


def _dot_tn(a, b):
    return lax.dot_general(a, b, (((0,), (0,)), ((), ())), preferred_element_type=F32)


def _split2(x):
    hi = x.astype(BF16)
    lo = (x - hi.astype(F32)).astype(BF16)
    return hi, lo


def _head_selector():
    return (jnp.arange(RWKV_WIDTH)[:, None] // RWKV_HEAD == jnp.arange(LANE)[None, :]).astype(BF16)


def _group_sum(x, sel):
    hi, lo = _split2(x)
    s_hi, s_lo = _split2(_dot(hi, sel) + _dot(lo, sel))
    return _dot_nt(s_hi, sel) + _dot_nt(s_lo, sel)


def _inproj_kernel(x_ref, g_ref, wg_ref, wr_ref, pg_ref, pr_ref):
    x = x_ref[...]
    n = x * lax.rsqrt(jnp.mean(x * x, axis=-1, keepdims=True) + NORM_EPS) * g_ref[...]
    nb = n.astype(BF16)
    pg_ref[...] = _dot(nb, wg_ref[...])
    pr_ref[...] = _dot(nb, wr_ref[...])


def _inproj(x, g, wg, wr, tm):
    t = x.shape[0]
    return pl.pallas_call(
        _inproj_kernel,
        grid=(t // tm,),
        in_specs=[
            pl.BlockSpec((tm, D_MODEL), lambda i: (i, 0)),
            pl.BlockSpec((1, D_MODEL), lambda i: (0, 0)),
            pl.BlockSpec((D_MODEL, GLA_PCOLS), lambda i: (0, 0)),
            pl.BlockSpec((D_MODEL, RWKV_COLS), lambda i: (0, 0)),
        ],
        out_specs=[
            pl.BlockSpec((tm, GLA_PCOLS), lambda i: (i, 0)),
            pl.BlockSpec((tm, RWKV_COLS), lambda i: (i, 0)),
        ],
        out_shape=[jax.ShapeDtypeStruct((t, GLA_PCOLS), F32), jax.ShapeDtypeStruct((t, RWKV_COLS), F32)],
        compiler_params=_cparams("parallel"),
        name="inproj",
    )(x, g, wg, wr)


def _gla_kernel(pg_ref, s0_ref, gw2_ref, gb_ref, gn_ref, bo_ref, tril_ref, o_ref, sout_ref, s_scr,
                *, bb, chunk, sub, t_valid):
    ci = pl.program_id(1)

    @pl.when(ci == 0)
    def _():
        s_scr[...] = s0_ref[...]

    bo = bo_ref[...]
    tril = tril_ref[...]
    lane = lax.broadcasted_iota(jnp.int32, (sub, LANE), 1) & (GLA_DK - 1)
    rowi = lax.broadcasted_iota(jnp.int32, (sub, LANE), 0)
    head0_s = lax.broadcasted_iota(jnp.int32, (sub, LANE), 1) < GLA_DK
    head0_c = lax.broadcasted_iota(jnp.int32, (chunk, LANE), 1) < GLA_DK

    for bi in range(bb):
        pg = pg_ref[bi]
        q = pg[:, 0:GLA_QK] * (GLA_DK ** -0.5)
        k = pg[:, GLA_QK:2 * GLA_QK]
        v = pg[:, 2 * GLA_QK:2 * GLA_QK + GLA_WIDTH]
        g = pg[:, 2 * GLA_QK + GLA_WIDTH:2 * GLA_QK + 2 * GLA_WIDTH]
        gl = pg[:, 2 * GLA_QK + 2 * GLA_WIDTH:]
        z = jnp.dot(gl, gw2_ref[...], precision=HIGHEST, preferred_element_type=F32) + gb_ref[...]
        lg = (jnp.minimum(z, 0.0) - jnp.log1p(jnp.exp(-jnp.abs(z)))) * (LOG2E / GLA_GATE_NORM)
        if t_valid < chunk:
            rows = lax.broadcasted_iota(jnp.int32, lg.shape, 0)
            lg = jnp.where(rows < t_valid, lg, 0.0)
        b = jnp.dot(tril, lg, precision=HIGHEST, preferred_element_type=F32)
        eb = jnp.exp2(b)
        blast = b[chunk - 1:chunk, :]
        kl = k * jnp.exp2(blast - b)
        qe = q * eb

        n_blk = chunk // sub
        n_pairs = GLA_HEADS // 2
        ps = []
        for hp in range(n_pairs):
            sl = slice(hp * LANE, (hp + 1) * LANE)
            for blk in range(n_blk):
                rs = slice(blk * sub, (blk + 1) * sub)
                qb, kb, bbk = q[rs, sl], k[rs, sl], b[rs, sl]
                for j in range(sub):
                    ps.append(qb * (kb[j:j + 1] * jnp.exp2(jnp.minimum(bbk - bbk[j:j + 1], 0.0))))
        red = _dot(jnp.concatenate(ps, axis=0).astype(BF16), bo)

        o_heads = []
        for hp in range(n_pairs):
            sl = slice(hp * LANE, (hp + 1) * LANE)
            kp, bp = k[:, sl], b[:, sl]
            row_blocks = []
            for blk in range(n_blk):
                rs = slice(blk * sub, (blk + 1) * sub)
                base = (hp * n_blk + blk) * sub * sub
                a = jnp.zeros((sub, LANE), F32)
                for j in range(sub):
                    a = jnp.where((lane == blk * sub + j) & (rowi >= j), red[base + j * sub:base + (j + 1) * sub], a)
                if blk > 0:
                    bref = bp[blk * sub - 1:blk * sub]
                    qt = q[rs, sl] * jnp.exp2(bp[rs] - bref)
                    kt = (kp * jnp.exp2(jnp.minimum(bref - bp, 0.0))).astype(BF16)
                    qt2 = jnp.concatenate([jnp.where(head0_s, qt, 0.0), jnp.where(head0_s, 0.0, qt)], axis=0)
                    off2 = _dot_nt(qt2.astype(BF16), kt)
                    a = jnp.where(lane < blk * sub, jnp.concatenate([off2[:sub], off2[sub:]], axis=1), a)
                row_blocks.append(a)
            a_pair = row_blocks[0] if n_blk == 1 else jnp.concatenate(row_blocks, axis=0)
            v0 = v[:, 2 * hp * GLA_DV:(2 * hp + 1) * GLA_DV]
            v1 = v[:, (2 * hp + 1) * GLA_DV:(2 * hp + 2) * GLA_DV]
            s_pair = s_scr[bi, 2 * hp:2 * hp + 2].reshape(2 * GLA_DK, GLA_DV)
            qe_p, kl_p = qe[:, sl], kl[:, sl]

            def by_head(x):
                return jnp.concatenate([jnp.where(head0_c, x, 0.0), jnp.where(head0_c, 0.0, x)], axis=0)

            if chunk == GLA_DK:
                v_rows = jnp.concatenate([v0, v1], axis=0)
            else:
                zpad = jnp.zeros((GLA_DK - chunk, GLA_DV), F32)
                v_rows = jnp.concatenate([v0, zpad, v1, zpad], axis=0)
            lhs = jnp.concatenate([by_head(a_pair), by_head(qe_p)], axis=1).astype(BF16)
            rhs = jnp.concatenate([v_rows, s_pair], axis=0).astype(BF16)
            o2 = _dot(lhs, rhs)
            upd = _dot_tn(by_head(kl_p).astype(BF16), jnp.concatenate([v0, v1], axis=0).astype(BF16))
            dcol = jnp.broadcast_to(jnp.exp2(blast[:, sl]), (8, LANE)).T[:, 0:1]
            s_new = dcol * s_pair + upd
            s_scr[bi, 2 * hp] = s_new[:GLA_DK]
            s_scr[bi, 2 * hp + 1] = s_new[GLA_DK:]
            for h2 in range(2):
                o_h = o2[h2 * chunk:(h2 + 1) * chunk]
                o_heads.append(o_h * lax.rsqrt(jnp.mean(o_h * o_h, axis=-1, keepdims=True) + NORM_EPS) * gn_ref[...])
        o = jnp.concatenate(o_heads, axis=1)
        o_ref[bi] = o * (g * _sigmoid(g))

    @pl.when(ci == pl.num_programs(1) - 1)
    def _():
        sout_ref[...] = s_scr[...]


def _gla(pg, s0, gw2p, gb, gn, *, bb, chunk, sub, t_valid):
    b, t, _ = pg.shape
    tril = jnp.tril(jnp.ones((chunk, chunk), F32))
    kern = functools.partial(_gla_kernel, bb=bb, chunk=chunk, sub=sub, t_valid=t_valid)
    return pl.pallas_call(
        kern,
        grid=(b // bb, t // chunk),
        in_specs=[
            pl.BlockSpec((bb, chunk, GLA_PCOLS), lambda i, j: (i, j, 0)),
            pl.BlockSpec((bb, GLA_HEADS, GLA_DK, GLA_DV), lambda i, j: (i, 0, 0, 0)),
            pl.BlockSpec((LANE, GLA_QK), lambda i, j: (0, 0)),
            pl.BlockSpec((1, GLA_QK), lambda i, j: (0, 0)),
            pl.BlockSpec((1, GLA_DV), lambda i, j: (0, 0)),
            pl.BlockSpec((LANE, LANE), lambda i, j: (0, 0)),
            pl.BlockSpec((chunk, chunk), lambda i, j: (0, 0)),
        ],
        out_specs=[
            pl.BlockSpec((bb, chunk, GLA_WIDTH), lambda i, j: (i, j, 0)),
            pl.BlockSpec((bb, GLA_HEADS, GLA_DK, GLA_DV), lambda i, j: (i, 0, 0, 0)),
        ],
        out_shape=[jax.ShapeDtypeStruct((b, t, GLA_WIDTH), F32),
                   jax.ShapeDtypeStruct((b, GLA_HEADS, GLA_DK, GLA_DV), F32)],
        scratch_shapes=[pltpu.VMEM((bb, GLA_HEADS, GLA_DK, GLA_DV), F32)],
        compiler_params=_cparams("parallel", "arbitrary"),
        name="gla_chunk",
    )(pg, s0, gw2p, gb, gn, _block_ones(LANE, GLA_DK), tril)


def _shifted_rows(pr, row0):
    rows = lax.broadcasted_iota(jnp.int32, pr.shape, 0)
    return jnp.where(rows == 0, row0, pltpu.roll(pr, 1, 0))


def _rwkv_pre_math(pr, prev, params, outs, rs, tb0, emit_vt):
    mu_ref, w0_ref, w2_ref, a0_ref, a2_ref, g2_ref, kk_ref, ka_ref, rk_ref, bo_ref = params
    r_out, w_out, k_out, kkn_out, kka_out, v_out, bv_out, gate_out = outs
    n_rows = pr.shape[0]
    xm = pr + (prev - pr) * mu_ref[...]
    wd = RWKV_WIDTH
    rr, rk, rv = xm[:, 0:wd], xm[:, wd:2 * wd], xm[:, 2 * wd:3 * wd]
    wa = xm[:, 3 * wd:3 * wd + LANE]
    gl2 = xm[:, 3 * wd + LANE:3 * wd + 2 * LANE]
    logw = -RWKV_DECAY_SCALE * _sigmoid(w0_ref[...] + _dot(jnp.tanh(wa).astype(BF16), w2_ref[...]))
    aa = _sigmoid(a0_ref[...] + _dot(wa.astype(BF16), a2_ref[...]))
    gate = _dot(_sigmoid(gl2).astype(BF16), g2_ref[...])
    bo = bo_ref[...]
    kk = rk * kk_ref[...]
    kk = kk / jnp.maximum(jnp.sqrt(_group_sum(kk * kk, bo)), 1e-12)
    k = rk * (1.0 + (aa - 1.0) * ka_ref[...])
    bv = _group_sum(rr * k * rk_ref[...], bo) * rv
    w = jnp.exp(logw)
    kka = kk * aa
    hd = RWKV_HEAD

    def pair(x, hp):
        return jnp.concatenate([x[:, hp * hd:(hp + 1) * hd], x[:, (hp + RWKV_PAIRS) * hd:(hp + RWKV_PAIRS + 1) * hd]],
                               axis=1)

    for hp in range(RWKV_PAIRS):
        r_out[0, hp, rs, :] = pair(rr, hp)
        w_out[0, hp, rs, :] = pair(w, hp)
        k_out[0, hp, rs, :] = pair(k, hp)
        kkn_out[0, hp, rs, :] = pair(kk, hp)
        kka_out[0, hp, rs, :] = pair(kka, hp)
    if emit_vt:
        vt = rv.T
        for tb in range(n_rows // REC_TB):
            ts = slice(tb * REC_TB, (tb + 1) * REC_TB)
            for hp in range(RWKV_PAIRS):
                lo, hi = hp * hd, (hp + RWKV_PAIRS) * hd
                v_out[0, tb0 + tb, hp] = jnp.concatenate([vt[lo:lo + hd, ts], vt[hi:hi + hd, ts]],
                                                         axis=1).astype(BF16)
    else:
        v_out[0, rs, :] = rv
    bv_out[0, rs, :] = bv
    gate_out[0, rs, :] = gate


def _rwkv_pre_kernel(pr_ref, aux_ref, *rest, tm, explicit_prev, emit_vt):
    params, outs, carry_scr = rest[:10], rest[10:18], rest[18]
    pr = pr_ref[0]
    if explicit_prev:
        prev = aux_ref[0]
    else:
        j = pl.program_id(1)
        prev = _shifted_rows(pr, jnp.where(j == 0, aux_ref[0], carry_scr[...]))
        carry_scr[...] = pr[tm - 1:tm, :]
    _rwkv_pre_math(pr, prev, params, outs, slice(0, tm), 0, emit_vt)


def _inproj_pre_kernel(x_ref, g_ref, wg_ref, wr_ref, first_ref, *rest, tm, sub):
    params, pg_out, outs, last_out, carry_scr = rest[:10], rest[10], rest[11:19], rest[19], rest[20]
    j = pl.program_id(1)
    prev_last = jnp.where(j == 0, first_ref[0], carry_scr[...])
    for h in range(tm // sub):
        rs = slice(h * sub, (h + 1) * sub)
        x = x_ref[0, rs, :]
        n = x * lax.rsqrt(jnp.mean(x * x, axis=-1, keepdims=True) + NORM_EPS) * g_ref[...]
        nb = n.astype(BF16)
        pg_out[0, rs, :] = _dot(nb, wg_ref[...])
        pr = _dot(nb, wr_ref[...])
        prev = _shifted_rows(pr, prev_last)
        prev_last = pr[sub - 1:sub, :]
        _rwkv_pre_math(pr, prev, params, outs, rs, h * (sub // REC_TB), True)
    carry_scr[...] = prev_last
    last_out[0] = prev_last


def _rwkv_pre(pr, aux, params, *, tm, explicit_prev, emit_vt):
    b, t, _ = pr.shape
    kern = functools.partial(_rwkv_pre_kernel, tm=tm, explicit_prev=explicit_prev, emit_vt=emit_vt)
    aux_spec = (pl.BlockSpec((1, tm, RWKV_COLS), lambda i, j: (i, j, 0)) if explicit_prev
                else pl.BlockSpec((1, 1, RWKV_COLS), lambda i, j: (i, 0, 0)))
    const = lambda shape: pl.BlockSpec(shape, lambda i, j: (0,) * len(shape))
    pair_spec = pl.BlockSpec((1, RWKV_PAIRS, tm, LANE), lambda i, j: (i, 0, j, 0))
    row_spec = pl.BlockSpec((1, tm, RWKV_WIDTH), lambda i, j: (i, j, 0))
    pair_shape = jax.ShapeDtypeStruct((b, RWKV_PAIRS, t, LANE), F32)
    row_shape = jax.ShapeDtypeStruct((b, t, RWKV_WIDTH), F32)
    if emit_vt:
        v_spec = pl.BlockSpec((1, tm // REC_TB, RWKV_PAIRS, RWKV_HEAD, LANE), lambda i, j: (i, j, 0, 0, 0))
        v_shape = jax.ShapeDtypeStruct((b, t // REC_TB, RWKV_PAIRS, RWKV_HEAD, LANE), BF16)
    else:
        v_spec, v_shape = row_spec, row_shape
    return pl.pallas_call(
        kern,
        grid=(b, t // tm),
        in_specs=[
            pl.BlockSpec((1, tm, RWKV_COLS), lambda i, j: (i, j, 0)),
            aux_spec,
            const((1, RWKV_COLS)), const((1, RWKV_WIDTH)), const((LANE, RWKV_WIDTH)), const((1, RWKV_WIDTH)),
            const((LANE, RWKV_WIDTH)), const((LANE, RWKV_WIDTH)), const((1, RWKV_WIDTH)), const((1, RWKV_WIDTH)),
            const((1, RWKV_WIDTH)), const((RWKV_WIDTH, LANE)),
        ],
        out_specs=[pair_spec] * 5 + [v_spec, row_spec, row_spec],
        out_shape=[pair_shape] * 5 + [v_shape, row_shape, row_shape],
        scratch_shapes=[pltpu.VMEM((1, RWKV_COLS), F32)],
        compiler_params=_cparams("parallel", "arbitrary"),
        name="rwkv_pre",
    )(pr, aux, *params)


def _inproj_pre(x, g, wg, wr, first_prev, params, *, tm, sub):
    b, t, _ = x.shape
    const = lambda shape: pl.BlockSpec(shape, lambda i, j: (0,) * len(shape))
    pair_spec = pl.BlockSpec((1, RWKV_PAIRS, tm, LANE), lambda i, j: (i, 0, j, 0))
    row_spec = pl.BlockSpec((1, tm, RWKV_WIDTH), lambda i, j: (i, j, 0))
    pair_shape = jax.ShapeDtypeStruct((b, RWKV_PAIRS, t, LANE), F32)
    row_shape = jax.ShapeDtypeStruct((b, t, RWKV_WIDTH), F32)
    return pl.pallas_call(
        functools.partial(_inproj_pre_kernel, tm=tm, sub=sub),
        grid=(b, t // tm),
        in_specs=[
            pl.BlockSpec((1, tm, D_MODEL), lambda i, j: (i, j, 0)),
            const((1, D_MODEL)), const((D_MODEL, GLA_PCOLS)), const((D_MODEL, RWKV_COLS)),
            pl.BlockSpec((1, 1, RWKV_COLS), lambda i, j: (i, 0, 0)),
            const((1, RWKV_COLS)), const((1, RWKV_WIDTH)), const((LANE, RWKV_WIDTH)), const((1, RWKV_WIDTH)),
            const((LANE, RWKV_WIDTH)), const((LANE, RWKV_WIDTH)), const((1, RWKV_WIDTH)), const((1, RWKV_WIDTH)),
            const((1, RWKV_WIDTH)), const((RWKV_WIDTH, LANE)),
        ],
        out_specs=[pl.BlockSpec((1, tm, GLA_PCOLS), lambda i, j: (i, j, 0))] + [pair_spec] * 5 + [
            pl.BlockSpec((1, tm // REC_TB, RWKV_PAIRS, RWKV_HEAD, LANE), lambda i, j: (i, j, 0, 0, 0)),
            row_spec, row_spec,
            pl.BlockSpec((1, 1, RWKV_COLS), lambda i, j: (i, 0, 0))],
        out_shape=[jax.ShapeDtypeStruct((b, t, GLA_PCOLS), F32)] + [pair_shape] * 5 + [
            jax.ShapeDtypeStruct((b, t // REC_TB, RWKV_PAIRS, RWKV_HEAD, LANE), BF16),
            row_shape, row_shape,
            jax.ShapeDtypeStruct((b, 1, RWKV_COLS), F32)],
        scratch_shapes=[pltpu.VMEM((1, RWKV_COLS), F32)],
        compiler_params=_cparams("parallel", "arbitrary"),
        name="inproj_pre",
    )(x, g, wg, wr, first_prev, *params)


def _rwkv_rec_kernel(r_ref, w_ref, k_ref, kk_ref, kka_ref, vt_ref, s0_ref, bo_ref, vsel_ref, ysel_ref,
                     y_ref, sout_ref, s_scr, t1_scr, t3_scr, yt_scr, *, bb, n_steps):
    tb = pl.program_id(1)
    nc = bb * RWKV_PAIRS
    hd = RWKV_HEAD

    @pl.when(tb == 0)
    def _():
        for c in range(nc):
            bi, hp = divmod(c, RWKV_PAIRS)
            s_scr[c] = jnp.concatenate([s0_ref[bi, hp], s0_ref[bi, hp + RWKV_PAIRS]], axis=1)

    bo = bo_ref[...]

    def step(t, u):
        row = pl.ds(t, 1)
        for c in range(nc):
            bi, hp = divmod(c, RWKV_PAIRS)
            t1_scr[c * hd:(c + 1) * hd, :] = (s_scr[c] * kk_ref[bi, hp, row, :]).astype(BF16)
        sab = _dot(t1_scr[...], bo)
        vb = _dot(vt_ref[...].reshape(nc * hd, LANE), vsel_ref[t])
        for c in range(nc):
            bi, hp = divmod(c, RWKV_PAIRS)
            rs = slice(c * hd, (c + 1) * hd)
            s2 = (s_scr[c] * w_ref[bi, hp, row, :] - sab[rs] * kka_ref[bi, hp, row, :]
                  + vb[rs] * k_ref[bi, hp, row, :])
            s_scr[c] = s2
            t3_scr[rs, :] = (s2 * r_ref[bi, hp, row, :]).astype(BF16)
        yt_scr[...] += _dot_nt(ysel_ref[u], t3_scr[...])

    n_inner = min(8, n_steps)

    def block8(t8, carry):
        yt_scr[...] = jnp.zeros(yt_scr.shape, F32)

        def inner(u, c2):
            step(t8 * 8 + u, u)
            return c2

        lax.fori_loop(0, n_inner, inner, 0, unroll=REC_UNROLL)
        t0 = pl.multiple_of(t8 * 8, 8)
        blk = RWKV_PAIRS * hd
        for bi in range(bb):
            for h2 in range(2):
                y_ref[0, pl.ds(t0, 8), bi * RWKV_WIDTH + h2 * blk:bi * RWKV_WIDTH + (h2 + 1) * blk] = (
                    yt_scr[h2 * 8:(h2 + 1) * 8, bi * blk:(bi + 1) * blk])
        return carry

    lax.fori_loop(0, (n_steps + 7) // 8, block8, 0)

    @pl.when(tb == pl.num_programs(1) - 1)
    def _():
        for c in range(nc):
            bi, hp = divmod(c, RWKV_PAIRS)
            s_c = s_scr[c]
            sout_ref[bi, hp] = s_c[:, :RWKV_HEAD]
            sout_ref[bi, hp + RWKV_PAIRS] = s_c[:, RWKV_HEAD:]


def _rwkv_rec(r, w, k, kk, kka, vt, s0, *, bb, n_steps):
    b, _, t, _ = r.shape
    tblk = min(REC_TB, t)
    ntb = t // tblk
    nc = bb * RWKV_PAIRS
    lane = jnp.arange(LANE)
    vsel = ((lane[None, :, None] // RWKV_HEAD == lane[None, None, :] // RWKV_HEAD)
            & (lane[None, :, None] % RWKV_HEAD == jnp.arange(RWKV_HEAD)[:, None, None])).astype(BF16)
    ysel = (jnp.arange(16)[None, :, None]
            == 8 * (lane[None, None, :] // RWKV_HEAD) + jnp.arange(8)[:, None, None]).astype(BF16)
    kern = functools.partial(_rwkv_rec_kernel, bb=bb, n_steps=n_steps)
    pair_spec = pl.BlockSpec((bb, RWKV_PAIRS, tblk, LANE), lambda i, j: (i, 0, j, 0))
    state_spec = pl.BlockSpec((bb, RWKV_HEADS, RWKV_HEAD, RWKV_HEAD), lambda i, j: (i, 0, 0, 0))
    ytb = max(tblk, 8)
    return pl.pallas_call(
        kern,
        grid=(b // bb, ntb),
        in_specs=[pair_spec] * 5 + [
            pl.BlockSpec((bb, 1, RWKV_PAIRS, RWKV_HEAD, LANE), lambda i, j: (i, j, 0, 0, 0)),
            state_spec,
            pl.BlockSpec((LANE, LANE), lambda i, j: (0, 0)),
            pl.BlockSpec((RWKV_HEAD, LANE, LANE), lambda i, j: (0, 0, 0)),
            pl.BlockSpec((8, 16, LANE), lambda i, j: (0, 0, 0)),
        ],
        out_specs=[
            pl.BlockSpec((1, ytb, bb * RWKV_WIDTH), lambda i, j: (i, j, 0)),
            state_spec,
        ],
        out_shape=[jax.ShapeDtypeStruct((b // bb, ntb * ytb, bb * RWKV_WIDTH), F32),
                   jax.ShapeDtypeStruct((b, RWKV_HEADS, RWKV_HEAD, RWKV_HEAD), F32)],
        scratch_shapes=[pltpu.VMEM((nc, RWKV_HEAD, LANE), F32),
                        pltpu.VMEM((nc * RWKV_HEAD, LANE), BF16),
                        pltpu.VMEM((nc * RWKV_HEAD, LANE), BF16),
                        pltpu.VMEM((16, nc * RWKV_HEAD), F32)],
        compiler_params=_cparams("parallel", "arbitrary"),
        name="rwkv_rec",
    )(r, w, k, kk, kka, vt, s0, _block_ones(LANE, RWKV_HEAD), vsel, ysel)


def _mix_router_body(x_ref, og_ref, y_ref, bv_ref, gate_ref, lnw_ref, lnb_ref, wo_ref, gffn_ref, wr_hi_ref, wr_lo_ref,
                     br_ref, bo_ref, tril_ref, h_ref, n2_ref, route_ref, cnt_scr):
    bo = bo_ref[...]
    y = y_ref[...]
    inv_n = 1.0 / RWKV_HEAD
    d = y - _group_sum(y, bo) * inv_n
    var = _group_sum(d * d, bo) * inv_n
    yn = d * lax.rsqrt(var + RWKV_GN_EPS) * lnw_ref[...] + lnb_ref[...] + bv_ref[...]
    o_rwkv = yn * gate_ref[...]
    mix = (_dot(og_ref[...].astype(BF16), wo_ref[0:GLA_WIDTH, :])
           + _dot(o_rwkv.astype(BF16), wo_ref[GLA_WIDTH:, :]))
    h = x_ref[...] + mix
    h_ref[...] = h
    n2 = h * lax.rsqrt(jnp.mean(h * h, axis=-1, keepdims=True) + NORM_EPS) * gffn_ref[...]
    n2_ref[...] = n2
    n2_hi, n2_lo = _split2(n2)
    lg = (_dot(n2_hi, wr_hi_ref[...]) + _dot(n2_hi, wr_lo_ref[...]) + _dot(n2_lo, wr_hi_ref[...])) + br_ref[...]
    neg = jnp.float32(-3.0e38)
    big = jnp.float32(1.0e9)
    lane = lax.broadcasted_iota(jnp.int32, lg.shape, 1).astype(F32)
    gmask = lane < N_GROUPS
    gmax = jnp.max(jnp.where(gmask, lg, neg), axis=1, keepdims=True)
    p_top = 1.0 / jnp.sum(jnp.where(gmask, jnp.exp(jnp.minimum(lg - gmax, 0.0)), 0.0), axis=1, keepdims=True)
    gidx = jnp.min(jnp.where(gmask & (lg == gmax), lane, big), axis=1, keepdims=True)
    e_lo = EXPERT_LANE0 + gidx * EXPERTS_PER_GROUP
    emask = (lane >= e_lo) & (lane < e_lo + EXPERTS_PER_GROUP)
    m1 = jnp.max(jnp.where(emask, lg, neg), axis=1, keepdims=True)
    e1 = jnp.min(jnp.where(emask & (lg == m1), lane, big), axis=1, keepdims=True)
    emask2 = emask & (lane != e1)
    m2 = jnp.max(jnp.where(emask2, lg, neg), axis=1, keepdims=True)
    e2 = jnp.min(jnp.where(emask2 & (lg == m2), lane, big), axis=1, keepdims=True)
    r21 = jnp.exp(m2 - m1)
    w1 = p_top / (1.0 + r21)
    w2 = p_top * r21 / (1.0 + r21)
    o1 = lane == e1
    o2 = lane == e2
    onehot = jnp.where(o1 | o2, 1.0, 0.0)
    rank = _dot(tril_ref[...], onehot.astype(BF16)) + cnt_scr[...]
    pos1 = jnp.sum(jnp.where(o1, rank, 0.0), axis=1, keepdims=True)
    pos2 = jnp.sum(jnp.where(o2, rank, 0.0), axis=1, keepdims=True)
    cnt_scr[...] += jnp.sum(onehot, axis=0, keepdims=True)
    route = jnp.where(lane == ROUTE_E1, e1 - EXPERT_LANE0, 0.0)
    route = jnp.where(lane == ROUTE_E2, e2 - EXPERT_LANE0, route)
    route = jnp.where(lane == ROUTE_W1, w1, route)
    route = jnp.where(lane == ROUTE_W2, w2, route)
    route = jnp.where(lane == ROUTE_P1, pos1, route)
    route_ref[...] = jnp.where(lane == ROUTE_P2, pos2, route)


def _mix_router_kernel(*refs, n_prompt_tiles):
    prompt_rows, sample_rows, rest = refs[0:5], refs[5:10], refs[10:]
    consts, (h_ref, n2_ref, route_ref, cnt_ref, cnt_scr) = rest[:9], rest[9:]
    i = pl.program_id(0)

    @pl.when(i == 0)
    def _():
        cnt_scr[...] = jnp.zeros(cnt_scr.shape, F32)

    @pl.when(i < n_prompt_tiles)
    def _():
        _mix_router_body(*prompt_rows, *consts, h_ref, n2_ref, route_ref, cnt_scr)

    @pl.when(i >= n_prompt_tiles)
    def _():
        _mix_router_body(*sample_rows, *consts, h_ref, n2_ref, route_ref, cnt_scr)

    cnt_ref[...] = cnt_scr[...]


def _mix_router(prompt_rows, sample_rows, lnw, lnb, wo, gffn, wr, br, *, seq_tiles):
    tm = MOE_TM
    n_p = prompt_rows[0].shape[0] // tm
    assert sample_rows[0].shape[0] == tm
    t = (n_p + 1) * tm
    widths = (D_MODEL, GLA_WIDTH, RWKV_WIDTH, RWKV_WIDTH, RWKV_WIDTH)
    p_specs = [pl.BlockSpec((tm, n), lambda i: (jnp.minimum(i, n_p - 1), 0)) for n in widths]
    p_specs[2] = pl.BlockSpec(
        (tm, RWKV_WIDTH), lambda i: (jnp.minimum(i, n_p - 1) % seq_tiles, jnp.minimum(i, n_p - 1) // seq_tiles))
    s_specs = [pl.BlockSpec((tm, n), lambda i: (0, 0)) for n in widths]
    const = lambda shape: pl.BlockSpec(shape, lambda i: (0,) * len(shape))
    row = lambda n: pl.BlockSpec((tm, n), lambda i: (i, 0))
    tril = jnp.tril(jnp.ones((tm, tm), F32), -1).astype(BF16)
    return pl.pallas_call(
        functools.partial(_mix_router_kernel, n_prompt_tiles=n_p),
        grid=(n_p + 1,),
        in_specs=p_specs + s_specs + [
            const((1, RWKV_WIDTH)), const((1, RWKV_WIDTH)), const((D_MODEL, D_MODEL)), const((1, D_MODEL)),
            const((D_MODEL, ROUTER_LANES)), const((D_MODEL, ROUTER_LANES)), const((1, ROUTER_LANES)),
            const((RWKV_WIDTH, LANE)),
            const((tm, tm))],
        out_specs=[row(D_MODEL), row(D_MODEL), row(ROUTER_LANES), const((1, ROUTER_LANES))],
        out_shape=[jax.ShapeDtypeStruct((t, D_MODEL), F32), jax.ShapeDtypeStruct((t, D_MODEL), F32),
                   jax.ShapeDtypeStruct((t, ROUTER_LANES), F32), jax.ShapeDtypeStruct((1, ROUTER_LANES), F32)],
        scratch_shapes=[pltpu.VMEM((1, ROUTER_LANES), F32)],
        compiler_params=_cparams("arbitrary"),
        name="mix_router",
    )(*prompt_rows, *sample_rows, lnw, lnb, wo, gffn, *_split2(wr), br, _head_selector(), tril)


ROW_DMA_UNROLL = 8


def _dispatch_kernel(slots_ref, x_ref, xs_hbm, sem):
    tm = x_ref.shape[0]

    def issue(r, c):
        src = x_ref.at[pl.ds(r, 1)]
        pltpu.make_async_copy(src, xs_hbm.at[pl.ds(slots_ref[0, 0, 2 * r], 1)], sem).start()
        pltpu.make_async_copy(src, xs_hbm.at[pl.ds(slots_ref[0, 0, 2 * r + 1], 1)], sem).start()
        return c

    lax.fori_loop(0, tm, issue, 0, unroll=ROW_DMA_UNROLL)
    for _ in range(2):
        pltpu.make_async_copy(x_ref, xs_hbm.at[pl.ds(0, tm)], sem).wait()


def _dispatch(n2, slots):
    t = n2.shape[0]
    tm = MOE_TM
    return pl.pallas_call(
        _dispatch_kernel,
        grid_spec=pltpu.PrefetchScalarGridSpec(
            num_scalar_prefetch=0,
            grid=(t // tm,),
            in_specs=[pl.BlockSpec((1, 1, 2 * tm), lambda i: (i, 0, 0), memory_space=pltpu.SMEM),
                      pl.BlockSpec((tm, D_MODEL), lambda i: (i, 0))],
            out_specs=pl.BlockSpec(memory_space=pl.ANY),
            scratch_shapes=[pltpu.SemaphoreType.DMA(())],
        ),
        out_shape=jax.ShapeDtypeStruct((2 * t, D_MODEL), F32),
        compiler_params=_cparams("arbitrary"),
        name="moe_dispatch",
    )(slots, n2)


def _experts_kernel(wt_ref, we_ref, wlo_ref, whi_ref, wfirst_ref, nw_ref,
                    xs_ref, w1_ref, w3_ref, w2_ref, ys_ref, wb1, wb3, wb2):
    w = pl.program_id(0)

    @pl.when(w < nw_ref[0])
    def _():
        new_expert = jnp.logical_or(w == 0, we_ref[w] != we_ref[jnp.maximum(w - 1, 0)])

        @pl.when(new_expert)
        def _():
            wb1[...] = w1_ref[0].astype(BF16)
            wb3[...] = w3_ref[0].astype(BF16)
            wb2[...] = w2_ref[0].astype(BF16)

        x = xs_ref[...].astype(BF16)
        a = _dot(x, wb1[...])
        b = _dot(x, wb3[...])
        o = _dot(((a * _sigmoid(a)) * b).astype(BF16), wb2[...])

        @pl.when(wfirst_ref[w] == 1)
        def _():
            ys_ref[...] = o

        @pl.when(wfirst_ref[w] == 0)
        def _():
            rows = lax.broadcasted_iota(jnp.int32, o.shape, 0)
            ys_ref[...] = jnp.where((rows >= wlo_ref[w]) & (rows < whi_ref[w]), o, ys_ref[...])


def _experts(xs, work, w1, w3, w2):
    s = xs.shape[0]
    ts = MOE_TS
    n_work = work[0].shape[0]
    return pl.pallas_call(
        _experts_kernel,
        grid_spec=pltpu.PrefetchScalarGridSpec(
            num_scalar_prefetch=6,
            grid=(n_work,),
            in_specs=[
                pl.BlockSpec((ts, D_MODEL), lambda w, wt, we, *_: (wt[w], 0)),
                pl.BlockSpec((1, D_MODEL, D_EXPERT), lambda w, wt, we, *_: (we[w], 0, 0)),
                pl.BlockSpec((1, D_MODEL, D_EXPERT), lambda w, wt, we, *_: (we[w], 0, 0)),
                pl.BlockSpec((1, D_EXPERT, D_MODEL), lambda w, wt, we, *_: (we[w], 0, 0)),
            ],
            out_specs=pl.BlockSpec((ts, D_MODEL), lambda w, wt, we, *_: (wt[w], 0)),
            scratch_shapes=[pltpu.VMEM((D_MODEL, D_EXPERT), BF16), pltpu.VMEM((D_MODEL, D_EXPERT), BF16),
                            pltpu.VMEM((D_EXPERT, D_MODEL), BF16)],
        ),
        out_shape=jax.ShapeDtypeStruct((s, D_MODEL), F32),
        compiler_params=_cparams("arbitrary"),
        name="moe_experts",
    )(*work, xs, w1, w3, w2)


def _expert_work_items(counts, total):
    ts = MOE_TS
    n_tiles = total // ts
    n_work = n_tiles + N_EXPERTS - 1
    offs = jnp.cumsum(counts) - counts
    t0 = (jnp.arange(n_tiles, dtype=jnp.int32) * ts)[:, None]
    lo = jnp.maximum(t0, offs[None, :])
    hi = jnp.minimum(t0 + ts, (offs + counts)[None, :])
    nonempty = (hi > lo).reshape(-1)
    nw = jnp.sum(nonempty.astype(jnp.int32))
    idx = jnp.nonzero(nonempty, size=n_work, fill_value=0)[0].astype(jnp.int32)
    idx = jnp.where(jnp.arange(n_work) < nw, idx, idx[jnp.maximum(nw - 1, 0)])
    wt = idx // N_EXPERTS
    we = idx % N_EXPERTS
    wlo = lo.reshape(-1)[idx] - wt * ts
    whi = hi.reshape(-1)[idx] - wt * ts
    wfirst = jnp.concatenate([jnp.ones((1,), jnp.int32), (wt[1:] != wt[:-1]).astype(jnp.int32)])
    return wt, we, wlo, whi, wfirst, nw.reshape(1)


def _combine_kernel(slots_ref, slots_next_ref, h_ref, route_ref, gfin_ref, ys_hbm, yp_ref, ysm_ref, gbuf, sems,
                    *, n_prompt_tiles):
    i = pl.program_id(0)
    n = pl.num_programs(0)
    tm = h_ref.shape[0]

    def gather(s_ref, buf):
        def issue(r, c):
            pltpu.make_async_copy(ys_hbm.at[pl.ds(s_ref[0, 0, 2 * r], 1)], gbuf.at[buf, 0, pl.ds(r, 1)],
                                  sems.at[buf]).start()
            pltpu.make_async_copy(ys_hbm.at[pl.ds(s_ref[0, 0, 2 * r + 1], 1)], gbuf.at[buf, 1, pl.ds(r, 1)],
                                  sems.at[buf]).start()
            return c
        lax.fori_loop(0, tm, issue, 0, unroll=ROW_DMA_UNROLL)

    cur = i % 2

    @pl.when(i == 0)
    def _():
        gather(slots_ref, 0)

    @pl.when(i + 1 < n)
    def _():
        gather(slots_next_ref, 1 - cur)

    for k in range(2):
        pltpu.make_async_copy(ys_hbm.at[pl.ds(0, tm)], gbuf.at[cur, k], sems.at[cur]).wait()
    route = route_ref[...]
    lane = lax.broadcasted_iota(jnp.int32, route.shape, 1)
    w1 = jnp.sum(jnp.where(lane == ROUTE_W1, route, 0.0), axis=1, keepdims=True)
    w2 = jnp.sum(jnp.where(lane == ROUTE_W2, route, 0.0), axis=1, keepdims=True)
    hf = h_ref[...] + (w1 * gbuf[cur, 0] + w2 * gbuf[cur, 1])
    y = hf * lax.rsqrt(jnp.mean(hf * hf, axis=-1, keepdims=True) + NORM_EPS) * gfin_ref[...]

    @pl.when(i < n_prompt_tiles)
    def _():
        yp_ref[...] = y

    @pl.when(i >= n_prompt_tiles)
    def _():
        ysm_ref[...] = y


def _combine(h, route, slots, ys, gfin, n_prompt_tiles):
    t = h.shape[0]
    tm = MOE_TM
    n_p = n_prompt_tiles
    return pl.pallas_call(
        functools.partial(_combine_kernel, n_prompt_tiles=n_p),
        grid_spec=pltpu.PrefetchScalarGridSpec(
            num_scalar_prefetch=0,
            grid=(t // tm,),
            in_specs=[pl.BlockSpec((1, 1, 2 * tm), lambda i: (i, 0, 0), memory_space=pltpu.SMEM),
                      pl.BlockSpec((1, 1, 2 * tm), lambda i: (jnp.minimum(i + 1, t // tm - 1), 0, 0),
                                   memory_space=pltpu.SMEM),
                      pl.BlockSpec((tm, D_MODEL), lambda i: (i, 0)),
                      pl.BlockSpec((tm, ROUTER_LANES), lambda i: (i, 0)),
                      pl.BlockSpec((1, D_MODEL), lambda i: (0, 0)),
                      pl.BlockSpec(memory_space=pl.ANY)],
            out_specs=[pl.BlockSpec((tm, D_MODEL), lambda i: (jnp.minimum(i, n_p - 1), 0)),
                       pl.BlockSpec((tm, D_MODEL), lambda i: (0, 0))],
            scratch_shapes=[pltpu.VMEM((2, 2, tm, D_MODEL), F32), pltpu.SemaphoreType.DMA((2,))],
        ),
        out_shape=[jax.ShapeDtypeStruct((n_p * tm, D_MODEL), F32), jax.ShapeDtypeStruct((tm, D_MODEL), F32)],
        compiler_params=_cparams("arbitrary"),
        name="moe_combine",
    )(slots, slots, h, route, gfin, ys)


def _v_tiles(v, tblk):
    b, t, _ = v.shape
    x = v.reshape(b, t // tblk, tblk, 2, RWKV_PAIRS, RWKV_HEAD).transpose(0, 1, 4, 5, 3, 2)
    x = jnp.pad(x, ((0, 0),) * 5 + ((0, RWKV_HEAD - tblk),))
    return x.reshape(b, t // tblk, RWKV_PAIRS, RWKV_HEAD, LANE).astype(BF16)


def _y_rows(y, bb, t):
    nb, tpad, _ = y.shape
    return y.reshape(nb, tpad, bb, RWKV_WIDTH).transpose(0, 2, 1, 3).reshape(nb * bb, tpad, RWKV_WIDTH)[:, :t]


def kernel(x_prompt, x_sample, state_gla, state_rwkv, state_shift, meta_tokens, norm_mix, w_in, gla_gate_w2,
           gla_gate_b, gla_norm, rwkv_mu, rwkv_w0, rwkv_w2, rwkv_a0, rwkv_a2, rwkv_g2, rwkv_kk, rwkv_ka, rwkv_rk,
           rwkv_ln_w, rwkv_ln_b, w_out, norm_ffn, router_group_w, router_group_b, router_expert_w,
           router_expert_b, moe_w1, moe_w3, moe_w2, norm_final):
    bp, tp, _ = x_prompt.shape
    bs, ts, _ = x_sample.shape
    assert state_gla.shape[0] == 1, "one layer"
    lyr = 0

    w_in_l = w_in[lyr]
    wg = jnp.pad(w_in_l[:, :GLA_COLS], ((0, 0), (0, GLA_PCOLS - GLA_COLS))).astype(BF16)
    wr = w_in_l[:, GLA_COLS:].astype(BF16)
    g_mix = norm_mix[lyr][None, :]
    gw2p = jnp.pad(gla_gate_w2[lyr], ((0, LANE - GLA_GATE_RANK), (0, 0)))
    gb = gla_gate_b[lyr][None, :]
    gn = gla_norm[lyr][None, :]
    w2p = jnp.pad(rwkv_w2[lyr], ((0, 64), (0, 0))).astype(BF16)
    a2p = jnp.pad(rwkv_a2[lyr], ((64, 0), (0, 0))).astype(BF16)
    pre_params = (rwkv_mu[lyr][None, :], rwkv_w0[lyr][None, :], w2p, rwkv_a0[lyr][None, :], a2p,
                  rwkv_g2[lyr].astype(BF16), rwkv_kk[lyr][None, :], rwkv_ka[lyr][None, :],
                  rwkv_rk[lyr].reshape(1, RWKV_WIDTH), _head_selector())
    lnw = rwkv_ln_w[lyr][None, :]
    lnb = rwkv_ln_b[lyr][None, :]
    wo = w_out[lyr].astype(BF16)
    gffn = norm_ffn[lyr][None, :]
    n_used = N_GROUPS + N_EXPERTS
    w_router = jnp.pad(
        jnp.concatenate([router_group_w[lyr],
                         router_expert_w[lyr].transpose(1, 0, 2).reshape(D_MODEL, N_EXPERTS)], axis=1),
        ((0, 0), (0, ROUTER_LANES - n_used)))
    b_router = jnp.pad(jnp.concatenate([router_group_b[lyr], router_expert_b[lyr].reshape(N_EXPERTS)]),
                       (0, ROUTER_LANES - n_used))[None, :]
    gfin = norm_final[None, :]

    pg_m, pr_m = _inproj(meta_tokens, g_mix, wg, wr, N_META)
    _, sg_m = _gla(pg_m[None], jnp.zeros((1, GLA_HEADS, GLA_DK, GLA_DV), F32), gw2p, gb, gn,
                   bb=1, chunk=N_META, sub=N_META, t_valid=N_META)
    r, w, k, kk, kka, v_m, _, _ = _rwkv_pre(pr_m[None], jnp.zeros((1, 1, RWKV_COLS), F32), pre_params,
                                            tm=N_META, explicit_prev=False, emit_vt=False)
    _, sr_m = _rwkv_rec(r, w, k, kk, kka, _v_tiles(v_m, N_META),
                        jnp.zeros((1, RWKV_HEADS, RWKV_HEAD, RWKV_HEAD), F32), bb=1, n_steps=N_META)

    xp = x_prompt.reshape(bp * tp, D_MODEL)
    first_prev = jnp.broadcast_to(pr_m[N_META - 1][None, None, :], (bp, 1, RWKV_COLS))
    pg_p, r, w, k, kk, kka, vt_p, bv_p, gate_p, shift_p = _inproj_pre(x_prompt, g_mix, wg, wr, first_prev, pre_params,
                                                                      tm=256, sub=128)
    og_p, sg_p = _gla(pg_p, jnp.broadcast_to(sg_m, (bp,) + sg_m.shape[1:]), gw2p, gb, gn,
                      bb=8, chunk=GLA_CHUNK, sub=GLA_SUB, t_valid=GLA_CHUNK)
    y_p, sr_p = _rwkv_rec(r, w, k, kk, kka, vt_p, jnp.broadcast_to(sr_m, (bp,) + sr_m.shape[1:]),
                          bb=bp, n_steps=REC_TB)
    prompt_rows = (xp, og_p.reshape(bp * tp, GLA_WIDTH), y_p.reshape(tp, bp * RWKV_WIDTH),
                   bv_p.reshape(bp * tp, RWKV_WIDTH), gate_p.reshape(bp * tp, RWKV_WIDTH))

    xs = x_sample.reshape(bs * ts, D_MODEL)
    pg_s, pr_s = _inproj(xs, g_mix, wg, wr, bs * ts)
    ts_pad = 8
    pg_s3 = jnp.pad(pg_s.reshape(bs, ts, GLA_PCOLS), ((0, 0), (0, ts_pad - ts), (0, 0)))
    og_s, sg_s = _gla(pg_s3, state_gla[lyr], gw2p, gb, gn, bb=8, chunk=ts_pad, sub=ts_pad, t_valid=ts)
    og_s = og_s[:, :ts]
    pr_s3 = pr_s.reshape(bs, ts, RWKV_COLS)
    prev_s = jnp.concatenate([state_shift[lyr][:, None, :], pr_s3[:, :-1]], axis=1)
    r, w, k, kk, kka, v_s, bv_s, gate_s = _rwkv_pre(pr_s3.reshape(1, bs * ts, RWKV_COLS),
                                                     prev_s.reshape(1, bs * ts, RWKV_COLS), pre_params,
                                                     tm=bs * ts, explicit_prev=True, emit_vt=False)
    unflat = lambda a: a.reshape(RWKV_PAIRS, bs, ts, LANE).transpose(1, 0, 2, 3)
    y_s, sr_s = _rwkv_rec(unflat(r), unflat(w), unflat(k), unflat(kk), unflat(kka),
                          _v_tiles(v_s.reshape(bs, ts, RWKV_WIDTH), ts), state_rwkv[lyr], bb=8, n_steps=ts)
    y_s = _y_rows(y_s, 8, ts)
    sample_rows = (xs, og_s.reshape(bs * ts, GLA_WIDTH), y_s.reshape(bs * ts, RWKV_WIDTH),
                   bv_s.reshape(bs * ts, RWKV_WIDTH), gate_s.reshape(bs * ts, RWKV_WIDTH))

    h_all, n2_all, route, counts = _mix_router(prompt_rows, sample_rows, lnw, lnb, wo, gffn, w_router, b_router,
                                               seq_tiles=tp // MOE_TM)
    n_tok = h_all.shape[0]
    n_p_tiles = (bp * tp) // MOE_TM
    cnt = counts[0, EXPERT_LANE0:EXPERT_LANE0 + N_EXPERTS].astype(jnp.int32)
    offs = jnp.cumsum(cnt) - cnt
    eid = route[:, ROUTE_E1:ROUTE_E2 + 1].astype(jnp.int32)
    pos = route[:, ROUTE_P1:ROUTE_P2 + 1].astype(jnp.int32)
    onehot = (eid[..., None] == jnp.arange(N_EXPERTS, dtype=jnp.int32)).astype(F32)
    off = jnp.einsum("tke,e->tk", onehot, offs.astype(F32), precision=HIGHEST).astype(jnp.int32)
    slots = (off + pos).reshape(n_tok // MOE_TM, 1, 2 * MOE_TM)
    xs_sorted = _dispatch(n2_all, slots)
    ys_sorted = _experts(xs_sorted, _expert_work_items(cnt, 2 * n_tok), moe_w1[lyr], moe_w3[lyr], moe_w2[lyr])
    y_prompt, y_sample = _combine(h_all, route, slots, ys_sorted, gfin, n_p_tiles)
    y_prompt = y_prompt.reshape(bp, tp, D_MODEL)
    y_sample = y_sample.reshape(bs, ts, D_MODEL)

    return (y_prompt, y_sample,
            sg_p[None], sr_p[None], shift_p[:, 0][None],
            sg_s[None], sr_s[None], pr_s3[:, -1][None])
```

```python
import functools

import jax
import jax.numpy as jnp
from jax import lax
from jax.experimental import pallas as pl
from jax.experimental.pallas import tpu as pltpu

F32 = jnp.float32
BF16 = jnp.bfloat16
HIGHEST = lax.Precision.HIGHEST

D_MODEL = 1024
N_META = 16
NORM_EPS = 1e-6
LOG2E = 1.4426950408889634
GLA_HEADS = 4
GLA_DK = 64
GLA_DV = 128
GLA_QK = GLA_HEADS * GLA_DK
GLA_WIDTH = GLA_HEADS * GLA_DV
GLA_GATE_RANK = 16
GLA_GATE_NORM = 16.0
GLA_CHUNK = 64
GLA_SUB = 8
GLA_COLS = 2 * GLA_QK + 2 * GLA_WIDTH + GLA_GATE_RANK
GLA_PCOLS = 2 * GLA_QK + 2 * GLA_WIDTH + 128
RWKV_WIDTH = 512
RWKV_HEAD = 64
RWKV_HEADS = 8
RWKV_PAIRS = RWKV_HEADS // 2
RWKV_DECAY_SCALE = 0.606531
RWKV_GN_EPS = 64e-5
RWKV_COLS = 3 * RWKV_WIDTH + 64 + 64 + 128
REC_TB = 64
REC_UNROLL = 8
INPROJ_TM = 512
PRE_TM = 256
SEQ_BLOCK = 8
N_GROUPS = 4
EXPERTS_PER_GROUP = 8
N_EXPERTS = 32
D_EXPERT = 512
ROUTER_LANES = 128
EXPERT_LANE0 = N_GROUPS
ROUTE_E1, ROUTE_E2, ROUTE_W1, ROUTE_W2, ROUTE_P1, ROUTE_P2 = range(6)
MOE_TM = 512
MOE_TS = 512

LANE = 128
VMEM_LIMIT = 56 * 1024 * 1024


def _cparams(*sem):
    return pltpu.CompilerParams(dimension_semantics=sem, vmem_limit_bytes=VMEM_LIMIT)


def _block_ones(n, blk):
    i = jnp.arange(n)
    return (i[:, None] // blk == i[None, :] // blk).astype(BF16)


def _sigmoid(x):
    return 1.0 / (1.0 + jnp.exp(-x))


def _dot(a, b):
    return jnp.dot(a, b, preferred_element_type=F32)


def _dot_nt(a, b):
    return lax.dot_general(a, b, (((1,), (1,)), ((), ())), preferred_element_type=F32)


def _dot_tn(a, b):
    return lax.dot_general(a, b, (((0,), (0,)), ((), ())), preferred_element_type=F32)


def _split2(x):
    hi = x.astype(BF16)
    lo = (x - hi.astype(F32)).astype(BF16)
    return hi, lo


def _head_selector():
    return (jnp.arange(RWKV_WIDTH)[:, None] // RWKV_HEAD == jnp.arange(LANE)[None, :]).astype(BF16)


def _group_sum(x, sel):
    hi, lo = _split2(x)
    s_hi, s_lo = _split2(_dot(hi, sel) + _dot(lo, sel))
    return _dot_nt(s_hi, sel) + _dot_nt(s_lo, sel)


def _inproj_kernel(x_ref, g_ref, wg_ref, wr_ref, pg_ref, pr_ref):
    x = x_ref[...]
    n = x * lax.rsqrt(jnp.mean(x * x, axis=-1, keepdims=True) + NORM_EPS) * g_ref[...]
    nb = n.astype(BF16)
    pg_ref[...] = _dot(nb, wg_ref[...])
    pr_ref[...] = _dot(nb, wr_ref[...])


def _inproj(x, g, wg, wr, tm):
    t = x.shape[0]
    return pl.pallas_call(
        _inproj_kernel,
        grid=(t // tm,),
        in_specs=[
            pl.BlockSpec((tm, D_MODEL), lambda i: (i, 0)),
            pl.BlockSpec((1, D_MODEL), lambda i: (0, 0)),
            pl.BlockSpec((D_MODEL, GLA_PCOLS), lambda i: (0, 0)),
            pl.BlockSpec((D_MODEL, RWKV_COLS), lambda i: (0, 0)),
        ],
        out_specs=[
            pl.BlockSpec((tm, GLA_PCOLS), lambda i: (i, 0)),
            pl.BlockSpec((tm, RWKV_COLS), lambda i: (i, 0)),
        ],
        out_shape=[jax.ShapeDtypeStruct((t, GLA_PCOLS), F32), jax.ShapeDtypeStruct((t, RWKV_COLS), F32)],
        compiler_params=_cparams("parallel"),
        name="inproj",
    )(x, g, wg, wr)


def _gla_kernel(pg_ref, s0_ref, gw2_ref, gb_ref, gn_ref, bo_ref, tril_ref, o_ref, sout_ref, s_scr,
                *, bb, chunk, sub, t_valid):
    ci = pl.program_id(1)

    @pl.when(ci == 0)
    def _():
        s_scr[...] = s0_ref[...]

    bo = bo_ref[...]
    tril = tril_ref[...]
    lane = lax.broadcasted_iota(jnp.int32, (sub, LANE), 1) & (GLA_DK - 1)
    rowi = lax.broadcasted_iota(jnp.int32, (sub, LANE), 0)
    head0_s = lax.broadcasted_iota(jnp.int32, (sub, LANE), 1) < GLA_DK
    head0_c = lax.broadcasted_iota(jnp.int32, (chunk, LANE), 1) < GLA_DK

    for bi in range(bb):
        pg = pg_ref[bi]
        q = pg[:, 0:GLA_QK] * (GLA_DK ** -0.5)
        k = pg[:, GLA_QK:2 * GLA_QK]
        v = pg[:, 2 * GLA_QK:2 * GLA_QK + GLA_WIDTH]
        g = pg[:, 2 * GLA_QK + GLA_WIDTH:2 * GLA_QK + 2 * GLA_WIDTH]
        gl = pg[:, 2 * GLA_QK + 2 * GLA_WIDTH:]
        z = jnp.dot(gl, gw2_ref[...], precision=HIGHEST, preferred_element_type=F32) + gb_ref[...]
        lg = (jnp.minimum(z, 0.0) - jnp.log1p(jnp.exp(-jnp.abs(z)))) * (LOG2E / GLA_GATE_NORM)
        if t_valid < chunk:
            rows = lax.broadcasted_iota(jnp.int32, lg.shape, 0)
            lg = jnp.where(rows < t_valid, lg, 0.0)
        b = jnp.dot(tril, lg, precision=HIGHEST, preferred_element_type=F32)
        eb = jnp.exp2(b)
        blast = b[chunk - 1:chunk, :]
        kl = k * jnp.exp2(blast - b)
        qe = q * eb

        n_blk = chunk // sub
        n_pairs = GLA_HEADS // 2
        ps = []
        for hp in range(n_pairs):
            sl = slice(hp * LANE, (hp + 1) * LANE)
            for blk in range(n_blk):
                rs = slice(blk * sub, (blk + 1) * sub)
                qb, kb, bbk = q[rs, sl], k[rs, sl], b[rs, sl]
                for j in range(sub):
                    ps.append(qb * (kb[j:j + 1] * jnp.exp2(jnp.minimum(bbk - bbk[j:j + 1], 0.0))))
        red = _dot(jnp.concatenate(ps, axis=0).astype(BF16), bo)

        o_heads = []
        for hp in range(n_pairs):
            sl = slice(hp * LANE, (hp + 1) * LANE)
            kp, bp = k[:, sl], b[:, sl]
            row_blocks = []
            for blk in range(n_blk):
                rs = slice(blk * sub, (blk + 1) * sub)
                base = (hp * n_blk + blk) * sub * sub
                a = jnp.zeros((sub, LANE), F32)
                for j in range(sub):
                    a = jnp.where((lane == blk * sub + j) & (rowi >= j), red[base + j * sub:base + (j + 1) * sub], a)
                if blk > 0:
                    bref = bp[blk * sub - 1:blk * sub]
                    qt = q[rs, sl] * jnp.exp2(bp[rs] - bref)
                    kt = (kp * jnp.exp2(jnp.minimum(bref - bp, 0.0))).astype(BF16)
                    qt2 = jnp.concatenate([jnp.where(head0_s, qt, 0.0), jnp.where(head0_s, 0.0, qt)], axis=0)
                    off2 = _dot_nt(qt2.astype(BF16), kt)
                    a = jnp.where(lane < blk * sub, jnp.concatenate([off2[:sub], off2[sub:]], axis=1), a)
                row_blocks.append(a)
            a_pair = row_blocks[0] if n_blk == 1 else jnp.concatenate(row_blocks, axis=0)
            v0 = v[:, 2 * hp * GLA_DV:(2 * hp + 1) * GLA_DV]
            v1 = v[:, (2 * hp + 1) * GLA_DV:(2 * hp + 2) * GLA_DV]
            s_pair = s_scr[bi, 2 * hp:2 * hp + 2].reshape(2 * GLA_DK, GLA_DV)
            qe_p, kl_p = qe[:, sl], kl[:, sl]

            def by_head(x):
                return jnp.concatenate([jnp.where(head0_c, x, 0.0), jnp.where(head0_c, 0.0, x)], axis=0)

            if chunk == GLA_DK:
                v_rows = jnp.concatenate([v0, v1], axis=0)
            else:
                zpad = jnp.zeros((GLA_DK - chunk, GLA_DV), F32)
                v_rows = jnp.concatenate([v0, zpad, v1, zpad], axis=0)
            lhs = jnp.concatenate([by_head(a_pair), by_head(qe_p)], axis=1).astype(BF16)
            rhs = jnp.concatenate([v_rows, s_pair], axis=0).astype(BF16)
            o2 = _dot(lhs, rhs)
            upd = _dot_tn(by_head(kl_p).astype(BF16), jnp.concatenate([v0, v1], axis=0).astype(BF16))
            dcol = jnp.broadcast_to(jnp.exp2(blast[:, sl]), (8, LANE)).T[:, 0:1]
            s_new = dcol * s_pair + upd
            s_scr[bi, 2 * hp] = s_new[:GLA_DK]
            s_scr[bi, 2 * hp + 1] = s_new[GLA_DK:]
            for h2 in range(2):
                o_h = o2[h2 * chunk:(h2 + 1) * chunk]
                o_heads.append(o_h * lax.rsqrt(jnp.mean(o_h * o_h, axis=-1, keepdims=True) + NORM_EPS) * gn_ref[...])
        o = jnp.concatenate(o_heads, axis=1)
        o_ref[bi] = o * (g * _sigmoid(g))

    @pl.when(ci == pl.num_programs(1) - 1)
    def _():
        sout_ref[...] = s_scr[...]


def _gla(pg, s0, gw2p, gb, gn, *, bb, chunk, sub, t_valid):
    b, t, _ = pg.shape
    tril = jnp.tril(jnp.ones((chunk, chunk), F32))
    kern = functools.partial(_gla_kernel, bb=bb, chunk=chunk, sub=sub, t_valid=t_valid)
    return pl.pallas_call(
        kern,
        grid=(b // bb, t // chunk),
        in_specs=[
            pl.BlockSpec((bb, chunk, GLA_PCOLS), lambda i, j: (i, j, 0)),
            pl.BlockSpec((bb, GLA_HEADS, GLA_DK, GLA_DV), lambda i, j: (i, 0, 0, 0)),
            pl.BlockSpec((LANE, GLA_QK), lambda i, j: (0, 0)),
            pl.BlockSpec((1, GLA_QK), lambda i, j: (0, 0)),
            pl.BlockSpec((1, GLA_DV), lambda i, j: (0, 0)),
            pl.BlockSpec((LANE, LANE), lambda i, j: (0, 0)),
            pl.BlockSpec((chunk, chunk), lambda i, j: (0, 0)),
        ],
        out_specs=[
            pl.BlockSpec((bb, chunk, GLA_WIDTH), lambda i, j: (i, j, 0)),
            pl.BlockSpec((bb, GLA_HEADS, GLA_DK, GLA_DV), lambda i, j: (i, 0, 0, 0)),
        ],
        out_shape=[jax.ShapeDtypeStruct((b, t, GLA_WIDTH), F32),
                   jax.ShapeDtypeStruct((b, GLA_HEADS, GLA_DK, GLA_DV), F32)],
        scratch_shapes=[pltpu.VMEM((bb, GLA_HEADS, GLA_DK, GLA_DV), F32)],
        compiler_params=_cparams("parallel", "arbitrary"),
        name="gla_chunk",
    )(pg, s0, gw2p, gb, gn, _block_ones(LANE, GLA_DK), tril)


def _shifted_rows(pr, row0):
    rows = lax.broadcasted_iota(jnp.int32, pr.shape, 0)
    return jnp.where(rows == 0, row0, pltpu.roll(pr, 1, 0))


def _rwkv_pre_math(pr, prev, params, outs, rs, tb0, emit_vt):
    mu_ref, w0_ref, w2_ref, a0_ref, a2_ref, g2_ref, kk_ref, ka_ref, rk_ref, bo_ref = params
    r_out, w_out, k_out, kkn_out, kka_out, v_out, bv_out, gate_out = outs
    n_rows = pr.shape[0]
    xm = pr + (prev - pr) * mu_ref[...]
    wd = RWKV_WIDTH
    rr, rk, rv = xm[:, 0:wd], xm[:, wd:2 * wd], xm[:, 2 * wd:3 * wd]
    wa = xm[:, 3 * wd:3 * wd + LANE]
    gl2 = xm[:, 3 * wd + LANE:3 * wd + 2 * LANE]
    logw = -RWKV_DECAY_SCALE * _sigmoid(w0_ref[...] + _dot(jnp.tanh(wa).astype(BF16), w2_ref[...]))
    aa = _sigmoid(a0_ref[...] + _dot(wa.astype(BF16), a2_ref[...]))
    gate = _dot(_sigmoid(gl2).astype(BF16), g2_ref[...])
    bo = bo_ref[...]
    kk = rk * kk_ref[...]
    kk = kk / jnp.maximum(jnp.sqrt(_group_sum(kk * kk, bo)), 1e-12)
    k = rk * (1.0 + (aa - 1.0) * ka_ref[...])
    bv = _group_sum(rr * k * rk_ref[...], bo) * rv
    w = jnp.exp(logw)
    kka = kk * aa
    hd = RWKV_HEAD

    def pair(x, hp):
        return jnp.concatenate([x[:, hp * hd:(hp + 1) * hd], x[:, (hp + RWKV_PAIRS) * hd:(hp + RWKV_PAIRS + 1) * hd]],
                               axis=1)

    for hp in range(RWKV_PAIRS):
        r_out[0, hp, rs, :] = pair(rr, hp)
        w_out[0, hp, rs, :] = pair(w, hp)
        k_out[0, hp, rs, :] = pair(k, hp)
        kkn_out[0, hp, rs, :] = pair(kk, hp)
        kka_out[0, hp, rs, :] = pair(kka, hp)
    if emit_vt:
        vt = rv.T
        for tb in range(n_rows // REC_TB):
            ts = slice(tb * REC_TB, (tb + 1) * REC_TB)
            for hp in range(RWKV_PAIRS):
                lo, hi = hp * hd, (hp + RWKV_PAIRS) * hd
                v_out[0, tb0 + tb, hp] = jnp.concatenate([vt[lo:lo + hd, ts], vt[hi:hi + hd, ts]],
                                                         axis=1).astype(BF16)
    else:
        v_out[0, rs, :] = rv
    bv_out[0, rs, :] = bv
    gate_out[0, rs, :] = gate


def _rwkv_pre_kernel(pr_ref, aux_ref, *rest, tm, explicit_prev, emit_vt):
    params, outs, carry_scr = rest[:10], rest[10:18], rest[18]
    pr = pr_ref[0]
    if explicit_prev:
        prev = aux_ref[0]
    else:
        j = pl.program_id(1)
        prev = _shifted_rows(pr, jnp.where(j == 0, aux_ref[0], carry_scr[...]))
        carry_scr[...] = pr[tm - 1:tm, :]
    _rwkv_pre_math(pr, prev, params, outs, slice(0, tm), 0, emit_vt)


def _rwkv_pre(pr, aux, params, *, tm, explicit_prev, emit_vt):
    b, t, _ = pr.shape
    kern = functools.partial(_rwkv_pre_kernel, tm=tm, explicit_prev=explicit_prev, emit_vt=emit_vt)
    aux_spec = (pl.BlockSpec((1, tm, RWKV_COLS), lambda i, j: (i, j, 0)) if explicit_prev
                else pl.BlockSpec((1, 1, RWKV_COLS), lambda i, j: (i, 0, 0)))
    const = lambda shape: pl.BlockSpec(shape, lambda i, j: (0,) * len(shape))
    pair_spec = pl.BlockSpec((1, RWKV_PAIRS, tm, LANE), lambda i, j: (i, 0, j, 0))
    row_spec = pl.BlockSpec((1, tm, RWKV_WIDTH), lambda i, j: (i, j, 0))
    pair_shape = jax.ShapeDtypeStruct((b, RWKV_PAIRS, t, LANE), F32)
    row_shape = jax.ShapeDtypeStruct((b, t, RWKV_WIDTH), F32)
    if emit_vt:
        v_spec = pl.BlockSpec((1, tm // REC_TB, RWKV_PAIRS, RWKV_HEAD, LANE), lambda i, j: (i, j, 0, 0, 0))
        v_shape = jax.ShapeDtypeStruct((b, t // REC_TB, RWKV_PAIRS, RWKV_HEAD, LANE), BF16)
    else:
        v_spec, v_shape = row_spec, row_shape
    return pl.pallas_call(
        kern,
        grid=(b, t // tm),
        in_specs=[
            pl.BlockSpec((1, tm, RWKV_COLS), lambda i, j: (i, j, 0)),
            aux_spec,
            const((1, RWKV_COLS)), const((1, RWKV_WIDTH)), const((LANE, RWKV_WIDTH)), const((1, RWKV_WIDTH)),
            const((LANE, RWKV_WIDTH)), const((LANE, RWKV_WIDTH)), const((1, RWKV_WIDTH)), const((1, RWKV_WIDTH)),
            const((1, RWKV_WIDTH)), const((RWKV_WIDTH, LANE)),
        ],
        out_specs=[pair_spec] * 5 + [v_spec, row_spec, row_spec],
        out_shape=[pair_shape] * 5 + [v_shape, row_shape, row_shape],
        scratch_shapes=[pltpu.VMEM((1, RWKV_COLS), F32)],
        compiler_params=_cparams("parallel", "arbitrary"),
        name="rwkv_pre",
    )(pr, aux, *params)


def _rwkv_rec_kernel(r_ref, w_ref, k_ref, kk_ref, kka_ref, vt_ref, s0_ref, bo_ref, vsel_ref, ysel_ref,
                     y_ref, sout_ref, s_scr, t1_scr, t3_scr, yt_scr, *, bb, n_steps):
    tb = pl.program_id(1)
    nc = bb * RWKV_PAIRS
    hd = RWKV_HEAD

    @pl.when(tb == 0)
    def _():
        for c in range(nc):
            bi, hp = divmod(c, RWKV_PAIRS)
            s_scr[c] = jnp.concatenate([s0_ref[bi, hp], s0_ref[bi, hp + RWKV_PAIRS]], axis=1)

    bo = bo_ref[...]

    def step(t, u):
        row = pl.ds(t, 1)
        for c in range(nc):
            bi, hp = divmod(c, RWKV_PAIRS)
            t1_scr[c * hd:(c + 1) * hd, :] = (s_scr[c] * kk_ref[bi, hp, row, :]).astype(BF16)
        sab = _dot(t1_scr[...], bo)
        vb = _dot(vt_ref[...].reshape(nc * hd, LANE), vsel_ref[t])
        for c in range(nc):
            bi, hp = divmod(c, RWKV_PAIRS)
            rs = slice(c * hd, (c + 1) * hd)
            s2 = (s_scr[c] * w_ref[bi, hp, row, :] - sab[rs] * kka_ref[bi, hp, row, :]
                  + vb[rs] * k_ref[bi, hp, row, :])
            s_scr[c] = s2
            t3_scr[rs, :] = (s2 * r_ref[bi, hp, row, :]).astype(BF16)
        yt_scr[...] += _dot_nt(ysel_ref[u], t3_scr[...])

    n_inner = min(8, n_steps)

    def block8(t8, carry):
        yt_scr[...] = jnp.zeros(yt_scr.shape, F32)

        def inner(u, c2):
            step(t8 * 8 + u, u)
            return c2

        lax.fori_loop(0, n_inner, inner, 0, unroll=REC_UNROLL)
        t0 = pl.multiple_of(t8 * 8, 8)
        blk = RWKV_PAIRS * hd
        for bi in range(bb):
            for h2 in range(2):
                y_ref[0, pl.ds(t0, 8), bi * RWKV_WIDTH + h2 * blk:bi * RWKV_WIDTH + (h2 + 1) * blk] = (
                    yt_scr[h2 * 8:(h2 + 1) * 8, bi * blk:(bi + 1) * blk])
        return carry

    lax.fori_loop(0, (n_steps + 7) // 8, block8, 0)

    @pl.when(tb == pl.num_programs(1) - 1)
    def _():
        for c in range(nc):
            bi, hp = divmod(c, RWKV_PAIRS)
            s_c = s_scr[c]
            sout_ref[bi, hp] = s_c[:, :RWKV_HEAD]
            sout_ref[bi, hp + RWKV_PAIRS] = s_c[:, RWKV_HEAD:]


def _rwkv_rec(r, w, k, kk, kka, vt, s0, *, bb, n_steps):
    b, _, t, _ = r.shape
    tblk = min(REC_TB, t)
    ntb = t // tblk
    nc = bb * RWKV_PAIRS
    lane = jnp.arange(LANE)
    vsel = ((lane[None, :, None] // RWKV_HEAD == lane[None, None, :] // RWKV_HEAD)
            & (lane[None, :, None] % RWKV_HEAD == jnp.arange(RWKV_HEAD)[:, None, None])).astype(BF16)
    ysel = (jnp.arange(16)[None, :, None]
            == 8 * (lane[None, None, :] // RWKV_HEAD) + jnp.arange(8)[:, None, None]).astype(BF16)
    kern = functools.partial(_rwkv_rec_kernel, bb=bb, n_steps=n_steps)
    pair_spec = pl.BlockSpec((bb, RWKV_PAIRS, tblk, LANE), lambda i, j: (i, 0, j, 0))
    state_spec = pl.BlockSpec((bb, RWKV_HEADS, RWKV_HEAD, RWKV_HEAD), lambda i, j: (i, 0, 0, 0))
    ytb = max(tblk, 8)
    return pl.pallas_call(
        kern,
        grid=(b // bb, ntb),
        in_specs=[pair_spec] * 5 + [
            pl.BlockSpec((bb, 1, RWKV_PAIRS, RWKV_HEAD, LANE), lambda i, j: (i, j, 0, 0, 0)),
            state_spec,
            pl.BlockSpec((LANE, LANE), lambda i, j: (0, 0)),
            pl.BlockSpec((RWKV_HEAD, LANE, LANE), lambda i, j: (0, 0, 0)),
            pl.BlockSpec((8, 16, LANE), lambda i, j: (0, 0, 0)),
        ],
        out_specs=[
            pl.BlockSpec((1, ytb, bb * RWKV_WIDTH), lambda i, j: (i, j, 0)),
            state_spec,
        ],
        out_shape=[jax.ShapeDtypeStruct((b // bb, ntb * ytb, bb * RWKV_WIDTH), F32),
                   jax.ShapeDtypeStruct((b, RWKV_HEADS, RWKV_HEAD, RWKV_HEAD), F32)],
        scratch_shapes=[pltpu.VMEM((nc, RWKV_HEAD, LANE), F32),
                        pltpu.VMEM((nc * RWKV_HEAD, LANE), BF16),
                        pltpu.VMEM((nc * RWKV_HEAD, LANE), BF16),
                        pltpu.VMEM((16, nc * RWKV_HEAD), F32)],
        compiler_params=_cparams("parallel", "arbitrary"),
        name="rwkv_rec",
    )(r, w, k, kk, kka, vt, s0, _block_ones(LANE, RWKV_HEAD), vsel, ysel)


def _mix_router_body(x_ref, og_ref, y_ref, bv_ref, gate_ref, lnw_ref, lnb_ref, wo_ref, gffn_ref, wr_hi_ref, wr_lo_ref,
                     br_ref, bo_ref, tril_ref, h_ref, n2_ref, route_ref, cnt_scr):
    bo = bo_ref[...]
    y = y_ref[...]
    inv_n = 1.0 / RWKV_HEAD
    d = y - _group_sum(y, bo) * inv_n
    var = _group_sum(d * d, bo) * inv_n
    yn = d * lax.rsqrt(var + RWKV_GN_EPS) * lnw_ref[...] + lnb_ref[...] + bv_ref[...]
    o_rwkv = yn * gate_ref[...]
    mix = (_dot(og_ref[...].astype(BF16), wo_ref[0:GLA_WIDTH, :])
           + _dot(o_rwkv.astype(BF16), wo_ref[GLA_WIDTH:, :]))
    h = x_ref[...] + mix
    h_ref[...] = h
    n2 = h * lax.rsqrt(jnp.mean(h * h, axis=-1, keepdims=True) + NORM_EPS) * gffn_ref[...]
    n2_ref[...] = n2
    n2_hi, n2_lo = _split2(n2)
    lg = (_dot(n2_hi, wr_hi_ref[...]) + _dot(n2_hi, wr_lo_ref[...]) + _dot(n2_lo, wr_hi_ref[...])) + br_ref[...]
    neg = jnp.float32(-3.0e38)
    big = jnp.float32(1.0e9)
    lane = lax.broadcasted_iota(jnp.int32, lg.shape, 1).astype(F32)
    gmask = lane < N_GROUPS
    gmax = jnp.max(jnp.where(gmask, lg, neg), axis=1, keepdims=True)
    p_top = 1.0 / jnp.sum(jnp.where(gmask, jnp.exp(jnp.minimum(lg - gmax, 0.0)), 0.0), axis=1, keepdims=True)
    gidx = jnp.min(jnp.where(gmask & (lg == gmax), lane, big), axis=1, keepdims=True)
    e_lo = EXPERT_LANE0 + gidx * EXPERTS_PER_GROUP
    emask = (lane >= e_lo) & (lane < e_lo + EXPERTS_PER_GROUP)
    m1 = jnp.max(jnp.where(emask, lg, neg), axis=1, keepdims=True)
    e1 = jnp.min(jnp.where(emask & (lg == m1), lane, big), axis=1, keepdims=True)
    emask2 = emask & (lane != e1)
    m2 = jnp.max(jnp.where(emask2, lg, neg), axis=1, keepdims=True)
    e2 = jnp.min(jnp.where(emask2 & (lg == m2), lane, big), axis=1, keepdims=True)
    r21 = jnp.exp(m2 - m1)
    w1 = p_top / (1.0 + r21)
    w2 = p_top * r21 / (1.0 + r21)
    o1 = lane == e1
    o2 = lane == e2
    onehot = jnp.where(o1 | o2, 1.0, 0.0)
    rank = _dot(tril_ref[...], onehot.astype(BF16)) + cnt_scr[...]
    pos1 = jnp.sum(jnp.where(o1, rank, 0.0), axis=1, keepdims=True)
    pos2 = jnp.sum(jnp.where(o2, rank, 0.0), axis=1, keepdims=True)
    cnt_scr[...] += jnp.sum(onehot, axis=0, keepdims=True)
    route = jnp.where(lane == ROUTE_E1, e1 - EXPERT_LANE0, 0.0)
    route = jnp.where(lane == ROUTE_E2, e2 - EXPERT_LANE0, route)
    route = jnp.where(lane == ROUTE_W1, w1, route)
    route = jnp.where(lane == ROUTE_W2, w2, route)
    route = jnp.where(lane == ROUTE_P1, pos1, route)
    route_ref[...] = jnp.where(lane == ROUTE_P2, pos2, route)


def _mix_router_kernel(*refs, n_prompt_tiles):
    prompt_rows, sample_rows, rest = refs[0:5], refs[5:10], refs[10:]
    consts, (h_ref, n2_ref, route_ref, cnt_ref, cnt_scr) = rest[:9], rest[9:]
    i = pl.program_id(0)

    @pl.when(i == 0)
    def _():
        cnt_scr[...] = jnp.zeros(cnt_scr.shape, F32)

    @pl.when(i < n_prompt_tiles)
    def _():
        _mix_router_body(*prompt_rows, *consts, h_ref, n2_ref, route_ref, cnt_scr)

    @pl.when(i >= n_prompt_tiles)
    def _():
        _mix_router_body(*sample_rows, *consts, h_ref, n2_ref, route_ref, cnt_scr)

    cnt_ref[...] = cnt_scr[...]


def _mix_router(prompt_rows, sample_rows, lnw, lnb, wo, gffn, wr, br, *, seq_tiles):
    tm = MOE_TM
    n_p = prompt_rows[0].shape[0] // tm
    assert sample_rows[0].shape[0] == tm
    t = (n_p + 1) * tm
    widths = (D_MODEL, GLA_WIDTH, RWKV_WIDTH, RWKV_WIDTH, RWKV_WIDTH)
    p_specs = [pl.BlockSpec((tm, n), lambda i: (jnp.minimum(i, n_p - 1), 0)) for n in widths]
    p_specs[2] = pl.BlockSpec(
        (tm, RWKV_WIDTH), lambda i: (jnp.minimum(i, n_p - 1) % seq_tiles, jnp.minimum(i, n_p - 1) // seq_tiles))
    s_specs = [pl.BlockSpec((tm, n), lambda i: (0, 0)) for n in widths]
    const = lambda shape: pl.BlockSpec(shape, lambda i: (0,) * len(shape))
    row = lambda n: pl.BlockSpec((tm, n), lambda i: (i, 0))
    tril = jnp.tril(jnp.ones((tm, tm), F32), -1).astype(BF16)
    return pl.pallas_call(
        functools.partial(_mix_router_kernel, n_prompt_tiles=n_p),
        grid=(n_p + 1,),
        in_specs=p_specs + s_specs + [
            const((1, RWKV_WIDTH)), const((1, RWKV_WIDTH)), const((D_MODEL, D_MODEL)), const((1, D_MODEL)),
            const((D_MODEL, ROUTER_LANES)), const((D_MODEL, ROUTER_LANES)), const((1, ROUTER_LANES)),
            const((RWKV_WIDTH, LANE)),
            const((tm, tm))],
        out_specs=[row(D_MODEL), row(D_MODEL), row(ROUTER_LANES), const((1, ROUTER_LANES))],
        out_shape=[jax.ShapeDtypeStruct((t, D_MODEL), F32), jax.ShapeDtypeStruct((t, D_MODEL), F32),
                   jax.ShapeDtypeStruct((t, ROUTER_LANES), F32), jax.ShapeDtypeStruct((1, ROUTER_LANES), F32)],
        scratch_shapes=[pltpu.VMEM((1, ROUTER_LANES), F32)],
        compiler_params=_cparams("arbitrary"),
        name="mix_router",
    )(*prompt_rows, *sample_rows, lnw, lnb, wo, gffn, *_split2(wr), br, _head_selector(), tril)


ROW_DMA_UNROLL = 8


def _dispatch_kernel(slots_ref, x_ref, xs_hbm, sem):
    tm = x_ref.shape[0]

    def issue(r, c):
        src = x_ref.at[pl.ds(r, 1)]
        pltpu.make_async_copy(src, xs_hbm.at[pl.ds(slots_ref[0, 0, 2 * r], 1)], sem).start()
        pltpu.make_async_copy(src, xs_hbm.at[pl.ds(slots_ref[0, 0, 2 * r + 1], 1)], sem).start()
        return c

    lax.fori_loop(0, tm, issue, 0, unroll=ROW_DMA_UNROLL)
    for _ in range(2):
        pltpu.make_async_copy(x_ref, xs_hbm.at[pl.ds(0, tm)], sem).wait()


def _dispatch(n2, slots):
    t = n2.shape[0]
    tm = MOE_TM
    return pl.pallas_call(
        _dispatch_kernel,
        grid_spec=pltpu.PrefetchScalarGridSpec(
            num_scalar_prefetch=0,
            grid=(t // tm,),
            in_specs=[pl.BlockSpec((1, 1, 2 * tm), lambda i: (i, 0, 0), memory_space=pltpu.SMEM),
                      pl.BlockSpec((tm, D_MODEL), lambda i: (i, 0))],
            out_specs=pl.BlockSpec(memory_space=pl.ANY),
            scratch_shapes=[pltpu.SemaphoreType.DMA(())],
        ),
        out_shape=jax.ShapeDtypeStruct((2 * t, D_MODEL), F32),
        compiler_params=_cparams("arbitrary"),
        name="moe_dispatch",
    )(slots, n2)


def _experts_kernel(wt_ref, we_ref, wlo_ref, whi_ref, wfirst_ref, nw_ref,
                    xs_ref, w1_ref, w3_ref, w2_ref, ys_ref, wb1, wb3, wb2):
    w = pl.program_id(0)

    @pl.when(w < nw_ref[0])
    def _():
        new_expert = jnp.logical_or(w == 0, we_ref[w] != we_ref[jnp.maximum(w - 1, 0)])

        @pl.when(new_expert)
        def _():
            wb1[...] = w1_ref[0].astype(BF16)
            wb3[...] = w3_ref[0].astype(BF16)
            wb2[...] = w2_ref[0].astype(BF16)

        x = xs_ref[...].astype(BF16)
        a = _dot(x, wb1[...])
        b = _dot(x, wb3[...])
        o = _dot(((a * _sigmoid(a)) * b).astype(BF16), wb2[...])

        @pl.when(wfirst_ref[w] == 1)
        def _():
            ys_ref[...] = o

        @pl.when(wfirst_ref[w] == 0)
        def _():
            rows = lax.broadcasted_iota(jnp.int32, o.shape, 0)
            ys_ref[...] = jnp.where((rows >= wlo_ref[w]) & (rows < whi_ref[w]), o, ys_ref[...])


def _experts(xs, work, w1, w3, w2):
    s = xs.shape[0]
    ts = MOE_TS
    n_work = work[0].shape[0]
    return pl.pallas_call(
        _experts_kernel,
        grid_spec=pltpu.PrefetchScalarGridSpec(
            num_scalar_prefetch=6,
            grid=(n_work,),
            in_specs=[
                pl.BlockSpec((ts, D_MODEL), lambda w, wt, we, *_: (wt[w], 0)),
                pl.BlockSpec((1, D_MODEL, D_EXPERT), lambda w, wt, we, *_: (we[w], 0, 0)),
                pl.BlockSpec((1, D_MODEL, D_EXPERT), lambda w, wt, we, *_: (we[w], 0, 0)),
                pl.BlockSpec((1, D_EXPERT, D_MODEL), lambda w, wt, we, *_: (we[w], 0, 0)),
            ],
            out_specs=pl.BlockSpec((ts, D_MODEL), lambda w, wt, we, *_: (wt[w], 0)),
            scratch_shapes=[pltpu.VMEM((D_MODEL, D_EXPERT), BF16), pltpu.VMEM((D_MODEL, D_EXPERT), BF16),
                            pltpu.VMEM((D_EXPERT, D_MODEL), BF16)],
        ),
        out_shape=jax.ShapeDtypeStruct((s, D_MODEL), F32),
        compiler_params=_cparams("arbitrary"),
        name="moe_experts",
    )(*work, xs, w1, w3, w2)


def _expert_work_items(counts, total):
    ts = MOE_TS
    n_tiles = total // ts
    n_work = n_tiles + N_EXPERTS - 1
    offs = jnp.cumsum(counts) - counts
    t0 = (jnp.arange(n_tiles, dtype=jnp.int32) * ts)[:, None]
    lo = jnp.maximum(t0, offs[None, :])
    hi = jnp.minimum(t0 + ts, (offs + counts)[None, :])
    nonempty = (hi > lo).reshape(-1)
    nw = jnp.sum(nonempty.astype(jnp.int32))
    idx = jnp.nonzero(nonempty, size=n_work, fill_value=0)[0].astype(jnp.int32)
    idx = jnp.where(jnp.arange(n_work) < nw, idx, idx[jnp.maximum(nw - 1, 0)])
    wt = idx // N_EXPERTS
    we = idx % N_EXPERTS
    wlo = lo.reshape(-1)[idx] - wt * ts
    whi = hi.reshape(-1)[idx] - wt * ts
    wfirst = jnp.concatenate([jnp.ones((1,), jnp.int32), (wt[1:] != wt[:-1]).astype(jnp.int32)])
    return wt, we, wlo, whi, wfirst, nw.reshape(1)


def _combine_kernel(slots_ref, slots_next_ref, h_ref, route_ref, gfin_ref, ys_hbm, yp_ref, ysm_ref, gbuf, sems,
                    *, n_prompt_tiles):
    i = pl.program_id(0)
    n = pl.num_programs(0)
    tm = h_ref.shape[0]

    def gather(s_ref, buf):
        def issue(r, c):
            pltpu.make_async_copy(ys_hbm.at[pl.ds(s_ref[0, 0, 2 * r], 1)], gbuf.at[buf, 0, pl.ds(r, 1)],
                                  sems.at[buf]).start()
            pltpu.make_async_copy(ys_hbm.at[pl.ds(s_ref[0, 0, 2 * r + 1], 1)], gbuf.at[buf, 1, pl.ds(r, 1)],
                                  sems.at[buf]).start()
            return c
        lax.fori_loop(0, tm, issue, 0, unroll=ROW_DMA_UNROLL)

    cur = i % 2

    @pl.when(i == 0)
    def _():
        gather(slots_ref, 0)

    @pl.when(i + 1 < n)
    def _():
        gather(slots_next_ref, 1 - cur)

    for k in range(2):
        pltpu.make_async_copy(ys_hbm.at[pl.ds(0, tm)], gbuf.at[cur, k], sems.at[cur]).wait()
    route = route_ref[...]
    lane = lax.broadcasted_iota(jnp.int32, route.shape, 1)
    w1 = jnp.sum(jnp.where(lane == ROUTE_W1, route, 0.0), axis=1, keepdims=True)
    w2 = jnp.sum(jnp.where(lane == ROUTE_W2, route, 0.0), axis=1, keepdims=True)
    hf = h_ref[...] + (w1 * gbuf[cur, 0] + w2 * gbuf[cur, 1])
    y = hf * lax.rsqrt(jnp.mean(hf * hf, axis=-1, keepdims=True) + NORM_EPS) * gfin_ref[...]

    @pl.when(i < n_prompt_tiles)
    def _():
        yp_ref[...] = y

    @pl.when(i >= n_prompt_tiles)
    def _():
        ysm_ref[...] = y


def _combine(h, route, slots, ys, gfin, n_prompt_tiles):
    t = h.shape[0]
    tm = MOE_TM
    n_p = n_prompt_tiles
    return pl.pallas_call(
        functools.partial(_combine_kernel, n_prompt_tiles=n_p),
        grid_spec=pltpu.PrefetchScalarGridSpec(
            num_scalar_prefetch=0,
            grid=(t // tm,),
            in_specs=[pl.BlockSpec((1, 1, 2 * tm), lambda i: (i, 0, 0), memory_space=pltpu.SMEM),
                      pl.BlockSpec((1, 1, 2 * tm), lambda i: (jnp.minimum(i + 1, t // tm - 1), 0, 0),
                                   memory_space=pltpu.SMEM),
                      pl.BlockSpec((tm, D_MODEL), lambda i: (i, 0)),
                      pl.BlockSpec((tm, ROUTER_LANES), lambda i: (i, 0)),
                      pl.BlockSpec((1, D_MODEL), lambda i: (0, 0)),
                      pl.BlockSpec(memory_space=pl.ANY)],
            out_specs=[pl.BlockSpec((tm, D_MODEL), lambda i: (jnp.minimum(i, n_p - 1), 0)),
                       pl.BlockSpec((tm, D_MODEL), lambda i: (0, 0))],
            scratch_shapes=[pltpu.VMEM((2, 2, tm, D_MODEL), F32), pltpu.SemaphoreType.DMA((2,))],
        ),
        out_shape=[jax.ShapeDtypeStruct((n_p * tm, D_MODEL), F32), jax.ShapeDtypeStruct((tm, D_MODEL), F32)],
        compiler_params=_cparams("arbitrary"),
        name="moe_combine",
    )(slots, slots, h, route, gfin, ys)


def _v_tiles(v, tblk):
    b, t, _ = v.shape
    x = v.reshape(b, t // tblk, tblk, 2, RWKV_PAIRS, RWKV_HEAD).transpose(0, 1, 4, 5, 3, 2)
    x = jnp.pad(x, ((0, 0),) * 5 + ((0, RWKV_HEAD - tblk),))
    return x.reshape(b, t // tblk, RWKV_PAIRS, RWKV_HEAD, LANE).astype(BF16)


def _y_rows(y, bb, t):
    nb, tpad, _ = y.shape
    return y.reshape(nb, tpad, bb, RWKV_WIDTH).transpose(0, 2, 1, 3).reshape(nb * bb, tpad, RWKV_WIDTH)[:, :t]


def kernel(x_prompt, x_sample, state_gla, state_rwkv, state_shift, meta_tokens, norm_mix, w_in, gla_gate_w2,
           gla_gate_b, gla_norm, rwkv_mu, rwkv_w0, rwkv_w2, rwkv_a0, rwkv_a2, rwkv_g2, rwkv_kk, rwkv_ka, rwkv_rk,
           rwkv_ln_w, rwkv_ln_b, w_out, norm_ffn, router_group_w, router_group_b, router_expert_w,
           router_expert_b, moe_w1, moe_w3, moe_w2, norm_final):
    bp, tp, _ = x_prompt.shape
    bs, ts, _ = x_sample.shape
    assert state_gla.shape[0] == 1, "one layer"
    lyr = 0

    w_in_l = w_in[lyr]
    wg = jnp.pad(w_in_l[:, :GLA_COLS], ((0, 0), (0, GLA_PCOLS - GLA_COLS))).astype(BF16)
    wr = w_in_l[:, GLA_COLS:].astype(BF16)
    g_mix = norm_mix[lyr][None, :]
    gw2p = jnp.pad(gla_gate_w2[lyr], ((0, LANE - GLA_GATE_RANK), (0, 0)))
    gb = gla_gate_b[lyr][None, :]
    gn = gla_norm[lyr][None, :]
    w2p = jnp.pad(rwkv_w2[lyr], ((0, 64), (0, 0))).astype(BF16)
    a2p = jnp.pad(rwkv_a2[lyr], ((64, 0), (0, 0))).astype(BF16)
    pre_params = (rwkv_mu[lyr][None, :], rwkv_w0[lyr][None, :], w2p, rwkv_a0[lyr][None, :], a2p,
                  rwkv_g2[lyr].astype(BF16), rwkv_kk[lyr][None, :], rwkv_ka[lyr][None, :],
                  rwkv_rk[lyr].reshape(1, RWKV_WIDTH), _head_selector())
    lnw = rwkv_ln_w[lyr][None, :]
    lnb = rwkv_ln_b[lyr][None, :]
    wo = w_out[lyr].astype(BF16)
    gffn = norm_ffn[lyr][None, :]
    n_used = N_GROUPS + N_EXPERTS
    w_router = jnp.pad(
        jnp.concatenate([router_group_w[lyr],
                         router_expert_w[lyr].transpose(1, 0, 2).reshape(D_MODEL, N_EXPERTS)], axis=1),
        ((0, 0), (0, ROUTER_LANES - n_used)))
    b_router = jnp.pad(jnp.concatenate([router_group_b[lyr], router_expert_b[lyr].reshape(N_EXPERTS)]),
                       (0, ROUTER_LANES - n_used))[None, :]
    gfin = norm_final[None, :]

    pg_m, pr_m = _inproj(meta_tokens, g_mix, wg, wr, N_META)
    _, sg_m = _gla(pg_m[None], jnp.zeros((1, GLA_HEADS, GLA_DK, GLA_DV), F32), gw2p, gb, gn,
                   bb=1, chunk=N_META, sub=N_META, t_valid=N_META)
    r, w, k, kk, kka, v_m, _, _ = _rwkv_pre(pr_m[None], jnp.zeros((1, 1, RWKV_COLS), F32), pre_params,
                                            tm=N_META, explicit_prev=False, emit_vt=False)
    _, sr_m = _rwkv_rec(r, w, k, kk, kka, _v_tiles(v_m, N_META),
                        jnp.zeros((1, RWKV_HEADS, RWKV_HEAD, RWKV_HEAD), F32), bb=1, n_steps=N_META)

    xp = x_prompt.reshape(bp * tp, D_MODEL)
    pg_p, pr_p = _inproj(xp, g_mix, wg, wr, INPROJ_TM)
    og_p, sg_p = _gla(pg_p.reshape(bp, tp, GLA_PCOLS), jnp.broadcast_to(sg_m, (bp,) + sg_m.shape[1:]), gw2p, gb, gn,
                      bb=SEQ_BLOCK, chunk=GLA_CHUNK, sub=GLA_SUB, t_valid=GLA_CHUNK)
    pr_p3 = pr_p.reshape(bp, tp, RWKV_COLS)
    first_prev = jnp.broadcast_to(pr_m[N_META - 1][None, None, :], (bp, 1, RWKV_COLS))
    r, w, k, kk, kka, vt_p, bv_p, gate_p = _rwkv_pre(pr_p3, first_prev, pre_params, tm=PRE_TM, explicit_prev=False,
                                                     emit_vt=True)
    y_p, sr_p = _rwkv_rec(r, w, k, kk, kka, vt_p, jnp.broadcast_to(sr_m, (bp,) + sr_m.shape[1:]),
                          bb=bp, n_steps=REC_TB)
    prompt_rows = (xp, og_p.reshape(bp * tp, GLA_WIDTH), y_p.reshape(tp, bp * RWKV_WIDTH),
                   bv_p.reshape(bp * tp, RWKV_WIDTH), gate_p.reshape(bp * tp, RWKV_WIDTH))

    xs = x_sample.reshape(bs * ts, D_MODEL)
    pg_s, pr_s = _inproj(xs, g_mix, wg, wr, bs * ts)
    ts_pad = 8
    pg_s3 = jnp.pad(pg_s.reshape(bs, ts, GLA_PCOLS), ((0, 0), (0, ts_pad - ts), (0, 0)))
    og_s, sg_s = _gla(pg_s3, state_gla[lyr], gw2p, gb, gn, bb=SEQ_BLOCK, chunk=ts_pad, sub=ts_pad, t_valid=ts)
    og_s = og_s[:, :ts]
    pr_s3 = pr_s.reshape(bs, ts, RWKV_COLS)
    prev_s = jnp.concatenate([state_shift[lyr][:, None, :], pr_s3[:, :-1]], axis=1)
    r, w, k, kk, kka, v_s, bv_s, gate_s = _rwkv_pre(pr_s3.reshape(1, bs * ts, RWKV_COLS),
                                                     prev_s.reshape(1, bs * ts, RWKV_COLS), pre_params,
                                                     tm=bs * ts, explicit_prev=True, emit_vt=False)
    unflat = lambda a: a.reshape(RWKV_PAIRS, bs, ts, LANE).transpose(1, 0, 2, 3)
    y_s, sr_s = _rwkv_rec(unflat(r), unflat(w), unflat(k), unflat(kk), unflat(kka),
                          _v_tiles(v_s.reshape(bs, ts, RWKV_WIDTH), ts), state_rwkv[lyr], bb=SEQ_BLOCK, n_steps=ts)
    y_s = _y_rows(y_s, 8, ts)
    sample_rows = (xs, og_s.reshape(bs * ts, GLA_WIDTH), y_s.reshape(bs * ts, RWKV_WIDTH),
                   bv_s.reshape(bs * ts, RWKV_WIDTH), gate_s.reshape(bs * ts, RWKV_WIDTH))

    h_all, n2_all, route, counts = _mix_router(prompt_rows, sample_rows, lnw, lnb, wo, gffn, w_router, b_router,
                                               seq_tiles=tp // MOE_TM)
    n_tok = h_all.shape[0]
    n_p_tiles = (bp * tp) // MOE_TM
    cnt = counts[0, EXPERT_LANE0:EXPERT_LANE0 + N_EXPERTS].astype(jnp.int32)
    offs = jnp.cumsum(cnt) - cnt
    eid = route[:, ROUTE_E1:ROUTE_E2 + 1].astype(jnp.int32)
    pos = route[:, ROUTE_P1:ROUTE_P2 + 1].astype(jnp.int32)
    onehot = (eid[..., None] == jnp.arange(N_EXPERTS, dtype=jnp.int32)).astype(F32)
    off = jnp.einsum("tke,e->tk", onehot, offs.astype(F32), precision=HIGHEST).astype(jnp.int32)
    slots = (off + pos).reshape(n_tok // MOE_TM, 1, 2 * MOE_TM)
    xs_sorted = _dispatch(n2_all, slots)
    ys_sorted = _experts(xs_sorted, _expert_work_items(cnt, 2 * n_tok), moe_w1[lyr], moe_w3[lyr], moe_w2[lyr])
    y_prompt, y_sample = _combine(h_all, route, slots, ys_sorted, gfin, n_p_tiles)
    y_prompt = y_prompt.reshape(bp, tp, D_MODEL)
    y_sample = y_sample.reshape(bs, ts, D_MODEL)

    return (y_prompt, y_sample,
            sg_p[None], sr_p[None], pr_p3[:, -1][None],
            sg_s[None], sr_s[None], pr_s3[:, -1][None])
```

```python
import functools

import jax
import jax.numpy as jnp
from jax import lax
from jax.experimental import pallas as pl
from jax.experimental.pallas import tpu as pltpu

F32 = jnp.float32
BF16 = jnp.bfloat16
HIGHEST = lax.Precision.HIGHEST

D_MODEL = 1024
N_META = 16
NORM_EPS = 1e-6
LOG2E = 1.4426950408889634
GLA_HEADS = 4
GLA_DK = 64
GLA_DV = 128
GLA_QK = GLA_HEADS * GLA_DK
GLA_WIDTH = GLA_HEADS * GLA_DV
GLA_GATE_RANK = 16
GLA_GATE_NORM = 16.0
GLA_CHUNK = 64
GLA_SUB = 8
GLA_COLS = 2 * GLA_QK + 2 * GLA_WIDTH + GLA_GATE_RANK
GLA_PCOLS = 2 * GLA_QK + 2 * GLA_WIDTH + 128
RWKV_WIDTH = 512
RWKV_HEAD = 64
RWKV_HEADS = 8
RWKV_PAIRS = RWKV_HEADS // 2
RWKV_DECAY_SCALE = 0.606531
RWKV_GN_EPS = 64e-5
RWKV_COLS = 3 * RWKV_WIDTH + 64 + 64 + 128
REC_TB = 64
REC_UNROLL = 8
INPROJ_TM = 512
PRE_TM = 256
SEQ_BLOCK = 8
N_GROUPS = 4
EXPERTS_PER_GROUP = 8
N_EXPERTS = 32
D_EXPERT = 512
ROUTER_LANES = 128
EXPERT_LANE0 = N_GROUPS
ROUTE_E1, ROUTE_E2, ROUTE_W1, ROUTE_W2, ROUTE_P1, ROUTE_P2 = range(6)
MOE_TM = 512
MOE_TS = 512

LANE = 128
VMEM_LIMIT = 56 * 1024 * 1024


def _cparams(*sem):
    return pltpu.CompilerParams(dimension_semantics=sem, vmem_limit_bytes=VMEM_LIMIT)


def _block_ones(n, blk):
    i = jnp.arange(n)
    return (i[:, None] // blk == i[None, :] // blk).astype(BF16)


def _sigmoid(x):
    return 1.0 / (1.0 + jnp.exp(-x))


def _dot(a, b):
    return jnp.dot(a, b, preferred_element_type=F32)


def _dot_nt(a, b):
    return lax.dot_general(a, b, (((1,), (1,)), ((), ())), preferred_element_type=F32)


def _dot_tn(a, b):
    return lax.dot_general(a, b, (((0,), (0,)), ((), ())), preferred_element_type=F32)


def _split2(x):
    hi = x.astype(BF16)
    lo = (x - hi.astype(F32)).astype(BF16)
    return hi, lo


def _head_selector():
    return (jnp.arange(RWKV_WIDTH)[:, None] // RWKV_HEAD == jnp.arange(LANE)[None, :]).astype(BF16)


def _group_sum(x, sel):
    hi, lo = _split2(x)
    s_hi, s_lo = _split2(_dot(hi, sel) + _dot(lo, sel))
    return _dot_nt(s_hi, sel) + _dot_nt(s_lo, sel)


def _inproj_kernel(x_ref, g_ref, wg_ref, wr_ref, pg_ref, pr_ref):
    x = x_ref[...]
    n = x * lax.rsqrt(jnp.mean(x * x, axis=-1, keepdims=True) + NORM_EPS) * g_ref[...]
    nb = n.astype(BF16)
    pg_ref[...] = _dot(nb, wg_ref[...])
    pr_ref[...] = _dot(nb, wr_ref[...])


def _inproj(x, g, wg, wr, tm):
    t = x.shape[0]
    return pl.pallas_call(
        _inproj_kernel,
        grid=(t // tm,),
        in_specs=[
            pl.BlockSpec((tm, D_MODEL), lambda i: (i, 0)),
            pl.BlockSpec((1, D_MODEL), lambda i: (0, 0)),
            pl.BlockSpec((D_MODEL, GLA_PCOLS), lambda i: (0, 0)),
            pl.BlockSpec((D_MODEL, RWKV_COLS), lambda i: (0, 0)),
        ],
        out_specs=[
            pl.BlockSpec((tm, GLA_PCOLS), lambda i: (i, 0)),
            pl.BlockSpec((tm, RWKV_COLS), lambda i: (i, 0)),
        ],
        out_shape=[jax.ShapeDtypeStruct((t, GLA_PCOLS), F32), jax.ShapeDtypeStruct((t, RWKV_COLS), F32)],
        compiler_params=_cparams("parallel"),
        name="inproj",
    )(x, g, wg, wr)


def _gla_kernel(pg_ref, s0_ref, gw2_ref, gb_ref, gn_ref, bo_ref, tril_ref, o_ref, sout_ref, s_scr,
                *, bb, chunk, sub, t_valid):
    ci = pl.program_id(1)

    @pl.when(ci == 0)
    def _():
        s_scr[...] = s0_ref[...]

    bo = bo_ref[...]
    tril = tril_ref[...]
    lane = lax.broadcasted_iota(jnp.int32, (sub, LANE), 1) & (GLA_DK - 1)
    rowi = lax.broadcasted_iota(jnp.int32, (sub, LANE), 0)
    head0_s = lax.broadcasted_iota(jnp.int32, (sub, LANE), 1) < GLA_DK
    head0_c = lax.broadcasted_iota(jnp.int32, (chunk, LANE), 1) < GLA_DK

    for bi in range(bb):
        pg = pg_ref[bi]
        q = pg[:, 0:GLA_QK] * (GLA_DK ** -0.5)
        k = pg[:, GLA_QK:2 * GLA_QK]
        v = pg[:, 2 * GLA_QK:2 * GLA_QK + GLA_WIDTH]
        g = pg[:, 2 * GLA_QK + GLA_WIDTH:2 * GLA_QK + 2 * GLA_WIDTH]
        gl = pg[:, 2 * GLA_QK + 2 * GLA_WIDTH:]
        z = jnp.dot(gl, gw2_ref[...], precision=HIGHEST, preferred_element_type=F32) + gb_ref[...]
        lg = (jnp.minimum(z, 0.0) - jnp.log1p(jnp.exp(-jnp.abs(z)))) * (LOG2E / GLA_GATE_NORM)
        if t_valid < chunk:
            rows = lax.broadcasted_iota(jnp.int32, lg.shape, 0)
            lg = jnp.where(rows < t_valid, lg, 0.0)
        b = jnp.dot(tril, lg, precision=HIGHEST, preferred_element_type=F32)
        eb = jnp.exp2(b)
        blast = b[chunk - 1:chunk, :]
        kl = k * jnp.exp2(blast - b)
        qe = q * eb

        n_blk = chunk // sub
        n_pairs = GLA_HEADS // 2
        ps = []
        for hp in range(n_pairs):
            sl = slice(hp * LANE, (hp + 1) * LANE)
            for blk in range(n_blk):
                rs = slice(blk * sub, (blk + 1) * sub)
                qb, kb, bbk = q[rs, sl], k[rs, sl], b[rs, sl]
                for j in range(sub):
                    ps.append(qb * (kb[j:j + 1] * jnp.exp2(jnp.minimum(bbk - bbk[j:j + 1], 0.0))))
        red = _dot(jnp.concatenate(ps, axis=0).astype(BF16), bo)

        o_heads = []
        for hp in range(n_pairs):
            sl = slice(hp * LANE, (hp + 1) * LANE)
            kp, bp = k[:, sl], b[:, sl]
            row_blocks = []
            for blk in range(n_blk):
                rs = slice(blk * sub, (blk + 1) * sub)
                base = (hp * n_blk + blk) * sub * sub
                a = jnp.zeros((sub, LANE), F32)
                for j in range(sub):
                    a = jnp.where((lane == blk * sub + j) & (rowi >= j), red[base + j * sub:base + (j + 1) * sub], a)
                if blk > 0:
                    bref = bp[blk * sub - 1:blk * sub]
                    qt = q[rs, sl] * jnp.exp2(bp[rs] - bref)
                    kt = (kp * jnp.exp2(jnp.minimum(bref - bp, 0.0))).astype(BF16)
                    qt2 = jnp.concatenate([jnp.where(head0_s, qt, 0.0), jnp.where(head0_s, 0.0, qt)], axis=0)
                    off2 = _dot_nt(qt2.astype(BF16), kt)
                    a = jnp.where(lane < blk * sub, jnp.concatenate([off2[:sub], off2[sub:]], axis=1), a)
                row_blocks.append(a)
            a_pair = row_blocks[0] if n_blk == 1 else jnp.concatenate(row_blocks, axis=0)
            v0 = v[:, 2 * hp * GLA_DV:(2 * hp + 1) * GLA_DV]
            v1 = v[:, (2 * hp + 1) * GLA_DV:(2 * hp + 2) * GLA_DV]
            s_pair = s_scr[bi, 2 * hp:2 * hp + 2].reshape(2 * GLA_DK, GLA_DV)
            qe_p, kl_p = qe[:, sl], kl[:, sl]

            def by_head(x):
                return jnp.concatenate([jnp.where(head0_c, x, 0.0), jnp.where(head0_c, 0.0, x)], axis=0)

            if chunk == GLA_DK:
                v_rows = jnp.concatenate([v0, v1], axis=0)
            else:
                zpad = jnp.zeros((GLA_DK - chunk, GLA_DV), F32)
                v_rows = jnp.concatenate([v0, zpad, v1, zpad], axis=0)
            lhs = jnp.concatenate([by_head(a_pair), by_head(qe_p)], axis=1).astype(BF16)
            rhs = jnp.concatenate([v_rows, s_pair], axis=0).astype(BF16)
            o2 = _dot(lhs, rhs)
            upd = _dot_tn(by_head(kl_p).astype(BF16), jnp.concatenate([v0, v1], axis=0).astype(BF16))
            dcol = jnp.broadcast_to(jnp.exp2(blast[:, sl]), (8, LANE)).T[:, 0:1]
            s_new = dcol * s_pair + upd
            s_scr[bi, 2 * hp] = s_new[:GLA_DK]
            s_scr[bi, 2 * hp + 1] = s_new[GLA_DK:]
            for h2 in range(2):
                o_h = o2[h2 * chunk:(h2 + 1) * chunk]
                o_heads.append(o_h * lax.rsqrt(jnp.mean(o_h * o_h, axis=-1, keepdims=True) + NORM_EPS) * gn_ref[...])
        o = jnp.concatenate(o_heads, axis=1)
        o_ref[bi] = o * (g * _sigmoid(g))

    @pl.when(ci == pl.num_programs(1) - 1)
    def _():
        sout_ref[...] = s_scr[...]


def _gla(pg, s0, gw2p, gb, gn, *, bb, chunk, sub, t_valid):
    b, t, _ = pg.shape
    tril = jnp.tril(jnp.ones((chunk, chunk), F32))
    kern = functools.partial(_gla_kernel, bb=bb, chunk=chunk, sub=sub, t_valid=t_valid)
    return pl.pallas_call(
        kern,
        grid=(b // bb, t // chunk),
        in_specs=[
            pl.BlockSpec((bb, chunk, GLA_PCOLS), lambda i, j: (i, j, 0)),
            pl.BlockSpec((bb, GLA_HEADS, GLA_DK, GLA_DV), lambda i, j: (i, 0, 0, 0)),
            pl.BlockSpec((LANE, GLA_QK), lambda i, j: (0, 0)),
            pl.BlockSpec((1, GLA_QK), lambda i, j: (0, 0)),
            pl.BlockSpec((1, GLA_DV), lambda i, j: (0, 0)),
            pl.BlockSpec((LANE, LANE), lambda i, j: (0, 0)),
            pl.BlockSpec((chunk, chunk), lambda i, j: (0, 0)),
        ],
        out_specs=[
            pl.BlockSpec((bb, chunk, GLA_WIDTH), lambda i, j: (i, j, 0)),
            pl.BlockSpec((bb, GLA_HEADS, GLA_DK, GLA_DV), lambda i, j: (i, 0, 0, 0)),
        ],
        out_shape=[jax.ShapeDtypeStruct((b, t, GLA_WIDTH), F32),
                   jax.ShapeDtypeStruct((b, GLA_HEADS, GLA_DK, GLA_DV), F32)],
        scratch_shapes=[pltpu.VMEM((bb, GLA_HEADS, GLA_DK, GLA_DV), F32)],
        compiler_params=_cparams("parallel", "arbitrary"),
        name="gla_chunk",
    )(pg, s0, gw2p, gb, gn, _block_ones(LANE, GLA_DK), tril)


def _shifted_rows(pr, row0):
    rows = lax.broadcasted_iota(jnp.int32, pr.shape, 0)
    return jnp.where(rows == 0, row0, pltpu.roll(pr, 1, 0))


def _rwkv_pre_math(pr, prev, params, outs, rs, tb0, emit_vt):
    mu_ref, w0_ref, w2_ref, a0_ref, a2_ref, g2_ref, kk_ref, ka_ref, rk_ref, bo_ref = params
    r_out, w_out, k_out, kkn_out, kka_out, v_out, bv_out, gate_out = outs
    n_rows = pr.shape[0]
    xm = pr + (prev - pr) * mu_ref[...]
    wd = RWKV_WIDTH
    rr, rk, rv = xm[:, 0:wd], xm[:, wd:2 * wd], xm[:, 2 * wd:3 * wd]
    wa = xm[:, 3 * wd:3 * wd + LANE]
    gl2 = xm[:, 3 * wd + LANE:3 * wd + 2 * LANE]
    logw = -RWKV_DECAY_SCALE * _sigmoid(w0_ref[...] + _dot(jnp.tanh(wa).astype(BF16), w2_ref[...]))
    aa = _sigmoid(a0_ref[...] + _dot(wa.astype(BF16), a2_ref[...]))
    gate = _dot(_sigmoid(gl2).astype(BF16), g2_ref[...])
    bo = bo_ref[...]
    kk = rk * kk_ref[...]
    kk = kk / jnp.maximum(jnp.sqrt(_group_sum(kk * kk, bo)), 1e-12)
    k = rk * (1.0 + (aa - 1.0) * ka_ref[...])
    bv = _group_sum(rr * k * rk_ref[...], bo) * rv
    w = jnp.exp(logw)
    kka = kk * aa
    hd = RWKV_HEAD

    def pair(x, hp):
        return jnp.concatenate([x[:, hp * hd:(hp + 1) * hd], x[:, (hp + RWKV_PAIRS) * hd:(hp + RWKV_PAIRS + 1) * hd]],
                               axis=1)

    for hp in range(RWKV_PAIRS):
        r_out[0, hp, rs, :] = pair(rr, hp)
        w_out[0, hp, rs, :] = pair(w, hp)
        k_out[0, hp, rs, :] = pair(k, hp)
        kkn_out[0, hp, rs, :] = pair(kk, hp)
        kka_out[0, hp, rs, :] = pair(kka, hp)
    if emit_vt:
        vt = rv.T
        for tb in range(n_rows // REC_TB):
            ts = slice(tb * REC_TB, (tb + 1) * REC_TB)
            for hp in range(RWKV_PAIRS):
                lo, hi = hp * hd, (hp + RWKV_PAIRS) * hd
                v_out[0, tb0 + tb, hp] = jnp.concatenate([vt[lo:lo + hd, ts], vt[hi:hi + hd, ts]],
                                                         axis=1).astype(BF16)
    else:
        v_out[0, rs, :] = rv
    bv_out[0, rs, :] = bv
    gate_out[0, rs, :] = gate


def _rwkv_pre_kernel(pr_ref, aux_ref, *rest, tm, explicit_prev, emit_vt):
    params, outs, carry_scr = rest[:10], rest[10:18], rest[18]
    pr = pr_ref[0]
    if explicit_prev:
        prev = aux_ref[0]
    else:
        j = pl.program_id(1)
        prev = _shifted_rows(pr, jnp.where(j == 0, aux_ref[0], carry_scr[...]))
        carry_scr[...] = pr[tm - 1:tm, :]
    _rwkv_pre_math(pr, prev, params, outs, slice(0, tm), 0, emit_vt)


def _rwkv_pre(pr, aux, params, *, tm, explicit_prev, emit_vt):
    b, t, _ = pr.shape
    kern = functools.partial(_rwkv_pre_kernel, tm=tm, explicit_prev=explicit_prev, emit_vt=emit_vt)
    aux_spec = (pl.BlockSpec((1, tm, RWKV_COLS), lambda i, j: (i, j, 0)) if explicit_prev
                else pl.BlockSpec((1, 1, RWKV_COLS), lambda i, j: (i, 0, 0)))
    const = lambda shape: pl.BlockSpec(shape, lambda i, j: (0,) * len(shape))
    pair_spec = pl.BlockSpec((1, RWKV_PAIRS, tm, LANE), lambda i, j: (i, 0, j, 0))
    row_spec = pl.BlockSpec((1, tm, RWKV_WIDTH), lambda i, j: (i, j, 0))
    pair_shape = jax.ShapeDtypeStruct((b, RWKV_PAIRS, t, LANE), F32)
    row_shape = jax.ShapeDtypeStruct((b, t, RWKV_WIDTH), F32)
    if emit_vt:
        v_spec = pl.BlockSpec((1, tm // REC_TB, RWKV_PAIRS, RWKV_HEAD, LANE), lambda i, j: (i, j, 0, 0, 0))
        v_shape = jax.ShapeDtypeStruct((b, t // REC_TB, RWKV_PAIRS, RWKV_HEAD, LANE), BF16)
    else:
        v_spec, v_shape = row_spec, row_shape
    return pl.pallas_call(
        kern,
        grid=(b, t // tm),
        in_specs=[
            pl.BlockSpec((1, tm, RWKV_COLS), lambda i, j: (i, j, 0)),
            aux_spec,
            const((1, RWKV_COLS)), const((1, RWKV_WIDTH)), const((LANE, RWKV_WIDTH)), const((1, RWKV_WIDTH)),
            const((LANE, RWKV_WIDTH)), const((LANE, RWKV_WIDTH)), const((1, RWKV_WIDTH)), const((1, RWKV_WIDTH)),
            const((1, RWKV_WIDTH)), const((RWKV_WIDTH, LANE)),
        ],
        out_specs=[pair_spec] * 5 + [v_spec, row_spec, row_spec],
        out_shape=[pair_shape] * 5 + [v_shape, row_shape, row_shape],
        scratch_shapes=[pltpu.VMEM((1, RWKV_COLS), F32)],
        compiler_params=_cparams("parallel", "arbitrary"),
        name="rwkv_pre",
    )(pr, aux, *params)


def _rwkv_rec_kernel(r_ref, w_ref, k_ref, kk_ref, kka_ref, vt_ref, s0_ref, bo_ref, vsel_ref, ysel_ref,
                     y_ref, sout_ref, s_scr, t1_scr, t3_scr, yt_scr, rows_scr, *, bb, n_steps):
    tb = pl.program_id(1)
    nc = bb * RWKV_PAIRS
    hd = RWKV_HEAD

    @pl.when(tb == 0)
    def _():
        for c in range(nc):
            bi, hp = divmod(c, RWKV_PAIRS)
            s_scr[c] = jnp.concatenate([s0_ref[bi, hp], s0_ref[bi, hp + RWKV_PAIRS]], axis=1)

    bo = bo_ref[...]

    row_refs = (r_ref, w_ref, k_ref, kk_ref, kka_ref)
    i_r, i_w, i_k, i_kk, i_kka = range(5)

    def step(t, u):
        row = slice(u, u + 1)
        for c in range(nc):
            t1_scr[c * hd:(c + 1) * hd, :] = (s_scr[c] * rows_scr[i_kk, c, row, :]).astype(BF16)
        sab = _dot(t1_scr[...], bo)
        vb = _dot(vt_ref[...].reshape(nc * hd, LANE), vsel_ref[t])
        for c in range(nc):
            rs = slice(c * hd, (c + 1) * hd)
            s2 = (s_scr[c] * rows_scr[i_w, c, row, :] - sab[rs] * rows_scr[i_kka, c, row, :]
                  + vb[rs] * rows_scr[i_k, c, row, :])
            s_scr[c] = s2
            t3_scr[rs, :] = (s2 * rows_scr[i_r, c, row, :]).astype(BF16)
        yt_scr[...] += _dot_nt(ysel_ref[u], t3_scr[...])

    n_inner = min(REC_UNROLL, n_steps)

    def block8(t8, carry):
        yt_scr[...] = jnp.zeros(yt_scr.shape, F32)
        t0 = pl.multiple_of(t8 * 8, 8)
        for a, ref in enumerate(row_refs):
            for c in range(nc):
                bi, hp = divmod(c, RWKV_PAIRS)
                rows_scr[a, c, 0:n_inner, :] = ref[bi, hp, pl.ds(t0, n_inner), :]
        for u in range(n_inner):
            step(t0 + u, u)
        blk = RWKV_PAIRS * hd
        for bi in range(bb):
            for h2 in range(2):
                y_ref[0, pl.ds(t0, 8), bi * RWKV_WIDTH + h2 * blk:bi * RWKV_WIDTH + (h2 + 1) * blk] = (
                    yt_scr[h2 * 8:(h2 + 1) * 8, bi * blk:(bi + 1) * blk])
        return carry

    lax.fori_loop(0, (n_steps + 7) // 8, block8, 0)

    @pl.when(tb == pl.num_programs(1) - 1)
    def _():
        for c in range(nc):
            bi, hp = divmod(c, RWKV_PAIRS)
            s_c = s_scr[c]
            sout_ref[bi, hp] = s_c[:, :RWKV_HEAD]
            sout_ref[bi, hp + RWKV_PAIRS] = s_c[:, RWKV_HEAD:]


def _rwkv_rec(r, w, k, kk, kka, vt, s0, *, bb, n_steps):
    b, _, t, _ = r.shape
    tblk = min(REC_TB, t)
    ntb = t // tblk
    nc = bb * RWKV_PAIRS
    lane = jnp.arange(LANE)
    vsel = ((lane[None, :, None] // RWKV_HEAD == lane[None, None, :] // RWKV_HEAD)
            & (lane[None, :, None] % RWKV_HEAD == jnp.arange(RWKV_HEAD)[:, None, None])).astype(BF16)
    ysel = (jnp.arange(16)[None, :, None]
            == 8 * (lane[None, None, :] // RWKV_HEAD) + jnp.arange(8)[:, None, None]).astype(BF16)
    kern = functools.partial(_rwkv_rec_kernel, bb=bb, n_steps=n_steps)
    pair_spec = pl.BlockSpec((bb, RWKV_PAIRS, tblk, LANE), lambda i, j: (i, 0, j, 0))
    state_spec = pl.BlockSpec((bb, RWKV_HEADS, RWKV_HEAD, RWKV_HEAD), lambda i, j: (i, 0, 0, 0))
    ytb = max(tblk, 8)
    return pl.pallas_call(
        kern,
        grid=(b // bb, ntb),
        in_specs=[pair_spec] * 5 + [
            pl.BlockSpec((bb, 1, RWKV_PAIRS, RWKV_HEAD, LANE), lambda i, j: (i, j, 0, 0, 0)),
            state_spec,
            pl.BlockSpec((LANE, LANE), lambda i, j: (0, 0)),
            pl.BlockSpec((RWKV_HEAD, LANE, LANE), lambda i, j: (0, 0, 0)),
            pl.BlockSpec((8, 16, LANE), lambda i, j: (0, 0, 0)),
        ],
        out_specs=[
            pl.BlockSpec((1, ytb, bb * RWKV_WIDTH), lambda i, j: (i, j, 0)),
            state_spec,
        ],
        out_shape=[jax.ShapeDtypeStruct((b // bb, ntb * ytb, bb * RWKV_WIDTH), F32),
                   jax.ShapeDtypeStruct((b, RWKV_HEADS, RWKV_HEAD, RWKV_HEAD), F32)],
        scratch_shapes=[pltpu.VMEM((nc, RWKV_HEAD, LANE), F32),
                        pltpu.VMEM((nc * RWKV_HEAD, LANE), BF16),
                        pltpu.VMEM((nc * RWKV_HEAD, LANE), BF16),
                        pltpu.VMEM((16, nc * RWKV_HEAD), F32),
                        pltpu.VMEM((5, nc, REC_UNROLL, LANE), F32)],
        compiler_params=_cparams("parallel", "arbitrary"),
        name="rwkv_rec",
    )(r, w, k, kk, kka, vt, s0, _block_ones(LANE, RWKV_HEAD), vsel, ysel)


def _mix_router_body(x_ref, og_ref, y_ref, bv_ref, gate_ref, lnw_ref, lnb_ref, wo_ref, gffn_ref, wr_hi_ref, wr_lo_ref,
                     br_ref, bo_ref, tril_ref, h_ref, n2_ref, route_ref, cnt_scr):
    bo = bo_ref[...]
    y = y_ref[...]
    inv_n = 1.0 / RWKV_HEAD
    d = y - _group_sum(y, bo) * inv_n
    var = _group_sum(d * d, bo) * inv_n
    yn = d * lax.rsqrt(var + RWKV_GN_EPS) * lnw_ref[...] + lnb_ref[...] + bv_ref[...]
    o_rwkv = yn * gate_ref[...]
    mix = (_dot(og_ref[...].astype(BF16), wo_ref[0:GLA_WIDTH, :])
           + _dot(o_rwkv.astype(BF16), wo_ref[GLA_WIDTH:, :]))
    h = x_ref[...] + mix
    h_ref[...] = h
    n2 = h * lax.rsqrt(jnp.mean(h * h, axis=-1, keepdims=True) + NORM_EPS) * gffn_ref[...]
    n2_ref[...] = n2
    n2_hi, n2_lo = _split2(n2)
    lg = (_dot(n2_hi, wr_hi_ref[...]) + _dot(n2_hi, wr_lo_ref[...]) + _dot(n2_lo, wr_hi_ref[...])) + br_ref[...]
    neg = jnp.float32(-3.0e38)
    big = jnp.float32(1.0e9)
    lane = lax.broadcasted_iota(jnp.int32, lg.shape, 1).astype(F32)
    gmask = lane < N_GROUPS
    gmax = jnp.max(jnp.where(gmask, lg, neg), axis=1, keepdims=True)
    p_top = 1.0 / jnp.sum(jnp.where(gmask, jnp.exp(jnp.minimum(lg - gmax, 0.0)), 0.0), axis=1, keepdims=True)
    gidx = jnp.min(jnp.where(gmask & (lg == gmax), lane, big), axis=1, keepdims=True)
    e_lo = EXPERT_LANE0 + gidx * EXPERTS_PER_GROUP
    emask = (lane >= e_lo) & (lane < e_lo + EXPERTS_PER_GROUP)
    m1 = jnp.max(jnp.where(emask, lg, neg), axis=1, keepdims=True)
    e1 = jnp.min(jnp.where(emask & (lg == m1), lane, big), axis=1, keepdims=True)
    emask2 = emask & (lane != e1)
    m2 = jnp.max(jnp.where(emask2, lg, neg), axis=1, keepdims=True)
    e2 = jnp.min(jnp.where(emask2 & (lg == m2), lane, big), axis=1, keepdims=True)
    r21 = jnp.exp(m2 - m1)
    w1 = p_top / (1.0 + r21)
    w2 = p_top * r21 / (1.0 + r21)
    o1 = lane == e1
    o2 = lane == e2
    onehot = jnp.where(o1 | o2, 1.0, 0.0)
    rank = _dot(tril_ref[...], onehot.astype(BF16)) + cnt_scr[...]
    pos1 = jnp.sum(jnp.where(o1, rank, 0.0), axis=1, keepdims=True)
    pos2 = jnp.sum(jnp.where(o2, rank, 0.0), axis=1, keepdims=True)
    cnt_scr[...] += jnp.sum(onehot, axis=0, keepdims=True)
    route = jnp.where(lane == ROUTE_E1, e1 - EXPERT_LANE0, 0.0)
    route = jnp.where(lane == ROUTE_E2, e2 - EXPERT_LANE0, route)
    route = jnp.where(lane == ROUTE_W1, w1, route)
    route = jnp.where(lane == ROUTE_W2, w2, route)
    route = jnp.where(lane == ROUTE_P1, pos1, route)
    route_ref[...] = jnp.where(lane == ROUTE_P2, pos2, route)


def _mix_router_kernel(*refs, n_prompt_tiles):
    prompt_rows, sample_rows, rest = refs[0:5], refs[5:10], refs[10:]
    consts, (h_ref, n2_ref, route_ref, cnt_ref, cnt_scr) = rest[:9], rest[9:]
    i = pl.program_id(0)

    @pl.when(i == 0)
    def _():
        cnt_scr[...] = jnp.zeros(cnt_scr.shape, F32)

    @pl.when(i < n_prompt_tiles)
    def _():
        _mix_router_body(*prompt_rows, *consts, h_ref, n2_ref, route_ref, cnt_scr)

    @pl.when(i >= n_prompt_tiles)
    def _():
        _mix_router_body(*sample_rows, *consts, h_ref, n2_ref, route_ref, cnt_scr)

    cnt_ref[...] = cnt_scr[...]


def _mix_router(prompt_rows, sample_rows, lnw, lnb, wo, gffn, wr, br, *, seq_tiles):
    tm = MOE_TM
    n_p = prompt_rows[0].shape[0] // tm
    assert sample_rows[0].shape[0] == tm
    t = (n_p + 1) * tm
    widths = (D_MODEL, GLA_WIDTH, RWKV_WIDTH, RWKV_WIDTH, RWKV_WIDTH)
    p_specs = [pl.BlockSpec((tm, n), lambda i: (jnp.minimum(i, n_p - 1), 0)) for n in widths]
    p_specs[2] = pl.BlockSpec(
        (tm, RWKV_WIDTH), lambda i: (jnp.minimum(i, n_p - 1) % seq_tiles, jnp.minimum(i, n_p - 1) // seq_tiles))
    s_specs = [pl.BlockSpec((tm, n), lambda i: (0, 0)) for n in widths]
    const = lambda shape: pl.BlockSpec(shape, lambda i: (0,) * len(shape))
    row = lambda n: pl.BlockSpec((tm, n), lambda i: (i, 0))
    tril = jnp.tril(jnp.ones((tm, tm), F32), -1).astype(BF16)
    return pl.pallas_call(
        functools.partial(_mix_router_kernel, n_prompt_tiles=n_p),
        grid=(n_p + 1,),
        in_specs=p_specs + s_specs + [
            const((1, RWKV_WIDTH)), const((1, RWKV_WIDTH)), const((D_MODEL, D_MODEL)), const((1, D_MODEL)),
            const((D_MODEL, ROUTER_LANES)), const((D_MODEL, ROUTER_LANES)), const((1, ROUTER_LANES)),
            const((RWKV_WIDTH, LANE)),
            const((tm, tm))],
        out_specs=[row(D_MODEL), row(D_MODEL), row(ROUTER_LANES), const((1, ROUTER_LANES))],
        out_shape=[jax.ShapeDtypeStruct((t, D_MODEL), F32), jax.ShapeDtypeStruct((t, D_MODEL), F32),
                   jax.ShapeDtypeStruct((t, ROUTER_LANES), F32), jax.ShapeDtypeStruct((1, ROUTER_LANES), F32)],
        scratch_shapes=[pltpu.VMEM((1, ROUTER_LANES), F32)],
        compiler_params=_cparams("arbitrary"),
        name="mix_router",
    )(*prompt_rows, *sample_rows, lnw, lnb, wo, gffn, *_split2(wr), br, _head_selector(), tril)


ROW_DMA_UNROLL = 8


def _dispatch_kernel(slots_ref, x_ref, xs_hbm, sem):
    tm = x_ref.shape[0]

    def issue(r, c):
        src = x_ref.at[pl.ds(r, 1)]
        pltpu.make_async_copy(src, xs_hbm.at[pl.ds(slots_ref[0, 0, 2 * r], 1)], sem).start()
        pltpu.make_async_copy(src, xs_hbm.at[pl.ds(slots_ref[0, 0, 2 * r + 1], 1)], sem).start()
        return c

    lax.fori_loop(0, tm, issue, 0, unroll=ROW_DMA_UNROLL)
    for _ in range(2):
        pltpu.make_async_copy(x_ref, xs_hbm.at[pl.ds(0, tm)], sem).wait()


def _dispatch(n2, slots):
    t = n2.shape[0]
    tm = MOE_TM
    return pl.pallas_call(
        _dispatch_kernel,
        grid_spec=pltpu.PrefetchScalarGridSpec(
            num_scalar_prefetch=0,
            grid=(t // tm,),
            in_specs=[pl.BlockSpec((1, 1, 2 * tm), lambda i: (i, 0, 0), memory_space=pltpu.SMEM),
                      pl.BlockSpec((tm, D_MODEL), lambda i: (i, 0))],
            out_specs=pl.BlockSpec(memory_space=pl.ANY),
            scratch_shapes=[pltpu.SemaphoreType.DMA(())],
        ),
        out_shape=jax.ShapeDtypeStruct((2 * t, D_MODEL), F32),
        compiler_params=_cparams("arbitrary"),
        name="moe_dispatch",
    )(slots, n2)


def _experts_kernel(wt_ref, we_ref, wlo_ref, whi_ref, wfirst_ref, nw_ref,
                    xs_ref, w1_ref, w3_ref, w2_ref, ys_ref, wb1, wb3, wb2):
    w = pl.program_id(0)

    @pl.when(w < nw_ref[0])
    def _():
        new_expert = jnp.logical_or(w == 0, we_ref[w] != we_ref[jnp.maximum(w - 1, 0)])

        @pl.when(new_expert)
        def _():
            wb1[...] = w1_ref[0].astype(BF16)
            wb3[...] = w3_ref[0].astype(BF16)
            wb2[...] = w2_ref[0].astype(BF16)

        x = xs_ref[...].astype(BF16)
        a = _dot(x, wb1[...])
        b = _dot(x, wb3[...])
        o = _dot(((a * _sigmoid(a)) * b).astype(BF16), wb2[...])

        @pl.when(wfirst_ref[w] == 1)
        def _():
            ys_ref[...] = o

        @pl.when(wfirst_ref[w] == 0)
        def _():
            rows = lax.broadcasted_iota(jnp.int32, o.shape, 0)
            ys_ref[...] = jnp.where((rows >= wlo_ref[w]) & (rows < whi_ref[w]), o, ys_ref[...])


def _experts(xs, work, w1, w3, w2):
    s = xs.shape[0]
    ts = MOE_TS
    n_work = work[0].shape[0]
    return pl.pallas_call(
        _experts_kernel,
        grid_spec=pltpu.PrefetchScalarGridSpec(
            num_scalar_prefetch=6,
            grid=(n_work,),
            in_specs=[
                pl.BlockSpec((ts, D_MODEL), lambda w, wt, we, *_: (wt[w], 0)),
                pl.BlockSpec((1, D_MODEL, D_EXPERT), lambda w, wt, we, *_: (we[w], 0, 0)),
                pl.BlockSpec((1, D_MODEL, D_EXPERT), lambda w, wt, we, *_: (we[w], 0, 0)),
                pl.BlockSpec((1, D_EXPERT, D_MODEL), lambda w, wt, we, *_: (we[w], 0, 0)),
            ],
            out_specs=pl.BlockSpec((ts, D_MODEL), lambda w, wt, we, *_: (wt[w], 0)),
            scratch_shapes=[pltpu.VMEM((D_MODEL, D_EXPERT), BF16), pltpu.VMEM((D_MODEL, D_EXPERT), BF16),
                            pltpu.VMEM((D_EXPERT, D_MODEL), BF16)],
        ),
        out_shape=jax.ShapeDtypeStruct((s, D_MODEL), F32),
        compiler_params=_cparams("arbitrary"),
        name="moe_experts",
    )(*work, xs, w1, w3, w2)


def _expert_work_items(counts, total):
    ts = MOE_TS
    n_tiles = total // ts
    n_work = n_tiles + N_EXPERTS - 1
    offs = jnp.cumsum(counts) - counts
    t0 = (jnp.arange(n_tiles, dtype=jnp.int32) * ts)[:, None]
    lo = jnp.maximum(t0, offs[None, :])
    hi = jnp.minimum(t0 + ts, (offs + counts)[None, :])
    nonempty = (hi > lo).reshape(-1)
    nw = jnp.sum(nonempty.astype(jnp.int32))
    idx = jnp.nonzero(nonempty, size=n_work, fill_value=0)[0].astype(jnp.int32)
    idx = jnp.where(jnp.arange(n_work) < nw, idx, idx[jnp.maximum(nw - 1, 0)])
    wt = idx // N_EXPERTS
    we = idx % N_EXPERTS
    wlo = lo.reshape(-1)[idx] - wt * ts
    whi = hi.reshape(-1)[idx] - wt * ts
    wfirst = jnp.concatenate([jnp.ones((1,), jnp.int32), (wt[1:] != wt[:-1]).astype(jnp.int32)])
    return wt, we, wlo, whi, wfirst, nw.reshape(1)


def _combine_kernel(slots_ref, slots_next_ref, h_ref, route_ref, gfin_ref, ys_hbm, yp_ref, ysm_ref, gbuf, sems,
                    *, n_prompt_tiles):
    i = pl.program_id(0)
    n = pl.num_programs(0)
    tm = h_ref.shape[0]

    def gather(s_ref, buf):
        def issue(r, c):
            pltpu.make_async_copy(ys_hbm.at[pl.ds(s_ref[0, 0, 2 * r], 1)], gbuf.at[buf, 0, pl.ds(r, 1)],
                                  sems.at[buf]).start()
            pltpu.make_async_copy(ys_hbm.at[pl.ds(s_ref[0, 0, 2 * r + 1], 1)], gbuf.at[buf, 1, pl.ds(r, 1)],
                                  sems.at[buf]).start()
            return c
        lax.fori_loop(0, tm, issue, 0, unroll=ROW_DMA_UNROLL)

    cur = i % 2

    @pl.when(i == 0)
    def _():
        gather(slots_ref, 0)

    @pl.when(i + 1 < n)
    def _():
        gather(slots_next_ref, 1 - cur)

    for k in range(2):
        pltpu.make_async_copy(ys_hbm.at[pl.ds(0, tm)], gbuf.at[cur, k], sems.at[cur]).wait()
    route = route_ref[...]
    lane = lax.broadcasted_iota(jnp.int32, route.shape, 1)
    w1 = jnp.sum(jnp.where(lane == ROUTE_W1, route, 0.0), axis=1, keepdims=True)
    w2 = jnp.sum(jnp.where(lane == ROUTE_W2, route, 0.0), axis=1, keepdims=True)
    hf = h_ref[...] + (w1 * gbuf[cur, 0] + w2 * gbuf[cur, 1])
    y = hf * lax.rsqrt(jnp.mean(hf * hf, axis=-1, keepdims=True) + NORM_EPS) * gfin_ref[...]

    @pl.when(i < n_prompt_tiles)
    def _():
        yp_ref[...] = y

    @pl.when(i >= n_prompt_tiles)
    def _():
        ysm_ref[...] = y


def _combine(h, route, slots, ys, gfin, n_prompt_tiles):
    t = h.shape[0]
    tm = MOE_TM
    n_p = n_prompt_tiles
    return pl.pallas_call(
        functools.partial(_combine_kernel, n_prompt_tiles=n_p),
        grid_spec=pltpu.PrefetchScalarGridSpec(
            num_scalar_prefetch=0,
            grid=(t // tm,),
            in_specs=[pl.BlockSpec((1, 1, 2 * tm), lambda i: (i, 0, 0), memory_space=pltpu.SMEM),
                      pl.BlockSpec((1, 1, 2 * tm), lambda i: (jnp.minimum(i + 1, t // tm - 1), 0, 0),
                                   memory_space=pltpu.SMEM),
                      pl.BlockSpec((tm, D_MODEL), lambda i: (i, 0)),
                      pl.BlockSpec((tm, ROUTER_LANES), lambda i: (i, 0)),
                      pl.BlockSpec((1, D_MODEL), lambda i: (0, 0)),
                      pl.BlockSpec(memory_space=pl.ANY)],
            out_specs=[pl.BlockSpec((tm, D_MODEL), lambda i: (jnp.minimum(i, n_p - 1), 0)),
                       pl.BlockSpec((tm, D_MODEL), lambda i: (0, 0))],
            scratch_shapes=[pltpu.VMEM((2, 2, tm, D_MODEL), F32), pltpu.SemaphoreType.DMA((2,))],
        ),
        out_shape=[jax.ShapeDtypeStruct((n_p * tm, D_MODEL), F32), jax.ShapeDtypeStruct((tm, D_MODEL), F32)],
        compiler_params=_cparams("arbitrary"),
        name="moe_combine",
    )(slots, slots, h, route, gfin, ys)


def _v_tiles(v, tblk):
    b, t, _ = v.shape
    x = v.reshape(b, t // tblk, tblk, 2, RWKV_PAIRS, RWKV_HEAD).transpose(0, 1, 4, 5, 3, 2)
    x = jnp.pad(x, ((0, 0),) * 5 + ((0, RWKV_HEAD - tblk),))
    return x.reshape(b, t // tblk, RWKV_PAIRS, RWKV_HEAD, LANE).astype(BF16)


def _y_rows(y, bb, t):
    nb, tpad, _ = y.shape
    return y.reshape(nb, tpad, bb, RWKV_WIDTH).transpose(0, 2, 1, 3).reshape(nb * bb, tpad, RWKV_WIDTH)[:, :t]


def kernel(x_prompt, x_sample, state_gla, state_rwkv, state_shift, meta_tokens, norm_mix, w_in, gla_gate_w2,
           gla_gate_b, gla_norm, rwkv_mu, rwkv_w0, rwkv_w2, rwkv_a0, rwkv_a2, rwkv_g2, rwkv_kk, rwkv_ka, rwkv_rk,
           rwkv_ln_w, rwkv_ln_b, w_out, norm_ffn, router_group_w, router_group_b, router_expert_w,
           router_expert_b, moe_w1, moe_w3, moe_w2, norm_final):
    bp, tp, _ = x_prompt.shape
    bs, ts, _ = x_sample.shape
    assert state_gla.shape[0] == 1, "one layer"
    lyr = 0

    w_in_l = w_in[lyr]
    wg = jnp.pad(w_in_l[:, :GLA_COLS], ((0, 0), (0, GLA_PCOLS - GLA_COLS))).astype(BF16)
    wr = w_in_l[:, GLA_COLS:].astype(BF16)
    g_mix = norm_mix[lyr][None, :]
    gw2p = jnp.pad(gla_gate_w2[lyr], ((0, LANE - GLA_GATE_RANK), (0, 0)))
    gb = gla_gate_b[lyr][None, :]
    gn = gla_norm[lyr][None, :]
    w2p = jnp.pad(rwkv_w2[lyr], ((0, 64), (0, 0))).astype(BF16)
    a2p = jnp.pad(rwkv_a2[lyr], ((64, 0), (0, 0))).astype(BF16)
    pre_params = (rwkv_mu[lyr][None, :], rwkv_w0[lyr][None, :], w2p, rwkv_a0[lyr][None, :], a2p,
                  rwkv_g2[lyr].astype(BF16), rwkv_kk[lyr][None, :], rwkv_ka[lyr][None, :],
                  rwkv_rk[lyr].reshape(1, RWKV_WIDTH), _head_selector())
    lnw = rwkv_ln_w[lyr][None, :]
    lnb = rwkv_ln_b[lyr][None, :]
    wo = w_out[lyr].astype(BF16)
    gffn = norm_ffn[lyr][None, :]
    n_used = N_GROUPS + N_EXPERTS
    w_router = jnp.pad(
        jnp.concatenate([router_group_w[lyr],
                         router_expert_w[lyr].transpose(1, 0, 2).reshape(D_MODEL, N_EXPERTS)], axis=1),
        ((0, 0), (0, ROUTER_LANES - n_used)))
    b_router = jnp.pad(jnp.concatenate([router_group_b[lyr], router_expert_b[lyr].reshape(N_EXPERTS)]),
                       (0, ROUTER_LANES - n_used))[None, :]
    gfin = norm_final[None, :]

    pg_m, pr_m = _inproj(meta_tokens, g_mix, wg, wr, N_META)
    _, sg_m = _gla(pg_m[None], jnp.zeros((1, GLA_HEADS, GLA_DK, GLA_DV), F32), gw2p, gb, gn,
                   bb=1, chunk=N_META, sub=N_META, t_valid=N_META)
    r, w, k, kk, kka, v_m, _, _ = _rwkv_pre(pr_m[None], jnp.zeros((1, 1, RWKV_COLS), F32), pre_params,
                                            tm=N_META, explicit_prev=False, emit_vt=False)
    _, sr_m = _rwkv_rec(r, w, k, kk, kka, _v_tiles(v_m, N_META),
                        jnp.zeros((1, RWKV_HEADS, RWKV_HEAD, RWKV_HEAD), F32), bb=1, n_steps=N_META)

    xp = x_prompt.reshape(bp * tp, D_MODEL)
    pg_p, pr_p = _inproj(xp, g_mix, wg, wr, INPROJ_TM)
    og_p, sg_p = _gla(pg_p.reshape(bp, tp, GLA_PCOLS), jnp.broadcast_to(sg_m, (bp,) + sg_m.shape[1:]), gw2p, gb, gn,
                      bb=SEQ_BLOCK, chunk=GLA_CHUNK, sub=GLA_SUB, t_valid=GLA_CHUNK)
    pr_p3 = pr_p.reshape(bp, tp, RWKV_COLS)
    first_prev = jnp.broadcast_to(pr_m[N_META - 1][None, None, :], (bp, 1, RWKV_COLS))
    r, w, k, kk, kka, vt_p, bv_p, gate_p = _rwkv_pre(pr_p3, first_prev, pre_params, tm=PRE_TM, explicit_prev=False,
                                                     emit_vt=True)
    y_p, sr_p = _rwkv_rec(r, w, k, kk, kka, vt_p, jnp.broadcast_to(sr_m, (bp,) + sr_m.shape[1:]),
                          bb=bp, n_steps=REC_TB)
    prompt_rows = (xp, og_p.reshape(bp * tp, GLA_WIDTH), y_p.reshape(tp, bp * RWKV_WIDTH),
                   bv_p.reshape(bp * tp, RWKV_WIDTH), gate_p.reshape(bp * tp, RWKV_WIDTH))

    xs = x_sample.reshape(bs * ts, D_MODEL)
    pg_s, pr_s = _inproj(xs, g_mix, wg, wr, bs * ts)
    ts_pad = 8
    pg_s3 = jnp.pad(pg_s.reshape(bs, ts, GLA_PCOLS), ((0, 0), (0, ts_pad - ts), (0, 0)))
    og_s, sg_s = _gla(pg_s3, state_gla[lyr], gw2p, gb, gn, bb=SEQ_BLOCK, chunk=ts_pad, sub=ts_pad, t_valid=ts)
    og_s = og_s[:, :ts]
    pr_s3 = pr_s.reshape(bs, ts, RWKV_COLS)
    prev_s = jnp.concatenate([state_shift[lyr][:, None, :], pr_s3[:, :-1]], axis=1)
    r, w, k, kk, kka, v_s, bv_s, gate_s = _rwkv_pre(pr_s3.reshape(1, bs * ts, RWKV_COLS),
                                                     prev_s.reshape(1, bs * ts, RWKV_COLS), pre_params,
                                                     tm=bs * ts, explicit_prev=True, emit_vt=False)
    unflat = lambda a: a.reshape(RWKV_PAIRS, bs, ts, LANE).transpose(1, 0, 2, 3)
    y_s, sr_s = _rwkv_rec(unflat(r), unflat(w), unflat(k), unflat(kk), unflat(kka),
                          _v_tiles(v_s.reshape(bs, ts, RWKV_WIDTH), ts), state_rwkv[lyr], bb=SEQ_BLOCK, n_steps=ts)
    y_s = _y_rows(y_s, 8, ts)
    sample_rows = (xs, og_s.reshape(bs * ts, GLA_WIDTH), y_s.reshape(bs * ts, RWKV_WIDTH),
                   bv_s.reshape(bs * ts, RWKV_WIDTH), gate_s.reshape(bs * ts, RWKV_WIDTH))

    h_all, n2_all, route, counts = _mix_router(prompt_rows, sample_rows, lnw, lnb, wo, gffn, w_router, b_router,
                                               seq_tiles=tp // MOE_TM)
    n_tok = h_all.shape[0]
    n_p_tiles = (bp * tp) // MOE_TM
    cnt = counts[0, EXPERT_LANE0:EXPERT_LANE0 + N_EXPERTS].astype(jnp.int32)
    offs = jnp.cumsum(cnt) - cnt
    eid = route[:, ROUTE_E1:ROUTE_E2 + 1].astype(jnp.int32)
    pos = route[:, ROUTE_P1:ROUTE_P2 + 1].astype(jnp.int32)
    onehot = (eid[..., None] == jnp.arange(N_EXPERTS, dtype=jnp.int32)).astype(F32)
    off = jnp.einsum("tke,e->tk", onehot, offs.astype(F32), precision=HIGHEST).astype(jnp.int32)
    slots = (off + pos).reshape(n_tok // MOE_TM, 1, 2 * MOE_TM)
    xs_sorted = _dispatch(n2_all, slots)
    ys_sorted = _experts(xs_sorted, _expert_work_items(cnt, 2 * n_tok), moe_w1[lyr], moe_w3[lyr], moe_w2[lyr])
    y_prompt, y_sample = _combine(h_all, route, slots, ys_sorted, gfin, n_p_tiles)
    y_prompt = y_prompt.reshape(bp, tp, D_MODEL)
    y_sample = y_sample.reshape(bs, ts, D_MODEL)

    return (y_prompt, y_sample,
            sg_p[None], sr_p[None], pr_p3[:, -1][None],
            sg_s[None], sr_s[None], pr_s3[:, -1][None])
```

```python
import functools

import jax
import jax.numpy as jnp
from jax import lax
from jax.experimental import pallas as pl
from jax.experimental.pallas import tpu as pltpu

F32 = jnp.float32
BF16 = jnp.bfloat16
HIGHEST = lax.Precision.HIGHEST

D_MODEL = 1024
N_META = 16
NORM_EPS = 1e-6
LOG2E = 1.4426950408889634
GLA_HEADS = 4
GLA_DK = 64
GLA_DV = 128
GLA_QK = GLA_HEADS * GLA_DK
GLA_WIDTH = GLA_HEADS * GLA_DV
GLA_GATE_RANK = 16
GLA_GATE_NORM = 16.0
GLA_CHUNK = 64
GLA_SUB = 8
GLA_COLS = 2 * GLA_QK + 2 * GLA_WIDTH + GLA_GATE_RANK
GLA_PCOLS = 2 * GLA_QK + 2 * GLA_WIDTH + 128
RWKV_WIDTH = 512
RWKV_HEAD = 64
RWKV_HEADS = 8
RWKV_PAIRS = RWKV_HEADS // 2
RWKV_DECAY_SCALE = 0.606531
RWKV_GN_EPS = 64e-5
RWKV_COLS = 3 * RWKV_WIDTH + 64 + 64 + 128
REC_TB = 64
REC_UNROLL = 8
INPROJ_TM = 512
PRE_TM = 256
SEQ_BLOCK = 8
N_GROUPS = 4
EXPERTS_PER_GROUP = 8
N_EXPERTS = 32
D_EXPERT = 512
ROUTER_LANES = 128
EXPERT_LANE0 = N_GROUPS
ROUTE_E1, ROUTE_E2, ROUTE_W1, ROUTE_W2, ROUTE_P1, ROUTE_P2 = range(6)
MOE_TM = 512
MOE_TS = 512

LANE = 128
VMEM_LIMIT = 56 * 1024 * 1024


def _cparams(*sem):
    return pltpu.CompilerParams(dimension_semantics=sem, vmem_limit_bytes=VMEM_LIMIT)


def _block_ones(n, blk):
    i = jnp.arange(n)
    return (i[:, None] // blk == i[None, :] // blk).astype(BF16)


def _sigmoid(x):
    return 1.0 / (1.0 + jnp.exp(-x))


def _dot(a, b):
    return jnp.dot(a, b, preferred_element_type=F32)


def _dot_nt(a, b):
    return lax.dot_general(a, b, (((1,), (1,)), ((), ())), preferred_element_type=F32)


def _dot_tn(a, b):
    return lax.dot_general(a, b, (((0,), (0,)), ((), ())), preferred_element_type=F32)


def _split2(x):
    hi = x.astype(BF16)
    lo = (x - hi.astype(F32)).astype(BF16)
    return hi, lo


def _head_selector():
    return (jnp.arange(RWKV_WIDTH)[:, None] // RWKV_HEAD == jnp.arange(LANE)[None, :]).astype(BF16)


def _group_sum(x, sel):
    hi, lo = _split2(x)
    s_hi, s_lo = _split2(_dot(hi, sel) + _dot(lo, sel))
    return _dot_nt(s_hi, sel) + _dot_nt(s_lo, sel)


def _inproj_kernel(x_ref, g_ref, wg_ref, wr_ref, pg_ref, pr_ref):
    x = x_ref[...]
    n = x * lax.rsqrt(jnp.mean(x * x, axis=-1, keepdims=True) + NORM_EPS) * g_ref[...]
    nb = n.astype(BF16)
    pg_ref[...] = _dot(nb, wg_ref[...])
    pr_ref[...] = _dot(nb, wr_ref[...])


def _inproj(x, g, wg, wr, tm):
    t = x.shape[0]
    return pl.pallas_call(
        _inproj_kernel,
        grid=(t // tm,),
        in_specs=[
            pl.BlockSpec((tm, D_MODEL), lambda i: (i, 0)),
            pl.BlockSpec((1, D_MODEL), lambda i: (0, 0)),
            pl.BlockSpec((D_MODEL, GLA_PCOLS), lambda i: (0, 0)),
            pl.BlockSpec((D_MODEL, RWKV_COLS), lambda i: (0, 0)),
        ],
        out_specs=[
            pl.BlockSpec((tm, GLA_PCOLS), lambda i: (i, 0)),
            pl.BlockSpec((tm, RWKV_COLS), lambda i: (i, 0)),
        ],
        out_shape=[jax.ShapeDtypeStruct((t, GLA_PCOLS), F32), jax.ShapeDtypeStruct((t, RWKV_COLS), F32)],
        compiler_params=_cparams("parallel"),
        name="inproj",
    )(x, g, wg, wr)


def _gla_kernel(pg_ref, s0_ref, gw2_ref, gb_ref, gn_ref, bo_ref, tril_ref, o_ref, sout_ref, s_scr,
                *, bb, chunk, sub, t_valid):
    ci = pl.program_id(1)

    @pl.when(ci == 0)
    def _():
        s_scr[...] = s0_ref[...]

    bo = bo_ref[...]
    tril = tril_ref[...]
    lane = lax.broadcasted_iota(jnp.int32, (sub, LANE), 1) & (GLA_DK - 1)
    rowi = lax.broadcasted_iota(jnp.int32, (sub, LANE), 0)
    head0_s = lax.broadcasted_iota(jnp.int32, (sub, LANE), 1) < GLA_DK
    head0_c = lax.broadcasted_iota(jnp.int32, (chunk, LANE), 1) < GLA_DK

    for bi in range(bb):
        pg = pg_ref[bi]
        q = pg[:, 0:GLA_QK] * (GLA_DK ** -0.5)
        k = pg[:, GLA_QK:2 * GLA_QK]
        v = pg[:, 2 * GLA_QK:2 * GLA_QK + GLA_WIDTH]
        g = pg[:, 2 * GLA_QK + GLA_WIDTH:2 * GLA_QK + 2 * GLA_WIDTH]
        gl = pg[:, 2 * GLA_QK + 2 * GLA_WIDTH:]
        z = jnp.dot(gl, gw2_ref[...], precision=HIGHEST, preferred_element_type=F32) + gb_ref[...]
        lg = (jnp.minimum(z, 0.0) - jnp.log1p(jnp.exp(-jnp.abs(z)))) * (LOG2E / GLA_GATE_NORM)
        if t_valid < chunk:
            rows = lax.broadcasted_iota(jnp.int32, lg.shape, 0)
            lg = jnp.where(rows < t_valid, lg, 0.0)
        b = jnp.dot(tril, lg, precision=HIGHEST, preferred_element_type=F32)
        eb = jnp.exp2(b)
        blast = b[chunk - 1:chunk, :]
        kl = k * jnp.exp2(blast - b)
        qe = q * eb

        n_blk = chunk // sub
        n_pairs = GLA_HEADS // 2
        ps = []
        for hp in range(n_pairs):
            sl = slice(hp * LANE, (hp + 1) * LANE)
            for blk in range(n_blk):
                rs = slice(blk * sub, (blk + 1) * sub)
                qb, kb, bbk = q[rs, sl], k[rs, sl], b[rs, sl]
                for j in range(sub):
                    ps.append(qb * (kb[j:j + 1] * jnp.exp2(jnp.minimum(bbk - bbk[j:j + 1], 0.0))))
        red = _dot(jnp.concatenate(ps, axis=0).astype(BF16), bo)

        o_heads = []
        for hp in range(n_pairs):
            sl = slice(hp * LANE, (hp + 1) * LANE)
            kp, bp = k[:, sl], b[:, sl]
            row_blocks = []
            for blk in range(n_blk):
                rs = slice(blk * sub, (blk + 1) * sub)
                base = (hp * n_blk + blk) * sub * sub
                a = jnp.zeros((sub, LANE), F32)
                for j in range(sub):
                    a = jnp.where((lane == blk * sub + j) & (rowi >= j), red[base + j * sub:base + (j + 1) * sub], a)
                if blk > 0:
                    bref = bp[blk * sub - 1:blk * sub]
                    qt = q[rs, sl] * jnp.exp2(bp[rs] - bref)
                    kt = (kp * jnp.exp2(jnp.minimum(bref - bp, 0.0))).astype(BF16)
                    qt2 = jnp.concatenate([jnp.where(head0_s, qt, 0.0), jnp.where(head0_s, 0.0, qt)], axis=0)
                    off2 = _dot_nt(qt2.astype(BF16), kt)
                    a = jnp.where(lane < blk * sub, jnp.concatenate([off2[:sub], off2[sub:]], axis=1), a)
                row_blocks.append(a)
            a_pair = row_blocks[0] if n_blk == 1 else jnp.concatenate(row_blocks, axis=0)
            v0 = v[:, 2 * hp * GLA_DV:(2 * hp + 1) * GLA_DV]
            v1 = v[:, (2 * hp + 1) * GLA_DV:(2 * hp + 2) * GLA_DV]
            s_pair = s_scr[bi, 2 * hp:2 * hp + 2].reshape(2 * GLA_DK, GLA_DV)
            qe_p, kl_p = qe[:, sl], kl[:, sl]

            def by_head(x):
                return jnp.concatenate([jnp.where(head0_c, x, 0.0), jnp.where(head0_c, 0.0, x)], axis=0)

            if chunk == GLA_DK:
                v_rows = jnp.concatenate([v0, v1], axis=0)
            else:
                zpad = jnp.zeros((GLA_DK - chunk, GLA_DV), F32)
                v_rows = jnp.concatenate([v0, zpad, v1, zpad], axis=0)
            lhs = jnp.concatenate([by_head(a_pair), by_head(qe_p)], axis=1).astype(BF16)
            rhs = jnp.concatenate([v_rows, s_pair], axis=0).astype(BF16)
            o2 = _dot(lhs, rhs)
            upd = _dot_tn(by_head(kl_p).astype(BF16), jnp.concatenate([v0, v1], axis=0).astype(BF16))
            dcol = jnp.broadcast_to(jnp.exp2(blast[:, sl]), (8, LANE)).T[:, 0:1]
            s_new = dcol * s_pair + upd
            s_scr[bi, 2 * hp] = s_new[:GLA_DK]
            s_scr[bi, 2 * hp + 1] = s_new[GLA_DK:]
            for h2 in range(2):
                o_h = o2[h2 * chunk:(h2 + 1) * chunk]
                o_heads.append(o_h * lax.rsqrt(jnp.mean(o_h * o_h, axis=-1, keepdims=True) + NORM_EPS) * gn_ref[...])
        o = jnp.concatenate(o_heads, axis=1)
        o_ref[bi] = o * (g * _sigmoid(g))

    @pl.when(ci == pl.num_programs(1) - 1)
    def _():
        sout_ref[...] = s_scr[...]


def _gla(pg, s0, gw2p, gb, gn, *, bb, chunk, sub, t_valid):
    b, t, _ = pg.shape
    tril = jnp.tril(jnp.ones((chunk, chunk), F32))
    kern = functools.partial(_gla_kernel, bb=bb, chunk=chunk, sub=sub, t_valid=t_valid)
    return pl.pallas_call(
        kern,
        grid=(b // bb, t // chunk),
        in_specs=[
            pl.BlockSpec((bb, chunk, GLA_PCOLS), lambda i, j: (i, j, 0)),
            pl.BlockSpec((bb, GLA_HEADS, GLA_DK, GLA_DV), lambda i, j: (i, 0, 0, 0)),
            pl.BlockSpec((LANE, GLA_QK), lambda i, j: (0, 0)),
            pl.BlockSpec((1, GLA_QK), lambda i, j: (0, 0)),
            pl.BlockSpec((1, GLA_DV), lambda i, j: (0, 0)),
            pl.BlockSpec((LANE, LANE), lambda i, j: (0, 0)),
            pl.BlockSpec((chunk, chunk), lambda i, j: (0, 0)),
        ],
        out_specs=[
            pl.BlockSpec((bb, chunk, GLA_WIDTH), lambda i, j: (i, j, 0)),
            pl.BlockSpec((bb, GLA_HEADS, GLA_DK, GLA_DV), lambda i, j: (i, 0, 0, 0)),
        ],
        out_shape=[jax.ShapeDtypeStruct((b, t, GLA_WIDTH), F32),
                   jax.ShapeDtypeStruct((b, GLA_HEADS, GLA_DK, GLA_DV), F32)],
        scratch_shapes=[pltpu.VMEM((bb, GLA_HEADS, GLA_DK, GLA_DV), F32)],
        compiler_params=_cparams("parallel", "arbitrary"),
        name="gla_chunk",
    )(pg, s0, gw2p, gb, gn, _block_ones(LANE, GLA_DK), tril)


def _shifted_rows(pr, row0):
    rows = lax.broadcasted_iota(jnp.int32, pr.shape, 0)
    return jnp.where(rows == 0, row0, pltpu.roll(pr, 1, 0))


def _rwkv_pre_math(pr, prev, params, outs, rs, tb0, emit_vt):
    mu_ref, w0_ref, w2_ref, a0_ref, a2_ref, g2_ref, kk_ref, ka_ref, rk_ref, bo_ref = params
    r_out, w_out, k_out, kkn_out, kka_out, v_out, bv_out, gate_out = outs
    n_rows = pr.shape[0]
    xm = pr + (prev - pr) * mu_ref[...]
    wd = RWKV_WIDTH
    rr, rk, rv = xm[:, 0:wd], xm[:, wd:2 * wd], xm[:, 2 * wd:3 * wd]
    wa = xm[:, 3 * wd:3 * wd + LANE]
    gl2 = xm[:, 3 * wd + LANE:3 * wd + 2 * LANE]
    logw = -RWKV_DECAY_SCALE * _sigmoid(w0_ref[...] + _dot(jnp.tanh(wa).astype(BF16), w2_ref[...]))
    aa = _sigmoid(a0_ref[...] + _dot(wa.astype(BF16), a2_ref[...]))
    gate = _dot(_sigmoid(gl2).astype(BF16), g2_ref[...])
    bo = bo_ref[...]
    kk = rk * kk_ref[...]
    kk = kk / jnp.maximum(jnp.sqrt(_group_sum(kk * kk, bo)), 1e-12)
    k = rk * (1.0 + (aa - 1.0) * ka_ref[...])
    bv = _group_sum(rr * k * rk_ref[...], bo) * rv
    w = jnp.exp(logw)
    kka = kk * aa
    hd = RWKV_HEAD

    def pair(x, hp):
        return jnp.concatenate([x[:, hp * hd:(hp + 1) * hd], x[:, (hp + RWKV_PAIRS) * hd:(hp + RWKV_PAIRS + 1) * hd]],
                               axis=1)

    for hp in range(RWKV_PAIRS):
        r_out[0, hp, rs, :] = pair(rr, hp)
        w_out[0, hp, rs, :] = pair(w, hp)
        k_out[0, hp, rs, :] = pair(k, hp)
        kkn_out[0, hp, rs, :] = pair(kk, hp)
        kka_out[0, hp, rs, :] = pair(kka, hp)
    if emit_vt:
        vt = rv.T
        for tb in range(n_rows // REC_TB):
            ts = slice(tb * REC_TB, (tb + 1) * REC_TB)
            for hp in range(RWKV_PAIRS):
                lo, hi = hp * hd, (hp + RWKV_PAIRS) * hd
                v_out[0, tb0 + tb, hp] = jnp.concatenate([vt[lo:lo + hd, ts], vt[hi:hi + hd, ts]],
                                                         axis=1).astype(BF16)
    else:
        v_out[0, rs, :] = rv
    bv_out[0, rs, :] = bv
    gate_out[0, rs, :] = gate


def _rwkv_pre_kernel(pr_ref, aux_ref, *rest, tm, explicit_prev, emit_vt):
    params, outs, carry_scr = rest[:10], rest[10:18], rest[18]
    pr = pr_ref[0]
    if explicit_prev:
        prev = aux_ref[0]
    else:
        j = pl.program_id(1)
        prev = _shifted_rows(pr, jnp.where(j == 0, aux_ref[0], carry_scr[...]))
        carry_scr[...] = pr[tm - 1:tm, :]
    _rwkv_pre_math(pr, prev, params, outs, slice(0, tm), 0, emit_vt)


def _rwkv_pre(pr, aux, params, *, tm, explicit_prev, emit_vt):
    b, t, _ = pr.shape
    kern = functools.partial(_rwkv_pre_kernel, tm=tm, explicit_prev=explicit_prev, emit_vt=emit_vt)
    aux_spec = (pl.BlockSpec((1, tm, RWKV_COLS), lambda i, j: (i, j, 0)) if explicit_prev
                else pl.BlockSpec((1, 1, RWKV_COLS), lambda i, j: (i, 0, 0)))
    const = lambda shape: pl.BlockSpec(shape, lambda i, j: (0,) * len(shape))
    pair_spec = pl.BlockSpec((1, RWKV_PAIRS, tm, LANE), lambda i, j: (i, 0, j, 0))
    row_spec = pl.BlockSpec((1, tm, RWKV_WIDTH), lambda i, j: (i, j, 0))
    pair_shape = jax.ShapeDtypeStruct((b, RWKV_PAIRS, t, LANE), F32)
    row_shape = jax.ShapeDtypeStruct((b, t, RWKV_WIDTH), F32)
    if emit_vt:
        v_spec = pl.BlockSpec((1, tm // REC_TB, RWKV_PAIRS, RWKV_HEAD, LANE), lambda i, j: (i, j, 0, 0, 0))
        v_shape = jax.ShapeDtypeStruct((b, t // REC_TB, RWKV_PAIRS, RWKV_HEAD, LANE), BF16)
    else:
        v_spec, v_shape = row_spec, row_shape
    return pl.pallas_call(
        kern,
        grid=(b, t // tm),
        in_specs=[
            pl.BlockSpec((1, tm, RWKV_COLS), lambda i, j: (i, j, 0)),
            aux_spec,
            const((1, RWKV_COLS)), const((1, RWKV_WIDTH)), const((LANE, RWKV_WIDTH)), const((1, RWKV_WIDTH)),
            const((LANE, RWKV_WIDTH)), const((LANE, RWKV_WIDTH)), const((1, RWKV_WIDTH)), const((1, RWKV_WIDTH)),
            const((1, RWKV_WIDTH)), const((RWKV_WIDTH, LANE)),
        ],
        out_specs=[pair_spec] * 5 + [v_spec, row_spec, row_spec],
        out_shape=[pair_shape] * 5 + [v_shape, row_shape, row_shape],
        scratch_shapes=[pltpu.VMEM((1, RWKV_COLS), F32)],
        compiler_params=_cparams("parallel", "arbitrary"),
        name="rwkv_pre",
    )(pr, aux, *params)


def _rwkv_rec_kernel(r_ref, w_ref, k_ref, kk_ref, kka_ref, vt_ref, s0_ref, bo_ref, vsel_ref, ysel_ref,
                     y_ref, sout_ref, s_scr, t1_scr, t3_scr, yt_scr, *, bb, n_steps):
    tb = pl.program_id(1)
    nc = bb * RWKV_PAIRS
    hd = RWKV_HEAD

    @pl.when(tb == 0)
    def _():
        for c in range(nc):
            bi, hp = divmod(c, RWKV_PAIRS)
            s_scr[c] = jnp.concatenate([s0_ref[bi, hp], s0_ref[bi, hp + RWKV_PAIRS]], axis=1)

    bo = bo_ref[...]

    def step(t, u):
        row = pl.ds(t, 1)
        for c in range(nc):
            bi, hp = divmod(c, RWKV_PAIRS)
            t1_scr[c * hd:(c + 1) * hd, :] = (s_scr[c] * kk_ref[bi, hp, row, :]).astype(BF16)
        sab = _dot(t1_scr[...], bo)
        vb = _dot(vt_ref[...].reshape(nc * hd, LANE), vsel_ref[t])
        for c in range(nc):
            bi, hp = divmod(c, RWKV_PAIRS)
            rs = slice(c * hd, (c + 1) * hd)
            s2 = (s_scr[c] * w_ref[bi, hp, row, :] - sab[rs] * kka_ref[bi, hp, row, :]
                  + vb[rs] * k_ref[bi, hp, row, :])
            s_scr[c] = s2
            t3_scr[rs, :] = (s2 * r_ref[bi, hp, row, :]).astype(BF16)
        yt_scr[...] += _dot_nt(ysel_ref[u], t3_scr[...])

    n_inner = min(8, n_steps)

    def block8(t8, carry):
        yt_scr[...] = jnp.zeros(yt_scr.shape, F32)

        def inner(u, c2):
            step(t8 * 8 + u, u)
            return c2

        lax.fori_loop(0, n_inner, inner, 0, unroll=REC_UNROLL)
        t0 = pl.multiple_of(t8 * 8, 8)
        blk = RWKV_PAIRS * hd
        for bi in range(bb):
            for h2 in range(2):
                y_ref[0, pl.ds(t0, 8), bi * RWKV_WIDTH + h2 * blk:bi * RWKV_WIDTH + (h2 + 1) * blk] = (
                    yt_scr[h2 * 8:(h2 + 1) * 8, bi * blk:(bi + 1) * blk])
        return carry

    lax.fori_loop(0, (n_steps + 7) // 8, block8, 0)

    @pl.when(tb == pl.num_programs(1) - 1)
    def _():
        for c in range(nc):
            bi, hp = divmod(c, RWKV_PAIRS)
            s_c = s_scr[c]
            sout_ref[bi, hp] = s_c[:, :RWKV_HEAD]
            sout_ref[bi, hp + RWKV_PAIRS] = s_c[:, RWKV_HEAD:]


def _rwkv_rec(r, w, k, kk, kka, vt, s0, *, bb, n_steps):
    b, _, t, _ = r.shape
    tblk = min(REC_TB, t)
    ntb = t // tblk
    nc = bb * RWKV_PAIRS
    lane = jnp.arange(LANE)
    vsel = ((lane[None, :, None] // RWKV_HEAD == lane[None, None, :] // RWKV_HEAD)
            & (lane[None, :, None] % RWKV_HEAD == jnp.arange(RWKV_HEAD)[:, None, None])).astype(BF16)
    ysel = (jnp.arange(16)[None, :, None]
            == 8 * (lane[None, None, :] // RWKV_HEAD) + jnp.arange(8)[:, None, None]).astype(BF16)
    kern = functools.partial(_rwkv_rec_kernel, bb=bb, n_steps=n_steps)
    pair_spec = pl.BlockSpec((bb, RWKV_PAIRS, tblk, LANE), lambda i, j: (i, 0, j, 0))
    state_spec = pl.BlockSpec((bb, RWKV_HEADS, RWKV_HEAD, RWKV_HEAD), lambda i, j: (i, 0, 0, 0))
    ytb = max(tblk, 8)
    return pl.pallas_call(
        kern,
        grid=(b // bb, ntb),
        in_specs=[pair_spec] * 5 + [
            pl.BlockSpec((bb, 1, RWKV_PAIRS, RWKV_HEAD, LANE), lambda i, j: (i, j, 0, 0, 0)),
            state_spec,
            pl.BlockSpec((LANE, LANE), lambda i, j: (0, 0)),
            pl.BlockSpec((RWKV_HEAD, LANE, LANE), lambda i, j: (0, 0, 0)),
            pl.BlockSpec((8, 16, LANE), lambda i, j: (0, 0, 0)),
        ],
        out_specs=[
            pl.BlockSpec((1, ytb, bb * RWKV_WIDTH), lambda i, j: (i, j, 0)),
            state_spec,
        ],
        out_shape=[jax.ShapeDtypeStruct((b // bb, ntb * ytb, bb * RWKV_WIDTH), F32),
                   jax.ShapeDtypeStruct((b, RWKV_HEADS, RWKV_HEAD, RWKV_HEAD), F32)],
        scratch_shapes=[pltpu.VMEM((nc, RWKV_HEAD, LANE), F32),
                        pltpu.VMEM((nc * RWKV_HEAD, LANE), BF16),
                        pltpu.VMEM((nc * RWKV_HEAD, LANE), BF16),
                        pltpu.VMEM((16, nc * RWKV_HEAD), F32)],
        compiler_params=_cparams("parallel", "arbitrary"),
        name="rwkv_rec",
    )(r, w, k, kk, kka, vt, s0, _block_ones(LANE, RWKV_HEAD), vsel, ysel)


def _mix_router_body(x_ref, og_ref, y_ref, bv_ref, gate_ref, lnw_ref, lnb_ref, wo_ref, gffn_ref, wr_hi_ref, wr_lo_ref,
                     br_ref, bo_ref, tril_ref, h_ref, n2_ref, route_ref, cnt_scr):
    bo = bo_ref[...]
    y = y_ref[...]
    inv_n = 1.0 / RWKV_HEAD
    d = y - _group_sum(y, bo) * inv_n
    var = _group_sum(d * d, bo) * inv_n
    yn = d * lax.rsqrt(var + RWKV_GN_EPS) * lnw_ref[...] + lnb_ref[...] + bv_ref[...]
    o_rwkv = yn * gate_ref[...]
    mix = (_dot(og_ref[...].astype(BF16), wo_ref[0:GLA_WIDTH, :])
           + _dot(o_rwkv.astype(BF16), wo_ref[GLA_WIDTH:, :]))
    h = x_ref[...] + mix
    h_ref[...] = h
    n2 = h * lax.rsqrt(jnp.mean(h * h, axis=-1, keepdims=True) + NORM_EPS) * gffn_ref[...]
    n2_ref[...] = n2
    n2_hi, n2_lo = _split2(n2)
    lg = (_dot(n2_hi, wr_hi_ref[...]) + _dot(n2_hi, wr_lo_ref[...]) + _dot(n2_lo, wr_hi_ref[...])) + br_ref[...]
    neg = jnp.float32(-3.0e38)
    big = jnp.float32(1.0e9)
    lane = lax.broadcasted_iota(jnp.int32, lg.shape, 1).astype(F32)
    gmask = lane < N_GROUPS
    gmax = jnp.max(jnp.where(gmask, lg, neg), axis=1, keepdims=True)
    p_top = 1.0 / jnp.sum(jnp.where(gmask, jnp.exp(jnp.minimum(lg - gmax, 0.0)), 0.0), axis=1, keepdims=True)
    gidx = jnp.min(jnp.where(gmask & (lg == gmax), lane, big), axis=1, keepdims=True)
    e_lo = EXPERT_LANE0 + gidx * EXPERTS_PER_GROUP
    emask = (lane >= e_lo) & (lane < e_lo + EXPERTS_PER_GROUP)
    m1 = jnp.max(jnp.where(emask, lg, neg), axis=1, keepdims=True)
    e1 = jnp.min(jnp.where(emask & (lg == m1), lane, big), axis=1, keepdims=True)
    emask2 = emask & (lane != e1)
    m2 = jnp.max(jnp.where(emask2, lg, neg), axis=1, keepdims=True)
    e2 = jnp.min(jnp.where(emask2 & (lg == m2), lane, big), axis=1, keepdims=True)
    r21 = jnp.exp(m2 - m1)
    w1 = p_top / (1.0 + r21)
    w2 = p_top * r21 / (1.0 + r21)
    o1 = lane == e1
    o2 = lane == e2
    onehot = jnp.where(o1 | o2, 1.0, 0.0)
    rank = _dot(tril_ref[...], onehot.astype(BF16)) + cnt_scr[...]
    pos1 = jnp.sum(jnp.where(o1, rank, 0.0), axis=1, keepdims=True)
    pos2 = jnp.sum(jnp.where(o2, rank, 0.0), axis=1, keepdims=True)
    cnt_scr[...] += jnp.sum(onehot, axis=0, keepdims=True)
    route = jnp.where(lane == ROUTE_E1, e1 - EXPERT_LANE0, 0.0)
    route = jnp.where(lane == ROUTE_E2, e2 - EXPERT_LANE0, route)
    route = jnp.where(lane == ROUTE_W1, w1, route)
    route = jnp.where(lane == ROUTE_W2, w2, route)
    route = jnp.where(lane == ROUTE_P1, pos1, route)
    route_ref[...] = jnp.where(lane == ROUTE_P2, pos2, route)


def _mix_router_kernel(*refs, n_prompt_tiles):
    prompt_rows, sample_rows, rest = refs[0:5], refs[5:10], refs[10:]
    consts, (h_ref, n2_ref, route_ref, cnt_ref, cnt_scr) = rest[:9], rest[9:]
    i = pl.program_id(0)

    @pl.when(i == 0)
    def _():
        cnt_scr[...] = jnp.zeros(cnt_scr.shape, F32)

    @pl.when(i < n_prompt_tiles)
    def _():
        _mix_router_body(*prompt_rows, *consts, h_ref, n2_ref, route_ref, cnt_scr)

    @pl.when(i >= n_prompt_tiles)
    def _():
        _mix_router_body(*sample_rows, *consts, h_ref, n2_ref, route_ref, cnt_scr)

    cnt_ref[...] = cnt_scr[...]


def _mix_router(prompt_rows, sample_rows, lnw, lnb, wo, gffn, wr, br, *, seq_tiles):
    tm = MOE_TM
    n_p = prompt_rows[0].shape[0] // tm
    assert sample_rows[0].shape[0] == tm
    t = (n_p + 1) * tm
    widths = (D_MODEL, GLA_WIDTH, RWKV_WIDTH, RWKV_WIDTH, RWKV_WIDTH)
    p_specs = [pl.BlockSpec((tm, n), lambda i: (jnp.minimum(i, n_p - 1), 0)) for n in widths]
    p_specs[2] = pl.BlockSpec(
        (tm, RWKV_WIDTH), lambda i: (jnp.minimum(i, n_p - 1) % seq_tiles, jnp.minimum(i, n_p - 1) // seq_tiles))
    s_specs = [pl.BlockSpec((tm, n), lambda i: (0, 0)) for n in widths]
    const = lambda shape: pl.BlockSpec(shape, lambda i: (0,) * len(shape))
    row = lambda n: pl.BlockSpec((tm, n), lambda i: (i, 0))
    tril = jnp.tril(jnp.ones((tm, tm), F32), -1).astype(BF16)
    return pl.pallas_call(
        functools.partial(_mix_router_kernel, n_prompt_tiles=n_p),
        grid=(n_p + 1,),
        in_specs=p_specs + s_specs + [
            const((1, RWKV_WIDTH)), const((1, RWKV_WIDTH)), const((D_MODEL, D_MODEL)), const((1, D_MODEL)),
            const((D_MODEL, ROUTER_LANES)), const((D_MODEL, ROUTER_LANES)), const((1, ROUTER_LANES)),
            const((RWKV_WIDTH, LANE)),
            const((tm, tm))],
        out_specs=[row(D_MODEL), row(D_MODEL), row(ROUTER_LANES), const((1, ROUTER_LANES))],
        out_shape=[jax.ShapeDtypeStruct((t, D_MODEL), F32), jax.ShapeDtypeStruct((t, D_MODEL), F32),
                   jax.ShapeDtypeStruct((t, ROUTER_LANES), F32), jax.ShapeDtypeStruct((1, ROUTER_LANES), F32)],
        scratch_shapes=[pltpu.VMEM((1, ROUTER_LANES), F32)],
        compiler_params=_cparams("arbitrary"),
        name="mix_router",
    )(*prompt_rows, *sample_rows, lnw, lnb, wo, gffn, *_split2(wr), br, _head_selector(), tril)


def _dispatch_kernel(slots_ref, x_ref, xs_hbm, sem):
    tm = x_ref.shape[0]

    for r in range(tm):
        src = x_ref.at[pl.ds(r, 1)]
        pltpu.make_async_copy(src, xs_hbm.at[pl.ds(slots_ref[0, 0, 2 * r], 1)], sem).start()
        pltpu.make_async_copy(src, xs_hbm.at[pl.ds(slots_ref[0, 0, 2 * r + 1], 1)], sem).start()
    for _ in range(2):
        pltpu.make_async_copy(x_ref, xs_hbm.at[pl.ds(0, tm)], sem).wait()


def _dispatch(n2, slots):
    t = n2.shape[0]
    tm = MOE_TM
    return pl.pallas_call(
        _dispatch_kernel,
        grid_spec=pltpu.PrefetchScalarGridSpec(
            num_scalar_prefetch=0,
            grid=(t // tm,),
            in_specs=[pl.BlockSpec((1, 1, 2 * tm), lambda i: (i, 0, 0), memory_space=pltpu.SMEM),
                      pl.BlockSpec((tm, D_MODEL), lambda i: (i, 0))],
            out_specs=pl.BlockSpec(memory_space=pl.ANY),
            scratch_shapes=[pltpu.SemaphoreType.DMA(())],
        ),
        out_shape=jax.ShapeDtypeStruct((2 * t, D_MODEL), F32),
        compiler_params=_cparams("arbitrary"),
        name="moe_dispatch",
    )(slots, n2)


def _experts_kernel(wt_ref, we_ref, wlo_ref, whi_ref, wfirst_ref, nw_ref,
                    xs_ref, w1_ref, w3_ref, w2_ref, ys_ref, wb1, wb3, wb2):
    w = pl.program_id(0)

    @pl.when(w < nw_ref[0])
    def _():
        new_expert = jnp.logical_or(w == 0, we_ref[w] != we_ref[jnp.maximum(w - 1, 0)])

        @pl.when(new_expert)
        def _():
            wb1[...] = w1_ref[0].astype(BF16)
            wb3[...] = w3_ref[0].astype(BF16)
            wb2[...] = w2_ref[0].astype(BF16)

        x = xs_ref[...].astype(BF16)
        a = _dot(x, wb1[...])
        b = _dot(x, wb3[...])
        o = _dot(((a * _sigmoid(a)) * b).astype(BF16), wb2[...])

        @pl.when(wfirst_ref[w] == 1)
        def _():
            ys_ref[...] = o

        @pl.when(wfirst_ref[w] == 0)
        def _():
            rows = lax.broadcasted_iota(jnp.int32, o.shape, 0)
            ys_ref[...] = jnp.where((rows >= wlo_ref[w]) & (rows < whi_ref[w]), o, ys_ref[...])


def _experts(xs, work, w1, w3, w2):
    s = xs.shape[0]
    ts = MOE_TS
    n_work = work[0].shape[0]
    return pl.pallas_call(
        _experts_kernel,
        grid_spec=pltpu.PrefetchScalarGridSpec(
            num_scalar_prefetch=6,
            grid=(n_work,),
            in_specs=[
                pl.BlockSpec((ts, D_MODEL), lambda w, wt, we, *_: (wt[w], 0)),
                pl.BlockSpec((1, D_MODEL, D_EXPERT), lambda w, wt, we, *_: (we[w], 0, 0)),
                pl.BlockSpec((1, D_MODEL, D_EXPERT), lambda w, wt, we, *_: (we[w], 0, 0)),
                pl.BlockSpec((1, D_EXPERT, D_MODEL), lambda w, wt, we, *_: (we[w], 0, 0)),
            ],
            out_specs=pl.BlockSpec((ts, D_MODEL), lambda w, wt, we, *_: (wt[w], 0)),
            scratch_shapes=[pltpu.VMEM((D_MODEL, D_EXPERT), BF16), pltpu.VMEM((D_MODEL, D_EXPERT), BF16),
                            pltpu.VMEM((D_EXPERT, D_MODEL), BF16)],
        ),
        out_shape=jax.ShapeDtypeStruct((s, D_MODEL), F32),
        compiler_params=_cparams("arbitrary"),
        name="moe_experts",
    )(*work, xs, w1, w3, w2)


def _expert_work_items(counts, total):
    ts = MOE_TS
    n_tiles = total // ts
    n_work = n_tiles + N_EXPERTS - 1
    offs = jnp.cumsum(counts) - counts
    t0 = (jnp.arange(n_tiles, dtype=jnp.int32) * ts)[:, None]
    lo = jnp.maximum(t0, offs[None, :])
    hi = jnp.minimum(t0 + ts, (offs + counts)[None, :])
    nonempty = (hi > lo).reshape(-1)
    nw = jnp.sum(nonempty.astype(jnp.int32))
    idx = jnp.nonzero(nonempty, size=n_work, fill_value=0)[0].astype(jnp.int32)
    idx = jnp.where(jnp.arange(n_work) < nw, idx, idx[jnp.maximum(nw - 1, 0)])
    wt = idx // N_EXPERTS
    we = idx % N_EXPERTS
    wlo = lo.reshape(-1)[idx] - wt * ts
    whi = hi.reshape(-1)[idx] - wt * ts
    wfirst = jnp.concatenate([jnp.ones((1,), jnp.int32), (wt[1:] != wt[:-1]).astype(jnp.int32)])
    return wt, we, wlo, whi, wfirst, nw.reshape(1)


def _combine_kernel(slots_ref, slots_next_ref, h_ref, route_ref, gfin_ref, ys_hbm, yp_ref, ysm_ref, gbuf, sems,
                    *, n_prompt_tiles):
    i = pl.program_id(0)
    n = pl.num_programs(0)
    tm = h_ref.shape[0]

    def gather(s_ref, buf):
        for r in range(tm):
            pltpu.make_async_copy(ys_hbm.at[pl.ds(s_ref[0, 0, 2 * r], 1)], gbuf.at[buf, 0, pl.ds(r, 1)],
                                  sems.at[buf]).start()
            pltpu.make_async_copy(ys_hbm.at[pl.ds(s_ref[0, 0, 2 * r + 1], 1)], gbuf.at[buf, 1, pl.ds(r, 1)],
                                  sems.at[buf]).start()

    cur = i % 2

    @pl.when(i == 0)
    def _():
        gather(slots_ref, 0)

    @pl.when(i + 1 < n)
    def _():
        gather(slots_next_ref, 1 - cur)

    for k in range(2):
        pltpu.make_async_copy(ys_hbm.at[pl.ds(0, tm)], gbuf.at[cur, k], sems.at[cur]).wait()
    route = route_ref[...]
    lane = lax.broadcasted_iota(jnp.int32, route.shape, 1)
    w1 = jnp.sum(jnp.where(lane == ROUTE_W1, route, 0.0), axis=1, keepdims=True)
    w2 = jnp.sum(jnp.where(lane == ROUTE_W2, route, 0.0), axis=1, keepdims=True)
    hf = h_ref[...] + (w1 * gbuf[cur, 0] + w2 * gbuf[cur, 1])
    y = hf * lax.rsqrt(jnp.mean(hf * hf, axis=-1, keepdims=True) + NORM_EPS) * gfin_ref[...]

    @pl.when(i < n_prompt_tiles)
    def _():
        yp_ref[...] = y

    @pl.when(i >= n_prompt_tiles)
    def _():
        ysm_ref[...] = y


def _combine(h, route, slots, ys, gfin, n_prompt_tiles):
    t = h.shape[0]
    tm = MOE_TM
    n_p = n_prompt_tiles
    return pl.pallas_call(
        functools.partial(_combine_kernel, n_prompt_tiles=n_p),
        grid_spec=pltpu.PrefetchScalarGridSpec(
            num_scalar_prefetch=0,
            grid=(t // tm,),
            in_specs=[pl.BlockSpec((1, 1, 2 * tm), lambda i: (i, 0, 0), memory_space=pltpu.SMEM),
                      pl.BlockSpec((1, 1, 2 * tm), lambda i: (jnp.minimum(i + 1, t // tm - 1), 0, 0),
                                   memory_space=pltpu.SMEM),
                      pl.BlockSpec((tm, D_MODEL), lambda i: (i, 0)),
                      pl.BlockSpec((tm, ROUTER_LANES), lambda i: (i, 0)),
                      pl.BlockSpec((1, D_MODEL), lambda i: (0, 0)),
                      pl.BlockSpec(memory_space=pl.ANY)],
            out_specs=[pl.BlockSpec((tm, D_MODEL), lambda i: (jnp.minimum(i, n_p - 1), 0)),
                       pl.BlockSpec((tm, D_MODEL), lambda i: (0, 0))],
            scratch_shapes=[pltpu.VMEM((2, 2, tm, D_MODEL), F32), pltpu.SemaphoreType.DMA((2,))],
        ),
        out_shape=[jax.ShapeDtypeStruct((n_p * tm, D_MODEL), F32), jax.ShapeDtypeStruct((tm, D_MODEL), F32)],
        compiler_params=_cparams("arbitrary"),
        name="moe_combine",
    )(slots, slots, h, route, gfin, ys)


def _v_tiles(v, tblk):
    b, t, _ = v.shape
    x = v.reshape(b, t // tblk, tblk, 2, RWKV_PAIRS, RWKV_HEAD).transpose(0, 1, 4, 5, 3, 2)
    x = jnp.pad(x, ((0, 0),) * 5 + ((0, RWKV_HEAD - tblk),))
    return x.reshape(b, t // tblk, RWKV_PAIRS, RWKV_HEAD, LANE).astype(BF16)


def _y_rows(y, bb, t):
    nb, tpad, _ = y.shape
    return y.reshape(nb, tpad, bb, RWKV_WIDTH).transpose(0, 2, 1, 3).reshape(nb * bb, tpad, RWKV_WIDTH)[:, :t]


def kernel(x_prompt, x_sample, state_gla, state_rwkv, state_shift, meta_tokens, norm_mix, w_in, gla_gate_w2,
           gla_gate_b, gla_norm, rwkv_mu, rwkv_w0, rwkv_w2, rwkv_a0, rwkv_a2, rwkv_g2, rwkv_kk, rwkv_ka, rwkv_rk,
           rwkv_ln_w, rwkv_ln_b, w_out, norm_ffn, router_group_w, router_group_b, router_expert_w,
           router_expert_b, moe_w1, moe_w3, moe_w2, norm_final):
    bp, tp, _ = x_prompt.shape
    bs, ts, _ = x_sample.shape
    assert state_gla.shape[0] == 1, "one layer"
    lyr = 0

    w_in_l = w_in[lyr]
    wg = jnp.pad(w_in_l[:, :GLA_COLS], ((0, 0), (0, GLA_PCOLS - GLA_COLS))).astype(BF16)
    wr = w_in_l[:, GLA_COLS:].astype(BF16)
    g_mix = norm_mix[lyr][None, :]
    gw2p = jnp.pad(gla_gate_w2[lyr], ((0, LANE - GLA_GATE_RANK), (0, 0)))
    gb = gla_gate_b[lyr][None, :]
    gn = gla_norm[lyr][None, :]
    w2p = jnp.pad(rwkv_w2[lyr], ((0, 64), (0, 0))).astype(BF16)
    a2p = jnp.pad(rwkv_a2[lyr], ((64, 0), (0, 0))).astype(BF16)
    pre_params = (rwkv_mu[lyr][None, :], rwkv_w0[lyr][None, :], w2p, rwkv_a0[lyr][None, :], a2p,
                  rwkv_g2[lyr].astype(BF16), rwkv_kk[lyr][None, :], rwkv_ka[lyr][None, :],
                  rwkv_rk[lyr].reshape(1, RWKV_WIDTH), _head_selector())
    lnw = rwkv_ln_w[lyr][None, :]
    lnb = rwkv_ln_b[lyr][None, :]
    wo = w_out[lyr].astype(BF16)
    gffn = norm_ffn[lyr][None, :]
    n_used = N_GROUPS + N_EXPERTS
    w_router = jnp.pad(
        jnp.concatenate([router_group_w[lyr],
                         router_expert_w[lyr].transpose(1, 0, 2).reshape(D_MODEL, N_EXPERTS)], axis=1),
        ((0, 0), (0, ROUTER_LANES - n_used)))
    b_router = jnp.pad(jnp.concatenate([router_group_b[lyr], router_expert_b[lyr].reshape(N_EXPERTS)]),
                       (0, ROUTER_LANES - n_used))[None, :]
    gfin = norm_final[None, :]

    pg_m, pr_m = _inproj(meta_tokens, g_mix, wg, wr, N_META)
    _, sg_m = _gla(pg_m[None], jnp.zeros((1, GLA_HEADS, GLA_DK, GLA_DV), F32), gw2p, gb, gn,
                   bb=1, chunk=N_META, sub=N_META, t_valid=N_META)
    r, w, k, kk, kka, v_m, _, _ = _rwkv_pre(pr_m[None], jnp.zeros((1, 1, RWKV_COLS), F32), pre_params,
                                            tm=N_META, explicit_prev=False, emit_vt=False)
    _, sr_m = _rwkv_rec(r, w, k, kk, kka, _v_tiles(v_m, N_META),
                        jnp.zeros((1, RWKV_HEADS, RWKV_HEAD, RWKV_HEAD), F32), bb=1, n_steps=N_META)

    xp = x_prompt.reshape(bp * tp, D_MODEL)
    pg_p, pr_p = _inproj(xp, g_mix, wg, wr, INPROJ_TM)
    og_p, sg_p = _gla(pg_p.reshape(bp, tp, GLA_PCOLS), jnp.broadcast_to(sg_m, (bp,) + sg_m.shape[1:]), gw2p, gb, gn,
                      bb=SEQ_BLOCK, chunk=GLA_CHUNK, sub=GLA_SUB, t_valid=GLA_CHUNK)
    pr_p3 = pr_p.reshape(bp, tp, RWKV_COLS)
    first_prev = jnp.broadcast_to(pr_m[N_META - 1][None, None, :], (bp, 1, RWKV_COLS))
    r, w, k, kk, kka, vt_p, bv_p, gate_p = _rwkv_pre(pr_p3, first_prev, pre_params, tm=PRE_TM, explicit_prev=False,
                                                     emit_vt=True)
    y_p, sr_p = _rwkv_rec(r, w, k, kk, kka, vt_p, jnp.broadcast_to(sr_m, (bp,) + sr_m.shape[1:]),
                          bb=bp, n_steps=REC_TB)
    prompt_rows = (xp, og_p.reshape(bp * tp, GLA_WIDTH), y_p.reshape(tp, bp * RWKV_WIDTH),
                   bv_p.reshape(bp * tp, RWKV_WIDTH), gate_p.reshape(bp * tp, RWKV_WIDTH))

    xs = x_sample.reshape(bs * ts, D_MODEL)
    pg_s, pr_s = _inproj(xs, g_mix, wg, wr, bs * ts)
    ts_pad = 8
    pg_s3 = jnp.pad(pg_s.reshape(bs, ts, GLA_PCOLS), ((0, 0), (0, ts_pad - ts), (0, 0)))
    og_s, sg_s = _gla(pg_s3, state_gla[lyr], gw2p, gb, gn, bb=SEQ_BLOCK, chunk=ts_pad, sub=ts_pad, t_valid=ts)
    og_s = og_s[:, :ts]
    pr_s3 = pr_s.reshape(bs, ts, RWKV_COLS)
    prev_s = jnp.concatenate([state_shift[lyr][:, None, :], pr_s3[:, :-1]], axis=1)
    r, w, k, kk, kka, v_s, bv_s, gate_s = _rwkv_pre(pr_s3.reshape(1, bs * ts, RWKV_COLS),
                                                     prev_s.reshape(1, bs * ts, RWKV_COLS), pre_params,
                                                     tm=bs * ts, explicit_prev=True, emit_vt=False)
    unflat = lambda a: a.reshape(RWKV_PAIRS, bs, ts, LANE).transpose(1, 0, 2, 3)
    y_s, sr_s = _rwkv_rec(unflat(r), unflat(w), unflat(k), unflat(kk), unflat(kka),
                          _v_tiles(v_s.reshape(bs, ts, RWKV_WIDTH), ts), state_rwkv[lyr], bb=SEQ_BLOCK, n_steps=ts)
    y_s = _y_rows(y_s, 8, ts)
    sample_rows = (xs, og_s.reshape(bs * ts, GLA_WIDTH), y_s.reshape(bs * ts, RWKV_WIDTH),
                   bv_s.reshape(bs * ts, RWKV_WIDTH), gate_s.reshape(bs * ts, RWKV_WIDTH))

    h_all, n2_all, route, counts = _mix_router(prompt_rows, sample_rows, lnw, lnb, wo, gffn, w_router, b_router,
                                               seq_tiles=tp // MOE_TM)
    n_tok = h_all.shape[0]
    n_p_tiles = (bp * tp) // MOE_TM
    cnt = counts[0, EXPERT_LANE0:EXPERT_LANE0 + N_EXPERTS].astype(jnp.int32)
    offs = jnp.cumsum(cnt) - cnt
    eid = route[:, ROUTE_E1:ROUTE_E2 + 1].astype(jnp.int32)
    pos = route[:, ROUTE_P1:ROUTE_P2 + 1].astype(jnp.int32)
    onehot = (eid[..., None] == jnp.arange(N_EXPERTS, dtype=jnp.int32)).astype(F32)
    off = jnp.einsum("tke,e->tk", onehot, offs.astype(F32), precision=HIGHEST).astype(jnp.int32)
    slots = (off + pos).reshape(n_tok // MOE_TM, 1, 2 * MOE_TM)
    xs_sorted = _dispatch(n2_all, slots)
    ys_sorted = _experts(xs_sorted, _expert_work_items(cnt, 2 * n_tok), moe_w1[lyr], moe_w3[lyr], moe_w2[lyr])
    y_prompt, y_sample = _combine(h_all, route, slots, ys_sorted, gfin, n_p_tiles)
    y_prompt = y_prompt.reshape(bp, tp, D_MODEL)
    y_sample = y_sample.reshape(bs, ts, D_MODEL)

    return (y_prompt, y_sample,
            sg_p[None], sr_p[None], pr_p3[:, -1][None],
            sg_s[None], sr_s[None], pr_s3[:, -1][None])
```

```python
import functools

import jax
import jax.numpy as jnp
from jax import lax
from jax.experimental import pallas as pl
from jax.experimental.pallas import tpu as pltpu

F32 = jnp.float32
BF16 = jnp.bfloat16
HIGHEST = lax.Precision.HIGHEST

D_MODEL = 1024
N_META = 16
NORM_EPS = 1e-6
LOG2E = 1.4426950408889634
GLA_HEADS = 4
GLA_DK = 64
GLA_DV = 128
GLA_QK = GLA_HEADS * GLA_DK
GLA_WIDTH = GLA_HEADS * GLA_DV
GLA_GATE_RANK = 16
GLA_GATE_NORM = 16.0
GLA_CHUNK = 64
GLA_SUB = 8
GLA_COLS = 2 * GLA_QK + 2 * GLA_WIDTH + GLA_GATE_RANK
GLA_PCOLS = 2 * GLA_QK + 2 * GLA_WIDTH + 128
RWKV_WIDTH = 512
RWKV_HEAD = 64
RWKV_HEADS = 8
RWKV_PAIRS = RWKV_HEADS // 2
RWKV_DECAY_SCALE = 0.606531
RWKV_GN_EPS = 64e-5
RWKV_COLS = 3 * RWKV_WIDTH + 64 + 64 + 128
REC_TB = 64
REC_UNROLL = 8
INPROJ_TM = 512
PRE_TM = 256
SEQ_BLOCK = 8
N_GROUPS = 4
EXPERTS_PER_GROUP = 8
N_EXPERTS = 32
D_EXPERT = 512
ROUTER_LANES = 128
EXPERT_LANE0 = N_GROUPS
ROUTE_E1, ROUTE_E2, ROUTE_W1, ROUTE_W2, ROUTE_P1, ROUTE_P2 = range(6)
MOE_TM = 512
MOE_TS = 512

LANE = 128
VMEM_LIMIT = 56 * 1024 * 1024


def _cparams(*sem):
    return pltpu.CompilerParams(dimension_semantics=sem, vmem_limit_bytes=VMEM_LIMIT)


def _block_ones(n, blk):
    i = jnp.arange(n)
    return (i[:, None] // blk == i[None, :] // blk).astype(BF16)


def _sigmoid(x):
    return 1.0 / (1.0 + jnp.exp(-x))


def _dot(a, b):
    return jnp.dot(a, b, preferred_element_type=F32)


def _dot_nt(a, b):
    return lax.dot_general(a, b, (((1,), (1,)), ((), ())), preferred_element_type=F32)


def _dot_tn(a, b):
    return lax.dot_general(a, b, (((0,), (0,)), ((), ())), preferred_element_type=F32)


def _split2(x):
    hi = x.astype(BF16)
    lo = (x - hi.astype(F32)).astype(BF16)
    return hi, lo


def _head_selector():
    return (jnp.arange(RWKV_WIDTH)[:, None] // RWKV_HEAD == jnp.arange(LANE)[None, :]).astype(BF16)


def _group_sum(x, sel):
    hi, lo = _split2(x)
    s_hi, s_lo = _split2(_dot(hi, sel) + _dot(lo, sel))
    return _dot_nt(s_hi, sel) + _dot_nt(s_lo, sel)


def _inproj_kernel(x_ref, g_ref, wg_ref, wr_ref, pg_ref, pr_ref):
    x = x_ref[...]
    n = x * lax.rsqrt(jnp.mean(x * x, axis=-1, keepdims=True) + NORM_EPS) * g_ref[...]
    nb = n.astype(BF16)
    pg_ref[...] = _dot(nb, wg_ref[...])
    pr_ref[...] = _dot(nb, wr_ref[...])


def _inproj(x, g, wg, wr, tm):
    t = x.shape[0]
    return pl.pallas_call(
        _inproj_kernel,
        grid=(t // tm,),
        in_specs=[
            pl.BlockSpec((tm, D_MODEL), lambda i: (i, 0)),
            pl.BlockSpec((1, D_MODEL), lambda i: (0, 0)),
            pl.BlockSpec((D_MODEL, GLA_PCOLS), lambda i: (0, 0)),
            pl.BlockSpec((D_MODEL, RWKV_COLS), lambda i: (0, 0)),
        ],
        out_specs=[
            pl.BlockSpec((tm, GLA_PCOLS), lambda i: (i, 0)),
            pl.BlockSpec((tm, RWKV_COLS), lambda i: (i, 0)),
        ],
        out_shape=[jax.ShapeDtypeStruct((t, GLA_PCOLS), F32), jax.ShapeDtypeStruct((t, RWKV_COLS), F32)],
        compiler_params=_cparams("parallel"),
        name="inproj",
    )(x, g, wg, wr)


def _gla_kernel(pg_ref, s0_ref, gw2_ref, gb_ref, gn_ref, bo_ref, tril_ref, o_ref, sout_ref, s_scr,
                *, bb, chunk, sub, t_valid):
    ci = pl.program_id(1)

    @pl.when(ci == 0)
    def _():
        s_scr[...] = s0_ref[...]

    bo = bo_ref[...]
    tril = tril_ref[...]
    lane = lax.broadcasted_iota(jnp.int32, (sub, LANE), 1) & (GLA_DK - 1)
    rowi = lax.broadcasted_iota(jnp.int32, (sub, LANE), 0)
    head0_s = lax.broadcasted_iota(jnp.int32, (sub, LANE), 1) < GLA_DK
    head0_c = lax.broadcasted_iota(jnp.int32, (chunk, LANE), 1) < GLA_DK

    for bi in range(bb):
        pg = pg_ref[bi]
        q = pg[:, 0:GLA_QK] * (GLA_DK ** -0.5)
        k = pg[:, GLA_QK:2 * GLA_QK]
        v = pg[:, 2 * GLA_QK:2 * GLA_QK + GLA_WIDTH]
        g = pg[:, 2 * GLA_QK + GLA_WIDTH:2 * GLA_QK + 2 * GLA_WIDTH]
        gl = pg[:, 2 * GLA_QK + 2 * GLA_WIDTH:]
        z = jnp.dot(gl, gw2_ref[...], precision=HIGHEST, preferred_element_type=F32) + gb_ref[...]
        lg = (jnp.minimum(z, 0.0) - jnp.log1p(jnp.exp(-jnp.abs(z)))) * (LOG2E / GLA_GATE_NORM)
        if t_valid < chunk:
            rows = lax.broadcasted_iota(jnp.int32, lg.shape, 0)
            lg = jnp.where(rows < t_valid, lg, 0.0)
        b = jnp.dot(tril, lg, precision=HIGHEST, preferred_element_type=F32)
        eb = jnp.exp2(b)
        blast = b[chunk - 1:chunk, :]
        kl = k * jnp.exp2(blast - b)
        qe = q * eb

        n_blk = chunk // sub
        n_pairs = GLA_HEADS // 2
        ps = []
        for hp in range(n_pairs):
            sl = slice(hp * LANE, (hp + 1) * LANE)
            for blk in range(n_blk):
                rs = slice(blk * sub, (blk + 1) * sub)
                qb, kb, bbk = q[rs, sl], k[rs, sl], b[rs, sl]
                for j in range(sub):
                    ps.append(qb * (kb[j:j + 1] * jnp.exp2(jnp.minimum(bbk - bbk[j:j + 1], 0.0))))
        red = _dot(jnp.concatenate(ps, axis=0).astype(BF16), bo)

        o_heads = []
        for hp in range(n_pairs):
            sl = slice(hp * LANE, (hp + 1) * LANE)
            kp, bp = k[:, sl], b[:, sl]
            row_blocks = []
            for blk in range(n_blk):
                rs = slice(blk * sub, (blk + 1) * sub)
                base = (hp * n_blk + blk) * sub * sub
                a = jnp.zeros((sub, LANE), F32)
                for j in range(sub):
                    a = jnp.where((lane == blk * sub + j) & (rowi >= j), red[base + j * sub:base + (j + 1) * sub], a)
                if blk > 0:
                    bref = bp[blk * sub - 1:blk * sub]
                    qt = q[rs, sl] * jnp.exp2(bp[rs] - bref)
                    kt = (kp * jnp.exp2(jnp.minimum(bref - bp, 0.0))).astype(BF16)
                    qt2 = jnp.concatenate([jnp.where(head0_s, qt, 0.0), jnp.where(head0_s, 0.0, qt)], axis=0)
                    off2 = _dot_nt(qt2.astype(BF16), kt)
                    a = jnp.where(lane < blk * sub, jnp.concatenate([off2[:sub], off2[sub:]], axis=1), a)
                row_blocks.append(a)
            a_pair = row_blocks[0] if n_blk == 1 else jnp.concatenate(row_blocks, axis=0)
            v0 = v[:, 2 * hp * GLA_DV:(2 * hp + 1) * GLA_DV]
            v1 = v[:, (2 * hp + 1) * GLA_DV:(2 * hp + 2) * GLA_DV]
            s_pair = s_scr[bi, 2 * hp:2 * hp + 2].reshape(2 * GLA_DK, GLA_DV)
            qe_p, kl_p = qe[:, sl], kl[:, sl]

            def by_head(x):
                return jnp.concatenate([jnp.where(head0_c, x, 0.0), jnp.where(head0_c, 0.0, x)], axis=0)

            if chunk == GLA_DK:
                v_rows = jnp.concatenate([v0, v1], axis=0)
            else:
                zpad = jnp.zeros((GLA_DK - chunk, GLA_DV), F32)
                v_rows = jnp.concatenate([v0, zpad, v1, zpad], axis=0)
            lhs = jnp.concatenate([by_head(a_pair), by_head(qe_p)], axis=1).astype(BF16)
            rhs = jnp.concatenate([v_rows, s_pair], axis=0).astype(BF16)
            o2 = _dot(lhs, rhs)
            upd = _dot_tn(by_head(kl_p).astype(BF16), jnp.concatenate([v0, v1], axis=0).astype(BF16))
            dcol = jnp.broadcast_to(jnp.exp2(blast[:, sl]), (8, LANE)).T[:, 0:1]
            s_new = dcol * s_pair + upd
            s_scr[bi, 2 * hp] = s_new[:GLA_DK]
            s_scr[bi, 2 * hp + 1] = s_new[GLA_DK:]
            for h2 in range(2):
                o_h = o2[h2 * chunk:(h2 + 1) * chunk]
                o_heads.append(o_h * lax.rsqrt(jnp.mean(o_h * o_h, axis=-1, keepdims=True) + NORM_EPS) * gn_ref[...])
        o = jnp.concatenate(o_heads, axis=1)
        o_ref[bi] = o * (g * _sigmoid(g))

    @pl.when(ci == pl.num_programs(1) - 1)
    def _():
        sout_ref[...] = s_scr[...]


def _gla(pg, s0, gw2p, gb, gn, *, bb, chunk, sub, t_valid):
    b, t, _ = pg.shape
    tril = jnp.tril(jnp.ones((chunk, chunk), F32))
    kern = functools.partial(_gla_kernel, bb=bb, chunk=chunk, sub=sub, t_valid=t_valid)
    return pl.pallas_call(
        kern,
        grid=(b // bb, t // chunk),
        in_specs=[
            pl.BlockSpec((bb, chunk, GLA_PCOLS), lambda i, j: (i, j, 0)),
            pl.BlockSpec((bb, GLA_HEADS, GLA_DK, GLA_DV), lambda i, j: (i, 0, 0, 0)),
            pl.BlockSpec((LANE, GLA_QK), lambda i, j: (0, 0)),
            pl.BlockSpec((1, GLA_QK), lambda i, j: (0, 0)),
            pl.BlockSpec((1, GLA_DV), lambda i, j: (0, 0)),
            pl.BlockSpec((LANE, LANE), lambda i, j: (0, 0)),
            pl.BlockSpec((chunk, chunk), lambda i, j: (0, 0)),
        ],
        out_specs=[
            pl.BlockSpec((bb, chunk, GLA_WIDTH), lambda i, j: (i, j, 0)),
            pl.BlockSpec((bb, GLA_HEADS, GLA_DK, GLA_DV), lambda i, j: (i, 0, 0, 0)),
        ],
        out_shape=[jax.ShapeDtypeStruct((b, t, GLA_WIDTH), F32),
                   jax.ShapeDtypeStruct((b, GLA_HEADS, GLA_DK, GLA_DV), F32)],
        scratch_shapes=[pltpu.VMEM((bb, GLA_HEADS, GLA_DK, GLA_DV), F32)],
        compiler_params=_cparams("parallel", "arbitrary"),
        name="gla_chunk",
    )(pg, s0, gw2p, gb, gn, _block_ones(LANE, GLA_DK), tril)


def _shifted_rows(pr, row0):
    rows = lax.broadcasted_iota(jnp.int32, pr.shape, 0)
    return jnp.where(rows == 0, row0, pltpu.roll(pr, 1, 0))


def _rwkv_pre_math(pr, prev, params, outs, rs, tb0, emit_vt):
    mu_ref, w0_ref, w2_ref, a0_ref, a2_ref, g2_ref, kk_ref, ka_ref, rk_ref, bo_ref = params
    r_out, w_out, k_out, kkn_out, kka_out, v_out, bv_out, gate_out = outs
    n_rows = pr.shape[0]
    xm = pr + (prev - pr) * mu_ref[...]
    wd = RWKV_WIDTH
    rr, rk, rv = xm[:, 0:wd], xm[:, wd:2 * wd], xm[:, 2 * wd:3 * wd]
    wa = xm[:, 3 * wd:3 * wd + LANE]
    gl2 = xm[:, 3 * wd + LANE:3 * wd + 2 * LANE]
    logw = -RWKV_DECAY_SCALE * _sigmoid(w0_ref[...] + _dot(jnp.tanh(wa).astype(BF16), w2_ref[...]))
    aa = _sigmoid(a0_ref[...] + _dot(wa.astype(BF16), a2_ref[...]))
    gate = _dot(_sigmoid(gl2).astype(BF16), g2_ref[...])
    bo = bo_ref[...]
    kk = rk * kk_ref[...]
    kk = kk / jnp.maximum(jnp.sqrt(_group_sum(kk * kk, bo)), 1e-12)
    k = rk * (1.0 + (aa - 1.0) * ka_ref[...])
    bv = _group_sum(rr * k * rk_ref[...], bo) * rv
    w = jnp.exp(logw)
    kka = kk * aa
    hd = RWKV_HEAD

    def pair(x, hp):
        return jnp.concatenate([x[:, hp * hd:(hp + 1) * hd], x[:, (hp + RWKV_PAIRS) * hd:(hp + RWKV_PAIRS + 1) * hd]],
                               axis=1)

    for hp in range(RWKV_PAIRS):
        r_out[0, hp, rs, :] = pair(rr, hp)
        w_out[0, hp, rs, :] = pair(w, hp)
        k_out[0, hp, rs, :] = pair(k, hp)
        kkn_out[0, hp, rs, :] = pair(kk, hp)
        kka_out[0, hp, rs, :] = pair(kka, hp)
    if emit_vt:
        vt = rv.T
        for tb in range(n_rows // REC_TB):
            ts = slice(tb * REC_TB, (tb + 1) * REC_TB)
            for hp in range(RWKV_PAIRS):
                lo, hi = hp * hd, (hp + RWKV_PAIRS) * hd
                v_out[0, tb0 + tb, hp] = jnp.concatenate([vt[lo:lo + hd, ts], vt[hi:hi + hd, ts]],
                                                         axis=1).astype(BF16)
    else:
        v_out[0, rs, :] = rv
    bv_out[0, rs, :] = bv
    gate_out[0, rs, :] = gate


def _rwkv_pre_kernel(pr_ref, aux_ref, *rest, tm, explicit_prev, emit_vt):
    params, outs, carry_scr = rest[:10], rest[10:18], rest[18]
    pr = pr_ref[0]
    if explicit_prev:
        prev = aux_ref[0]
    else:
        j = pl.program_id(1)
        prev = _shifted_rows(pr, jnp.where(j == 0, aux_ref[0], carry_scr[...]))
        carry_scr[...] = pr[tm - 1:tm, :]
    _rwkv_pre_math(pr, prev, params, outs, slice(0, tm), 0, emit_vt)


def _rwkv_pre(pr, aux, params, *, tm, explicit_prev, emit_vt):
    b, t, _ = pr.shape
    kern = functools.partial(_rwkv_pre_kernel, tm=tm, explicit_prev=explicit_prev, emit_vt=emit_vt)
    aux_spec = (pl.BlockSpec((1, tm, RWKV_COLS), lambda i, j: (i, j, 0)) if explicit_prev
                else pl.BlockSpec((1, 1, RWKV_COLS), lambda i, j: (i, 0, 0)))
    const = lambda shape: pl.BlockSpec(shape, lambda i, j: (0,) * len(shape))
    pair_spec = pl.BlockSpec((1, RWKV_PAIRS, tm, LANE), lambda i, j: (i, 0, j, 0))
    row_spec = pl.BlockSpec((1, tm, RWKV_WIDTH), lambda i, j: (i, j, 0))
    pair_shape = jax.ShapeDtypeStruct((b, RWKV_PAIRS, t, LANE), F32)
    row_shape = jax.ShapeDtypeStruct((b, t, RWKV_WIDTH), F32)
    if emit_vt:
        v_spec = pl.BlockSpec((1, tm // REC_TB, RWKV_PAIRS, RWKV_HEAD, LANE), lambda i, j: (i, j, 0, 0, 0))
        v_shape = jax.ShapeDtypeStruct((b, t // REC_TB, RWKV_PAIRS, RWKV_HEAD, LANE), BF16)
    else:
        v_spec, v_shape = row_spec, row_shape
    return pl.pallas_call(
        kern,
        grid=(b, t // tm),
        in_specs=[
            pl.BlockSpec((1, tm, RWKV_COLS), lambda i, j: (i, j, 0)),
            aux_spec,
            const((1, RWKV_COLS)), const((1, RWKV_WIDTH)), const((LANE, RWKV_WIDTH)), const((1, RWKV_WIDTH)),
            const((LANE, RWKV_WIDTH)), const((LANE, RWKV_WIDTH)), const((1, RWKV_WIDTH)), const((1, RWKV_WIDTH)),
            const((1, RWKV_WIDTH)), const((RWKV_WIDTH, LANE)),
        ],
        out_specs=[pair_spec] * 5 + [v_spec, row_spec, row_spec],
        out_shape=[pair_shape] * 5 + [v_shape, row_shape, row_shape],
        scratch_shapes=[pltpu.VMEM((1, RWKV_COLS), F32)],
        compiler_params=_cparams("parallel", "arbitrary"),
        name="rwkv_pre",
    )(pr, aux, *params)


def _rwkv_rec_kernel(r_ref, w_ref, k_ref, kk_ref, kka_ref, vt_ref, s0_ref, bo_ref, vsel_ref, ysel_ref,
                     y_ref, sout_ref, s_scr, t1_scr, t3_scr, yt_scr, *, bb, n_steps):
    tb = pl.program_id(1)
    nc = bb * RWKV_PAIRS
    hd = RWKV_HEAD

    @pl.when(tb == 0)
    def _():
        for c in range(nc):
            bi, hp = divmod(c, RWKV_PAIRS)
            s_scr[c] = jnp.concatenate([s0_ref[bi, hp], s0_ref[bi, hp + RWKV_PAIRS]], axis=1)

    bo = bo_ref[...]

    def step(t, u):
        row = pl.ds(t, 1)
        for c in range(nc):
            bi, hp = divmod(c, RWKV_PAIRS)
            t1_scr[c * hd:(c + 1) * hd, :] = (s_scr[c] * kk_ref[bi, hp, row, :]).astype(BF16)
        sab = _dot(t1_scr[...], bo)
        vb = _dot(vt_ref[...].reshape(nc * hd, LANE), vsel_ref[t])
        for c in range(nc):
            bi, hp = divmod(c, RWKV_PAIRS)
            rs = slice(c * hd, (c + 1) * hd)
            s2 = (s_scr[c] * w_ref[bi, hp, row, :] - sab[rs] * kka_ref[bi, hp, row, :]
                  + vb[rs] * k_ref[bi, hp, row, :])
            s_scr[c] = s2
            t3_scr[rs, :] = (s2 * r_ref[bi, hp, row, :]).astype(BF16)
        yt_scr[...] += _dot_nt(ysel_ref[u], t3_scr[...])

    n_inner = min(8, n_steps)

    def block8(t8, carry):
        yt_scr[...] = jnp.zeros(yt_scr.shape, F32)

        def inner(u, c2):
            step(t8 * 8 + u, u)
            return c2

        lax.fori_loop(0, n_inner, inner, 0, unroll=REC_UNROLL)
        t0 = pl.multiple_of(t8 * 8, 8)
        blk = RWKV_PAIRS * hd
        for bi in range(bb):
            for h2 in range(2):
                y_ref[0, pl.ds(t0, 8), bi * RWKV_WIDTH + h2 * blk:bi * RWKV_WIDTH + (h2 + 1) * blk] = (
                    yt_scr[h2 * 8:(h2 + 1) * 8, bi * blk:(bi + 1) * blk])
        return carry

    lax.fori_loop(0, (n_steps + 7) // 8, block8, 0)

    @pl.when(tb == pl.num_programs(1) - 1)
    def _():
        for c in range(nc):
            bi, hp = divmod(c, RWKV_PAIRS)
            s_c = s_scr[c]
            sout_ref[bi, hp] = s_c[:, :RWKV_HEAD]
            sout_ref[bi, hp + RWKV_PAIRS] = s_c[:, RWKV_HEAD:]


def _rwkv_rec(r, w, k, kk, kka, vt, s0, *, bb, n_steps):
    b, _, t, _ = r.shape
    tblk = min(REC_TB, t)
    ntb = t // tblk
    nc = bb * RWKV_PAIRS
    lane = jnp.arange(LANE)
    vsel = ((lane[None, :, None] // RWKV_HEAD == lane[None, None, :] // RWKV_HEAD)
            & (lane[None, :, None] % RWKV_HEAD == jnp.arange(RWKV_HEAD)[:, None, None])).astype(BF16)
    ysel = (jnp.arange(16)[None, :, None]
            == 8 * (lane[None, None, :] // RWKV_HEAD) + jnp.arange(8)[:, None, None]).astype(BF16)
    kern = functools.partial(_rwkv_rec_kernel, bb=bb, n_steps=n_steps)
    pair_spec = pl.BlockSpec((bb, RWKV_PAIRS, tblk, LANE), lambda i, j: (i, 0, j, 0))
    state_spec = pl.BlockSpec((bb, RWKV_HEADS, RWKV_HEAD, RWKV_HEAD), lambda i, j: (i, 0, 0, 0))
    ytb = max(tblk, 8)
    return pl.pallas_call(
        kern,
        grid=(b // bb, ntb),
        in_specs=[pair_spec] * 5 + [
            pl.BlockSpec((bb, 1, RWKV_PAIRS, RWKV_HEAD, LANE), lambda i, j: (i, j, 0, 0, 0)),
            state_spec,
            pl.BlockSpec((LANE, LANE), lambda i, j: (0, 0)),
            pl.BlockSpec((RWKV_HEAD, LANE, LANE), lambda i, j: (0, 0, 0)),
            pl.BlockSpec((8, 16, LANE), lambda i, j: (0, 0, 0)),
        ],
        out_specs=[
            pl.BlockSpec((1, ytb, bb * RWKV_WIDTH), lambda i, j: (i, j, 0)),
            state_spec,
        ],
        out_shape=[jax.ShapeDtypeStruct((b // bb, ntb * ytb, bb * RWKV_WIDTH), F32),
                   jax.ShapeDtypeStruct((b, RWKV_HEADS, RWKV_HEAD, RWKV_HEAD), F32)],
        scratch_shapes=[pltpu.VMEM((nc, RWKV_HEAD, LANE), F32),
                        pltpu.VMEM((nc * RWKV_HEAD, LANE), BF16),
                        pltpu.VMEM((nc * RWKV_HEAD, LANE), BF16),
                        pltpu.VMEM((16, nc * RWKV_HEAD), F32)],
        compiler_params=_cparams("parallel", "arbitrary"),
        name="rwkv_rec",
    )(r, w, k, kk, kka, vt, s0, _block_ones(LANE, RWKV_HEAD), vsel, ysel)


def _mix_router_body(x_ref, og_ref, y_ref, bv_ref, gate_ref, lnw_ref, lnb_ref, wo_ref, gffn_ref, wr_hi_ref, wr_lo_ref,
                     br_ref, bo_ref, tril_ref, h_ref, n2_ref, route_ref, cnt_scr):
    bo = bo_ref[...]
    y = y_ref[...]
    inv_n = 1.0 / RWKV_HEAD
    d = y - _group_sum(y, bo) * inv_n
    var = _group_sum(d * d, bo) * inv_n
    yn = d * lax.rsqrt(var + RWKV_GN_EPS) * lnw_ref[...] + lnb_ref[...] + bv_ref[...]
    o_rwkv = yn * gate_ref[...]
    mix = (_dot(og_ref[...].astype(BF16), wo_ref[0:GLA_WIDTH, :])
           + _dot(o_rwkv.astype(BF16), wo_ref[GLA_WIDTH:, :]))
    h = x_ref[...] + mix
    h_ref[...] = h
    n2 = h * lax.rsqrt(jnp.mean(h * h, axis=-1, keepdims=True) + NORM_EPS) * gffn_ref[...]
    n2_ref[...] = n2
    n2_hi, n2_lo = _split2(n2)
    lg = (_dot(n2_hi, wr_hi_ref[...]) + _dot(n2_hi, wr_lo_ref[...]) + _dot(n2_lo, wr_hi_ref[...])) + br_ref[...]
    neg = jnp.float32(-3.0e38)
    big = jnp.float32(1.0e9)
    lane = lax.broadcasted_iota(jnp.int32, lg.shape, 1).astype(F32)
    gmask = lane < N_GROUPS
    gmax = jnp.max(jnp.where(gmask, lg, neg), axis=1, keepdims=True)
    p_top = 1.0 / jnp.sum(jnp.where(gmask, jnp.exp(jnp.minimum(lg - gmax, 0.0)), 0.0), axis=1, keepdims=True)
    gidx = jnp.min(jnp.where(gmask & (lg == gmax), lane, big), axis=1, keepdims=True)
    e_lo = EXPERT_LANE0 + gidx * EXPERTS_PER_GROUP
    emask = (lane >= e_lo) & (lane < e_lo + EXPERTS_PER_GROUP)
    m1 = jnp.max(jnp.where(emask, lg, neg), axis=1, keepdims=True)
    e1 = jnp.min(jnp.where(emask & (lg == m1), lane, big), axis=1, keepdims=True)
    emask2 = emask & (lane != e1)
    m2 = jnp.max(jnp.where(emask2, lg, neg), axis=1, keepdims=True)
    e2 = jnp.min(jnp.where(emask2 & (lg == m2), lane, big), axis=1, keepdims=True)
    r21 = jnp.exp(m2 - m1)
    w1 = p_top / (1.0 + r21)
    w2 = p_top * r21 / (1.0 + r21)
    o1 = lane == e1
    o2 = lane == e2
    onehot = jnp.where(o1 | o2, 1.0, 0.0)
    rank = _dot(tril_ref[...], onehot.astype(BF16)) + cnt_scr[...]
    pos1 = jnp.sum(jnp.where(o1, rank, 0.0), axis=1, keepdims=True)
    pos2 = jnp.sum(jnp.where(o2, rank, 0.0), axis=1, keepdims=True)
    cnt_scr[...] += jnp.sum(onehot, axis=0, keepdims=True)
    route = jnp.where(lane == ROUTE_E1, e1 - EXPERT_LANE0, 0.0)
    route = jnp.where(lane == ROUTE_E2, e2 - EXPERT_LANE0, route)
    route = jnp.where(lane == ROUTE_W1, w1, route)
    route = jnp.where(lane == ROUTE_W2, w2, route)
    route = jnp.where(lane == ROUTE_P1, pos1, route)
    route_ref[...] = jnp.where(lane == ROUTE_P2, pos2, route)


def _mix_router_kernel(*refs, n_prompt_tiles):
    prompt_rows, sample_rows, rest = refs[0:5], refs[5:10], refs[10:]
    consts, (h_ref, n2_ref, route_ref, cnt_ref, cnt_scr) = rest[:9], rest[9:]
    i = pl.program_id(0)

    @pl.when(i == 0)
    def _():
        cnt_scr[...] = jnp.zeros(cnt_scr.shape, F32)

    @pl.when(i < n_prompt_tiles)
    def _():
        _mix_router_body(*prompt_rows, *consts, h_ref, n2_ref, route_ref, cnt_scr)

    @pl.when(i >= n_prompt_tiles)
    def _():
        _mix_router_body(*sample_rows, *consts, h_ref, n2_ref, route_ref, cnt_scr)

    cnt_ref[...] = cnt_scr[...]


def _mix_router(prompt_rows, sample_rows, lnw, lnb, wo, gffn, wr, br, *, seq_tiles):
    tm = MOE_TM
    n_p = prompt_rows[0].shape[0] // tm
    assert sample_rows[0].shape[0] == tm
    t = (n_p + 1) * tm
    widths = (D_MODEL, GLA_WIDTH, RWKV_WIDTH, RWKV_WIDTH, RWKV_WIDTH)
    p_specs = [pl.BlockSpec((tm, n), lambda i: (jnp.minimum(i, n_p - 1), 0)) for n in widths]
    p_specs[2] = pl.BlockSpec(
        (tm, RWKV_WIDTH), lambda i: (jnp.minimum(i, n_p - 1) % seq_tiles, jnp.minimum(i, n_p - 1) // seq_tiles))
    s_specs = [pl.BlockSpec((tm, n), lambda i: (0, 0)) for n in widths]
    const = lambda shape: pl.BlockSpec(shape, lambda i: (0,) * len(shape))
    row = lambda n: pl.BlockSpec((tm, n), lambda i: (i, 0))
    tril = jnp.tril(jnp.ones((tm, tm), F32), -1).astype(BF16)
    return pl.pallas_call(
        functools.partial(_mix_router_kernel, n_prompt_tiles=n_p),
        grid=(n_p + 1,),
        in_specs=p_specs + s_specs + [
            const((1, RWKV_WIDTH)), const((1, RWKV_WIDTH)), const((D_MODEL, D_MODEL)), const((1, D_MODEL)),
            const((D_MODEL, ROUTER_LANES)), const((D_MODEL, ROUTER_LANES)), const((1, ROUTER_LANES)),
            const((RWKV_WIDTH, LANE)),
            const((tm, tm))],
        out_specs=[row(D_MODEL), row(D_MODEL), row(ROUTER_LANES), const((1, ROUTER_LANES))],
        out_shape=[jax.ShapeDtypeStruct((t, D_MODEL), F32), jax.ShapeDtypeStruct((t, D_MODEL), F32),
                   jax.ShapeDtypeStruct((t, ROUTER_LANES), F32), jax.ShapeDtypeStruct((1, ROUTER_LANES), F32)],
        scratch_shapes=[pltpu.VMEM((1, ROUTER_LANES), F32)],
        compiler_params=_cparams("arbitrary"),
        name="mix_router",
    )(*prompt_rows, *sample_rows, lnw, lnb, wo, gffn, *_split2(wr), br, _head_selector(), tril)


def _dispatch_kernel(slots_ref, x_ref, xs_hbm, sem):
    tm = x_ref.shape[0]

    for r in range(tm):
        src = x_ref.at[pl.ds(r, 1)]
        pltpu.make_async_copy(src, xs_hbm.at[pl.ds(slots_ref[0, 0, 2 * r], 1)], sem).start(priority=0)
        pltpu.make_async_copy(src, xs_hbm.at[pl.ds(slots_ref[0, 0, 2 * r + 1], 1)], sem).start(priority=1)
    for _ in range(2):
        pltpu.make_async_copy(x_ref, xs_hbm.at[pl.ds(0, tm)], sem).wait()


def _dispatch(n2, slots):
    t = n2.shape[0]
    tm = MOE_TM
    return pl.pallas_call(
        _dispatch_kernel,
        grid_spec=pltpu.PrefetchScalarGridSpec(
            num_scalar_prefetch=0,
            grid=(t // tm,),
            in_specs=[pl.BlockSpec((1, 1, 2 * tm), lambda i: (i, 0, 0), memory_space=pltpu.SMEM),
                      pl.BlockSpec((tm, D_MODEL), lambda i: (i, 0))],
            out_specs=pl.BlockSpec(memory_space=pl.ANY),
            scratch_shapes=[pltpu.SemaphoreType.DMA(())],
        ),
        out_shape=jax.ShapeDtypeStruct((2 * t, D_MODEL), F32),
        compiler_params=_cparams("arbitrary"),
        name="moe_dispatch",
    )(slots, n2)


def _experts_kernel(wt_ref, we_ref, wlo_ref, whi_ref, wfirst_ref, nw_ref,
                    xs_ref, w1_ref, w3_ref, w2_ref, ys_ref, wb1, wb3, wb2):
    w = pl.program_id(0)

    @pl.when(w < nw_ref[0])
    def _():
        new_expert = jnp.logical_or(w == 0, we_ref[w] != we_ref[jnp.maximum(w - 1, 0)])

        @pl.when(new_expert)
        def _():
            wb1[...] = w1_ref[0].astype(BF16)
            wb3[...] = w3_ref[0].astype(BF16)
            wb2[...] = w2_ref[0].astype(BF16)

        x = xs_ref[...].astype(BF16)
        a = _dot(x, wb1[...])
        b = _dot(x, wb3[...])
        o = _dot(((a * _sigmoid(a)) * b).astype(BF16), wb2[...])

        @pl.when(wfirst_ref[w] == 1)
        def _():
            ys_ref[...] = o

        @pl.when(wfirst_ref[w] == 0)
        def _():
            rows = lax.broadcasted_iota(jnp.int32, o.shape, 0)
            ys_ref[...] = jnp.where((rows >= wlo_ref[w]) & (rows < whi_ref[w]), o, ys_ref[...])


def _experts(xs, work, w1, w3, w2):
    s = xs.shape[0]
    ts = MOE_TS
    n_work = work[0].shape[0]
    return pl.pallas_call(
        _experts_kernel,
        grid_spec=pltpu.PrefetchScalarGridSpec(
            num_scalar_prefetch=6,
            grid=(n_work,),
            in_specs=[
                pl.BlockSpec((ts, D_MODEL), lambda w, wt, we, *_: (wt[w], 0)),
                pl.BlockSpec((1, D_MODEL, D_EXPERT), lambda w, wt, we, *_: (we[w], 0, 0)),
                pl.BlockSpec((1, D_MODEL, D_EXPERT), lambda w, wt, we, *_: (we[w], 0, 0)),
                pl.BlockSpec((1, D_EXPERT, D_MODEL), lambda w, wt, we, *_: (we[w], 0, 0)),
            ],
            out_specs=pl.BlockSpec((ts, D_MODEL), lambda w, wt, we, *_: (wt[w], 0)),
            scratch_shapes=[pltpu.VMEM((D_MODEL, D_EXPERT), BF16), pltpu.VMEM((D_MODEL, D_EXPERT), BF16),
                            pltpu.VMEM((D_EXPERT, D_MODEL), BF16)],
        ),
        out_shape=jax.ShapeDtypeStruct((s, D_MODEL), F32),
        compiler_params=_cparams("arbitrary"),
        name="moe_experts",
    )(*work, xs, w1, w3, w2)


def _expert_work_items(counts, total):
    ts = MOE_TS
    n_tiles = total // ts
    n_work = n_tiles + N_EXPERTS - 1
    offs = jnp.cumsum(counts) - counts
    t0 = (jnp.arange(n_tiles, dtype=jnp.int32) * ts)[:, None]
    lo = jnp.maximum(t0, offs[None, :])
    hi = jnp.minimum(t0 + ts, (offs + counts)[None, :])
    nonempty = (hi > lo).reshape(-1)
    nw = jnp.sum(nonempty.astype(jnp.int32))
    idx = jnp.nonzero(nonempty, size=n_work, fill_value=0)[0].astype(jnp.int32)
    idx = jnp.where(jnp.arange(n_work) < nw, idx, idx[jnp.maximum(nw - 1, 0)])
    wt = idx // N_EXPERTS
    we = idx % N_EXPERTS
    wlo = lo.reshape(-1)[idx] - wt * ts
    whi = hi.reshape(-1)[idx] - wt * ts
    wfirst = jnp.concatenate([jnp.ones((1,), jnp.int32), (wt[1:] != wt[:-1]).astype(jnp.int32)])
    return wt, we, wlo, whi, wfirst, nw.reshape(1)


def _combine_kernel(slots_ref, slots_next_ref, h_ref, route_ref, gfin_ref, ys_hbm, yp_ref, ysm_ref, gbuf, sems,
                    *, n_prompt_tiles):
    i = pl.program_id(0)
    n = pl.num_programs(0)
    tm = h_ref.shape[0]

    def gather(s_ref, buf):
        for r in range(tm):
            pltpu.make_async_copy(ys_hbm.at[pl.ds(s_ref[0, 0, 2 * r], 1)], gbuf.at[buf, 0, pl.ds(r, 1)],
                                  sems.at[buf]).start(priority=0)
            pltpu.make_async_copy(ys_hbm.at[pl.ds(s_ref[0, 0, 2 * r + 1], 1)], gbuf.at[buf, 1, pl.ds(r, 1)],
                                  sems.at[buf]).start(priority=1)

    cur = i % 2

    @pl.when(i == 0)
    def _():
        gather(slots_ref, 0)

    @pl.when(i + 1 < n)
    def _():
        gather(slots_next_ref, 1 - cur)

    for k in range(2):
        pltpu.make_async_copy(ys_hbm.at[pl.ds(0, tm)], gbuf.at[cur, k], sems.at[cur]).wait()
    route = route_ref[...]
    lane = lax.broadcasted_iota(jnp.int32, route.shape, 1)
    w1 = jnp.sum(jnp.where(lane == ROUTE_W1, route, 0.0), axis=1, keepdims=True)
    w2 = jnp.sum(jnp.where(lane == ROUTE_W2, route, 0.0), axis=1, keepdims=True)
    hf = h_ref[...] + (w1 * gbuf[cur, 0] + w2 * gbuf[cur, 1])
    y = hf * lax.rsqrt(jnp.mean(hf * hf, axis=-1, keepdims=True) + NORM_EPS) * gfin_ref[...]

    @pl.when(i < n_prompt_tiles)
    def _():
        yp_ref[...] = y

    @pl.when(i >= n_prompt_tiles)
    def _():
        ysm_ref[...] = y


def _combine(h, route, slots, ys, gfin, n_prompt_tiles):
    t = h.shape[0]
    tm = MOE_TM
    n_p = n_prompt_tiles
    return pl.pallas_call(
        functools.partial(_combine_kernel, n_prompt_tiles=n_p),
        grid_spec=pltpu.PrefetchScalarGridSpec(
            num_scalar_prefetch=0,
            grid=(t // tm,),
            in_specs=[pl.BlockSpec((1, 1, 2 * tm), lambda i: (i, 0, 0), memory_space=pltpu.SMEM),
                      pl.BlockSpec((1, 1, 2 * tm), lambda i: (jnp.minimum(i + 1, t // tm - 1), 0, 0),
                                   memory_space=pltpu.SMEM),
                      pl.BlockSpec((tm, D_MODEL), lambda i: (i, 0)),
                      pl.BlockSpec((tm, ROUTER_LANES), lambda i: (i, 0)),
                      pl.BlockSpec((1, D_MODEL), lambda i: (0, 0)),
                      pl.BlockSpec(memory_space=pl.ANY)],
            out_specs=[pl.BlockSpec((tm, D_MODEL), lambda i: (jnp.minimum(i, n_p - 1), 0)),
                       pl.BlockSpec((tm, D_MODEL), lambda i: (0, 0))],
            scratch_shapes=[pltpu.VMEM((2, 2, tm, D_MODEL), F32), pltpu.SemaphoreType.DMA((2,))],
        ),
        out_shape=[jax.ShapeDtypeStruct((n_p * tm, D_MODEL), F32), jax.ShapeDtypeStruct((tm, D_MODEL), F32)],
        compiler_params=_cparams("arbitrary"),
        name="moe_combine",
    )(slots, slots, h, route, gfin, ys)


def _v_tiles(v, tblk):
    b, t, _ = v.shape
    x = v.reshape(b, t // tblk, tblk, 2, RWKV_PAIRS, RWKV_HEAD).transpose(0, 1, 4, 5, 3, 2)
    x = jnp.pad(x, ((0, 0),) * 5 + ((0, RWKV_HEAD - tblk),))
    return x.reshape(b, t // tblk, RWKV_PAIRS, RWKV_HEAD, LANE).astype(BF16)


def _y_rows(y, bb, t):
    nb, tpad, _ = y.shape
    return y.reshape(nb, tpad, bb, RWKV_WIDTH).transpose(0, 2, 1, 3).reshape(nb * bb, tpad, RWKV_WIDTH)[:, :t]


def kernel(x_prompt, x_sample, state_gla, state_rwkv, state_shift, meta_tokens, norm_mix, w_in, gla_gate_w2,
           gla_gate_b, gla_norm, rwkv_mu, rwkv_w0, rwkv_w2, rwkv_a0, rwkv_a2, rwkv_g2, rwkv_kk, rwkv_ka, rwkv_rk,
           rwkv_ln_w, rwkv_ln_b, w_out, norm_ffn, router_group_w, router_group_b, router_expert_w,
           router_expert_b, moe_w1, moe_w3, moe_w2, norm_final):
    bp, tp, _ = x_prompt.shape
    bs, ts, _ = x_sample.shape
    assert state_gla.shape[0] == 1, "one layer"
    lyr = 0

    w_in_l = w_in[lyr]
    wg = jnp.pad(w_in_l[:, :GLA_COLS], ((0, 0), (0, GLA_PCOLS - GLA_COLS))).astype(BF16)
    wr = w_in_l[:, GLA_COLS:].astype(BF16)
    g_mix = norm_mix[lyr][None, :]
    gw2p = jnp.pad(gla_gate_w2[lyr], ((0, LANE - GLA_GATE_RANK), (0, 0)))
    gb = gla_gate_b[lyr][None, :]
    gn = gla_norm[lyr][None, :]
    w2p = jnp.pad(rwkv_w2[lyr], ((0, 64), (0, 0))).astype(BF16)
    a2p = jnp.pad(rwkv_a2[lyr], ((64, 0), (0, 0))).astype(BF16)
    pre_params = (rwkv_mu[lyr][None, :], rwkv_w0[lyr][None, :], w2p, rwkv_a0[lyr][None, :], a2p,
                  rwkv_g2[lyr].astype(BF16), rwkv_kk[lyr][None, :], rwkv_ka[lyr][None, :],
                  rwkv_rk[lyr].reshape(1, RWKV_WIDTH), _head_selector())
    lnw = rwkv_ln_w[lyr][None, :]
    lnb = rwkv_ln_b[lyr][None, :]
    wo = w_out[lyr].astype(BF16)
    gffn = norm_ffn[lyr][None, :]
    n_used = N_GROUPS + N_EXPERTS
    w_router = jnp.pad(
        jnp.concatenate([router_group_w[lyr],
                         router_expert_w[lyr].transpose(1, 0, 2).reshape(D_MODEL, N_EXPERTS)], axis=1),
        ((0, 0), (0, ROUTER_LANES - n_used)))
    b_router = jnp.pad(jnp.concatenate([router_group_b[lyr], router_expert_b[lyr].reshape(N_EXPERTS)]),
                       (0, ROUTER_LANES - n_used))[None, :]
    gfin = norm_final[None, :]

    pg_m, pr_m = _inproj(meta_tokens, g_mix, wg, wr, N_META)
    _, sg_m = _gla(pg_m[None], jnp.zeros((1, GLA_HEADS, GLA_DK, GLA_DV), F32), gw2p, gb, gn,
                   bb=1, chunk=N_META, sub=N_META, t_valid=N_META)
    r, w, k, kk, kka, v_m, _, _ = _rwkv_pre(pr_m[None], jnp.zeros((1, 1, RWKV_COLS), F32), pre_params,
                                            tm=N_META, explicit_prev=False, emit_vt=False)
    _, sr_m = _rwkv_rec(r, w, k, kk, kka, _v_tiles(v_m, N_META),
                        jnp.zeros((1, RWKV_HEADS, RWKV_HEAD, RWKV_HEAD), F32), bb=1, n_steps=N_META)

    xp = x_prompt.reshape(bp * tp, D_MODEL)
    pg_p, pr_p = _inproj(xp, g_mix, wg, wr, INPROJ_TM)
    og_p, sg_p = _gla(pg_p.reshape(bp, tp, GLA_PCOLS), jnp.broadcast_to(sg_m, (bp,) + sg_m.shape[1:]), gw2p, gb, gn,
                      bb=SEQ_BLOCK, chunk=GLA_CHUNK, sub=GLA_SUB, t_valid=GLA_CHUNK)
    pr_p3 = pr_p.reshape(bp, tp, RWKV_COLS)
    first_prev = jnp.broadcast_to(pr_m[N_META - 1][None, None, :], (bp, 1, RWKV_COLS))
    r, w, k, kk, kka, vt_p, bv_p, gate_p = _rwkv_pre(pr_p3, first_prev, pre_params, tm=PRE_TM, explicit_prev=False,
                                                     emit_vt=True)
    y_p, sr_p = _rwkv_rec(r, w, k, kk, kka, vt_p, jnp.broadcast_to(sr_m, (bp,) + sr_m.shape[1:]),
                          bb=bp, n_steps=REC_TB)
    prompt_rows = (xp, og_p.reshape(bp * tp, GLA_WIDTH), y_p.reshape(tp, bp * RWKV_WIDTH),
                   bv_p.reshape(bp * tp, RWKV_WIDTH), gate_p.reshape(bp * tp, RWKV_WIDTH))

    xs = x_sample.reshape(bs * ts, D_MODEL)
    pg_s, pr_s = _inproj(xs, g_mix, wg, wr, bs * ts)
    ts_pad = 8
    pg_s3 = jnp.pad(pg_s.reshape(bs, ts, GLA_PCOLS), ((0, 0), (0, ts_pad - ts), (0, 0)))
    og_s, sg_s = _gla(pg_s3, state_gla[lyr], gw2p, gb, gn, bb=SEQ_BLOCK, chunk=ts_pad, sub=ts_pad, t_valid=ts)
    og_s = og_s[:, :ts]
    pr_s3 = pr_s.reshape(bs, ts, RWKV_COLS)
    prev_s = jnp.concatenate([state_shift[lyr][:, None, :], pr_s3[:, :-1]], axis=1)
    r, w, k, kk, kka, v_s, bv_s, gate_s = _rwkv_pre(pr_s3.reshape(1, bs * ts, RWKV_COLS),
                                                     prev_s.reshape(1, bs * ts, RWKV_COLS), pre_params,
                                                     tm=bs * ts, explicit_prev=True, emit_vt=False)
    unflat = lambda a: a.reshape(RWKV_PAIRS, bs, ts, LANE).transpose(1, 0, 2, 3)
    y_s, sr_s = _rwkv_rec(unflat(r), unflat(w), unflat(k), unflat(kk), unflat(kka),
                          _v_tiles(v_s.reshape(bs, ts, RWKV_WIDTH), ts), state_rwkv[lyr], bb=SEQ_BLOCK, n_steps=ts)
    y_s = _y_rows(y_s, 8, ts)
    sample_rows = (xs, og_s.reshape(bs * ts, GLA_WIDTH), y_s.reshape(bs * ts, RWKV_WIDTH),
                   bv_s.reshape(bs * ts, RWKV_WIDTH), gate_s.reshape(bs * ts, RWKV_WIDTH))

    h_all, n2_all, route, counts = _mix_router(prompt_rows, sample_rows, lnw, lnb, wo, gffn, w_router, b_router,
                                               seq_tiles=tp // MOE_TM)
    n_tok = h_all.shape[0]
    n_p_tiles = (bp * tp) // MOE_TM
    cnt = counts[0, EXPERT_LANE0:EXPERT_LANE0 + N_EXPERTS].astype(jnp.int32)
    offs = jnp.cumsum(cnt) - cnt
    eid = route[:, ROUTE_E1:ROUTE_E2 + 1].astype(jnp.int32)
    pos = route[:, ROUTE_P1:ROUTE_P2 + 1].astype(jnp.int32)
    onehot = (eid[..., None] == jnp.arange(N_EXPERTS, dtype=jnp.int32)).astype(F32)
    off = jnp.einsum("tke,e->tk", onehot, offs.astype(F32), precision=HIGHEST).astype(jnp.int32)
    slots = (off + pos).reshape(n_tok // MOE_TM, 1, 2 * MOE_TM)
    xs_sorted = _dispatch(n2_all, slots)
    ys_sorted = _experts(xs_sorted, _expert_work_items(cnt, 2 * n_tok), moe_w1[lyr], moe_w3[lyr], moe_w2[lyr])
    y_prompt, y_sample = _combine(h_all, route, slots, ys_sorted, gfin, n_p_tiles)
    y_prompt = y_prompt.reshape(bp, tp, D_MODEL)
    y_sample = y_sample.reshape(bs, ts, D_MODEL)

    return (y_prompt, y_sample,
            sg_p[None], sr_p[None], pr_p3[:, -1][None],
            sg_s[None], sr_s[None], pr_s3[:, -1][None])
```

```python
import functools

import jax
import jax.numpy as jnp
from jax import lax
from jax.experimental import pallas as pl
from jax.experimental.pallas import tpu as pltpu

F32 = jnp.float32
BF16 = jnp.bfloat16
HIGHEST = lax.Precision.HIGHEST

D_MODEL = 1024
N_META = 16
NORM_EPS = 1e-6
LOG2E = 1.4426950408889634
GLA_HEADS = 4
GLA_DK = 64
GLA_DV = 128
GLA_QK = GLA_HEADS * GLA_DK
GLA_WIDTH = GLA_HEADS * GLA_DV
GLA_GATE_RANK = 16
GLA_GATE_NORM = 16.0
GLA_CHUNK = 64
GLA_SUB = 8
GLA_COLS = 2 * GLA_QK + 2 * GLA_WIDTH + GLA_GATE_RANK
GLA_PCOLS = 2 * GLA_QK + 2 * GLA_WIDTH + 128
RWKV_WIDTH = 512
RWKV_HEAD = 64
RWKV_HEADS = 8
RWKV_PAIRS = RWKV_HEADS // 2
RWKV_DECAY_SCALE = 0.606531
RWKV_GN_EPS = 64e-5
RWKV_COLS = 3 * RWKV_WIDTH + 64 + 64 + 128
REC_TB = 64
REC_UNROLL = 8
INPROJ_TM = 512
PRE_TM = 256
SEQ_BLOCK = 8
N_GROUPS = 4
EXPERTS_PER_GROUP = 8
N_EXPERTS = 32
D_EXPERT = 512
ROUTER_LANES = 128
EXPERT_LANE0 = N_GROUPS
ROUTE_E1, ROUTE_E2, ROUTE_W1, ROUTE_W2, ROUTE_P1, ROUTE_P2 = range(6)
MOE_TM = 512
MOE_TS = 512

LANE = 128
VMEM_LIMIT = 56 * 1024 * 1024


def _cparams(*sem):
    return pltpu.CompilerParams(dimension_semantics=sem, vmem_limit_bytes=VMEM_LIMIT)


def _block_ones(n, blk):
    i = jnp.arange(n)
    return (i[:, None] // blk == i[None, :] // blk).astype(BF16)


def _sigmoid(x):
    return 1.0 / (1.0 + jnp.exp(-x))


def _dot(a, b):
    return jnp.dot(a, b, preferred_element_type=F32)


def _dot_nt(a, b):
    return lax.dot_general(a, b, (((1,), (1,)), ((), ())), preferred_element_type=F32)


def _dot_tn(a, b):
    return lax.dot_general(a, b, (((0,), (0,)), ((), ())), preferred_element_type=F32)


def _split2(x):
    hi = x.astype(BF16)
    lo = (x - hi.astype(F32)).astype(BF16)
    return hi, lo


def _head_selector():
    return (jnp.arange(RWKV_WIDTH)[:, None] // RWKV_HEAD == jnp.arange(LANE)[None, :]).astype(BF16)


def _group_sum(x, sel):
    hi, lo = _split2(x)
    s_hi, s_lo = _split2(_dot(hi, sel) + _dot(lo, sel))
    return _dot_nt(s_hi, sel) + _dot_nt(s_lo, sel)


def _inproj_kernel(x_ref, g_ref, wg_ref, wr_ref, pg_ref, pr_ref):
    x = x_ref[...]
    n = x * lax.rsqrt(jnp.mean(x * x, axis=-1, keepdims=True) + NORM_EPS) * g_ref[...]
    nb = n.astype(BF16)
    pg_ref[...] = _dot(nb, wg_ref[...])
    pr_ref[...] = _dot(nb, wr_ref[...])


def _inproj(x, g, wg, wr, tm):
    t = x.shape[0]
    return pl.pallas_call(
        _inproj_kernel,
        grid=(t // tm,),
        in_specs=[
            pl.BlockSpec((tm, D_MODEL), lambda i: (i, 0)),
            pl.BlockSpec((1, D_MODEL), lambda i: (0, 0)),
            pl.BlockSpec((D_MODEL, GLA_PCOLS), lambda i: (0, 0)),
            pl.BlockSpec((D_MODEL, RWKV_COLS), lambda i: (0, 0)),
        ],
        out_specs=[
            pl.BlockSpec((tm, GLA_PCOLS), lambda i: (i, 0)),
            pl.BlockSpec((tm, RWKV_COLS), lambda i: (i, 0)),
        ],
        out_shape=[jax.ShapeDtypeStruct((t, GLA_PCOLS), F32), jax.ShapeDtypeStruct((t, RWKV_COLS), F32)],
        compiler_params=_cparams("parallel"),
        name="inproj",
    )(x, g, wg, wr)


def _gla_kernel(pg_ref, s0_ref, gw2_ref, gb_ref, gn_ref, bo_ref, tril_ref, o_ref, sout_ref, s_scr,
                *, bb, chunk, sub, t_valid):
    ci = pl.program_id(1)

    @pl.when(ci == 0)
    def _():
        s_scr[...] = s0_ref[...]

    bo = bo_ref[...]
    tril = tril_ref[...]
    lane = lax.broadcasted_iota(jnp.int32, (sub, LANE), 1) & (GLA_DK - 1)
    rowi = lax.broadcasted_iota(jnp.int32, (sub, LANE), 0)
    head0_s = lax.broadcasted_iota(jnp.int32, (sub, LANE), 1) < GLA_DK
    head0_c = lax.broadcasted_iota(jnp.int32, (chunk, LANE), 1) < GLA_DK

    for bi in range(bb):
        pg = pg_ref[bi]
        q = pg[:, 0:GLA_QK] * (GLA_DK ** -0.5)
        k = pg[:, GLA_QK:2 * GLA_QK]
        v = pg[:, 2 * GLA_QK:2 * GLA_QK + GLA_WIDTH]
        g = pg[:, 2 * GLA_QK + GLA_WIDTH:2 * GLA_QK + 2 * GLA_WIDTH]
        gl = pg[:, 2 * GLA_QK + 2 * GLA_WIDTH:]
        z = jnp.dot(gl, gw2_ref[...], precision=HIGHEST, preferred_element_type=F32) + gb_ref[...]
        lg = (jnp.minimum(z, 0.0) - jnp.log1p(jnp.exp(-jnp.abs(z)))) * (LOG2E / GLA_GATE_NORM)
        if t_valid < chunk:
            rows = lax.broadcasted_iota(jnp.int32, lg.shape, 0)
            lg = jnp.where(rows < t_valid, lg, 0.0)
        b = jnp.dot(tril, lg, precision=HIGHEST, preferred_element_type=F32)
        eb = jnp.exp2(b)
        blast = b[chunk - 1:chunk, :]
        kl = k * jnp.exp2(blast - b)
        qe = q * eb

        n_blk = chunk // sub
        n_pairs = GLA_HEADS // 2
        ps = []
        for hp in range(n_pairs):
            sl = slice(hp * LANE, (hp + 1) * LANE)
            for blk in range(n_blk):
                rs = slice(blk * sub, (blk + 1) * sub)
                qb, kb, bbk = q[rs, sl], k[rs, sl], b[rs, sl]
                for j in range(sub):
                    ps.append(qb * (kb[j:j + 1] * jnp.exp2(jnp.minimum(bbk - bbk[j:j + 1], 0.0))))
        red = _dot(jnp.concatenate(ps, axis=0).astype(BF16), bo)

        o_heads = []
        for hp in range(n_pairs):
            sl = slice(hp * LANE, (hp + 1) * LANE)
            kp, bp = k[:, sl], b[:, sl]
            row_blocks = []
            for blk in range(n_blk):
                rs = slice(blk * sub, (blk + 1) * sub)
                base = (hp * n_blk + blk) * sub * sub
                a = jnp.zeros((sub, LANE), F32)
                for j in range(sub):
                    a = jnp.where((lane == blk * sub + j) & (rowi >= j), red[base + j * sub:base + (j + 1) * sub], a)
                if blk > 0:
                    bref = bp[blk * sub - 1:blk * sub]
                    qt = q[rs, sl] * jnp.exp2(bp[rs] - bref)
                    kt = (kp * jnp.exp2(jnp.minimum(bref - bp, 0.0))).astype(BF16)
                    qt2 = jnp.concatenate([jnp.where(head0_s, qt, 0.0), jnp.where(head0_s, 0.0, qt)], axis=0)
                    off2 = _dot_nt(qt2.astype(BF16), kt)
                    a = jnp.where(lane < blk * sub, jnp.concatenate([off2[:sub], off2[sub:]], axis=1), a)
                row_blocks.append(a)
            a_pair = row_blocks[0] if n_blk == 1 else jnp.concatenate(row_blocks, axis=0)
            v0 = v[:, 2 * hp * GLA_DV:(2 * hp + 1) * GLA_DV]
            v1 = v[:, (2 * hp + 1) * GLA_DV:(2 * hp + 2) * GLA_DV]
            s_pair = s_scr[bi, 2 * hp:2 * hp + 2].reshape(2 * GLA_DK, GLA_DV)
            qe_p, kl_p = qe[:, sl], kl[:, sl]

            def by_head(x):
                return jnp.concatenate([jnp.where(head0_c, x, 0.0), jnp.where(head0_c, 0.0, x)], axis=0)

            if chunk == GLA_DK:
                v_rows = jnp.concatenate([v0, v1], axis=0)
            else:
                zpad = jnp.zeros((GLA_DK - chunk, GLA_DV), F32)
                v_rows = jnp.concatenate([v0, zpad, v1, zpad], axis=0)
            lhs = jnp.concatenate([by_head(a_pair), by_head(qe_p)], axis=1).astype(BF16)
            rhs = jnp.concatenate([v_rows, s_pair], axis=0).astype(BF16)
            o2 = _dot(lhs, rhs)
            upd = _dot_tn(by_head(kl_p).astype(BF16), jnp.concatenate([v0, v1], axis=0).astype(BF16))
            dcol = jnp.broadcast_to(jnp.exp2(blast[:, sl]), (8, LANE)).T[:, 0:1]
            s_new = dcol * s_pair + upd
            s_scr[bi, 2 * hp] = s_new[:GLA_DK]
            s_scr[bi, 2 * hp + 1] = s_new[GLA_DK:]
            for h2 in range(2):
                o_h = o2[h2 * chunk:(h2 + 1) * chunk]
                o_heads.append(o_h * lax.rsqrt(jnp.mean(o_h * o_h, axis=-1, keepdims=True) + NORM_EPS) * gn_ref[...])
        o = jnp.concatenate(o_heads, axis=1)
        o_ref[bi] = o * (g * _sigmoid(g))

    @pl.when(ci == pl.num_programs(1) - 1)
    def _():
        sout_ref[...] = s_scr[...]


def _gla(pg, s0, gw2p, gb, gn, *, bb, chunk, sub, t_valid):
    b, t, _ = pg.shape
    tril = jnp.tril(jnp.ones((chunk, chunk), F32))
    kern = functools.partial(_gla_kernel, bb=bb, chunk=chunk, sub=sub, t_valid=t_valid)
    return pl.pallas_call(
        kern,
        grid=(b // bb, t // chunk),
        in_specs=[
            pl.BlockSpec((bb, chunk, GLA_PCOLS), lambda i, j: (i, j, 0)),
            pl.BlockSpec((bb, GLA_HEADS, GLA_DK, GLA_DV), lambda i, j: (i, 0, 0, 0)),
            pl.BlockSpec((LANE, GLA_QK), lambda i, j: (0, 0)),
            pl.BlockSpec((1, GLA_QK), lambda i, j: (0, 0)),
            pl.BlockSpec((1, GLA_DV), lambda i, j: (0, 0)),
            pl.BlockSpec((LANE, LANE), lambda i, j: (0, 0)),
            pl.BlockSpec((chunk, chunk), lambda i, j: (0, 0)),
        ],
        out_specs=[
            pl.BlockSpec((bb, chunk, GLA_WIDTH), lambda i, j: (i, j, 0)),
            pl.BlockSpec((bb, GLA_HEADS, GLA_DK, GLA_DV), lambda i, j: (i, 0, 0, 0)),
        ],
        out_shape=[jax.ShapeDtypeStruct((b, t, GLA_WIDTH), F32),
                   jax.ShapeDtypeStruct((b, GLA_HEADS, GLA_DK, GLA_DV), F32)],
        scratch_shapes=[pltpu.VMEM((bb, GLA_HEADS, GLA_DK, GLA_DV), F32)],
        compiler_params=_cparams("parallel", "arbitrary"),
        name="gla_chunk",
    )(pg, s0, gw2p, gb, gn, _block_ones(LANE, GLA_DK), tril)


def _rwkv_pre_kernel(pr_ref, aux_ref, mu_ref, w0_ref, w2_ref, a0_ref, a2_ref, g2_ref, kk_ref, ka_ref, rk_ref, bo_ref,
                     r_out, w_out, k_out, kkn_out, kka_out, v_out, bv_out, gate_out, carry_scr,
                     *, tm, explicit_prev, emit_vt):
    pr = pr_ref[0]
    if explicit_prev:
        prev = aux_ref[0]
    else:
        j = pl.program_id(1)
        row0 = jnp.where(j == 0, aux_ref[0], carry_scr[...])
        rows = lax.broadcasted_iota(jnp.int32, pr.shape, 0)
        prev = jnp.where(rows == 0, row0, pltpu.roll(pr, 1, 0))
        carry_scr[...] = pr[tm - 1:tm, :]
    xm = pr + (prev - pr) * mu_ref[...]
    wd = RWKV_WIDTH
    rr, rk, rv = xm[:, 0:wd], xm[:, wd:2 * wd], xm[:, 2 * wd:3 * wd]
    wa = xm[:, 3 * wd:3 * wd + LANE]
    gl2 = xm[:, 3 * wd + LANE:3 * wd + 2 * LANE]
    logw = -RWKV_DECAY_SCALE * _sigmoid(w0_ref[...] + _dot(jnp.tanh(wa).astype(BF16), w2_ref[...]))
    aa = _sigmoid(a0_ref[...] + _dot(wa.astype(BF16), a2_ref[...]))
    gate = _dot(_sigmoid(gl2).astype(BF16), g2_ref[...])
    bo = bo_ref[...]
    kk = rk * kk_ref[...]
    kk = kk / jnp.maximum(jnp.sqrt(_group_sum(kk * kk, bo)), 1e-12)
    k = rk * (1.0 + (aa - 1.0) * ka_ref[...])
    bv = _group_sum(rr * k * rk_ref[...], bo) * rv
    w = jnp.exp(logw)
    kka = kk * aa
    hd = RWKV_HEAD

    def pair(x, hp):
        return jnp.concatenate([x[:, hp * hd:(hp + 1) * hd], x[:, (hp + RWKV_PAIRS) * hd:(hp + RWKV_PAIRS + 1) * hd]],
                               axis=1)

    for hp in range(RWKV_PAIRS):
        r_out[0, hp] = pair(rr, hp)
        w_out[0, hp] = pair(w, hp)
        k_out[0, hp] = pair(k, hp)
        kkn_out[0, hp] = pair(kk, hp)
        kka_out[0, hp] = pair(kka, hp)
    if emit_vt:
        vt = rv.T
        for tb in range(tm // REC_TB):
            ts = slice(tb * REC_TB, (tb + 1) * REC_TB)
            for hp in range(RWKV_PAIRS):
                lo, hi = hp * hd, (hp + RWKV_PAIRS) * hd
                v_out[0, tb, hp] = jnp.concatenate([vt[lo:lo + hd, ts], vt[hi:hi + hd, ts]], axis=1).astype(BF16)
    else:
        v_out[0] = rv
    bv_out[0] = bv
    gate_out[0] = gate


def _rwkv_pre(pr, aux, params, *, tm, explicit_prev, emit_vt):
    b, t, _ = pr.shape
    kern = functools.partial(_rwkv_pre_kernel, tm=tm, explicit_prev=explicit_prev, emit_vt=emit_vt)
    aux_spec = (pl.BlockSpec((1, tm, RWKV_COLS), lambda i, j: (i, j, 0)) if explicit_prev
                else pl.BlockSpec((1, 1, RWKV_COLS), lambda i, j: (i, 0, 0)))
    const = lambda shape: pl.BlockSpec(shape, lambda i, j: (0,) * len(shape))
    pair_spec = pl.BlockSpec((1, RWKV_PAIRS, tm, LANE), lambda i, j: (i, 0, j, 0))
    row_spec = pl.BlockSpec((1, tm, RWKV_WIDTH), lambda i, j: (i, j, 0))
    pair_shape = jax.ShapeDtypeStruct((b, RWKV_PAIRS, t, LANE), F32)
    row_shape = jax.ShapeDtypeStruct((b, t, RWKV_WIDTH), F32)
    if emit_vt:
        v_spec = pl.BlockSpec((1, tm // REC_TB, RWKV_PAIRS, RWKV_HEAD, LANE), lambda i, j: (i, j, 0, 0, 0))
        v_shape = jax.ShapeDtypeStruct((b, t // REC_TB, RWKV_PAIRS, RWKV_HEAD, LANE), BF16)
    else:
        v_spec, v_shape = row_spec, row_shape
    return pl.pallas_call(
        kern,
        grid=(b, t // tm),
        in_specs=[
            pl.BlockSpec((1, tm, RWKV_COLS), lambda i, j: (i, j, 0)),
            aux_spec,
            const((1, RWKV_COLS)), const((1, RWKV_WIDTH)), const((LANE, RWKV_WIDTH)), const((1, RWKV_WIDTH)),
            const((LANE, RWKV_WIDTH)), const((LANE, RWKV_WIDTH)), const((1, RWKV_WIDTH)), const((1, RWKV_WIDTH)),
            const((1, RWKV_WIDTH)), const((RWKV_WIDTH, LANE)),
        ],
        out_specs=[pair_spec] * 5 + [v_spec, row_spec, row_spec],
        out_shape=[pair_shape] * 5 + [v_shape, row_shape, row_shape],
        scratch_shapes=[pltpu.VMEM((1, RWKV_COLS), F32)],
        compiler_params=_cparams("parallel", "arbitrary"),
        name="rwkv_pre",
    )(pr, aux, *params)


def _rwkv_rec_kernel(r_ref, w_ref, k_ref, kk_ref, kka_ref, vt_ref, s0_ref, bo_ref, vsel_ref, ysel_ref,
                     y_ref, sout_ref, s_scr, t1_scr, t3_scr, yt_scr, *, bb, n_steps):
    tb = pl.program_id(1)
    nc = bb * RWKV_PAIRS
    hd = RWKV_HEAD

    @pl.when(tb == 0)
    def _():
        for c in range(nc):
            bi, hp = divmod(c, RWKV_PAIRS)
            s_scr[c] = jnp.concatenate([s0_ref[bi, hp], s0_ref[bi, hp + RWKV_PAIRS]], axis=1)

    bo = bo_ref[...]

    def step(t, u):
        row = pl.ds(t, 1)
        for c in range(nc):
            bi, hp = divmod(c, RWKV_PAIRS)
            t1_scr[c * hd:(c + 1) * hd, :] = (s_scr[c] * kk_ref[bi, hp, row, :]).astype(BF16)
        sab = _dot(t1_scr[...], bo)
        vb = _dot(vt_ref[...].reshape(nc * hd, LANE), vsel_ref[t])
        for c in range(nc):
            bi, hp = divmod(c, RWKV_PAIRS)
            rs = slice(c * hd, (c + 1) * hd)
            s2 = (s_scr[c] * w_ref[bi, hp, row, :] - sab[rs] * kka_ref[bi, hp, row, :]
                  + vb[rs] * k_ref[bi, hp, row, :])
            s_scr[c] = s2
            t3_scr[rs, :] = (s2 * r_ref[bi, hp, row, :]).astype(BF16)
        yt_scr[...] += _dot_nt(ysel_ref[u], t3_scr[...])

    n_inner = min(8, n_steps)

    def block8(t8, carry):
        yt_scr[...] = jnp.zeros(yt_scr.shape, F32)

        def inner(u, c2):
            step(t8 * 8 + u, u)
            return c2

        lax.fori_loop(0, n_inner, inner, 0, unroll=REC_UNROLL)
        t0 = pl.multiple_of(t8 * 8, 8)
        blk = RWKV_PAIRS * hd
        for bi in range(bb):
            for h2 in range(2):
                y_ref[0, pl.ds(t0, 8), bi * RWKV_WIDTH + h2 * blk:bi * RWKV_WIDTH + (h2 + 1) * blk] = (
                    yt_scr[h2 * 8:(h2 + 1) * 8, bi * blk:(bi + 1) * blk])
        return carry

    lax.fori_loop(0, (n_steps + 7) // 8, block8, 0)

    @pl.when(tb == pl.num_programs(1) - 1)
    def _():
        for c in range(nc):
            bi, hp = divmod(c, RWKV_PAIRS)
            s_c = s_scr[c]
            sout_ref[bi, hp] = s_c[:, :RWKV_HEAD]
            sout_ref[bi, hp + RWKV_PAIRS] = s_c[:, RWKV_HEAD:]


def _rwkv_rec(r, w, k, kk, kka, vt, s0, *, bb, n_steps):
    b, _, t, _ = r.shape
    tblk = min(REC_TB, t)
    ntb = t // tblk
    nc = bb * RWKV_PAIRS
    lane = jnp.arange(LANE)
    vsel = ((lane[None, :, None] // RWKV_HEAD == lane[None, None, :] // RWKV_HEAD)
            & (lane[None, :, None] % RWKV_HEAD == jnp.arange(RWKV_HEAD)[:, None, None])).astype(BF16)
    ysel = (jnp.arange(16)[None, :, None]
            == 8 * (lane[None, None, :] // RWKV_HEAD) + jnp.arange(8)[:, None, None]).astype(BF16)
    kern = functools.partial(_rwkv_rec_kernel, bb=bb, n_steps=n_steps)
    pair_spec = pl.BlockSpec((bb, RWKV_PAIRS, tblk, LANE), lambda i, j: (i, 0, j, 0))
    state_spec = pl.BlockSpec((bb, RWKV_HEADS, RWKV_HEAD, RWKV_HEAD), lambda i, j: (i, 0, 0, 0))
    ytb = max(tblk, 8)
    return pl.pallas_call(
        kern,
        grid=(b // bb, ntb),
        in_specs=[pair_spec] * 5 + [
            pl.BlockSpec((bb, 1, RWKV_PAIRS, RWKV_HEAD, LANE), lambda i, j: (i, j, 0, 0, 0)),
            state_spec,
            pl.BlockSpec((LANE, LANE), lambda i, j: (0, 0)),
            pl.BlockSpec((RWKV_HEAD, LANE, LANE), lambda i, j: (0, 0, 0)),
            pl.BlockSpec((8, 16, LANE), lambda i, j: (0, 0, 0)),
        ],
        out_specs=[
            pl.BlockSpec((1, ytb, bb * RWKV_WIDTH), lambda i, j: (i, j, 0)),
            state_spec,
        ],
        out_shape=[jax.ShapeDtypeStruct((b // bb, ntb * ytb, bb * RWKV_WIDTH), F32),
                   jax.ShapeDtypeStruct((b, RWKV_HEADS, RWKV_HEAD, RWKV_HEAD), F32)],
        scratch_shapes=[pltpu.VMEM((nc, RWKV_HEAD, LANE), F32),
                        pltpu.VMEM((nc * RWKV_HEAD, LANE), BF16),
                        pltpu.VMEM((nc * RWKV_HEAD, LANE), BF16),
                        pltpu.VMEM((16, nc * RWKV_HEAD), F32)],
        compiler_params=_cparams("parallel", "arbitrary"),
        name="rwkv_rec",
    )(r, w, k, kk, kka, vt, s0, _block_ones(LANE, RWKV_HEAD), vsel, ysel)


def _mix_router_body(x_ref, og_ref, y_ref, bv_ref, gate_ref, lnw_ref, lnb_ref, wo_ref, gffn_ref, wr_hi_ref, wr_lo_ref,
                     br_ref, bo_ref, tril_ref, h_ref, n2_ref, route_ref, cnt_scr):
    bo = bo_ref[...]
    y = y_ref[...]
    inv_n = 1.0 / RWKV_HEAD
    d = y - _group_sum(y, bo) * inv_n
    var = _group_sum(d * d, bo) * inv_n
    yn = d * lax.rsqrt(var + RWKV_GN_EPS) * lnw_ref[...] + lnb_ref[...] + bv_ref[...]
    o_rwkv = yn * gate_ref[...]
    mix = (_dot(og_ref[...].astype(BF16), wo_ref[0:GLA_WIDTH, :])
           + _dot(o_rwkv.astype(BF16), wo_ref[GLA_WIDTH:, :]))
    h = x_ref[...] + mix
    h_ref[...] = h
    n2 = h * lax.rsqrt(jnp.mean(h * h, axis=-1, keepdims=True) + NORM_EPS) * gffn_ref[...]
    n2_ref[...] = n2
    n2_hi, n2_lo = _split2(n2)
    lg = (_dot(n2_hi, wr_hi_ref[...]) + _dot(n2_hi, wr_lo_ref[...]) + _dot(n2_lo, wr_hi_ref[...])) + br_ref[...]
    neg = jnp.float32(-3.0e38)
    big = jnp.float32(1.0e9)
    lane = lax.broadcasted_iota(jnp.int32, lg.shape, 1).astype(F32)
    gmask = lane < N_GROUPS
    gmax = jnp.max(jnp.where(gmask, lg, neg), axis=1, keepdims=True)
    p_top = 1.0 / jnp.sum(jnp.where(gmask, jnp.exp(jnp.minimum(lg - gmax, 0.0)), 0.0), axis=1, keepdims=True)
    gidx = jnp.min(jnp.where(gmask & (lg == gmax), lane, big), axis=1, keepdims=True)
    e_lo = EXPERT_LANE0 + gidx * EXPERTS_PER_GROUP
    emask = (lane >= e_lo) & (lane < e_lo + EXPERTS_PER_GROUP)
    m1 = jnp.max(jnp.where(emask, lg, neg), axis=1, keepdims=True)
    e1 = jnp.min(jnp.where(emask & (lg == m1), lane, big), axis=1, keepdims=True)
    emask2 = emask & (lane != e1)
    m2 = jnp.max(jnp.where(emask2, lg, neg), axis=1, keepdims=True)
    e2 = jnp.min(jnp.where(emask2 & (lg == m2), lane, big), axis=1, keepdims=True)
    r21 = jnp.exp(m2 - m1)
    w1 = p_top / (1.0 + r21)
    w2 = p_top * r21 / (1.0 + r21)
    o1 = lane == e1
    o2 = lane == e2
    onehot = jnp.where(o1 | o2, 1.0, 0.0)
    rank = _dot(tril_ref[...], onehot.astype(BF16)) + cnt_scr[...]
    pos1 = jnp.sum(jnp.where(o1, rank, 0.0), axis=1, keepdims=True)
    pos2 = jnp.sum(jnp.where(o2, rank, 0.0), axis=1, keepdims=True)
    cnt_scr[...] += jnp.sum(onehot, axis=0, keepdims=True)
    route = jnp.where(lane == ROUTE_E1, e1 - EXPERT_LANE0, 0.0)
    route = jnp.where(lane == ROUTE_E2, e2 - EXPERT_LANE0, route)
    route = jnp.where(lane == ROUTE_W1, w1, route)
    route = jnp.where(lane == ROUTE_W2, w2, route)
    route = jnp.where(lane == ROUTE_P1, pos1, route)
    route_ref[...] = jnp.where(lane == ROUTE_P2, pos2, route)


def _mix_router_kernel(*refs, n_prompt_tiles):
    prompt_rows, sample_rows, rest = refs[0:5], refs[5:10], refs[10:]
    consts, (h_ref, n2_ref, route_ref, cnt_ref, cnt_scr) = rest[:9], rest[9:]
    i = pl.program_id(0)

    @pl.when(i == 0)
    def _():
        cnt_scr[...] = jnp.zeros(cnt_scr.shape, F32)

    @pl.when(i < n_prompt_tiles)
    def _():
        _mix_router_body(*prompt_rows, *consts, h_ref, n2_ref, route_ref, cnt_scr)

    @pl.when(i >= n_prompt_tiles)
    def _():
        _mix_router_body(*sample_rows, *consts, h_ref, n2_ref, route_ref, cnt_scr)

    cnt_ref[...] = cnt_scr[...]


def _mix_router(prompt_rows, sample_rows, lnw, lnb, wo, gffn, wr, br, *, seq_tiles):
    tm = MOE_TM
    n_p = prompt_rows[0].shape[0] // tm
    assert sample_rows[0].shape[0] == tm
    t = (n_p + 1) * tm
    widths = (D_MODEL, GLA_WIDTH, RWKV_WIDTH, RWKV_WIDTH, RWKV_WIDTH)
    p_specs = [pl.BlockSpec((tm, n), lambda i: (jnp.minimum(i, n_p - 1), 0)) for n in widths]
    p_specs[2] = pl.BlockSpec(
        (tm, RWKV_WIDTH), lambda i: (jnp.minimum(i, n_p - 1) % seq_tiles, jnp.minimum(i, n_p - 1) // seq_tiles))
    s_specs = [pl.BlockSpec((tm, n), lambda i: (0, 0)) for n in widths]
    const = lambda shape: pl.BlockSpec(shape, lambda i: (0,) * len(shape))
    row = lambda n: pl.BlockSpec((tm, n), lambda i: (i, 0))
    tril = jnp.tril(jnp.ones((tm, tm), F32), -1).astype(BF16)
    return pl.pallas_call(
        functools.partial(_mix_router_kernel, n_prompt_tiles=n_p),
        grid=(n_p + 1,),
        in_specs=p_specs + s_specs + [
            const((1, RWKV_WIDTH)), const((1, RWKV_WIDTH)), const((D_MODEL, D_MODEL)), const((1, D_MODEL)),
            const((D_MODEL, ROUTER_LANES)), const((D_MODEL, ROUTER_LANES)), const((1, ROUTER_LANES)),
            const((RWKV_WIDTH, LANE)),
            const((tm, tm))],
        out_specs=[row(D_MODEL), row(D_MODEL), row(ROUTER_LANES), const((1, ROUTER_LANES))],
        out_shape=[jax.ShapeDtypeStruct((t, D_MODEL), F32), jax.ShapeDtypeStruct((t, D_MODEL), F32),
                   jax.ShapeDtypeStruct((t, ROUTER_LANES), F32), jax.ShapeDtypeStruct((1, ROUTER_LANES), F32)],
        scratch_shapes=[pltpu.VMEM((1, ROUTER_LANES), F32)],
        compiler_params=_cparams("arbitrary"),
        name="mix_router",
    )(*prompt_rows, *sample_rows, lnw, lnb, wo, gffn, *_split2(wr), br, _head_selector(), tril)


def _dispatch_kernel(slots_ref, x_ref, xs_hbm, sem):
    tm = x_ref.shape[0]

    for r in range(tm):
        src = x_ref.at[pl.ds(r, 1)]
        pltpu.make_async_copy(src, xs_hbm.at[pl.ds(slots_ref[0, 0, 2 * r], 1)], sem).start(priority=0)
        pltpu.make_async_copy(src, xs_hbm.at[pl.ds(slots_ref[0, 0, 2 * r + 1], 1)], sem).start(priority=1)
    for _ in range(2):
        pltpu.make_async_copy(x_ref, xs_hbm.at[pl.ds(0, tm)], sem).wait()


def _dispatch(n2, slots):
    t = n2.shape[0]
    tm = MOE_TM
    return pl.pallas_call(
        _dispatch_kernel,
        grid_spec=pltpu.PrefetchScalarGridSpec(
            num_scalar_prefetch=0,
            grid=(t // tm,),
            in_specs=[pl.BlockSpec((1, 1, 2 * tm), lambda i: (i, 0, 0), memory_space=pltpu.SMEM),
                      pl.BlockSpec((tm, D_MODEL), lambda i: (i, 0))],
            out_specs=pl.BlockSpec(memory_space=pl.ANY),
            scratch_shapes=[pltpu.SemaphoreType.DMA(())],
        ),
        out_shape=jax.ShapeDtypeStruct((2 * t, D_MODEL), F32),
        compiler_params=_cparams("arbitrary"),
        name="moe_dispatch",
    )(slots, n2)


def _experts_kernel(wt_ref, we_ref, wlo_ref, whi_ref, wfirst_ref, nw_ref,
                    xs_ref, w1_ref, w3_ref, w2_ref, ys_ref, wb1, wb3, wb2):
    w = pl.program_id(0)

    @pl.when(w < nw_ref[0])
    def _():
        new_expert = jnp.logical_or(w == 0, we_ref[w] != we_ref[jnp.maximum(w - 1, 0)])

        @pl.when(new_expert)
        def _():
            wb1[...] = w1_ref[0].astype(BF16)
            wb3[...] = w3_ref[0].astype(BF16)
            wb2[...] = w2_ref[0].astype(BF16)

        x = xs_ref[...].astype(BF16)
        a = _dot(x, wb1[...])
        b = _dot(x, wb3[...])
        o = _dot(((a * _sigmoid(a)) * b).astype(BF16), wb2[...])

        @pl.when(wfirst_ref[w] == 1)
        def _():
            ys_ref[...] = o

        @pl.when(wfirst_ref[w] == 0)
        def _():
            rows = lax.broadcasted_iota(jnp.int32, o.shape, 0)
            ys_ref[...] = jnp.where((rows >= wlo_ref[w]) & (rows < whi_ref[w]), o, ys_ref[...])


def _experts(xs, work, w1, w3, w2):
    s = xs.shape[0]
    ts = MOE_TS
    n_work = work[0].shape[0]
    return pl.pallas_call(
        _experts_kernel,
        grid_spec=pltpu.PrefetchScalarGridSpec(
            num_scalar_prefetch=6,
            grid=(n_work,),
            in_specs=[
                pl.BlockSpec((ts, D_MODEL), lambda w, wt, we, *_: (wt[w], 0)),
                pl.BlockSpec((1, D_MODEL, D_EXPERT), lambda w, wt, we, *_: (we[w], 0, 0)),
                pl.BlockSpec((1, D_MODEL, D_EXPERT), lambda w, wt, we, *_: (we[w], 0, 0)),
                pl.BlockSpec((1, D_EXPERT, D_MODEL), lambda w, wt, we, *_: (we[w], 0, 0)),
            ],
            out_specs=pl.BlockSpec((ts, D_MODEL), lambda w, wt, we, *_: (wt[w], 0)),
            scratch_shapes=[pltpu.VMEM((D_MODEL, D_EXPERT), BF16), pltpu.VMEM((D_MODEL, D_EXPERT), BF16),
                            pltpu.VMEM((D_EXPERT, D_MODEL), BF16)],
        ),
        out_shape=jax.ShapeDtypeStruct((s, D_MODEL), F32),
        compiler_params=_cparams("arbitrary"),
        name="moe_experts",
    )(*work, xs, w1, w3, w2)


def _expert_work_items(counts, total):
    ts = MOE_TS
    n_tiles = total // ts
    n_work = n_tiles + N_EXPERTS - 1
    offs = jnp.cumsum(counts) - counts
    t0 = (jnp.arange(n_tiles, dtype=jnp.int32) * ts)[:, None]
    lo = jnp.maximum(t0, offs[None, :])
    hi = jnp.minimum(t0 + ts, (offs + counts)[None, :])
    nonempty = (hi > lo).reshape(-1)
    nw = jnp.sum(nonempty.astype(jnp.int32))
    idx = jnp.nonzero(nonempty, size=n_work, fill_value=0)[0].astype(jnp.int32)
    idx = jnp.where(jnp.arange(n_work) < nw, idx, idx[jnp.maximum(nw - 1, 0)])
    wt = idx // N_EXPERTS
    we = idx % N_EXPERTS
    wlo = lo.reshape(-1)[idx] - wt * ts
    whi = hi.reshape(-1)[idx] - wt * ts
    wfirst = jnp.concatenate([jnp.ones((1,), jnp.int32), (wt[1:] != wt[:-1]).astype(jnp.int32)])
    return wt, we, wlo, whi, wfirst, nw.reshape(1)


def _combine_kernel(slots_ref, slots_next_ref, h_ref, route_ref, gfin_ref, ys_hbm, yp_ref, ysm_ref, gbuf, sems,
                    *, n_prompt_tiles):
    i = pl.program_id(0)
    n = pl.num_programs(0)
    tm = h_ref.shape[0]

    def gather(s_ref, buf):
        for r in range(tm):
            pltpu.make_async_copy(ys_hbm.at[pl.ds(s_ref[0, 0, 2 * r], 1)], gbuf.at[buf, 0, pl.ds(r, 1)],
                                  sems.at[buf]).start(priority=0)
            pltpu.make_async_copy(ys_hbm.at[pl.ds(s_ref[0, 0, 2 * r + 1], 1)], gbuf.at[buf, 1, pl.ds(r, 1)],
                                  sems.at[buf]).start(priority=1)

    cur = i % 2

    @pl.when(i == 0)
    def _():
        gather(slots_ref, 0)

    @pl.when(i + 1 < n)
    def _():
        gather(slots_next_ref, 1 - cur)

    for k in range(2):
        pltpu.make_async_copy(ys_hbm.at[pl.ds(0, tm)], gbuf.at[cur, k], sems.at[cur]).wait()
    route = route_ref[...]
    lane = lax.broadcasted_iota(jnp.int32, route.shape, 1)
    w1 = jnp.sum(jnp.where(lane == ROUTE_W1, route, 0.0), axis=1, keepdims=True)
    w2 = jnp.sum(jnp.where(lane == ROUTE_W2, route, 0.0), axis=1, keepdims=True)
    hf = h_ref[...] + (w1 * gbuf[cur, 0] + w2 * gbuf[cur, 1])
    y = hf * lax.rsqrt(jnp.mean(hf * hf, axis=-1, keepdims=True) + NORM_EPS) * gfin_ref[...]

    @pl.when(i < n_prompt_tiles)
    def _():
        yp_ref[...] = y

    @pl.when(i >= n_prompt_tiles)
    def _():
        ysm_ref[...] = y


def _combine(h, route, slots, ys, gfin, n_prompt_tiles):
    t = h.shape[0]
    tm = MOE_TM
    n_p = n_prompt_tiles
    return pl.pallas_call(
        functools.partial(_combine_kernel, n_prompt_tiles=n_p),
        grid_spec=pltpu.PrefetchScalarGridSpec(
            num_scalar_prefetch=0,
            grid=(t // tm,),
            in_specs=[pl.BlockSpec((1, 1, 2 * tm), lambda i: (i, 0, 0), memory_space=pltpu.SMEM),
                      pl.BlockSpec((1, 1, 2 * tm), lambda i: (jnp.minimum(i + 1, t // tm - 1), 0, 0),
                                   memory_space=pltpu.SMEM),
                      pl.BlockSpec((tm, D_MODEL), lambda i: (i, 0)),
                      pl.BlockSpec((tm, ROUTER_LANES), lambda i: (i, 0)),
                      pl.BlockSpec((1, D_MODEL), lambda i: (0, 0)),
                      pl.BlockSpec(memory_space=pl.ANY)],
            out_specs=[pl.BlockSpec((tm, D_MODEL), lambda i: (jnp.minimum(i, n_p - 1), 0)),
                       pl.BlockSpec((tm, D_MODEL), lambda i: (0, 0))],
            scratch_shapes=[pltpu.VMEM((2, 2, tm, D_MODEL), F32), pltpu.SemaphoreType.DMA((2,))],
        ),
        out_shape=[jax.ShapeDtypeStruct((n_p * tm, D_MODEL), F32), jax.ShapeDtypeStruct((tm, D_MODEL), F32)],
        compiler_params=_cparams("arbitrary"),
        name="moe_combine",
    )(slots, slots, h, route, gfin, ys)


def _v_tiles(v, tblk):
    b, t, _ = v.shape
    x = v.reshape(b, t // tblk, tblk, 2, RWKV_PAIRS, RWKV_HEAD).transpose(0, 1, 4, 5, 3, 2)
    x = jnp.pad(x, ((0, 0),) * 5 + ((0, RWKV_HEAD - tblk),))
    return x.reshape(b, t // tblk, RWKV_PAIRS, RWKV_HEAD, LANE).astype(BF16)


def _y_rows(y, bb, t):
    nb, tpad, _ = y.shape
    return y.reshape(nb, tpad, bb, RWKV_WIDTH).transpose(0, 2, 1, 3).reshape(nb * bb, tpad, RWKV_WIDTH)[:, :t]


def kernel(x_prompt, x_sample, state_gla, state_rwkv, state_shift, meta_tokens, norm_mix, w_in, gla_gate_w2,
           gla_gate_b, gla_norm, rwkv_mu, rwkv_w0, rwkv_w2, rwkv_a0, rwkv_a2, rwkv_g2, rwkv_kk, rwkv_ka, rwkv_rk,
           rwkv_ln_w, rwkv_ln_b, w_out, norm_ffn, router_group_w, router_group_b, router_expert_w,
           router_expert_b, moe_w1, moe_w3, moe_w2, norm_final):
    bp, tp, _ = x_prompt.shape
    bs, ts, _ = x_sample.shape
    assert state_gla.shape[0] == 1, "one layer"
    lyr = 0

    w_in_l = w_in[lyr]
    wg = jnp.pad(w_in_l[:, :GLA_COLS], ((0, 0), (0, GLA_PCOLS - GLA_COLS))).astype(BF16)
    wr = w_in_l[:, GLA_COLS:].astype(BF16)
    g_mix = norm_mix[lyr][None, :]
    gw2p = jnp.pad(gla_gate_w2[lyr], ((0, LANE - GLA_GATE_RANK), (0, 0)))
    gb = gla_gate_b[lyr][None, :]
    gn = gla_norm[lyr][None, :]
    w2p = jnp.pad(rwkv_w2[lyr], ((0, 64), (0, 0))).astype(BF16)
    a2p = jnp.pad(rwkv_a2[lyr], ((64, 0), (0, 0))).astype(BF16)
    pre_params = (rwkv_mu[lyr][None, :], rwkv_w0[lyr][None, :], w2p, rwkv_a0[lyr][None, :], a2p,
                  rwkv_g2[lyr].astype(BF16), rwkv_kk[lyr][None, :], rwkv_ka[lyr][None, :],
                  rwkv_rk[lyr].reshape(1, RWKV_WIDTH), _head_selector())
    lnw = rwkv_ln_w[lyr][None, :]
    lnb = rwkv_ln_b[lyr][None, :]
    wo = w_out[lyr].astype(BF16)
    gffn = norm_ffn[lyr][None, :]
    n_used = N_GROUPS + N_EXPERTS
    w_router = jnp.pad(
        jnp.concatenate([router_group_w[lyr],
                         router_expert_w[lyr].transpose(1, 0, 2).reshape(D_MODEL, N_EXPERTS)], axis=1),
        ((0, 0), (0, ROUTER_LANES - n_used)))
    b_router = jnp.pad(jnp.concatenate([router_group_b[lyr], router_expert_b[lyr].reshape(N_EXPERTS)]),
                       (0, ROUTER_LANES - n_used))[None, :]
    gfin = norm_final[None, :]

    pg_m, pr_m = _inproj(meta_tokens, g_mix, wg, wr, N_META)
    _, sg_m = _gla(pg_m[None], jnp.zeros((1, GLA_HEADS, GLA_DK, GLA_DV), F32), gw2p, gb, gn,
                   bb=1, chunk=N_META, sub=N_META, t_valid=N_META)
    r, w, k, kk, kka, v_m, _, _ = _rwkv_pre(pr_m[None], jnp.zeros((1, 1, RWKV_COLS), F32), pre_params,
                                            tm=N_META, explicit_prev=False, emit_vt=False)
    _, sr_m = _rwkv_rec(r, w, k, kk, kka, _v_tiles(v_m, N_META),
                        jnp.zeros((1, RWKV_HEADS, RWKV_HEAD, RWKV_HEAD), F32), bb=1, n_steps=N_META)

    xp = x_prompt.reshape(bp * tp, D_MODEL)
    pg_p, pr_p = _inproj(xp, g_mix, wg, wr, INPROJ_TM)
    og_p, sg_p = _gla(pg_p.reshape(bp, tp, GLA_PCOLS), jnp.broadcast_to(sg_m, (bp,) + sg_m.shape[1:]), gw2p, gb, gn,
                      bb=SEQ_BLOCK, chunk=GLA_CHUNK, sub=GLA_SUB, t_valid=GLA_CHUNK)
    pr_p3 = pr_p.reshape(bp, tp, RWKV_COLS)
    first_prev = jnp.broadcast_to(pr_m[N_META - 1][None, None, :], (bp, 1, RWKV_COLS))
    r, w, k, kk, kka, vt_p, bv_p, gate_p = _rwkv_pre(pr_p3, first_prev, pre_params, tm=PRE_TM, explicit_prev=False,
                                                     emit_vt=True)
    y_p, sr_p = _rwkv_rec(r, w, k, kk, kka, vt_p, jnp.broadcast_to(sr_m, (bp,) + sr_m.shape[1:]),
                          bb=bp, n_steps=REC_TB)
    prompt_rows = (xp, og_p.reshape(bp * tp, GLA_WIDTH), y_p.reshape(tp, bp * RWKV_WIDTH),
                   bv_p.reshape(bp * tp, RWKV_WIDTH), gate_p.reshape(bp * tp, RWKV_WIDTH))

    xs = x_sample.reshape(bs * ts, D_MODEL)
    pg_s, pr_s = _inproj(xs, g_mix, wg, wr, bs * ts)
    ts_pad = 8
    pg_s3 = jnp.pad(pg_s.reshape(bs, ts, GLA_PCOLS), ((0, 0), (0, ts_pad - ts), (0, 0)))
    og_s, sg_s = _gla(pg_s3, state_gla[lyr], gw2p, gb, gn, bb=SEQ_BLOCK, chunk=ts_pad, sub=ts_pad, t_valid=ts)
    og_s = og_s[:, :ts]
    pr_s3 = pr_s.reshape(bs, ts, RWKV_COLS)
    prev_s = jnp.concatenate([state_shift[lyr][:, None, :], pr_s3[:, :-1]], axis=1)
    r, w, k, kk, kka, v_s, bv_s, gate_s = _rwkv_pre(pr_s3.reshape(1, bs * ts, RWKV_COLS),
                                                     prev_s.reshape(1, bs * ts, RWKV_COLS), pre_params,
                                                     tm=bs * ts, explicit_prev=True, emit_vt=False)
    unflat = lambda a: a.reshape(RWKV_PAIRS, bs, ts, LANE).transpose(1, 0, 2, 3)
    y_s, sr_s = _rwkv_rec(unflat(r), unflat(w), unflat(k), unflat(kk), unflat(kka),
                          _v_tiles(v_s.reshape(bs, ts, RWKV_WIDTH), ts), state_rwkv[lyr], bb=SEQ_BLOCK, n_steps=ts)
    y_s = _y_rows(y_s, 8, ts)
    sample_rows = (xs, og_s.reshape(bs * ts, GLA_WIDTH), y_s.reshape(bs * ts, RWKV_WIDTH),
                   bv_s.reshape(bs * ts, RWKV_WIDTH), gate_s.reshape(bs * ts, RWKV_WIDTH))

    h_all, n2_all, route, counts = _mix_router(prompt_rows, sample_rows, lnw, lnb, wo, gffn, w_router, b_router,
                                               seq_tiles=tp // MOE_TM)
    n_tok = h_all.shape[0]
    n_p_tiles = (bp * tp) // MOE_TM
    cnt = counts[0, EXPERT_LANE0:EXPERT_LANE0 + N_EXPERTS].astype(jnp.int32)
    offs = jnp.cumsum(cnt) - cnt
    eid = route[:, ROUTE_E1:ROUTE_E2 + 1].astype(jnp.int32)
    pos = route[:, ROUTE_P1:ROUTE_P2 + 1].astype(jnp.int32)
    onehot = (eid[..., None] == jnp.arange(N_EXPERTS, dtype=jnp.int32)).astype(F32)
    off = jnp.einsum("tke,e->tk", onehot, offs.astype(F32), precision=HIGHEST).astype(jnp.int32)
    slots = (off + pos).reshape(n_tok // MOE_TM, 1, 2 * MOE_TM)
    xs_sorted = _dispatch(n2_all, slots)
    ys_sorted = _experts(xs_sorted, _expert_work_items(cnt, 2 * n_tok), moe_w1[lyr], moe_w3[lyr], moe_w2[lyr])
    y_prompt, y_sample = _combine(h_all, route, slots, ys_sorted, gfin, n_p_tiles)
    y_prompt = y_prompt.reshape(bp, tp, D_MODEL)
    y_sample = y_sample.reshape(bs, ts, D_MODEL)

    return (y_prompt, y_sample,
            sg_p[None], sr_p[None], pr_p3[:, -1][None],
            sg_s[None], sr_s[None], pr_s3[:, -1][None])
```

```python
import functools

import jax
import jax.numpy as jnp
from jax import lax
from jax.experimental import pallas as pl
from jax.experimental.pallas import tpu as pltpu

F32 = jnp.float32
BF16 = jnp.bfloat16
HIGHEST = lax.Precision.HIGHEST

D_MODEL = 1024
N_META = 16
NORM_EPS = 1e-6
LOG2E = 1.4426950408889634
GLA_HEADS = 4
GLA_DK = 64
GLA_DV = 128
GLA_QK = GLA_HEADS * GLA_DK
GLA_WIDTH = GLA_HEADS * GLA_DV
GLA_GATE_RANK = 16
GLA_GATE_NORM = 16.0
GLA_CHUNK = 64
GLA_SUB = 8
GLA_COLS = 2 * GLA_QK + 2 * GLA_WIDTH + GLA_GATE_RANK
GLA_PCOLS = 2 * GLA_QK + 2 * GLA_WIDTH + 128
RWKV_WIDTH = 512
RWKV_HEAD = 64
RWKV_HEADS = 8
RWKV_PAIRS = RWKV_HEADS // 2
RWKV_DECAY_SCALE = 0.606531
RWKV_GN_EPS = 64e-5
RWKV_COLS = 3 * RWKV_WIDTH + 64 + 64 + 128
REC_TB = 64
REC_UNROLL = 8
INPROJ_TM = 512
PRE_TM = 256
SEQ_BLOCK = 8
N_GROUPS = 4
EXPERTS_PER_GROUP = 8
N_EXPERTS = 32
D_EXPERT = 512
ROUTER_LANES = 128
EXPERT_LANE0 = N_GROUPS
ROUTE_E1, ROUTE_E2, ROUTE_W1, ROUTE_W2, ROUTE_P1, ROUTE_P2 = range(6)
MOE_TM = 512
MOE_TS = 512

LANE = 128
VMEM_LIMIT = 56 * 1024 * 1024


def _cparams(*sem):
    return pltpu.CompilerParams(dimension_semantics=sem, vmem_limit_bytes=VMEM_LIMIT)


def _block_ones(n, blk):
    i = jnp.arange(n)
    return (i[:, None] // blk == i[None, :] // blk).astype(BF16)


def _sigmoid(x):
    return 1.0 / (1.0 + jnp.exp(-x))


def _dot(a, b):
    return jnp.dot(a, b, preferred_element_type=F32)


def _dot_nt(a, b):
    return lax.dot_general(a, b, (((1,), (1,)), ((), ())), preferred_element_type=F32)


def _dot_tn(a, b):
    return lax.dot_general(a, b, (((0,), (0,)), ((), ())), preferred_element_type=F32)


def _split2(x):
    hi = x.astype(BF16)
    lo = (x - hi.astype(F32)).astype(BF16)
    return hi, lo


def _head_selector():
    return (jnp.arange(RWKV_WIDTH)[:, None] // RWKV_HEAD == jnp.arange(LANE)[None, :]).astype(BF16)


def _group_sum(x, sel):
    hi, lo = _split2(x)
    s_hi, s_lo = _split2(_dot(hi, sel) + _dot(lo, sel))
    return _dot_nt(s_hi, sel) + _dot_nt(s_lo, sel)


def _inproj_kernel(x_ref, g_ref, wg_ref, wr_ref, pg_ref, pr_ref):
    x = x_ref[...]
    n = x * lax.rsqrt(jnp.mean(x * x, axis=-1, keepdims=True) + NORM_EPS) * g_ref[...]
    nb = n.astype(BF16)
    pg_ref[...] = _dot(nb, wg_ref[...])
    pr_ref[...] = _dot(nb, wr_ref[...])


def _inproj(x, g, wg, wr, tm):
    t = x.shape[0]
    return pl.pallas_call(
        _inproj_kernel,
        grid=(t // tm,),
        in_specs=[
            pl.BlockSpec((tm, D_MODEL), lambda i: (i, 0)),
            pl.BlockSpec((1, D_MODEL), lambda i: (0, 0)),
            pl.BlockSpec((D_MODEL, GLA_PCOLS), lambda i: (0, 0)),
            pl.BlockSpec((D_MODEL, RWKV_COLS), lambda i: (0, 0)),
        ],
        out_specs=[
            pl.BlockSpec((tm, GLA_PCOLS), lambda i: (i, 0)),
            pl.BlockSpec((tm, RWKV_COLS), lambda i: (i, 0)),
        ],
        out_shape=[jax.ShapeDtypeStruct((t, GLA_PCOLS), F32), jax.ShapeDtypeStruct((t, RWKV_COLS), F32)],
        compiler_params=_cparams("parallel"),
        name="inproj",
    )(x, g, wg, wr)


def _gla_kernel(pg_ref, s0_ref, gw2_ref, gb_ref, gn_ref, bo_ref, tril_ref, o_ref, sout_ref, s_scr,
                *, bb, chunk, sub, t_valid):
    ci = pl.program_id(1)

    @pl.when(ci == 0)
    def _():
        s_scr[...] = s0_ref[...]

    bo = bo_ref[...]
    tril = tril_ref[...]
    lane = lax.broadcasted_iota(jnp.int32, (sub, LANE), 1) & (GLA_DK - 1)
    rowi = lax.broadcasted_iota(jnp.int32, (sub, LANE), 0)
    head0_s = lax.broadcasted_iota(jnp.int32, (sub, LANE), 1) < GLA_DK
    head0_c = lax.broadcasted_iota(jnp.int32, (chunk, LANE), 1) < GLA_DK

    for bi in range(bb):
        pg = pg_ref[bi]
        q = pg[:, 0:GLA_QK] * (GLA_DK ** -0.5)
        k = pg[:, GLA_QK:2 * GLA_QK]
        v = pg[:, 2 * GLA_QK:2 * GLA_QK + GLA_WIDTH]
        g = pg[:, 2 * GLA_QK + GLA_WIDTH:2 * GLA_QK + 2 * GLA_WIDTH]
        gl = pg[:, 2 * GLA_QK + 2 * GLA_WIDTH:]
        z = jnp.dot(gl, gw2_ref[...], precision=HIGHEST, preferred_element_type=F32) + gb_ref[...]
        lg = (jnp.minimum(z, 0.0) - jnp.log1p(jnp.exp(-jnp.abs(z)))) * (LOG2E / GLA_GATE_NORM)
        if t_valid < chunk:
            rows = lax.broadcasted_iota(jnp.int32, lg.shape, 0)
            lg = jnp.where(rows < t_valid, lg, 0.0)
        b = jnp.dot(tril, lg, precision=HIGHEST, preferred_element_type=F32)
        eb = jnp.exp2(b)
        blast = b[chunk - 1:chunk, :]
        kl = k * jnp.exp2(blast - b)
        qe = q * eb

        n_blk = chunk // sub
        n_pairs = GLA_HEADS // 2
        ps = []
        for hp in range(n_pairs):
            sl = slice(hp * LANE, (hp + 1) * LANE)
            for blk in range(n_blk):
                rs = slice(blk * sub, (blk + 1) * sub)
                qb, kb, bbk = q[rs, sl], k[rs, sl], b[rs, sl]
                for j in range(sub):
                    ps.append(qb * (kb[j:j + 1] * jnp.exp2(jnp.minimum(bbk - bbk[j:j + 1], 0.0))))
        red = _dot(jnp.concatenate(ps, axis=0).astype(BF16), bo)

        o_heads = []
        for hp in range(n_pairs):
            sl = slice(hp * LANE, (hp + 1) * LANE)
            kp, bp = k[:, sl], b[:, sl]
            row_blocks = []
            for blk in range(n_blk):
                rs = slice(blk * sub, (blk + 1) * sub)
                base = (hp * n_blk + blk) * sub * sub
                a = jnp.zeros((sub, LANE), F32)
                for j in range(sub):
                    a = jnp.where((lane == blk * sub + j) & (rowi >= j), red[base + j * sub:base + (j + 1) * sub], a)
                if blk > 0:
                    bref = bp[blk * sub - 1:blk * sub]
                    qt = q[rs, sl] * jnp.exp2(bp[rs] - bref)
                    kt = (kp * jnp.exp2(jnp.minimum(bref - bp, 0.0))).astype(BF16)
                    qt2 = jnp.concatenate([jnp.where(head0_s, qt, 0.0), jnp.where(head0_s, 0.0, qt)], axis=0)
                    off2 = _dot_nt(qt2.astype(BF16), kt)
                    a = jnp.where(lane < blk * sub, jnp.concatenate([off2[:sub], off2[sub:]], axis=1), a)
                row_blocks.append(a)
            a_pair = row_blocks[0] if n_blk == 1 else jnp.concatenate(row_blocks, axis=0)
            v0 = v[:, 2 * hp * GLA_DV:(2 * hp + 1) * GLA_DV]
            v1 = v[:, (2 * hp + 1) * GLA_DV:(2 * hp + 2) * GLA_DV]
            s_pair = s_scr[bi, 2 * hp:2 * hp + 2].reshape(2 * GLA_DK, GLA_DV)
            qe_p, kl_p = qe[:, sl], kl[:, sl]

            def by_head(x):
                return jnp.concatenate([jnp.where(head0_c, x, 0.0), jnp.where(head0_c, 0.0, x)], axis=0)

            if chunk == GLA_DK:
                v_rows = jnp.concatenate([v0, v1], axis=0)
            else:
                zpad = jnp.zeros((GLA_DK - chunk, GLA_DV), F32)
                v_rows = jnp.concatenate([v0, zpad, v1, zpad], axis=0)
            lhs = jnp.concatenate([by_head(a_pair), by_head(qe_p)], axis=1).astype(BF16)
            rhs = jnp.concatenate([v_rows, s_pair], axis=0).astype(BF16)
            o2 = _dot(lhs, rhs)
            upd = _dot_tn(by_head(kl_p).astype(BF16), jnp.concatenate([v0, v1], axis=0).astype(BF16))
            dcol = jnp.broadcast_to(jnp.exp2(blast[:, sl]), (8, LANE)).T[:, 0:1]
            s_new = dcol * s_pair + upd
            s_scr[bi, 2 * hp] = s_new[:GLA_DK]
            s_scr[bi, 2 * hp + 1] = s_new[GLA_DK:]
            for h2 in range(2):
                o_h = o2[h2 * chunk:(h2 + 1) * chunk]
                o_heads.append(o_h * lax.rsqrt(jnp.mean(o_h * o_h, axis=-1, keepdims=True) + NORM_EPS) * gn_ref[...])
        o = jnp.concatenate(o_heads, axis=1)
        o_ref[bi] = o * (g * _sigmoid(g))

    @pl.when(ci == pl.num_programs(1) - 1)
    def _():
        sout_ref[...] = s_scr[...]


def _gla(pg, s0, gw2p, gb, gn, *, bb, chunk, sub, t_valid):
    b, t, _ = pg.shape
    tril = jnp.tril(jnp.ones((chunk, chunk), F32))
    kern = functools.partial(_gla_kernel, bb=bb, chunk=chunk, sub=sub, t_valid=t_valid)
    return pl.pallas_call(
        kern,
        grid=(b // bb, t // chunk),
        in_specs=[
            pl.BlockSpec((bb, chunk, GLA_PCOLS), lambda i, j: (i, j, 0)),
            pl.BlockSpec((bb, GLA_HEADS, GLA_DK, GLA_DV), lambda i, j: (i, 0, 0, 0)),
            pl.BlockSpec((LANE, GLA_QK), lambda i, j: (0, 0)),
            pl.BlockSpec((1, GLA_QK), lambda i, j: (0, 0)),
            pl.BlockSpec((1, GLA_DV), lambda i, j: (0, 0)),
            pl.BlockSpec((LANE, LANE), lambda i, j: (0, 0)),
            pl.BlockSpec((chunk, chunk), lambda i, j: (0, 0)),
        ],
        out_specs=[
            pl.BlockSpec((bb, chunk, GLA_WIDTH), lambda i, j: (i, j, 0)),
            pl.BlockSpec((bb, GLA_HEADS, GLA_DK, GLA_DV), lambda i, j: (i, 0, 0, 0)),
        ],
        out_shape=[jax.ShapeDtypeStruct((b, t, GLA_WIDTH), F32),
                   jax.ShapeDtypeStruct((b, GLA_HEADS, GLA_DK, GLA_DV), F32)],
        scratch_shapes=[pltpu.VMEM((bb, GLA_HEADS, GLA_DK, GLA_DV), F32)],
        compiler_params=_cparams("parallel", "arbitrary"),
        name="gla_chunk",
    )(pg, s0, gw2p, gb, gn, _block_ones(LANE, GLA_DK), tril)


def _rwkv_pre_kernel(pr_ref, aux_ref, mu_ref, w0_ref, w2_ref, a0_ref, a2_ref, g2_ref, kk_ref, ka_ref, rk_ref, bo_ref,
                     r_out, w_out, k_out, kkn_out, kka_out, v_out, bv_out, gate_out, carry_scr,
                     *, tm, explicit_prev, emit_vt, pair_out):
    pr = pr_ref[0]
    if explicit_prev:
        prev = aux_ref[0]
    else:
        j = pl.program_id(1)
        row0 = jnp.where(j == 0, aux_ref[0], carry_scr[...])
        rows = lax.broadcasted_iota(jnp.int32, pr.shape, 0)
        prev = jnp.where(rows == 0, row0, pltpu.roll(pr, 1, 0))
        carry_scr[...] = pr[tm - 1:tm, :]
    xm = pr + (prev - pr) * mu_ref[...]
    wd = RWKV_WIDTH
    rr, rk, rv = xm[:, 0:wd], xm[:, wd:2 * wd], xm[:, 2 * wd:3 * wd]
    wa = xm[:, 3 * wd:3 * wd + LANE]
    gl2 = xm[:, 3 * wd + LANE:3 * wd + 2 * LANE]
    logw = -RWKV_DECAY_SCALE * _sigmoid(w0_ref[...] + _dot(jnp.tanh(wa).astype(BF16), w2_ref[...]))
    aa = _sigmoid(a0_ref[...] + _dot(wa.astype(BF16), a2_ref[...]))
    gate = _dot(_sigmoid(gl2).astype(BF16), g2_ref[...])
    bo = bo_ref[...]
    kk = rk * kk_ref[...]
    kk = kk / jnp.maximum(jnp.sqrt(_group_sum(kk * kk, bo)), 1e-12)
    k = rk * (1.0 + (aa - 1.0) * ka_ref[...])
    bv = _group_sum(rr * k * rk_ref[...], bo) * rv
    w = jnp.exp(logw)
    kka = kk * aa
    hd = RWKV_HEAD

    def pair(x, hp):
        return jnp.concatenate([x[:, hp * hd:(hp + 1) * hd], x[:, (hp + RWKV_PAIRS) * hd:(hp + RWKV_PAIRS + 1) * hd]],
                               axis=1)

    if pair_out:
        for hp in range(RWKV_PAIRS):
            r_out[0, hp] = pair(rr, hp)
            w_out[0, hp] = pair(w, hp)
            k_out[0, hp] = pair(k, hp)
            kkn_out[0, hp] = pair(kk, hp)
            kka_out[0, hp] = pair(kka, hp)
    else:
        r_out[0], w_out[0], k_out[0], kkn_out[0], kka_out[0] = rr, w, k, kk, kka
    if emit_vt:
        vt = rv.T
        for tb in range(tm // REC_TB):
            ts = slice(tb * REC_TB, (tb + 1) * REC_TB)
            for hp in range(RWKV_PAIRS):
                lo, hi = hp * hd, (hp + RWKV_PAIRS) * hd
                v_out[0, tb, hp] = jnp.concatenate([vt[lo:lo + hd, ts], vt[hi:hi + hd, ts]], axis=1).astype(BF16)
    else:
        v_out[0] = rv
    bv_out[0] = bv
    gate_out[0] = gate


def _rwkv_pre(pr, aux, params, *, tm, explicit_prev, emit_vt, pair_out=True):
    b, t, _ = pr.shape
    kern = functools.partial(_rwkv_pre_kernel, tm=tm, explicit_prev=explicit_prev, emit_vt=emit_vt,
                             pair_out=pair_out)
    aux_spec = (pl.BlockSpec((1, tm, RWKV_COLS), lambda i, j: (i, j, 0)) if explicit_prev
                else pl.BlockSpec((1, 1, RWKV_COLS), lambda i, j: (i, 0, 0)))
    const = lambda shape: pl.BlockSpec(shape, lambda i, j: (0,) * len(shape))
    pair_spec = pl.BlockSpec((1, RWKV_PAIRS, tm, LANE), lambda i, j: (i, 0, j, 0))
    row_spec = pl.BlockSpec((1, tm, RWKV_WIDTH), lambda i, j: (i, j, 0))
    pair_shape = jax.ShapeDtypeStruct((b, RWKV_PAIRS, t, LANE), F32)
    row_shape = jax.ShapeDtypeStruct((b, t, RWKV_WIDTH), F32)
    if emit_vt:
        v_spec = pl.BlockSpec((1, tm // REC_TB, RWKV_PAIRS, RWKV_HEAD, LANE), lambda i, j: (i, j, 0, 0, 0))
        v_shape = jax.ShapeDtypeStruct((b, t // REC_TB, RWKV_PAIRS, RWKV_HEAD, LANE), BF16)
    else:
        v_spec, v_shape = row_spec, row_shape
    if not pair_out:
        pair_spec, pair_shape = row_spec, row_shape
    return pl.pallas_call(
        kern,
        grid=(b, t // tm),
        in_specs=[
            pl.BlockSpec((1, tm, RWKV_COLS), lambda i, j: (i, j, 0)),
            aux_spec,
            const((1, RWKV_COLS)), const((1, RWKV_WIDTH)), const((LANE, RWKV_WIDTH)), const((1, RWKV_WIDTH)),
            const((LANE, RWKV_WIDTH)), const((LANE, RWKV_WIDTH)), const((1, RWKV_WIDTH)), const((1, RWKV_WIDTH)),
            const((1, RWKV_WIDTH)), const((RWKV_WIDTH, LANE)),
        ],
        out_specs=[pair_spec] * 5 + [v_spec, row_spec, row_spec],
        out_shape=[pair_shape] * 5 + [v_shape, row_shape, row_shape],
        scratch_shapes=[pltpu.VMEM((1, RWKV_COLS), F32)],
        compiler_params=_cparams("parallel", "arbitrary"),
        name="rwkv_pre",
    )(pr, aux, *params)


def _rwkv_rec_kernel(r_ref, w_ref, k_ref, kk_ref, kka_ref, vt_ref, s0_ref, bo_ref, vsel_ref, ysel_ref,
                     y_ref, sout_ref, s_scr, t1_scr, t3_scr, yt_scr, *, bb, n_steps):
    tb = pl.program_id(1)
    nc = bb * RWKV_PAIRS
    hd = RWKV_HEAD

    @pl.when(tb == 0)
    def _():
        for c in range(nc):
            bi, hp = divmod(c, RWKV_PAIRS)
            s_scr[c] = jnp.concatenate([s0_ref[bi, hp], s0_ref[bi, hp + RWKV_PAIRS]], axis=1)

    bo = bo_ref[...]

    def step(t, u):
        row = pl.ds(t, 1)
        for c in range(nc):
            bi, hp = divmod(c, RWKV_PAIRS)
            t1_scr[c * hd:(c + 1) * hd, :] = (s_scr[c] * kk_ref[bi, hp, row, :]).astype(BF16)
        sab = _dot(t1_scr[...], bo)
        vb = _dot(vt_ref[...].reshape(nc * hd, LANE), vsel_ref[t])
        for c in range(nc):
            bi, hp = divmod(c, RWKV_PAIRS)
            rs = slice(c * hd, (c + 1) * hd)
            s2 = (s_scr[c] * w_ref[bi, hp, row, :] - sab[rs] * kka_ref[bi, hp, row, :]
                  + vb[rs] * k_ref[bi, hp, row, :])
            s_scr[c] = s2
            t3_scr[rs, :] = (s2 * r_ref[bi, hp, row, :]).astype(BF16)
        yt_scr[...] += _dot_nt(ysel_ref[u], t3_scr[...])

    n_inner = min(8, n_steps)

    def block8(t8, carry):
        yt_scr[...] = jnp.zeros(yt_scr.shape, F32)

        def inner(u, c2):
            step(t8 * 8 + u, u)
            return c2

        lax.fori_loop(0, n_inner, inner, 0, unroll=REC_UNROLL)
        t0 = pl.multiple_of(t8 * 8, 8)
        blk = RWKV_PAIRS * hd
        for bi in range(bb):
            for h2 in range(2):
                y_ref[0, pl.ds(t0, 8), bi * RWKV_WIDTH + h2 * blk:bi * RWKV_WIDTH + (h2 + 1) * blk] = (
                    yt_scr[h2 * 8:(h2 + 1) * 8, bi * blk:(bi + 1) * blk])
        return carry

    lax.fori_loop(0, (n_steps + 7) // 8, block8, 0)

    @pl.when(tb == pl.num_programs(1) - 1)
    def _():
        for c in range(nc):
            bi, hp = divmod(c, RWKV_PAIRS)
            s_c = s_scr[c]
            sout_ref[bi, hp] = s_c[:, :RWKV_HEAD]
            sout_ref[bi, hp + RWKV_PAIRS] = s_c[:, RWKV_HEAD:]


def _rwkv_rec(r, w, k, kk, kka, vt, s0, *, bb, n_steps):
    b, _, t, _ = r.shape
    tblk = min(REC_TB, t)
    ntb = t // tblk
    nc = bb * RWKV_PAIRS
    lane = jnp.arange(LANE)
    vsel = ((lane[None, :, None] // RWKV_HEAD == lane[None, None, :] // RWKV_HEAD)
            & (lane[None, :, None] % RWKV_HEAD == jnp.arange(RWKV_HEAD)[:, None, None])).astype(BF16)
    ysel = (jnp.arange(16)[None, :, None]
            == 8 * (lane[None, None, :] // RWKV_HEAD) + jnp.arange(8)[:, None, None]).astype(BF16)
    kern = functools.partial(_rwkv_rec_kernel, bb=bb, n_steps=n_steps)
    pair_spec = pl.BlockSpec((bb, RWKV_PAIRS, tblk, LANE), lambda i, j: (i, 0, j, 0))
    state_spec = pl.BlockSpec((bb, RWKV_HEADS, RWKV_HEAD, RWKV_HEAD), lambda i, j: (i, 0, 0, 0))
    ytb = max(tblk, 8)
    return pl.pallas_call(
        kern,
        grid=(b // bb, ntb),
        in_specs=[pair_spec] * 5 + [
            pl.BlockSpec((bb, 1, RWKV_PAIRS, RWKV_HEAD, LANE), lambda i, j: (i, j, 0, 0, 0)),
            state_spec,
            pl.BlockSpec((LANE, LANE), lambda i, j: (0, 0)),
            pl.BlockSpec((RWKV_HEAD, LANE, LANE), lambda i, j: (0, 0, 0)),
            pl.BlockSpec((8, 16, LANE), lambda i, j: (0, 0, 0)),
        ],
        out_specs=[
            pl.BlockSpec((1, ytb, bb * RWKV_WIDTH), lambda i, j: (i, j, 0)),
            state_spec,
        ],
        out_shape=[jax.ShapeDtypeStruct((b // bb, ntb * ytb, bb * RWKV_WIDTH), F32),
                   jax.ShapeDtypeStruct((b, RWKV_HEADS, RWKV_HEAD, RWKV_HEAD), F32)],
        scratch_shapes=[pltpu.VMEM((nc, RWKV_HEAD, LANE), F32),
                        pltpu.VMEM((nc * RWKV_HEAD, LANE), BF16),
                        pltpu.VMEM((nc * RWKV_HEAD, LANE), BF16),
                        pltpu.VMEM((16, nc * RWKV_HEAD), F32)],
        compiler_params=_cparams("parallel", "arbitrary"),
        name="rwkv_rec",
    )(r, w, k, kk, kka, vt, s0, _block_ones(LANE, RWKV_HEAD), vsel, ysel)


def _rwkv_lanes_kernel(r_ref, w_ref, k_ref, kk_ref, kka_ref, v_ref, s0_ref, y_ref, sout_ref, vt_scr, yt_scr,
                       *, n_steps, n_seq):
    hd = RWKV_HEAD
    for t in range(n_steps):
        rows = slice(t * n_seq, (t + 1) * n_seq)
        r_t, w_t, k_t = r_ref[rows, :].T, w_ref[rows, :].T, k_ref[rows, :].T
        kk_t, kka_t = kk_ref[rows, :].T, kka_ref[rows, :].T
        vt_scr[...] = v_ref[rows, :].T
        src = s0_ref if t == 0 else sout_ref
        for h2 in range(2):
            hs = slice(h2 * hd, (h2 + 1) * hd)
            r_h, w_h, k_h, kk_h, kka_h = r_t[hs], w_t[hs], k_t[hs], kk_t[hs], kka_t[hs]

            def value_row(i, carry):
                s = src[h2, i]
                sab = jnp.sum(s * kk_h, axis=0, keepdims=True)
                v_i = vt_scr[pl.ds(h2 * hd + i, 1), :]
                s2 = s * w_h - sab * kka_h + v_i * k_h
                sout_ref[h2, i] = s2
                yt_scr[pl.ds(h2 * hd + i, 1), :] = jnp.sum(s2 * r_h, axis=0, keepdims=True)
                return carry

            lax.fori_loop(0, hd, value_row, 0, unroll=8)
        y_ref[rows, :] = yt_scr[...].T


def _rwkv_lanes(r, w, k, kk, kka, v, s0, *, n_steps):
    n_tok = r.shape[0]
    n_seq = n_tok // n_steps
    assert n_seq == LANE, "one lane per sequence"
    row_spec = pl.BlockSpec((n_tok, LANE), lambda hp: (0, hp))
    state_spec = pl.BlockSpec((2, RWKV_HEAD, RWKV_HEAD, n_seq), lambda hp: (hp, 0, 0, 0))
    return pl.pallas_call(
        functools.partial(_rwkv_lanes_kernel, n_steps=n_steps, n_seq=n_seq),
        grid=(RWKV_HEADS // 2,),
        in_specs=[row_spec] * 6 + [state_spec],
        out_specs=[row_spec, state_spec],
        out_shape=[jax.ShapeDtypeStruct((n_tok, RWKV_WIDTH), F32),
                   jax.ShapeDtypeStruct((RWKV_HEADS, RWKV_HEAD, RWKV_HEAD, n_seq), F32)],
        scratch_shapes=[pltpu.VMEM((LANE, n_seq), F32), pltpu.VMEM((LANE, n_seq), F32)],
        compiler_params=_cparams("parallel"),
        name="rwkv_lanes",
    )(r, w, k, kk, kka, v, s0)


def _mix_router_body(x_ref, og_ref, y_ref, bv_ref, gate_ref, lnw_ref, lnb_ref, wo_ref, gffn_ref, wr_hi_ref, wr_lo_ref,
                     br_ref, bo_ref, tril_ref, h_ref, n2_ref, route_ref, cnt_scr):
    bo = bo_ref[...]
    y = y_ref[...]
    inv_n = 1.0 / RWKV_HEAD
    d = y - _group_sum(y, bo) * inv_n
    var = _group_sum(d * d, bo) * inv_n
    yn = d * lax.rsqrt(var + RWKV_GN_EPS) * lnw_ref[...] + lnb_ref[...] + bv_ref[...]
    o_rwkv = yn * gate_ref[...]
    mix = (_dot(og_ref[...].astype(BF16), wo_ref[0:GLA_WIDTH, :])
           + _dot(o_rwkv.astype(BF16), wo_ref[GLA_WIDTH:, :]))
    h = x_ref[...] + mix
    h_ref[...] = h
    n2 = h * lax.rsqrt(jnp.mean(h * h, axis=-1, keepdims=True) + NORM_EPS) * gffn_ref[...]
    n2_ref[...] = n2
    n2_hi, n2_lo = _split2(n2)
    lg = (_dot(n2_hi, wr_hi_ref[...]) + _dot(n2_hi, wr_lo_ref[...]) + _dot(n2_lo, wr_hi_ref[...])) + br_ref[...]
    neg = jnp.float32(-3.0e38)
    big = jnp.float32(1.0e9)
    lane = lax.broadcasted_iota(jnp.int32, lg.shape, 1).astype(F32)
    gmask = lane < N_GROUPS
    gmax = jnp.max(jnp.where(gmask, lg, neg), axis=1, keepdims=True)
    p_top = 1.0 / jnp.sum(jnp.where(gmask, jnp.exp(jnp.minimum(lg - gmax, 0.0)), 0.0), axis=1, keepdims=True)
    gidx = jnp.min(jnp.where(gmask & (lg == gmax), lane, big), axis=1, keepdims=True)
    e_lo = EXPERT_LANE0 + gidx * EXPERTS_PER_GROUP
    emask = (lane >= e_lo) & (lane < e_lo + EXPERTS_PER_GROUP)
    m1 = jnp.max(jnp.where(emask, lg, neg), axis=1, keepdims=True)
    e1 = jnp.min(jnp.where(emask & (lg == m1), lane, big), axis=1, keepdims=True)
    emask2 = emask & (lane != e1)
    m2 = jnp.max(jnp.where(emask2, lg, neg), axis=1, keepdims=True)
    e2 = jnp.min(jnp.where(emask2 & (lg == m2), lane, big), axis=1, keepdims=True)
    r21 = jnp.exp(m2 - m1)
    w1 = p_top / (1.0 + r21)
    w2 = p_top * r21 / (1.0 + r21)
    o1 = lane == e1
    o2 = lane == e2
    onehot = jnp.where(o1 | o2, 1.0, 0.0)
    rank = _dot(tril_ref[...], onehot.astype(BF16)) + cnt_scr[...]
    pos1 = jnp.sum(jnp.where(o1, rank, 0.0), axis=1, keepdims=True)
    pos2 = jnp.sum(jnp.where(o2, rank, 0.0), axis=1, keepdims=True)
    cnt_scr[...] += jnp.sum(onehot, axis=0, keepdims=True)
    route = jnp.where(lane == ROUTE_E1, e1 - EXPERT_LANE0, 0.0)
    route = jnp.where(lane == ROUTE_E2, e2 - EXPERT_LANE0, route)
    route = jnp.where(lane == ROUTE_W1, w1, route)
    route = jnp.where(lane == ROUTE_W2, w2, route)
    route = jnp.where(lane == ROUTE_P1, pos1, route)
    route_ref[...] = jnp.where(lane == ROUTE_P2, pos2, route)


def _mix_router_kernel(*refs, n_prompt_tiles):
    prompt_rows, sample_rows, rest = refs[0:5], refs[5:10], refs[10:]
    consts, (h_ref, n2_ref, route_ref, cnt_ref, cnt_scr) = rest[:9], rest[9:]
    i = pl.program_id(0)

    @pl.when(i == 0)
    def _():
        cnt_scr[...] = jnp.zeros(cnt_scr.shape, F32)

    @pl.when(i < n_prompt_tiles)
    def _():
        _mix_router_body(*prompt_rows, *consts, h_ref, n2_ref, route_ref, cnt_scr)

    @pl.when(i >= n_prompt_tiles)
    def _():
        _mix_router_body(*sample_rows, *consts, h_ref, n2_ref, route_ref, cnt_scr)

    cnt_ref[...] = cnt_scr[...]


def _mix_router(prompt_rows, sample_rows, lnw, lnb, wo, gffn, wr, br, *, seq_tiles):
    tm = MOE_TM
    n_p = prompt_rows[0].shape[0] // tm
    assert sample_rows[0].shape[0] == tm
    t = (n_p + 1) * tm
    widths = (D_MODEL, GLA_WIDTH, RWKV_WIDTH, RWKV_WIDTH, RWKV_WIDTH)
    p_specs = [pl.BlockSpec((tm, n), lambda i: (jnp.minimum(i, n_p - 1), 0)) for n in widths]
    p_specs[2] = pl.BlockSpec(
        (tm, RWKV_WIDTH), lambda i: (jnp.minimum(i, n_p - 1) % seq_tiles, jnp.minimum(i, n_p - 1) // seq_tiles))
    s_specs = [pl.BlockSpec((tm, n), lambda i: (0, 0)) for n in widths]
    const = lambda shape: pl.BlockSpec(shape, lambda i: (0,) * len(shape))
    row = lambda n: pl.BlockSpec((tm, n), lambda i: (i, 0))
    tril = jnp.tril(jnp.ones((tm, tm), F32), -1).astype(BF16)
    return pl.pallas_call(
        functools.partial(_mix_router_kernel, n_prompt_tiles=n_p),
        grid=(n_p + 1,),
        in_specs=p_specs + s_specs + [
            const((1, RWKV_WIDTH)), const((1, RWKV_WIDTH)), const((D_MODEL, D_MODEL)), const((1, D_MODEL)),
            const((D_MODEL, ROUTER_LANES)), const((D_MODEL, ROUTER_LANES)), const((1, ROUTER_LANES)),
            const((RWKV_WIDTH, LANE)),
            const((tm, tm))],
        out_specs=[row(D_MODEL), row(D_MODEL), row(ROUTER_LANES), const((1, ROUTER_LANES))],
        out_shape=[jax.ShapeDtypeStruct((t, D_MODEL), F32), jax.ShapeDtypeStruct((t, D_MODEL), F32),
                   jax.ShapeDtypeStruct((t, ROUTER_LANES), F32), jax.ShapeDtypeStruct((1, ROUTER_LANES), F32)],
        scratch_shapes=[pltpu.VMEM((1, ROUTER_LANES), F32)],
        compiler_params=_cparams("arbitrary"),
        name="mix_router",
    )(*prompt_rows, *sample_rows, lnw, lnb, wo, gffn, *_split2(wr), br, _head_selector(), tril)


def _dispatch_kernel(slots_ref, x_ref, xs_hbm, sem):
    tm = x_ref.shape[0]

    for r in range(tm):
        src = x_ref.at[pl.ds(r, 1)]
        pltpu.make_async_copy(src, xs_hbm.at[pl.ds(slots_ref[0, 0, 2 * r], 1)], sem).start(priority=0)
        pltpu.make_async_copy(src, xs_hbm.at[pl.ds(slots_ref[0, 0, 2 * r + 1], 1)], sem).start(priority=1)
    for _ in range(2):
        pltpu.make_async_copy(x_ref, xs_hbm.at[pl.ds(0, tm)], sem).wait()


def _dispatch(n2, slots):
    t = n2.shape[0]
    tm = MOE_TM
    return pl.pallas_call(
        _dispatch_kernel,
        grid_spec=pltpu.PrefetchScalarGridSpec(
            num_scalar_prefetch=0,
            grid=(t // tm,),
            in_specs=[pl.BlockSpec((1, 1, 2 * tm), lambda i: (i, 0, 0), memory_space=pltpu.SMEM),
                      pl.BlockSpec((tm, D_MODEL), lambda i: (i, 0))],
            out_specs=pl.BlockSpec(memory_space=pl.ANY),
            scratch_shapes=[pltpu.SemaphoreType.DMA(())],
        ),
        out_shape=jax.ShapeDtypeStruct((2 * t, D_MODEL), F32),
        compiler_params=_cparams("arbitrary"),
        name="moe_dispatch",
    )(slots, n2)


def _experts_kernel(wt_ref, we_ref, wlo_ref, whi_ref, wfirst_ref, nw_ref,
                    xs_ref, w1_ref, w3_ref, w2_ref, ys_ref, wb1, wb3, wb2):
    w = pl.program_id(0)

    @pl.when(w < nw_ref[0])
    def _():
        new_expert = jnp.logical_or(w == 0, we_ref[w] != we_ref[jnp.maximum(w - 1, 0)])

        @pl.when(new_expert)
        def _():
            wb1[...] = w1_ref[0].astype(BF16)
            wb3[...] = w3_ref[0].astype(BF16)
            wb2[...] = w2_ref[0].astype(BF16)

        x = xs_ref[...].astype(BF16)
        a = _dot(x, wb1[...])
        b = _dot(x, wb3[...])
        o = _dot(((a * _sigmoid(a)) * b).astype(BF16), wb2[...])

        @pl.when(wfirst_ref[w] == 1)
        def _():
            ys_ref[...] = o

        @pl.when(wfirst_ref[w] == 0)
        def _():
            rows = lax.broadcasted_iota(jnp.int32, o.shape, 0)
            ys_ref[...] = jnp.where((rows >= wlo_ref[w]) & (rows < whi_ref[w]), o, ys_ref[...])


def _experts(xs, work, w1, w3, w2):
    s = xs.shape[0]
    ts = MOE_TS
    n_work = work[0].shape[0]
    return pl.pallas_call(
        _experts_kernel,
        grid_spec=pltpu.PrefetchScalarGridSpec(
            num_scalar_prefetch=6,
            grid=(n_work,),
            in_specs=[
                pl.BlockSpec((ts, D_MODEL), lambda w, wt, we, *_: (wt[w], 0)),
                pl.BlockSpec((1, D_MODEL, D_EXPERT), lambda w, wt, we, *_: (we[w], 0, 0)),
                pl.BlockSpec((1, D_MODEL, D_EXPERT), lambda w, wt, we, *_: (we[w], 0, 0)),
                pl.BlockSpec((1, D_EXPERT, D_MODEL), lambda w, wt, we, *_: (we[w], 0, 0)),
            ],
            out_specs=pl.BlockSpec((ts, D_MODEL), lambda w, wt, we, *_: (wt[w], 0)),
            scratch_shapes=[pltpu.VMEM((D_MODEL, D_EXPERT), BF16), pltpu.VMEM((D_MODEL, D_EXPERT), BF16),
                            pltpu.VMEM((D_EXPERT, D_MODEL), BF16)],
        ),
        out_shape=jax.ShapeDtypeStruct((s, D_MODEL), F32),
        compiler_params=_cparams("arbitrary"),
        name="moe_experts",
    )(*work, xs, w1, w3, w2)


def _expert_work_items(counts, total):
    ts = MOE_TS
    n_tiles = total // ts
    n_work = n_tiles + N_EXPERTS - 1
    offs = jnp.cumsum(counts) - counts
    t0 = (jnp.arange(n_tiles, dtype=jnp.int32) * ts)[:, None]
    lo = jnp.maximum(t0, offs[None, :])
    hi = jnp.minimum(t0 + ts, (offs + counts)[None, :])
    nonempty = (hi > lo).reshape(-1)
    nw = jnp.sum(nonempty.astype(jnp.int32))
    idx = jnp.nonzero(nonempty, size=n_work, fill_value=0)[0].astype(jnp.int32)
    idx = jnp.where(jnp.arange(n_work) < nw, idx, idx[jnp.maximum(nw - 1, 0)])
    wt = idx // N_EXPERTS
    we = idx % N_EXPERTS
    wlo = lo.reshape(-1)[idx] - wt * ts
    whi = hi.reshape(-1)[idx] - wt * ts
    wfirst = jnp.concatenate([jnp.ones((1,), jnp.int32), (wt[1:] != wt[:-1]).astype(jnp.int32)])
    return wt, we, wlo, whi, wfirst, nw.reshape(1)


def _combine_kernel(slots_ref, slots_next_ref, h_ref, route_ref, gfin_ref, ys_hbm, yp_ref, ysm_ref, gbuf, sems,
                    *, n_prompt_tiles):
    i = pl.program_id(0)
    n = pl.num_programs(0)
    tm = h_ref.shape[0]

    def gather(s_ref, buf):
        for r in range(tm):
            pltpu.make_async_copy(ys_hbm.at[pl.ds(s_ref[0, 0, 2 * r], 1)], gbuf.at[buf, 0, pl.ds(r, 1)],
                                  sems.at[buf]).start(priority=0)
            pltpu.make_async_copy(ys_hbm.at[pl.ds(s_ref[0, 0, 2 * r + 1], 1)], gbuf.at[buf, 1, pl.ds(r, 1)],
                                  sems.at[buf]).start(priority=1)

    cur = i % 2

    @pl.when(i == 0)
    def _():
        gather(slots_ref, 0)

    @pl.when(i + 1 < n)
    def _():
        gather(slots_next_ref, 1 - cur)

    for k in range(2):
        pltpu.make_async_copy(ys_hbm.at[pl.ds(0, tm)], gbuf.at[cur, k], sems.at[cur]).wait()
    route = route_ref[...]
    lane = lax.broadcasted_iota(jnp.int32, route.shape, 1)
    w1 = jnp.sum(jnp.where(lane == ROUTE_W1, route, 0.0), axis=1, keepdims=True)
    w2 = jnp.sum(jnp.where(lane == ROUTE_W2, route, 0.0), axis=1, keepdims=True)
    hf = h_ref[...] + (w1 * gbuf[cur, 0] + w2 * gbuf[cur, 1])
    y = hf * lax.rsqrt(jnp.mean(hf * hf, axis=-1, keepdims=True) + NORM_EPS) * gfin_ref[...]

    @pl.when(i < n_prompt_tiles)
    def _():
        yp_ref[...] = y

    @pl.when(i >= n_prompt_tiles)
    def _():
        ysm_ref[...] = y


def _combine(h, route, slots, ys, gfin, n_prompt_tiles):
    t = h.shape[0]
    tm = MOE_TM
    n_p = n_prompt_tiles
    return pl.pallas_call(
        functools.partial(_combine_kernel, n_prompt_tiles=n_p),
        grid_spec=pltpu.PrefetchScalarGridSpec(
            num_scalar_prefetch=0,
            grid=(t // tm,),
            in_specs=[pl.BlockSpec((1, 1, 2 * tm), lambda i: (i, 0, 0), memory_space=pltpu.SMEM),
                      pl.BlockSpec((1, 1, 2 * tm), lambda i: (jnp.minimum(i + 1, t // tm - 1), 0, 0),
                                   memory_space=pltpu.SMEM),
                      pl.BlockSpec((tm, D_MODEL), lambda i: (i, 0)),
                      pl.BlockSpec((tm, ROUTER_LANES), lambda i: (i, 0)),
                      pl.BlockSpec((1, D_MODEL), lambda i: (0, 0)),
                      pl.BlockSpec(memory_space=pl.ANY)],
            out_specs=[pl.BlockSpec((tm, D_MODEL), lambda i: (jnp.minimum(i, n_p - 1), 0)),
                       pl.BlockSpec((tm, D_MODEL), lambda i: (0, 0))],
            scratch_shapes=[pltpu.VMEM((2, 2, tm, D_MODEL), F32), pltpu.SemaphoreType.DMA((2,))],
        ),
        out_shape=[jax.ShapeDtypeStruct((n_p * tm, D_MODEL), F32), jax.ShapeDtypeStruct((tm, D_MODEL), F32)],
        compiler_params=_cparams("arbitrary"),
        name="moe_combine",
    )(slots, slots, h, route, gfin, ys)


def _v_tiles(v, tblk):
    b, t, _ = v.shape
    x = v.reshape(b, t // tblk, tblk, 2, RWKV_PAIRS, RWKV_HEAD).transpose(0, 1, 4, 5, 3, 2)
    x = jnp.pad(x, ((0, 0),) * 5 + ((0, RWKV_HEAD - tblk),))
    return x.reshape(b, t // tblk, RWKV_PAIRS, RWKV_HEAD, LANE).astype(BF16)


def _y_rows(y, bb, t):
    nb, tpad, _ = y.shape
    return y.reshape(nb, tpad, bb, RWKV_WIDTH).transpose(0, 2, 1, 3).reshape(nb * bb, tpad, RWKV_WIDTH)[:, :t]


def kernel(x_prompt, x_sample, state_gla, state_rwkv, state_shift, meta_tokens, norm_mix, w_in, gla_gate_w2,
           gla_gate_b, gla_norm, rwkv_mu, rwkv_w0, rwkv_w2, rwkv_a0, rwkv_a2, rwkv_g2, rwkv_kk, rwkv_ka, rwkv_rk,
           rwkv_ln_w, rwkv_ln_b, w_out, norm_ffn, router_group_w, router_group_b, router_expert_w,
           router_expert_b, moe_w1, moe_w3, moe_w2, norm_final):
    bp, tp, _ = x_prompt.shape
    bs, ts, _ = x_sample.shape
    assert state_gla.shape[0] == 1, "one layer"
    lyr = 0

    w_in_l = w_in[lyr]
    wg = jnp.pad(w_in_l[:, :GLA_COLS], ((0, 0), (0, GLA_PCOLS - GLA_COLS))).astype(BF16)
    wr = w_in_l[:, GLA_COLS:].astype(BF16)
    g_mix = norm_mix[lyr][None, :]
    gw2p = jnp.pad(gla_gate_w2[lyr], ((0, LANE - GLA_GATE_RANK), (0, 0)))
    gb = gla_gate_b[lyr][None, :]
    gn = gla_norm[lyr][None, :]
    w2p = jnp.pad(rwkv_w2[lyr], ((0, 64), (0, 0))).astype(BF16)
    a2p = jnp.pad(rwkv_a2[lyr], ((64, 0), (0, 0))).astype(BF16)
    pre_params = (rwkv_mu[lyr][None, :], rwkv_w0[lyr][None, :], w2p, rwkv_a0[lyr][None, :], a2p,
                  rwkv_g2[lyr].astype(BF16), rwkv_kk[lyr][None, :], rwkv_ka[lyr][None, :],
                  rwkv_rk[lyr].reshape(1, RWKV_WIDTH), _head_selector())
    lnw = rwkv_ln_w[lyr][None, :]
    lnb = rwkv_ln_b[lyr][None, :]
    wo = w_out[lyr].astype(BF16)
    gffn = norm_ffn[lyr][None, :]
    n_used = N_GROUPS + N_EXPERTS
    w_router = jnp.pad(
        jnp.concatenate([router_group_w[lyr],
                         router_expert_w[lyr].transpose(1, 0, 2).reshape(D_MODEL, N_EXPERTS)], axis=1),
        ((0, 0), (0, ROUTER_LANES - n_used)))
    b_router = jnp.pad(jnp.concatenate([router_group_b[lyr], router_expert_b[lyr].reshape(N_EXPERTS)]),
                       (0, ROUTER_LANES - n_used))[None, :]
    gfin = norm_final[None, :]

    pg_m, pr_m = _inproj(meta_tokens, g_mix, wg, wr, N_META)
    _, sg_m = _gla(pg_m[None], jnp.zeros((1, GLA_HEADS, GLA_DK, GLA_DV), F32), gw2p, gb, gn,
                   bb=1, chunk=N_META, sub=N_META, t_valid=N_META)
    r, w, k, kk, kka, v_m, _, _ = _rwkv_pre(pr_m[None], jnp.zeros((1, 1, RWKV_COLS), F32), pre_params,
                                            tm=N_META, explicit_prev=False, emit_vt=False)
    _, sr_m = _rwkv_rec(r, w, k, kk, kka, _v_tiles(v_m, N_META),
                        jnp.zeros((1, RWKV_HEADS, RWKV_HEAD, RWKV_HEAD), F32), bb=1, n_steps=N_META)

    xp = x_prompt.reshape(bp * tp, D_MODEL)
    pg_p, pr_p = _inproj(xp, g_mix, wg, wr, INPROJ_TM)
    og_p, sg_p = _gla(pg_p.reshape(bp, tp, GLA_PCOLS), jnp.broadcast_to(sg_m, (bp,) + sg_m.shape[1:]), gw2p, gb, gn,
                      bb=SEQ_BLOCK, chunk=GLA_CHUNK, sub=GLA_SUB, t_valid=GLA_CHUNK)
    pr_p3 = pr_p.reshape(bp, tp, RWKV_COLS)
    first_prev = jnp.broadcast_to(pr_m[N_META - 1][None, None, :], (bp, 1, RWKV_COLS))
    r, w, k, kk, kka, vt_p, bv_p, gate_p = _rwkv_pre(pr_p3, first_prev, pre_params, tm=PRE_TM, explicit_prev=False,
                                                     emit_vt=True)
    y_p, sr_p = _rwkv_rec(r, w, k, kk, kka, vt_p, jnp.broadcast_to(sr_m, (bp,) + sr_m.shape[1:]),
                          bb=bp, n_steps=REC_TB)
    prompt_rows = (xp, og_p.reshape(bp * tp, GLA_WIDTH), y_p.reshape(tp, bp * RWKV_WIDTH),
                   bv_p.reshape(bp * tp, RWKV_WIDTH), gate_p.reshape(bp * tp, RWKV_WIDTH))

    xs = x_sample.transpose(1, 0, 2).reshape(ts * bs, D_MODEL)
    pg_s, pr_s = _inproj(xs, g_mix, wg, wr, bs * ts)
    ts_pad = 8
    pg_s3 = jnp.pad(pg_s.reshape(ts, bs, GLA_PCOLS).transpose(1, 0, 2), ((0, 0), (0, ts_pad - ts), (0, 0)))
    og_s, sg_s = _gla(pg_s3, state_gla[lyr], gw2p, gb, gn, bb=SEQ_BLOCK, chunk=ts_pad, sub=ts_pad, t_valid=ts)
    og_s = og_s[:, :ts].transpose(1, 0, 2).reshape(ts * bs, GLA_WIDTH)
    pr_s3 = pr_s.reshape(ts, bs, RWKV_COLS)
    prev_s = jnp.concatenate([state_shift[lyr][None], pr_s3[:-1]], axis=0)
    r, w, k, kk, kka, v_s, bv_s, gate_s = _rwkv_pre(pr_s3.reshape(1, bs * ts, RWKV_COLS),
                                                     prev_s.reshape(1, bs * ts, RWKV_COLS), pre_params,
                                                     tm=bs * ts, explicit_prev=True, emit_vt=False, pair_out=False)
    y_s, sr_s = _rwkv_lanes(r[0], w[0], k[0], kk[0], kka[0], v_s[0], state_rwkv[lyr].transpose(1, 2, 3, 0),
                            n_steps=ts)
    sr_s = sr_s.transpose(3, 0, 1, 2)
    sample_rows = (xs, og_s, y_s, bv_s[0], gate_s[0])

    h_all, n2_all, route, counts = _mix_router(prompt_rows, sample_rows, lnw, lnb, wo, gffn, w_router, b_router,
                                               seq_tiles=tp // MOE_TM)
    n_tok = h_all.shape[0]
    n_p_tiles = (bp * tp) // MOE_TM
    cnt = counts[0, EXPERT_LANE0:EXPERT_LANE0 + N_EXPERTS].astype(jnp.int32)
    offs = jnp.cumsum(cnt) - cnt
    eid = route[:, ROUTE_E1:ROUTE_E2 + 1].astype(jnp.int32)
    pos = route[:, ROUTE_P1:ROUTE_P2 + 1].astype(jnp.int32)
    onehot = (eid[..., None] == jnp.arange(N_EXPERTS, dtype=jnp.int32)).astype(F32)
    off = jnp.einsum("tke,e->tk", onehot, offs.astype(F32), precision=HIGHEST).astype(jnp.int32)
    slots = (off + pos).reshape(n_tok // MOE_TM, 1, 2 * MOE_TM)
    xs_sorted = _dispatch(n2_all, slots)
    ys_sorted = _experts(xs_sorted, _expert_work_items(cnt, 2 * n_tok), moe_w1[lyr], moe_w3[lyr], moe_w2[lyr])
    y_prompt, y_sample = _combine(h_all, route, slots, ys_sorted, gfin, n_p_tiles)
    y_prompt = y_prompt.reshape(bp, tp, D_MODEL)
    y_sample = y_sample.reshape(ts, bs, D_MODEL).transpose(1, 0, 2)

    return (y_prompt, y_sample,
            sg_p[None], sr_p[None], pr_p3[:, -1][None],
            sg_s[None], sr_s[None], pr_s3[ts - 1][None])
```

```python
import functools

import jax
import jax.numpy as jnp
from jax import lax
from jax.experimental import pallas as pl
from jax.experimental.pallas import tpu as pltpu

F32 = jnp.float32
BF16 = jnp.bfloat16
HIGHEST = lax.Precision.HIGHEST

D_MODEL = 1024
N_META = 16
NORM_EPS = 1e-6
LOG2E = 1.4426950408889634
GLA_HEADS = 4
GLA_DK = 64
GLA_DV = 128
GLA_QK = GLA_HEADS * GLA_DK
GLA_WIDTH = GLA_HEADS * GLA_DV
GLA_GATE_RANK = 16
GLA_GATE_NORM = 16.0
GLA_CHUNK = 64
GLA_SUB = 8
GLA_COLS = 2 * GLA_QK + 2 * GLA_WIDTH + GLA_GATE_RANK
GLA_PCOLS = 2 * GLA_QK + 2 * GLA_WIDTH + 128
RWKV_WIDTH = 512
RWKV_HEAD = 64
RWKV_HEADS = 8
RWKV_PAIRS = RWKV_HEADS // 2
RWKV_DECAY_SCALE = 0.606531
RWKV_GN_EPS = 64e-5
RWKV_COLS = 3 * RWKV_WIDTH + 64 + 64 + 128
REC_TB = 64
REC_UNROLL = 8
INPROJ_TM = 512
PRE_TM = 256
SEQ_BLOCK = 8
N_GROUPS = 4
EXPERTS_PER_GROUP = 8
N_EXPERTS = 32
D_EXPERT = 512
ROUTER_LANES = 128
EXPERT_LANE0 = N_GROUPS
ROUTE_E1, ROUTE_E2, ROUTE_W1, ROUTE_W2, ROUTE_P1, ROUTE_P2 = range(6)
MOE_TM = 512
MOE_TS = 512

LANE = 128
VMEM_LIMIT = 56 * 1024 * 1024


def _cparams(*sem):
    return pltpu.CompilerParams(dimension_semantics=sem, vmem_limit_bytes=VMEM_LIMIT)


def _block_ones(n, blk):
    i = jnp.arange(n)
    return (i[:, None] // blk == i[None, :] // blk).astype(BF16)


def _sigmoid(x):
    return 1.0 / (1.0 + jnp.exp(-x))


def _dot(a, b):
    return jnp.dot(a, b, preferred_element_type=F32)


def _dot_nt(a, b):
    return lax.dot_general(a, b, (((1,), (1,)), ((), ())), preferred_element_type=F32)


def _dot_tn(a, b):
    return lax.dot_general(a, b, (((0,), (0,)), ((), ())), preferred_element_type=F32)


def _split2(x):
    hi = x.astype(BF16)
    lo = (x - hi.astype(F32)).astype(BF16)
    return hi, lo


def _head_selector():
    return (jnp.arange(RWKV_WIDTH)[:, None] // RWKV_HEAD == jnp.arange(LANE)[None, :]).astype(BF16)


def _group_sum(x, sel):
    hi, lo = _split2(x)
    s_hi, s_lo = _split2(_dot(hi, sel) + _dot(lo, sel))
    return _dot_nt(s_hi, sel) + _dot_nt(s_lo, sel)


def _inproj_kernel(x_ref, g_ref, wg_ref, wr_ref, pg_ref, pr_ref):
    x = x_ref[...]
    n = x * lax.rsqrt(jnp.mean(x * x, axis=-1, keepdims=True) + NORM_EPS) * g_ref[...]
    nb = n.astype(BF16)
    pg_ref[...] = _dot(nb, wg_ref[...])
    pr_ref[...] = _dot(nb, wr_ref[...])


def _inproj(x, g, wg, wr, tm):
    t = x.shape[0]
    return pl.pallas_call(
        _inproj_kernel,
        grid=(t // tm,),
        in_specs=[
            pl.BlockSpec((tm, D_MODEL), lambda i: (i, 0)),
            pl.BlockSpec((1, D_MODEL), lambda i: (0, 0)),
            pl.BlockSpec((D_MODEL, GLA_PCOLS), lambda i: (0, 0)),
            pl.BlockSpec((D_MODEL, RWKV_COLS), lambda i: (0, 0)),
        ],
        out_specs=[
            pl.BlockSpec((tm, GLA_PCOLS), lambda i: (i, 0)),
            pl.BlockSpec((tm, RWKV_COLS), lambda i: (i, 0)),
        ],
        out_shape=[jax.ShapeDtypeStruct((t, GLA_PCOLS), F32), jax.ShapeDtypeStruct((t, RWKV_COLS), F32)],
        compiler_params=_cparams("parallel"),
        name="inproj",
    )(x, g, wg, wr)


def _gla_kernel(pg_ref, s0_ref, gw2_ref, gb_ref, gn_ref, bo_ref, tril_ref, o_ref, sout_ref, s_scr,
                *, bb, chunk, sub, t_valid):
    ci = pl.program_id(1)

    @pl.when(ci == 0)
    def _():
        s_scr[...] = s0_ref[...]

    bo = bo_ref[...]
    tril = tril_ref[...]
    lane = lax.broadcasted_iota(jnp.int32, (sub, LANE), 1) & (GLA_DK - 1)
    rowi = lax.broadcasted_iota(jnp.int32, (sub, LANE), 0)
    head0_s = lax.broadcasted_iota(jnp.int32, (sub, LANE), 1) < GLA_DK
    head0_c = lax.broadcasted_iota(jnp.int32, (chunk, LANE), 1) < GLA_DK

    for bi in range(bb):
        pg = pg_ref[bi]
        q = pg[:, 0:GLA_QK] * (GLA_DK ** -0.5)
        k = pg[:, GLA_QK:2 * GLA_QK]
        v = pg[:, 2 * GLA_QK:2 * GLA_QK + GLA_WIDTH]
        g = pg[:, 2 * GLA_QK + GLA_WIDTH:2 * GLA_QK + 2 * GLA_WIDTH]
        gl = pg[:, 2 * GLA_QK + 2 * GLA_WIDTH:]
        z = jnp.dot(gl, gw2_ref[...], precision=HIGHEST, preferred_element_type=F32) + gb_ref[...]
        lg = (jnp.minimum(z, 0.0) - jnp.log1p(jnp.exp(-jnp.abs(z)))) * (LOG2E / GLA_GATE_NORM)
        if t_valid < chunk:
            rows = lax.broadcasted_iota(jnp.int32, lg.shape, 0)
            lg = jnp.where(rows < t_valid, lg, 0.0)
        b = jnp.dot(tril, lg, precision=HIGHEST, preferred_element_type=F32)
        eb = jnp.exp2(b)
        blast = b[chunk - 1:chunk, :]
        kl = k * jnp.exp2(blast - b)
        qe = q * eb

        n_blk = chunk // sub
        n_pairs = GLA_HEADS // 2
        ps = []
        for hp in range(n_pairs):
            sl = slice(hp * LANE, (hp + 1) * LANE)
            for blk in range(n_blk):
                rs = slice(blk * sub, (blk + 1) * sub)
                qb, kb, bbk = q[rs, sl], k[rs, sl], b[rs, sl]
                for j in range(sub):
                    ps.append(qb * (kb[j:j + 1] * jnp.exp2(jnp.minimum(bbk - bbk[j:j + 1], 0.0))))
        red = _dot(jnp.concatenate(ps, axis=0).astype(BF16), bo)

        o_heads = []
        for hp in range(n_pairs):
            sl = slice(hp * LANE, (hp + 1) * LANE)
            kp, bp = k[:, sl], b[:, sl]
            row_blocks = []
            for blk in range(n_blk):
                rs = slice(blk * sub, (blk + 1) * sub)
                base = (hp * n_blk + blk) * sub * sub
                a = jnp.zeros((sub, LANE), F32)
                for j in range(sub):
                    a = jnp.where((lane == blk * sub + j) & (rowi >= j), red[base + j * sub:base + (j + 1) * sub], a)
                if blk > 0:
                    bref = bp[blk * sub - 1:blk * sub]
                    qt = q[rs, sl] * jnp.exp2(bp[rs] - bref)
                    kt = (kp * jnp.exp2(jnp.minimum(bref - bp, 0.0))).astype(BF16)
                    qt2 = jnp.concatenate([jnp.where(head0_s, qt, 0.0), jnp.where(head0_s, 0.0, qt)], axis=0)
                    off2 = _dot_nt(qt2.astype(BF16), kt)
                    a = jnp.where(lane < blk * sub, jnp.concatenate([off2[:sub], off2[sub:]], axis=1), a)
                row_blocks.append(a)
            a_pair = row_blocks[0] if n_blk == 1 else jnp.concatenate(row_blocks, axis=0)
            v0 = v[:, 2 * hp * GLA_DV:(2 * hp + 1) * GLA_DV]
            v1 = v[:, (2 * hp + 1) * GLA_DV:(2 * hp + 2) * GLA_DV]
            s_pair = s_scr[bi, 2 * hp:2 * hp + 2].reshape(2 * GLA_DK, GLA_DV)
            qe_p, kl_p = qe[:, sl], kl[:, sl]

            def by_head(x):
                return jnp.concatenate([jnp.where(head0_c, x, 0.0), jnp.where(head0_c, 0.0, x)], axis=0)

            if chunk == GLA_DK:
                v_rows = jnp.concatenate([v0, v1], axis=0)
            else:
                zpad = jnp.zeros((GLA_DK - chunk, GLA_DV), F32)
                v_rows = jnp.concatenate([v0, zpad, v1, zpad], axis=0)
            lhs = jnp.concatenate([by_head(a_pair), by_head(qe_p)], axis=1).astype(BF16)
            rhs = jnp.concatenate([v_rows, s_pair], axis=0).astype(BF16)
            o2 = _dot(lhs, rhs)
            upd = _dot_tn(by_head(kl_p).astype(BF16), jnp.concatenate([v0, v1], axis=0).astype(BF16))
            dcol = jnp.broadcast_to(jnp.exp2(blast[:, sl]), (8, LANE)).T[:, 0:1]
            s_new = dcol * s_pair + upd
            s_scr[bi, 2 * hp] = s_new[:GLA_DK]
            s_scr[bi, 2 * hp + 1] = s_new[GLA_DK:]
            for h2 in range(2):
                o_h = o2[h2 * chunk:(h2 + 1) * chunk]
                o_heads.append(o_h * lax.rsqrt(jnp.mean(o_h * o_h, axis=-1, keepdims=True) + NORM_EPS) * gn_ref[...])
        o = jnp.concatenate(o_heads, axis=1)
        o_ref[bi] = o * (g * _sigmoid(g))

    @pl.when(ci == pl.num_programs(1) - 1)
    def _():
        sout_ref[...] = s_scr[...]


def _gla(pg, s0, gw2p, gb, gn, *, bb, chunk, sub, t_valid):
    b, t, _ = pg.shape
    tril = jnp.tril(jnp.ones((chunk, chunk), F32))
    kern = functools.partial(_gla_kernel, bb=bb, chunk=chunk, sub=sub, t_valid=t_valid)
    return pl.pallas_call(
        kern,
        grid=(b // bb, t // chunk),
        in_specs=[
            pl.BlockSpec((bb, chunk, GLA_PCOLS), lambda i, j: (i, j, 0)),
            pl.BlockSpec((bb, GLA_HEADS, GLA_DK, GLA_DV), lambda i, j: (i, 0, 0, 0)),
            pl.BlockSpec((LANE, GLA_QK), lambda i, j: (0, 0)),
            pl.BlockSpec((1, GLA_QK), lambda i, j: (0, 0)),
            pl.BlockSpec((1, GLA_DV), lambda i, j: (0, 0)),
            pl.BlockSpec((LANE, LANE), lambda i, j: (0, 0)),
            pl.BlockSpec((chunk, chunk), lambda i, j: (0, 0)),
        ],
        out_specs=[
            pl.BlockSpec((bb, chunk, GLA_WIDTH), lambda i, j: (i, j, 0)),
            pl.BlockSpec((bb, GLA_HEADS, GLA_DK, GLA_DV), lambda i, j: (i, 0, 0, 0)),
        ],
        out_shape=[jax.ShapeDtypeStruct((b, t, GLA_WIDTH), F32),
                   jax.ShapeDtypeStruct((b, GLA_HEADS, GLA_DK, GLA_DV), F32)],
        scratch_shapes=[pltpu.VMEM((bb, GLA_HEADS, GLA_DK, GLA_DV), F32)],
        compiler_params=_cparams("parallel", "arbitrary"),
        name="gla_chunk",
    )(pg, s0, gw2p, gb, gn, _block_ones(LANE, GLA_DK), tril)


def _rwkv_pre_kernel(pr_ref, aux_ref, mu_ref, w0_ref, w2_ref, a0_ref, a2_ref, g2_ref, kk_ref, ka_ref, rk_ref, bo_ref,
                     r_out, w_out, k_out, kkn_out, kka_out, v_out, bv_out, gate_out, carry_scr,
                     *, tm, explicit_prev, emit_vt, pair_out):
    pr = pr_ref[0]
    if explicit_prev:
        prev = aux_ref[0]
    else:
        j = pl.program_id(1)
        row0 = jnp.where(j == 0, aux_ref[0], carry_scr[...])
        rows = lax.broadcasted_iota(jnp.int32, pr.shape, 0)
        prev = jnp.where(rows == 0, row0, pltpu.roll(pr, 1, 0))
        carry_scr[...] = pr[tm - 1:tm, :]
    xm = pr + (prev - pr) * mu_ref[...]
    wd = RWKV_WIDTH
    rr, rk, rv = xm[:, 0:wd], xm[:, wd:2 * wd], xm[:, 2 * wd:3 * wd]
    wa = xm[:, 3 * wd:3 * wd + LANE]
    gl2 = xm[:, 3 * wd + LANE:3 * wd + 2 * LANE]
    logw = -RWKV_DECAY_SCALE * _sigmoid(w0_ref[...] + _dot(jnp.tanh(wa).astype(BF16), w2_ref[...]))
    aa = _sigmoid(a0_ref[...] + _dot(wa.astype(BF16), a2_ref[...]))
    gate = _dot(_sigmoid(gl2).astype(BF16), g2_ref[...])
    bo = bo_ref[...]
    kk = rk * kk_ref[...]
    kk = kk / jnp.maximum(jnp.sqrt(_group_sum(kk * kk, bo)), 1e-12)
    k = rk * (1.0 + (aa - 1.0) * ka_ref[...])
    bv = _group_sum(rr * k * rk_ref[...], bo) * rv
    w = jnp.exp(logw)
    kka = kk * aa
    hd = RWKV_HEAD

    def pair(x, hp):
        return jnp.concatenate([x[:, hp * hd:(hp + 1) * hd], x[:, (hp + RWKV_PAIRS) * hd:(hp + RWKV_PAIRS + 1) * hd]],
                               axis=1)

    if pair_out:
        for hp in range(RWKV_PAIRS):
            r_out[0, hp] = pair(rr, hp)
            w_out[0, hp] = pair(w, hp)
            k_out[0, hp] = pair(k, hp)
            kkn_out[0, hp] = pair(kk, hp)
            kka_out[0, hp] = pair(kka, hp)
    else:
        r_out[0], w_out[0], k_out[0], kkn_out[0], kka_out[0] = rr, w, k, kk, kka
    if emit_vt:
        vt = rv.T
        for tb in range(tm // REC_TB):
            ts = slice(tb * REC_TB, (tb + 1) * REC_TB)
            for hp in range(RWKV_PAIRS):
                lo, hi = hp * hd, (hp + RWKV_PAIRS) * hd
                v_out[0, tb, hp] = jnp.concatenate([vt[lo:lo + hd, ts], vt[hi:hi + hd, ts]], axis=1).astype(BF16)
    else:
        v_out[0] = rv
    bv_out[0] = bv
    gate_out[0] = gate


def _rwkv_pre(pr, aux, params, *, tm, explicit_prev, emit_vt, pair_out=True):
    b, t, _ = pr.shape
    kern = functools.partial(_rwkv_pre_kernel, tm=tm, explicit_prev=explicit_prev, emit_vt=emit_vt,
                             pair_out=pair_out)
    aux_spec = (pl.BlockSpec((1, tm, RWKV_COLS), lambda i, j: (i, j, 0)) if explicit_prev
                else pl.BlockSpec((1, 1, RWKV_COLS), lambda i, j: (i, 0, 0)))
    const = lambda shape: pl.BlockSpec(shape, lambda i, j: (0,) * len(shape))
    pair_spec = pl.BlockSpec((1, RWKV_PAIRS, tm, LANE), lambda i, j: (i, 0, j, 0))
    row_spec = pl.BlockSpec((1, tm, RWKV_WIDTH), lambda i, j: (i, j, 0))
    pair_shape = jax.ShapeDtypeStruct((b, RWKV_PAIRS, t, LANE), F32)
    row_shape = jax.ShapeDtypeStruct((b, t, RWKV_WIDTH), F32)
    if emit_vt:
        v_spec = pl.BlockSpec((1, tm // REC_TB, RWKV_PAIRS, RWKV_HEAD, LANE), lambda i, j: (i, j, 0, 0, 0))
        v_shape = jax.ShapeDtypeStruct((b, t // REC_TB, RWKV_PAIRS, RWKV_HEAD, LANE), BF16)
    else:
        v_spec, v_shape = row_spec, row_shape
    if not pair_out:
        pair_spec, pair_shape = row_spec, row_shape
    return pl.pallas_call(
        kern,
        grid=(b, t // tm),
        in_specs=[
            pl.BlockSpec((1, tm, RWKV_COLS), lambda i, j: (i, j, 0)),
            aux_spec,
            const((1, RWKV_COLS)), const((1, RWKV_WIDTH)), const((LANE, RWKV_WIDTH)), const((1, RWKV_WIDTH)),
            const((LANE, RWKV_WIDTH)), const((LANE, RWKV_WIDTH)), const((1, RWKV_WIDTH)), const((1, RWKV_WIDTH)),
            const((1, RWKV_WIDTH)), const((RWKV_WIDTH, LANE)),
        ],
        out_specs=[pair_spec] * 5 + [v_spec, row_spec, row_spec],
        out_shape=[pair_shape] * 5 + [v_shape, row_shape, row_shape],
        scratch_shapes=[pltpu.VMEM((1, RWKV_COLS), F32)],
        compiler_params=_cparams("parallel", "arbitrary"),
        name="rwkv_pre",
    )(pr, aux, *params)


def _rwkv_rec_kernel(r_ref, w_ref, k_ref, kk_ref, kka_ref, vt_ref, s0_ref, bo_ref, vsel_ref, ysel_ref,
                     y_ref, sout_ref, s_scr, t1_scr, t3_scr, yt_scr, *, bb, n_steps):
    tb = pl.program_id(1)
    nc = bb * RWKV_PAIRS
    hd = RWKV_HEAD

    @pl.when(tb == 0)
    def _():
        for c in range(nc):
            bi, hp = divmod(c, RWKV_PAIRS)
            s_scr[c] = jnp.concatenate([s0_ref[bi, hp], s0_ref[bi, hp + RWKV_PAIRS]], axis=1)

    bo = bo_ref[...]

    def step(t, u):
        row = pl.ds(t, 1)
        for c in range(nc):
            bi, hp = divmod(c, RWKV_PAIRS)
            t1_scr[c * hd:(c + 1) * hd, :] = (s_scr[c] * kk_ref[bi, hp, row, :]).astype(BF16)
        sab = _dot(t1_scr[...], bo)
        vb = _dot(vt_ref[...].reshape(nc * hd, LANE), vsel_ref[t])
        for c in range(nc):
            bi, hp = divmod(c, RWKV_PAIRS)
            rs = slice(c * hd, (c + 1) * hd)
            s2 = (s_scr[c] * w_ref[bi, hp, row, :] - sab[rs] * kka_ref[bi, hp, row, :]
                  + vb[rs] * k_ref[bi, hp, row, :])
            s_scr[c] = s2
            t3_scr[rs, :] = (s2 * r_ref[bi, hp, row, :]).astype(BF16)
        yt_scr[...] += _dot_nt(ysel_ref[u], t3_scr[...])

    n_inner = min(8, n_steps)

    def block8(t8, carry):
        yt_scr[...] = jnp.zeros(yt_scr.shape, F32)

        def inner(u, c2):
            step(t8 * 8 + u, u)
            return c2

        lax.fori_loop(0, n_inner, inner, 0, unroll=REC_UNROLL)
        t0 = pl.multiple_of(t8 * 8, 8)
        blk = RWKV_PAIRS * hd
        for bi in range(bb):
            for h2 in range(2):
                y_ref[0, pl.ds(t0, 8), bi * RWKV_WIDTH + h2 * blk:bi * RWKV_WIDTH + (h2 + 1) * blk] = (
                    yt_scr[h2 * 8:(h2 + 1) * 8, bi * blk:(bi + 1) * blk])
        return carry

    lax.fori_loop(0, (n_steps + 7) // 8, block8, 0)

    @pl.when(tb == pl.num_programs(1) - 1)
    def _():
        for c in range(nc):
            bi, hp = divmod(c, RWKV_PAIRS)
            s_c = s_scr[c]
            sout_ref[bi, hp] = s_c[:, :RWKV_HEAD]
            sout_ref[bi, hp + RWKV_PAIRS] = s_c[:, RWKV_HEAD:]


def _rwkv_rec(r, w, k, kk, kka, vt, s0, *, bb, n_steps):
    b, _, t, _ = r.shape
    tblk = min(REC_TB, t)
    ntb = t // tblk
    nc = bb * RWKV_PAIRS
    lane = jnp.arange(LANE)
    vsel = ((lane[None, :, None] // RWKV_HEAD == lane[None, None, :] // RWKV_HEAD)
            & (lane[None, :, None] % RWKV_HEAD == jnp.arange(RWKV_HEAD)[:, None, None])).astype(BF16)
    ysel = (jnp.arange(16)[None, :, None]
            == 8 * (lane[None, None, :] // RWKV_HEAD) + jnp.arange(8)[:, None, None]).astype(BF16)
    kern = functools.partial(_rwkv_rec_kernel, bb=bb, n_steps=n_steps)
    pair_spec = pl.BlockSpec((bb, RWKV_PAIRS, tblk, LANE), lambda i, j: (i, 0, j, 0))
    state_spec = pl.BlockSpec((bb, RWKV_HEADS, RWKV_HEAD, RWKV_HEAD), lambda i, j: (i, 0, 0, 0))
    ytb = max(tblk, 8)
    return pl.pallas_call(
        kern,
        grid=(b // bb, ntb),
        in_specs=[pair_spec] * 5 + [
            pl.BlockSpec((bb, 1, RWKV_PAIRS, RWKV_HEAD, LANE), lambda i, j: (i, j, 0, 0, 0)),
            state_spec,
            pl.BlockSpec((LANE, LANE), lambda i, j: (0, 0)),
            pl.BlockSpec((RWKV_HEAD, LANE, LANE), lambda i, j: (0, 0, 0)),
            pl.BlockSpec((8, 16, LANE), lambda i, j: (0, 0, 0)),
        ],
        out_specs=[
            pl.BlockSpec((1, ytb, bb * RWKV_WIDTH), lambda i, j: (i, j, 0)),
            state_spec,
        ],
        out_shape=[jax.ShapeDtypeStruct((b // bb, ntb * ytb, bb * RWKV_WIDTH), F32),
                   jax.ShapeDtypeStruct((b, RWKV_HEADS, RWKV_HEAD, RWKV_HEAD), F32)],
        scratch_shapes=[pltpu.VMEM((nc, RWKV_HEAD, LANE), F32),
                        pltpu.VMEM((nc * RWKV_HEAD, LANE), BF16),
                        pltpu.VMEM((nc * RWKV_HEAD, LANE), BF16),
                        pltpu.VMEM((16, nc * RWKV_HEAD), F32)],
        compiler_params=_cparams("parallel", "arbitrary"),
        name="rwkv_rec",
    )(r, w, k, kk, kka, vt, s0, _block_ones(LANE, RWKV_HEAD), vsel, ysel)


def _rwkv_lanes_kernel(r_ref, w_ref, k_ref, kk_ref, kka_ref, v_ref, s0_ref, y_ref, sout_ref, vt_scr, yt_scr,
                       *, n_steps, n_seq):
    hd = RWKV_HEAD
    for t in range(n_steps):
        rows = slice(t * n_seq, (t + 1) * n_seq)
        r_t, w_t, k_t = r_ref[rows, :].T, w_ref[rows, :].T, k_ref[rows, :].T
        kk_t, kka_t = kk_ref[rows, :].T, kka_ref[rows, :].T
        vt_scr[...] = v_ref[rows, :].T
        src = s0_ref if t == 0 else sout_ref
        for h2 in range(2):
            hs = slice(h2 * hd, (h2 + 1) * hd)
            r_h, w_h, k_h, kk_h, kka_h = r_t[hs], w_t[hs], k_t[hs], kk_t[hs], kka_t[hs]

            def value_row(i, carry):
                s = src[h2, i]
                sab = jnp.sum(s * kk_h, axis=0, keepdims=True)
                v_i = vt_scr[pl.ds(h2 * hd + i, 1), :]
                s2 = s * w_h - sab * kka_h + v_i * k_h
                sout_ref[h2, i] = s2
                yt_scr[pl.ds(h2 * hd + i, 1), :] = jnp.sum(s2 * r_h, axis=0, keepdims=True)
                return carry

            lax.fori_loop(0, hd, value_row, 0, unroll=8)
        y_ref[rows, :] = yt_scr[...].T


def _rwkv_lanes(r, w, k, kk, kka, v, s0, *, n_steps):
    n_tok = r.shape[0]
    n_seq = n_tok // n_steps
    assert n_seq == LANE, "one lane per sequence"
    row_spec = pl.BlockSpec((n_tok, LANE), lambda hp: (0, hp))
    state_spec = pl.BlockSpec((2, RWKV_HEAD, RWKV_HEAD, n_seq), lambda hp: (hp, 0, 0, 0))
    return pl.pallas_call(
        functools.partial(_rwkv_lanes_kernel, n_steps=n_steps, n_seq=n_seq),
        grid=(RWKV_HEADS // 2,),
        in_specs=[row_spec] * 6 + [state_spec],
        out_specs=[row_spec, state_spec],
        out_shape=[jax.ShapeDtypeStruct((n_tok, RWKV_WIDTH), F32),
                   jax.ShapeDtypeStruct((RWKV_HEADS, RWKV_HEAD, RWKV_HEAD, n_seq), F32)],
        scratch_shapes=[pltpu.VMEM((LANE, n_seq), F32), pltpu.VMEM((LANE, n_seq), F32)],
        compiler_params=_cparams("parallel"),
        name="rwkv_lanes",
    )(r, w, k, kk, kka, v, s0)


def _mix_router_body(x_ref, og_ref, y_ref, bv_ref, gate_ref, lnw_ref, lnb_ref, wo_ref, gffn_ref, wr_hi_ref, wr_lo_ref,
                     br_ref, bo_ref, tril_ref, h_ref, n2_ref, route_ref, route_t_ref, cnt_scr):
    bo = bo_ref[...]
    y = y_ref[...]
    inv_n = 1.0 / RWKV_HEAD
    d = y - _group_sum(y, bo) * inv_n
    var = _group_sum(d * d, bo) * inv_n
    yn = d * lax.rsqrt(var + RWKV_GN_EPS) * lnw_ref[...] + lnb_ref[...] + bv_ref[...]
    o_rwkv = yn * gate_ref[...]
    mix = (_dot(og_ref[...].astype(BF16), wo_ref[0:GLA_WIDTH, :])
           + _dot(o_rwkv.astype(BF16), wo_ref[GLA_WIDTH:, :]))
    h = x_ref[...] + mix
    h_ref[...] = h
    n2 = h * lax.rsqrt(jnp.mean(h * h, axis=-1, keepdims=True) + NORM_EPS) * gffn_ref[...]
    n2_ref[...] = n2
    n2_hi, n2_lo = _split2(n2)
    lg = (_dot(n2_hi, wr_hi_ref[...]) + _dot(n2_hi, wr_lo_ref[...]) + _dot(n2_lo, wr_hi_ref[...])) + br_ref[...]
    neg = jnp.float32(-3.0e38)
    big = jnp.float32(1.0e9)
    lane = lax.broadcasted_iota(jnp.int32, lg.shape, 1).astype(F32)
    gmask = lane < N_GROUPS
    gmax = jnp.max(jnp.where(gmask, lg, neg), axis=1, keepdims=True)
    p_top = 1.0 / jnp.sum(jnp.where(gmask, jnp.exp(jnp.minimum(lg - gmax, 0.0)), 0.0), axis=1, keepdims=True)
    gidx = jnp.min(jnp.where(gmask & (lg == gmax), lane, big), axis=1, keepdims=True)
    e_lo = EXPERT_LANE0 + gidx * EXPERTS_PER_GROUP
    emask = (lane >= e_lo) & (lane < e_lo + EXPERTS_PER_GROUP)
    m1 = jnp.max(jnp.where(emask, lg, neg), axis=1, keepdims=True)
    e1 = jnp.min(jnp.where(emask & (lg == m1), lane, big), axis=1, keepdims=True)
    emask2 = emask & (lane != e1)
    m2 = jnp.max(jnp.where(emask2, lg, neg), axis=1, keepdims=True)
    e2 = jnp.min(jnp.where(emask2 & (lg == m2), lane, big), axis=1, keepdims=True)
    r21 = jnp.exp(m2 - m1)
    w1 = p_top / (1.0 + r21)
    w2 = p_top * r21 / (1.0 + r21)
    o1 = lane == e1
    o2 = lane == e2
    onehot = jnp.where(o1 | o2, 1.0, 0.0)
    rank = _dot(tril_ref[...], onehot.astype(BF16)) + cnt_scr[...]
    pos1 = jnp.sum(jnp.where(o1, rank, 0.0), axis=1, keepdims=True)
    pos2 = jnp.sum(jnp.where(o2, rank, 0.0), axis=1, keepdims=True)
    cnt_scr[...] += jnp.sum(onehot, axis=0, keepdims=True)
    route = jnp.where(lane == ROUTE_E1, e1 - EXPERT_LANE0, 0.0)
    route = jnp.where(lane == ROUTE_E2, e2 - EXPERT_LANE0, route)
    route = jnp.where(lane == ROUTE_W1, w1, route)
    route = jnp.where(lane == ROUTE_W2, w2, route)
    route = jnp.where(lane == ROUTE_P1, pos1, route)
    route = jnp.where(lane == ROUTE_P2, pos2, route)
    route_ref[...] = route
    route_t_ref[0] = route.T[0:8, :]


def _mix_router_kernel(*refs, n_prompt_tiles):
    prompt_rows, sample_rows, rest = refs[0:5], refs[5:10], refs[10:]
    consts, (h_ref, n2_ref, route_ref, route_t_ref, cnt_ref, cnt_scr) = rest[:9], rest[9:]
    i = pl.program_id(0)

    @pl.when(i == 0)
    def _():
        cnt_scr[...] = jnp.zeros(cnt_scr.shape, F32)

    @pl.when(i < n_prompt_tiles)
    def _():
        _mix_router_body(*prompt_rows, *consts, h_ref, n2_ref, route_ref, route_t_ref, cnt_scr)

    @pl.when(i >= n_prompt_tiles)
    def _():
        _mix_router_body(*sample_rows, *consts, h_ref, n2_ref, route_ref, route_t_ref, cnt_scr)

    cnt_ref[...] = cnt_scr[...]


def _mix_router(prompt_rows, sample_rows, lnw, lnb, wo, gffn, wr, br, *, seq_tiles):
    tm = MOE_TM
    n_p = prompt_rows[0].shape[0] // tm
    assert sample_rows[0].shape[0] == tm
    t = (n_p + 1) * tm
    widths = (D_MODEL, GLA_WIDTH, RWKV_WIDTH, RWKV_WIDTH, RWKV_WIDTH)
    p_specs = [pl.BlockSpec((tm, n), lambda i: (jnp.minimum(i, n_p - 1), 0)) for n in widths]
    p_specs[2] = pl.BlockSpec(
        (tm, RWKV_WIDTH), lambda i: (jnp.minimum(i, n_p - 1) % seq_tiles, jnp.minimum(i, n_p - 1) // seq_tiles))
    s_specs = [pl.BlockSpec((tm, n), lambda i: (0, 0)) for n in widths]
    const = lambda shape: pl.BlockSpec(shape, lambda i: (0,) * len(shape))
    row = lambda n: pl.BlockSpec((tm, n), lambda i: (i, 0))
    tril = jnp.tril(jnp.ones((tm, tm), F32), -1).astype(BF16)
    return pl.pallas_call(
        functools.partial(_mix_router_kernel, n_prompt_tiles=n_p),
        grid=(n_p + 1,),
        in_specs=p_specs + s_specs + [
            const((1, RWKV_WIDTH)), const((1, RWKV_WIDTH)), const((D_MODEL, D_MODEL)), const((1, D_MODEL)),
            const((D_MODEL, ROUTER_LANES)), const((D_MODEL, ROUTER_LANES)), const((1, ROUTER_LANES)),
            const((RWKV_WIDTH, LANE)),
            const((tm, tm))],
        out_specs=[row(D_MODEL), row(D_MODEL), row(ROUTER_LANES), pl.BlockSpec((1, 8, tm), lambda i: (i, 0, 0)),
                   const((1, ROUTER_LANES))],
        out_shape=[jax.ShapeDtypeStruct((t, D_MODEL), F32), jax.ShapeDtypeStruct((t, D_MODEL), F32),
                   jax.ShapeDtypeStruct((t, ROUTER_LANES), F32), jax.ShapeDtypeStruct((t // tm, 8, tm), F32),
                   jax.ShapeDtypeStruct((1, ROUTER_LANES), F32)],
        scratch_shapes=[pltpu.VMEM((1, ROUTER_LANES), F32)],
        compiler_params=_cparams("arbitrary"),
        name="mix_router",
    )(*prompt_rows, *sample_rows, lnw, lnb, wo, gffn, *_split2(wr), br, _head_selector(), tril)


def _dispatch_kernel(slots_ref, x_ref, xs_hbm, sem):
    tm = x_ref.shape[0]

    for r in range(tm):
        src = x_ref.at[pl.ds(r, 1)]
        pltpu.make_async_copy(src, xs_hbm.at[pl.ds(slots_ref[0, 0, r], 1)], sem).start(priority=0)
        pltpu.make_async_copy(src, xs_hbm.at[pl.ds(slots_ref[0, 0, tm + r], 1)], sem).start(priority=1)
    for _ in range(2):
        pltpu.make_async_copy(x_ref, xs_hbm.at[pl.ds(0, tm)], sem).wait()


def _dispatch(n2, slots):
    t = n2.shape[0]
    tm = MOE_TM
    return pl.pallas_call(
        _dispatch_kernel,
        grid_spec=pltpu.PrefetchScalarGridSpec(
            num_scalar_prefetch=0,
            grid=(t // tm,),
            in_specs=[pl.BlockSpec((1, 1, 2 * tm), lambda i: (i, 0, 0), memory_space=pltpu.SMEM),
                      pl.BlockSpec((tm, D_MODEL), lambda i: (i, 0))],
            out_specs=pl.BlockSpec(memory_space=pl.ANY),
            scratch_shapes=[pltpu.SemaphoreType.DMA(())],
        ),
        out_shape=jax.ShapeDtypeStruct((2 * t, D_MODEL), F32),
        compiler_params=_cparams("arbitrary"),
        name="moe_dispatch",
    )(slots, n2)


def _experts_kernel(wt_ref, we_ref, wlo_ref, whi_ref, wfirst_ref, nw_ref,
                    xs_ref, w1_ref, w3_ref, w2_ref, ys_ref, wb1, wb3, wb2):
    w = pl.program_id(0)

    @pl.when(w < nw_ref[0])
    def _():
        new_expert = jnp.logical_or(w == 0, we_ref[w] != we_ref[jnp.maximum(w - 1, 0)])

        @pl.when(new_expert)
        def _():
            wb1[...] = w1_ref[0].astype(BF16)
            wb3[...] = w3_ref[0].astype(BF16)
            wb2[...] = w2_ref[0].astype(BF16)

        x = xs_ref[...].astype(BF16)
        a = _dot(x, wb1[...])
        b = _dot(x, wb3[...])
        o = _dot(((a * _sigmoid(a)) * b).astype(BF16), wb2[...])

        @pl.when(wfirst_ref[w] == 1)
        def _():
            ys_ref[...] = o

        @pl.when(wfirst_ref[w] == 0)
        def _():
            rows = lax.broadcasted_iota(jnp.int32, o.shape, 0)
            ys_ref[...] = jnp.where((rows >= wlo_ref[w]) & (rows < whi_ref[w]), o, ys_ref[...])


def _experts(xs, work, w1, w3, w2):
    s = xs.shape[0]
    ts = MOE_TS
    n_work = work[0].shape[0]
    return pl.pallas_call(
        _experts_kernel,
        grid_spec=pltpu.PrefetchScalarGridSpec(
            num_scalar_prefetch=6,
            grid=(n_work,),
            in_specs=[
                pl.BlockSpec((ts, D_MODEL), lambda w, wt, we, *_: (wt[w], 0)),
                pl.BlockSpec((1, D_MODEL, D_EXPERT), lambda w, wt, we, *_: (we[w], 0, 0)),
                pl.BlockSpec((1, D_MODEL, D_EXPERT), lambda w, wt, we, *_: (we[w], 0, 0)),
                pl.BlockSpec((1, D_EXPERT, D_MODEL), lambda w, wt, we, *_: (we[w], 0, 0)),
            ],
            out_specs=pl.BlockSpec((ts, D_MODEL), lambda w, wt, we, *_: (wt[w], 0)),
            scratch_shapes=[pltpu.VMEM((D_MODEL, D_EXPERT), BF16), pltpu.VMEM((D_MODEL, D_EXPERT), BF16),
                            pltpu.VMEM((D_EXPERT, D_MODEL), BF16)],
        ),
        out_shape=jax.ShapeDtypeStruct((s, D_MODEL), F32),
        compiler_params=_cparams("arbitrary"),
        name="moe_experts",
    )(*work, xs, w1, w3, w2)


def _expert_work_items(counts, total):
    ts = MOE_TS
    n_tiles = total // ts
    n_work = n_tiles + N_EXPERTS - 1
    offs = jnp.cumsum(counts) - counts
    t0 = (jnp.arange(n_tiles, dtype=jnp.int32) * ts)[:, None]
    lo = jnp.maximum(t0, offs[None, :])
    hi = jnp.minimum(t0 + ts, (offs + counts)[None, :])
    nonempty = (hi > lo).reshape(-1)
    nw = jnp.sum(nonempty.astype(jnp.int32))
    idx = jnp.nonzero(nonempty, size=n_work, fill_value=0)[0].astype(jnp.int32)
    idx = jnp.where(jnp.arange(n_work) < nw, idx, idx[jnp.maximum(nw - 1, 0)])
    wt = idx // N_EXPERTS
    we = idx % N_EXPERTS
    wlo = lo.reshape(-1)[idx] - wt * ts
    whi = hi.reshape(-1)[idx] - wt * ts
    wfirst = jnp.concatenate([jnp.ones((1,), jnp.int32), (wt[1:] != wt[:-1]).astype(jnp.int32)])
    return wt, we, wlo, whi, wfirst, nw.reshape(1)


def _combine_kernel(slots_ref, slots_next_ref, h_ref, route_ref, gfin_ref, ys_hbm, yp_ref, ysm_ref, gbuf, sems,
                    *, n_prompt_tiles):
    i = pl.program_id(0)
    n = pl.num_programs(0)
    tm = h_ref.shape[0]

    def gather(s_ref, buf):
        for r in range(tm):
            pltpu.make_async_copy(ys_hbm.at[pl.ds(s_ref[0, 0, r], 1)], gbuf.at[buf, 0, pl.ds(r, 1)],
                                  sems.at[buf]).start(priority=0)
            pltpu.make_async_copy(ys_hbm.at[pl.ds(s_ref[0, 0, tm + r], 1)], gbuf.at[buf, 1, pl.ds(r, 1)],
                                  sems.at[buf]).start(priority=1)

    cur = i % 2

    @pl.when(i == 0)
    def _():
        gather(slots_ref, 0)

    @pl.when(i + 1 < n)
    def _():
        gather(slots_next_ref, 1 - cur)

    for k in range(2):
        pltpu.make_async_copy(ys_hbm.at[pl.ds(0, tm)], gbuf.at[cur, k], sems.at[cur]).wait()
    route = route_ref[...]
    lane = lax.broadcasted_iota(jnp.int32, route.shape, 1)
    w1 = jnp.sum(jnp.where(lane == ROUTE_W1, route, 0.0), axis=1, keepdims=True)
    w2 = jnp.sum(jnp.where(lane == ROUTE_W2, route, 0.0), axis=1, keepdims=True)
    hf = h_ref[...] + (w1 * gbuf[cur, 0] + w2 * gbuf[cur, 1])
    y = hf * lax.rsqrt(jnp.mean(hf * hf, axis=-1, keepdims=True) + NORM_EPS) * gfin_ref[...]

    @pl.when(i < n_prompt_tiles)
    def _():
        yp_ref[...] = y

    @pl.when(i >= n_prompt_tiles)
    def _():
        ysm_ref[...] = y


def _combine(h, route, slots, ys, gfin, n_prompt_tiles):
    t = h.shape[0]
    tm = MOE_TM
    n_p = n_prompt_tiles
    return pl.pallas_call(
        functools.partial(_combine_kernel, n_prompt_tiles=n_p),
        grid_spec=pltpu.PrefetchScalarGridSpec(
            num_scalar_prefetch=0,
            grid=(t // tm,),
            in_specs=[pl.BlockSpec((1, 1, 2 * tm), lambda i: (i, 0, 0), memory_space=pltpu.SMEM),
                      pl.BlockSpec((1, 1, 2 * tm), lambda i: (jnp.minimum(i + 1, t // tm - 1), 0, 0),
                                   memory_space=pltpu.SMEM),
                      pl.BlockSpec((tm, D_MODEL), lambda i: (i, 0)),
                      pl.BlockSpec((tm, ROUTER_LANES), lambda i: (i, 0)),
                      pl.BlockSpec((1, D_MODEL), lambda i: (0, 0)),
                      pl.BlockSpec(memory_space=pl.ANY)],
            out_specs=[pl.BlockSpec((tm, D_MODEL), lambda i: (jnp.minimum(i, n_p - 1), 0)),
                       pl.BlockSpec((tm, D_MODEL), lambda i: (0, 0))],
            scratch_shapes=[pltpu.VMEM((2, 2, tm, D_MODEL), F32), pltpu.SemaphoreType.DMA((2,))],
        ),
        out_shape=[jax.ShapeDtypeStruct((n_p * tm, D_MODEL), F32), jax.ShapeDtypeStruct((tm, D_MODEL), F32)],
        compiler_params=_cparams("arbitrary"),
        name="moe_combine",
    )(slots, slots, h, route, gfin, ys)


def _v_tiles(v, tblk):
    b, t, _ = v.shape
    x = v.reshape(b, t // tblk, tblk, 2, RWKV_PAIRS, RWKV_HEAD).transpose(0, 1, 4, 5, 3, 2)
    x = jnp.pad(x, ((0, 0),) * 5 + ((0, RWKV_HEAD - tblk),))
    return x.reshape(b, t // tblk, RWKV_PAIRS, RWKV_HEAD, LANE).astype(BF16)


def kernel(x_prompt, x_sample, state_gla, state_rwkv, state_shift, meta_tokens, norm_mix, w_in, gla_gate_w2,
           gla_gate_b, gla_norm, rwkv_mu, rwkv_w0, rwkv_w2, rwkv_a0, rwkv_a2, rwkv_g2, rwkv_kk, rwkv_ka, rwkv_rk,
           rwkv_ln_w, rwkv_ln_b, w_out, norm_ffn, router_group_w, router_group_b, router_expert_w,
           router_expert_b, moe_w1, moe_w3, moe_w2, norm_final):
    bp, tp, _ = x_prompt.shape
    bs, ts, _ = x_sample.shape
    assert state_gla.shape[0] == 1, "one layer"
    lyr = 0

    w_in_l = w_in[lyr]
    wg = jnp.pad(w_in_l[:, :GLA_COLS], ((0, 0), (0, GLA_PCOLS - GLA_COLS))).astype(BF16)
    wr = w_in_l[:, GLA_COLS:].astype(BF16)
    g_mix = norm_mix[lyr][None, :]
    gw2p = jnp.pad(gla_gate_w2[lyr], ((0, LANE - GLA_GATE_RANK), (0, 0)))
    gb = gla_gate_b[lyr][None, :]
    gn = gla_norm[lyr][None, :]
    w2p = jnp.pad(rwkv_w2[lyr], ((0, 64), (0, 0))).astype(BF16)
    a2p = jnp.pad(rwkv_a2[lyr], ((64, 0), (0, 0))).astype(BF16)
    pre_params = (rwkv_mu[lyr][None, :], rwkv_w0[lyr][None, :], w2p, rwkv_a0[lyr][None, :], a2p,
                  rwkv_g2[lyr].astype(BF16), rwkv_kk[lyr][None, :], rwkv_ka[lyr][None, :],
                  rwkv_rk[lyr].reshape(1, RWKV_WIDTH), _head_selector())
    lnw = rwkv_ln_w[lyr][None, :]
    lnb = rwkv_ln_b[lyr][None, :]
    wo = w_out[lyr].astype(BF16)
    gffn = norm_ffn[lyr][None, :]
    n_used = N_GROUPS + N_EXPERTS
    w_router = jnp.pad(
        jnp.concatenate([router_group_w[lyr],
                         router_expert_w[lyr].transpose(1, 0, 2).reshape(D_MODEL, N_EXPERTS)], axis=1),
        ((0, 0), (0, ROUTER_LANES - n_used)))
    b_router = jnp.pad(jnp.concatenate([router_group_b[lyr], router_expert_b[lyr].reshape(N_EXPERTS)]),
                       (0, ROUTER_LANES - n_used))[None, :]
    gfin = norm_final[None, :]

    pg_m, pr_m = _inproj(meta_tokens, g_mix, wg, wr, N_META)
    _, sg_m = _gla(pg_m[None], jnp.zeros((1, GLA_HEADS, GLA_DK, GLA_DV), F32), gw2p, gb, gn,
                   bb=1, chunk=N_META, sub=N_META, t_valid=N_META)
    r, w, k, kk, kka, v_m, _, _ = _rwkv_pre(pr_m[None], jnp.zeros((1, 1, RWKV_COLS), F32), pre_params,
                                            tm=N_META, explicit_prev=False, emit_vt=False)
    _, sr_m = _rwkv_rec(r, w, k, kk, kka, _v_tiles(v_m, N_META),
                        jnp.zeros((1, RWKV_HEADS, RWKV_HEAD, RWKV_HEAD), F32), bb=1, n_steps=N_META)

    xp = x_prompt.reshape(bp * tp, D_MODEL)
    pg_p, pr_p = _inproj(xp, g_mix, wg, wr, INPROJ_TM)
    og_p, sg_p = _gla(pg_p.reshape(bp, tp, GLA_PCOLS), jnp.broadcast_to(sg_m, (bp,) + sg_m.shape[1:]), gw2p, gb, gn,
                      bb=SEQ_BLOCK, chunk=GLA_CHUNK, sub=GLA_SUB, t_valid=GLA_CHUNK)
    pr_p3 = pr_p.reshape(bp, tp, RWKV_COLS)
    first_prev = jnp.broadcast_to(pr_m[N_META - 1][None, None, :], (bp, 1, RWKV_COLS))
    r, w, k, kk, kka, vt_p, bv_p, gate_p = _rwkv_pre(pr_p3, first_prev, pre_params, tm=PRE_TM, explicit_prev=False,
                                                     emit_vt=True)
    y_p, sr_p = _rwkv_rec(r, w, k, kk, kka, vt_p, jnp.broadcast_to(sr_m, (bp,) + sr_m.shape[1:]),
                          bb=bp, n_steps=REC_TB)
    prompt_rows = (xp, og_p.reshape(bp * tp, GLA_WIDTH), y_p.reshape(tp, bp * RWKV_WIDTH),
                   bv_p.reshape(bp * tp, RWKV_WIDTH), gate_p.reshape(bp * tp, RWKV_WIDTH))

    xs = x_sample.transpose(1, 0, 2).reshape(ts * bs, D_MODEL)
    pg_s, pr_s = _inproj(xs, g_mix, wg, wr, bs * ts)
    ts_pad = 8
    pg_s3 = jnp.pad(pg_s.reshape(ts, bs, GLA_PCOLS).transpose(1, 0, 2), ((0, 0), (0, ts_pad - ts), (0, 0)))
    og_s, sg_s = _gla(pg_s3, state_gla[lyr], gw2p, gb, gn, bb=SEQ_BLOCK, chunk=ts_pad, sub=ts_pad, t_valid=ts)
    og_s = og_s[:, :ts].transpose(1, 0, 2).reshape(ts * bs, GLA_WIDTH)
    pr_s3 = pr_s.reshape(ts, bs, RWKV_COLS)
    prev_s = jnp.concatenate([state_shift[lyr][None], pr_s3[:-1]], axis=0)
    r, w, k, kk, kka, v_s, bv_s, gate_s = _rwkv_pre(pr_s3.reshape(1, bs * ts, RWKV_COLS),
                                                     prev_s.reshape(1, bs * ts, RWKV_COLS), pre_params,
                                                     tm=bs * ts, explicit_prev=True, emit_vt=False, pair_out=False)
    y_s, sr_s = _rwkv_lanes(r[0], w[0], k[0], kk[0], kka[0], v_s[0], state_rwkv[lyr].transpose(1, 2, 3, 0),
                            n_steps=ts)
    sr_s = sr_s.transpose(3, 0, 1, 2)
    sample_rows = (xs, og_s, y_s, bv_s[0], gate_s[0])

    h_all, n2_all, route, route_t, counts = _mix_router(prompt_rows, sample_rows, lnw, lnb, wo, gffn, w_router,
                                                        b_router, seq_tiles=tp // MOE_TM)
    n_tok = h_all.shape[0]
    n_p_tiles = (bp * tp) // MOE_TM
    cnt = counts[0, EXPERT_LANE0:EXPERT_LANE0 + N_EXPERTS].astype(jnp.int32)
    offs = jnp.cumsum(cnt) - cnt
    eid = route_t[:, ROUTE_E1:ROUTE_E2 + 1, :].astype(jnp.int32)
    pos = route_t[:, ROUTE_P1:ROUTE_P2 + 1, :].astype(jnp.int32)
    off = sum(jnp.where(eid == e, offs[e], 0) for e in range(N_EXPERTS))
    slots = (off + pos).reshape(n_tok // MOE_TM, 1, 2 * MOE_TM)
    xs_sorted = _dispatch(n2_all, slots)
    ys_sorted = _experts(xs_sorted, _expert_work_items(cnt, 2 * n_tok), moe_w1[lyr], moe_w3[lyr], moe_w2[lyr])
    y_prompt, y_sample = _combine(h_all, route, slots, ys_sorted, gfin, n_p_tiles)
    y_prompt = y_prompt.reshape(bp, tp, D_MODEL)
    y_sample = y_sample.reshape(ts, bs, D_MODEL).transpose(1, 0, 2)

    return (y_prompt, y_sample,
            sg_p[None], sr_p[None], pr_p3[:, -1][None],
            sg_s[None], sr_s[None], pr_s3[ts - 1][None])
```

```python
import functools

import jax
import jax.numpy as jnp
from jax import lax
from jax.experimental import pallas as pl
from jax.experimental.pallas import tpu as pltpu

F32 = jnp.float32
BF16 = jnp.bfloat16
HIGHEST = lax.Precision.HIGHEST

D_MODEL = 1024
N_META = 16
NORM_EPS = 1e-6
LOG2E = 1.4426950408889634
GLA_HEADS = 4
GLA_DK = 64
GLA_DV = 128
GLA_QK = GLA_HEADS * GLA_DK
GLA_WIDTH = GLA_HEADS * GLA_DV
GLA_GATE_RANK = 16
GLA_GATE_NORM = 16.0
GLA_CHUNK = 64
GLA_SUB = 8
GLA_COLS = 2 * GLA_QK + 2 * GLA_WIDTH + GLA_GATE_RANK
GLA_PCOLS = 2 * GLA_QK + 2 * GLA_WIDTH + 128
RWKV_WIDTH = 512
RWKV_HEAD = 64
RWKV_HEADS = 8
RWKV_PAIRS = RWKV_HEADS // 2
RWKV_DECAY_SCALE = 0.606531
RWKV_GN_EPS = 64e-5
RWKV_COLS = 3 * RWKV_WIDTH + 64 + 64 + 128
REC_TB = 64
REC_UNROLL = 8
INPROJ_TM = 512
PRE_TM = 256
SEQ_BLOCK = 8
N_GROUPS = 4
EXPERTS_PER_GROUP = 8
N_EXPERTS = 32
D_EXPERT = 512
ROUTER_LANES = 128
EXPERT_LANE0 = N_GROUPS
ROUTE_E1, ROUTE_E2, ROUTE_W1, ROUTE_W2, ROUTE_P1, ROUTE_P2 = range(6)
MOE_TM = 512
MOE_TS = 512

LANE = 128
VMEM_LIMIT = 56 * 1024 * 1024


def _cparams(*sem):
    return pltpu.CompilerParams(dimension_semantics=sem, vmem_limit_bytes=VMEM_LIMIT)


def _block_ones(n, blk):
    i = jnp.arange(n)
    return (i[:, None] // blk == i[None, :] // blk).astype(BF16)


def _sigmoid(x):
    return 1.0 / (1.0 + jnp.exp(-x))


def _dot(a, b):
    return jnp.dot(a, b, preferred_element_type=F32)


def _dot_nt(a, b):
    return lax.dot_general(a, b, (((1,), (1,)), ((), ())), preferred_element_type=F32)


def _dot_tn(a, b):
    return lax.dot_general(a, b, (((0,), (0,)), ((), ())), preferred_element_type=F32)


def _split2(x):
    hi = x.astype(BF16)
    lo = (x - hi.astype(F32)).astype(BF16)
    return hi, lo


def _head_selector():
    return (jnp.arange(RWKV_WIDTH)[:, None] // RWKV_HEAD == jnp.arange(LANE)[None, :]).astype(BF16)


def _group_sum(x, sel):
    hi, lo = _split2(x)
    s_hi, s_lo = _split2(_dot(hi, sel) + _dot(lo, sel))
    return _dot_nt(s_hi, sel) + _dot_nt(s_lo, sel)


def _inproj_kernel(x_ref, g_ref, wg_ref, wr_ref, pg_ref, pr_ref):
    x = x_ref[...]
    n = x * lax.rsqrt(jnp.mean(x * x, axis=-1, keepdims=True) + NORM_EPS) * g_ref[...]
    nb = n.astype(BF16)
    pg_ref[...] = _dot(nb, wg_ref[...])
    pr_ref[...] = _dot(nb, wr_ref[...])


def _inproj(x, g, wg, wr, tm):
    t = x.shape[0]
    return pl.pallas_call(
        _inproj_kernel,
        grid=(t // tm,),
        in_specs=[
            pl.BlockSpec((tm, D_MODEL), lambda i: (i, 0)),
            pl.BlockSpec((1, D_MODEL), lambda i: (0, 0)),
            pl.BlockSpec((D_MODEL, GLA_PCOLS), lambda i: (0, 0)),
            pl.BlockSpec((D_MODEL, RWKV_COLS), lambda i: (0, 0)),
        ],
        out_specs=[
            pl.BlockSpec((tm, GLA_PCOLS), lambda i: (i, 0)),
            pl.BlockSpec((tm, RWKV_COLS), lambda i: (i, 0)),
        ],
        out_shape=[jax.ShapeDtypeStruct((t, GLA_PCOLS), F32), jax.ShapeDtypeStruct((t, RWKV_COLS), F32)],
        compiler_params=_cparams("parallel"),
        name="inproj",
    )(x, g, wg, wr)


def _gla_kernel(pg_ref, s0_ref, gw2_ref, gb_ref, gn_ref, bo_ref, tril_ref, o_ref, sout_ref, s_scr,
                *, bb, chunk, sub, t_valid):
    ci = pl.program_id(1)

    @pl.when(ci == 0)
    def _():
        s_scr[...] = s0_ref[...]

    bo = bo_ref[...]
    tril = tril_ref[...]
    lane = lax.broadcasted_iota(jnp.int32, (sub, LANE), 1) & (GLA_DK - 1)
    rowi = lax.broadcasted_iota(jnp.int32, (sub, LANE), 0)
    head0_s = lax.broadcasted_iota(jnp.int32, (sub, LANE), 1) < GLA_DK
    head0_c = lax.broadcasted_iota(jnp.int32, (chunk, LANE), 1) < GLA_DK

    gl_all = pg_ref[:, :, 2 * GLA_QK + 2 * GLA_WIDTH:].reshape(bb * chunk, LANE)
    z_all = jnp.dot(gl_all, gw2_ref[...], precision=HIGHEST, preferred_element_type=F32) + gb_ref[...]
    lg_all = (jnp.minimum(z_all, 0.0) - jnp.log1p(jnp.exp(-jnp.abs(z_all)))) * (LOG2E / GLA_GATE_NORM)
    if t_valid < chunk:
        rows = lax.broadcasted_iota(jnp.int32, lg_all.shape, 0) & (chunk - 1)
        lg_all = jnp.where(rows < t_valid, lg_all, 0.0)
    if tril.shape[0] == bb * chunk:
        b_all = jnp.dot(tril, lg_all, precision=HIGHEST, preferred_element_type=F32)

    for bi in range(bb):
        pg = pg_ref[bi]
        q = pg[:, 0:GLA_QK] * (GLA_DK ** -0.5)
        k = pg[:, GLA_QK:2 * GLA_QK]
        v = pg[:, 2 * GLA_QK:2 * GLA_QK + GLA_WIDTH]
        g = pg[:, 2 * GLA_QK + GLA_WIDTH:2 * GLA_QK + 2 * GLA_WIDTH]
        if tril.shape[0] == bb * chunk:
            b = b_all[bi * chunk:(bi + 1) * chunk]
        else:
            b = jnp.dot(tril, lg_all[bi * chunk:(bi + 1) * chunk], precision=HIGHEST, preferred_element_type=F32)
        eb = jnp.exp2(b)
        blast = b[chunk - 1:chunk, :]
        kl = k * jnp.exp2(blast - b)
        qe = q * eb

        n_blk = chunk // sub
        n_pairs = GLA_HEADS // 2
        ps = []
        for hp in range(n_pairs):
            sl = slice(hp * LANE, (hp + 1) * LANE)
            for blk in range(n_blk):
                rs = slice(blk * sub, (blk + 1) * sub)
                qb, kb, bbk = q[rs, sl], k[rs, sl], b[rs, sl]
                for j in range(sub):
                    ps.append(qb * (kb[j:j + 1] * jnp.exp2(jnp.minimum(bbk - bbk[j:j + 1], 0.0))))
        red = _dot(jnp.concatenate(ps, axis=0).astype(BF16), bo)

        o_heads = []
        for hp in range(n_pairs):
            sl = slice(hp * LANE, (hp + 1) * LANE)
            kp, bp = k[:, sl], b[:, sl]
            row_blocks = []
            for blk in range(n_blk):
                rs = slice(blk * sub, (blk + 1) * sub)
                base = (hp * n_blk + blk) * sub * sub
                a = jnp.zeros((sub, LANE), F32)
                for j in range(sub):
                    a = jnp.where((lane == blk * sub + j) & (rowi >= j), red[base + j * sub:base + (j + 1) * sub], a)
                if blk > 0:
                    bref = bp[blk * sub - 1:blk * sub]
                    qt = q[rs, sl] * jnp.exp2(bp[rs] - bref)
                    kt = (kp * jnp.exp2(jnp.minimum(bref - bp, 0.0))).astype(BF16)
                    qt2 = jnp.concatenate([jnp.where(head0_s, qt, 0.0), jnp.where(head0_s, 0.0, qt)], axis=0)
                    off2 = _dot_nt(qt2.astype(BF16), kt)
                    a = jnp.where(lane < blk * sub, jnp.concatenate([off2[:sub], off2[sub:]], axis=1), a)
                row_blocks.append(a)
            a_pair = row_blocks[0] if n_blk == 1 else jnp.concatenate(row_blocks, axis=0)
            v0 = v[:, 2 * hp * GLA_DV:(2 * hp + 1) * GLA_DV]
            v1 = v[:, (2 * hp + 1) * GLA_DV:(2 * hp + 2) * GLA_DV]
            s_pair = s_scr[bi, 2 * hp:2 * hp + 2].reshape(2 * GLA_DK, GLA_DV)
            qe_p, kl_p = qe[:, sl], kl[:, sl]

            def by_head(x):
                return jnp.concatenate([jnp.where(head0_c, x, 0.0), jnp.where(head0_c, 0.0, x)], axis=0)

            if chunk == GLA_DK:
                v_rows = jnp.concatenate([v0, v1], axis=0)
            else:
                zpad = jnp.zeros((GLA_DK - chunk, GLA_DV), F32)
                v_rows = jnp.concatenate([v0, zpad, v1, zpad], axis=0)
            lhs = jnp.concatenate([by_head(a_pair), by_head(qe_p)], axis=1).astype(BF16)
            rhs = jnp.concatenate([v_rows, s_pair], axis=0).astype(BF16)
            o2 = _dot(lhs, rhs)
            upd = _dot_tn(by_head(kl_p).astype(BF16), jnp.concatenate([v0, v1], axis=0).astype(BF16))
            dcol = jnp.broadcast_to(jnp.exp2(blast[:, sl]), (8, LANE)).T[:, 0:1]
            s_new = dcol * s_pair + upd
            s_scr[bi, 2 * hp] = s_new[:GLA_DK]
            s_scr[bi, 2 * hp + 1] = s_new[GLA_DK:]
            for h2 in range(2):
                o_h = o2[h2 * chunk:(h2 + 1) * chunk]
                o_heads.append(o_h * lax.rsqrt(jnp.mean(o_h * o_h, axis=-1, keepdims=True) + NORM_EPS) * gn_ref[...])
        o = jnp.concatenate(o_heads, axis=1)
        o_ref[bi] = o * (g * _sigmoid(g))

    @pl.when(ci == pl.num_programs(1) - 1)
    def _():
        sout_ref[...] = s_scr[...]


def _gla(pg, s0, gw2p, gb, gn, *, bb, chunk, sub, t_valid):
    b, t, _ = pg.shape
    assert chunk & (chunk - 1) == 0
    tril = jnp.tril(jnp.ones((chunk, chunk), F32))
    if bb * chunk <= LANE:
        tril = jnp.kron(jnp.eye(bb, dtype=F32), tril)
    kern = functools.partial(_gla_kernel, bb=bb, chunk=chunk, sub=sub, t_valid=t_valid)
    return pl.pallas_call(
        kern,
        grid=(b // bb, t // chunk),
        in_specs=[
            pl.BlockSpec((bb, chunk, GLA_PCOLS), lambda i, j: (i, j, 0)),
            pl.BlockSpec((bb, GLA_HEADS, GLA_DK, GLA_DV), lambda i, j: (i, 0, 0, 0)),
            pl.BlockSpec((LANE, GLA_QK), lambda i, j: (0, 0)),
            pl.BlockSpec((1, GLA_QK), lambda i, j: (0, 0)),
            pl.BlockSpec((1, GLA_DV), lambda i, j: (0, 0)),
            pl.BlockSpec((LANE, LANE), lambda i, j: (0, 0)),
            pl.BlockSpec(tril.shape, lambda i, j: (0, 0)),
        ],
        out_specs=[
            pl.BlockSpec((bb, chunk, GLA_WIDTH), lambda i, j: (i, j, 0)),
            pl.BlockSpec((bb, GLA_HEADS, GLA_DK, GLA_DV), lambda i, j: (i, 0, 0, 0)),
        ],
        out_shape=[jax.ShapeDtypeStruct((b, t, GLA_WIDTH), F32),
                   jax.ShapeDtypeStruct((b, GLA_HEADS, GLA_DK, GLA_DV), F32)],
        scratch_shapes=[pltpu.VMEM((bb, GLA_HEADS, GLA_DK, GLA_DV), F32)],
        compiler_params=_cparams("parallel", "arbitrary"),
        name="gla_chunk",
    )(pg, s0, gw2p, gb, gn, _block_ones(LANE, GLA_DK), tril)


def _rwkv_pre_kernel(pr_ref, aux_ref, mu_ref, w0_ref, w2_ref, a0_ref, a2_ref, g2_ref, kk_ref, ka_ref, rk_ref, bo_ref,
                     r_out, w_out, k_out, kkn_out, kka_out, v_out, bv_out, gate_out, carry_scr,
                     *, tm, explicit_prev, emit_vt, pair_out):
    pr = pr_ref[0]
    if explicit_prev:
        prev = aux_ref[0]
    else:
        j = pl.program_id(1)
        row0 = jnp.where(j == 0, aux_ref[0], carry_scr[...])
        rows = lax.broadcasted_iota(jnp.int32, pr.shape, 0)
        prev = jnp.where(rows == 0, row0, pltpu.roll(pr, 1, 0))
        carry_scr[...] = pr[tm - 1:tm, :]
    xm = pr + (prev - pr) * mu_ref[...]
    wd = RWKV_WIDTH
    rr, rk, rv = xm[:, 0:wd], xm[:, wd:2 * wd], xm[:, 2 * wd:3 * wd]
    wa = xm[:, 3 * wd:3 * wd + LANE]
    gl2 = xm[:, 3 * wd + LANE:3 * wd + 2 * LANE]
    logw = -RWKV_DECAY_SCALE * _sigmoid(w0_ref[...] + _dot(jnp.tanh(wa).astype(BF16), w2_ref[...]))
    aa = _sigmoid(a0_ref[...] + _dot(wa.astype(BF16), a2_ref[...]))
    gate = _dot(_sigmoid(gl2).astype(BF16), g2_ref[...])
    bo = bo_ref[...]
    kk = rk * kk_ref[...]
    kk = kk / jnp.maximum(jnp.sqrt(_group_sum(kk * kk, bo)), 1e-12)
    k = rk * (1.0 + (aa - 1.0) * ka_ref[...])
    bv = _group_sum(rr * k * rk_ref[...], bo) * rv
    w = jnp.exp(logw)
    kka = kk * aa
    hd = RWKV_HEAD

    def pair(x, hp):
        return jnp.concatenate([x[:, hp * hd:(hp + 1) * hd], x[:, (hp + RWKV_PAIRS) * hd:(hp + RWKV_PAIRS + 1) * hd]],
                               axis=1)

    if pair_out:
        for hp in range(RWKV_PAIRS):
            r_out[0, hp] = pair(rr, hp)
            w_out[0, hp] = pair(w, hp)
            k_out[0, hp] = pair(k, hp)
            kkn_out[0, hp] = pair(kk, hp)
            kka_out[0, hp] = pair(kka, hp)
    else:
        r_out[0], w_out[0], k_out[0], kkn_out[0], kka_out[0] = rr, w, k, kk, kka
    if emit_vt:
        vt = rv.T
        for tb in range(tm // REC_TB):
            ts = slice(tb * REC_TB, (tb + 1) * REC_TB)
            for hp in range(RWKV_PAIRS):
                lo, hi = hp * hd, (hp + RWKV_PAIRS) * hd
                v_out[0, tb, hp] = jnp.concatenate([vt[lo:lo + hd, ts], vt[hi:hi + hd, ts]], axis=1).astype(BF16)
    else:
        v_out[0] = rv
    bv_out[0] = bv
    gate_out[0] = gate


def _rwkv_pre(pr, aux, params, *, tm, explicit_prev, emit_vt, pair_out=True):
    b, t, _ = pr.shape
    kern = functools.partial(_rwkv_pre_kernel, tm=tm, explicit_prev=explicit_prev, emit_vt=emit_vt,
                             pair_out=pair_out)
    aux_spec = (pl.BlockSpec((1, tm, RWKV_COLS), lambda i, j: (i, j, 0)) if explicit_prev
                else pl.BlockSpec((1, 1, RWKV_COLS), lambda i, j: (i, 0, 0)))
    const = lambda shape: pl.BlockSpec(shape, lambda i, j: (0,) * len(shape))
    pair_spec = pl.BlockSpec((1, RWKV_PAIRS, tm, LANE), lambda i, j: (i, 0, j, 0))
    row_spec = pl.BlockSpec((1, tm, RWKV_WIDTH), lambda i, j: (i, j, 0))
    pair_shape = jax.ShapeDtypeStruct((b, RWKV_PAIRS, t, LANE), F32)
    row_shape = jax.ShapeDtypeStruct((b, t, RWKV_WIDTH), F32)
    if emit_vt:
        v_spec = pl.BlockSpec((1, tm // REC_TB, RWKV_PAIRS, RWKV_HEAD, LANE), lambda i, j: (i, j, 0, 0, 0))
        v_shape = jax.ShapeDtypeStruct((b, t // REC_TB, RWKV_PAIRS, RWKV_HEAD, LANE), BF16)
    else:
        v_spec, v_shape = row_spec, row_shape
    if not pair_out:
        pair_spec, pair_shape = row_spec, row_shape
    return pl.pallas_call(
        kern,
        grid=(b, t // tm),
        in_specs=[
            pl.BlockSpec((1, tm, RWKV_COLS), lambda i, j: (i, j, 0)),
            aux_spec,
            const((1, RWKV_COLS)), const((1, RWKV_WIDTH)), const((LANE, RWKV_WIDTH)), const((1, RWKV_WIDTH)),
            const((LANE, RWKV_WIDTH)), const((LANE, RWKV_WIDTH)), const((1, RWKV_WIDTH)), const((1, RWKV_WIDTH)),
            const((1, RWKV_WIDTH)), const((RWKV_WIDTH, LANE)),
        ],
        out_specs=[pair_spec] * 5 + [v_spec, row_spec, row_spec],
        out_shape=[pair_shape] * 5 + [v_shape, row_shape, row_shape],
        scratch_shapes=[pltpu.VMEM((1, RWKV_COLS), F32)],
        compiler_params=_cparams("parallel", "arbitrary"),
        name="rwkv_pre",
    )(pr, aux, *params)


def _rwkv_rec_kernel(r_ref, w_ref, k_ref, kk_ref, kka_ref, vt_ref, s0_ref, bo_ref, vsel_ref, ysel_ref,
                     y_ref, sout_ref, s_scr, t1_scr, t3_scr, yt_scr, *, bb, n_steps):
    tb = pl.program_id(1)
    nc = bb * RWKV_PAIRS
    hd = RWKV_HEAD

    @pl.when(tb == 0)
    def _():
        for c in range(nc):
            bi, hp = divmod(c, RWKV_PAIRS)
            s_scr[c] = jnp.concatenate([s0_ref[bi, hp], s0_ref[bi, hp + RWKV_PAIRS]], axis=1)

    bo = bo_ref[...]

    def step(t, u):
        row = pl.ds(t, 1)
        for c in range(nc):
            bi, hp = divmod(c, RWKV_PAIRS)
            t1_scr[c * hd:(c + 1) * hd, :] = (s_scr[c] * kk_ref[bi, hp, row, :]).astype(BF16)
        sab = _dot(t1_scr[...], bo)
        vb = _dot(vt_ref[...].reshape(nc * hd, LANE), vsel_ref[t])
        for c in range(nc):
            bi, hp = divmod(c, RWKV_PAIRS)
            rs = slice(c * hd, (c + 1) * hd)
            s2 = (s_scr[c] * w_ref[bi, hp, row, :] - sab[rs] * kka_ref[bi, hp, row, :]
                  + vb[rs] * k_ref[bi, hp, row, :])
            s_scr[c] = s2
            t3_scr[rs, :] = (s2 * r_ref[bi, hp, row, :]).astype(BF16)
        yt_scr[...] += _dot_nt(ysel_ref[u], t3_scr[...])

    n_inner = min(8, n_steps)

    def block8(t8, carry):
        yt_scr[...] = jnp.zeros(yt_scr.shape, F32)

        def inner(u, c2):
            step(t8 * 8 + u, u)
            return c2

        lax.fori_loop(0, n_inner, inner, 0, unroll=REC_UNROLL)
        t0 = pl.multiple_of(t8 * 8, 8)
        blk = RWKV_PAIRS * hd
        for bi in range(bb):
            for h2 in range(2):
                y_ref[0, pl.ds(t0, 8), bi * RWKV_WIDTH + h2 * blk:bi * RWKV_WIDTH + (h2 + 1) * blk] = (
                    yt_scr[h2 * 8:(h2 + 1) * 8, bi * blk:(bi + 1) * blk])
        return carry

    lax.fori_loop(0, (n_steps + 7) // 8, block8, 0)

    @pl.when(tb == pl.num_programs(1) - 1)
    def _():
        for c in range(nc):
            bi, hp = divmod(c, RWKV_PAIRS)
            s_c = s_scr[c]
            sout_ref[bi, hp] = s_c[:, :RWKV_HEAD]
            sout_ref[bi, hp + RWKV_PAIRS] = s_c[:, RWKV_HEAD:]


def _rwkv_rec(r, w, k, kk, kka, vt, s0, *, bb, n_steps):
    b, _, t, _ = r.shape
    tblk = min(REC_TB, t)
    ntb = t // tblk
    nc = bb * RWKV_PAIRS
    lane = jnp.arange(LANE)
    vsel = ((lane[None, :, None] // RWKV_HEAD == lane[None, None, :] // RWKV_HEAD)
            & (lane[None, :, None] % RWKV_HEAD == jnp.arange(RWKV_HEAD)[:, None, None])).astype(BF16)
    ysel = (jnp.arange(16)[None, :, None]
            == 8 * (lane[None, None, :] // RWKV_HEAD) + jnp.arange(8)[:, None, None]).astype(BF16)
    kern = functools.partial(_rwkv_rec_kernel, bb=bb, n_steps=n_steps)
    pair_spec = pl.BlockSpec((bb, RWKV_PAIRS, tblk, LANE), lambda i, j: (i, 0, j, 0))
    state_spec = pl.BlockSpec((bb, RWKV_HEADS, RWKV_HEAD, RWKV_HEAD), lambda i, j: (i, 0, 0, 0))
    ytb = max(tblk, 8)
    return pl.pallas_call(
        kern,
        grid=(b // bb, ntb),
        in_specs=[pair_spec] * 5 + [
            pl.BlockSpec((bb, 1, RWKV_PAIRS, RWKV_HEAD, LANE), lambda i, j: (i, j, 0, 0, 0)),
            state_spec,
            pl.BlockSpec((LANE, LANE), lambda i, j: (0, 0)),
            pl.BlockSpec((RWKV_HEAD, LANE, LANE), lambda i, j: (0, 0, 0)),
            pl.BlockSpec((8, 16, LANE), lambda i, j: (0, 0, 0)),
        ],
        out_specs=[
            pl.BlockSpec((1, ytb, bb * RWKV_WIDTH), lambda i, j: (i, j, 0)),
            state_spec,
        ],
        out_shape=[jax.ShapeDtypeStruct((b // bb, ntb * ytb, bb * RWKV_WIDTH), F32),
                   jax.ShapeDtypeStruct((b, RWKV_HEADS, RWKV_HEAD, RWKV_HEAD), F32)],
        scratch_shapes=[pltpu.VMEM((nc, RWKV_HEAD, LANE), F32),
                        pltpu.VMEM((nc * RWKV_HEAD, LANE), BF16),
                        pltpu.VMEM((nc * RWKV_HEAD, LANE), BF16),
                        pltpu.VMEM((16, nc * RWKV_HEAD), F32)],
        compiler_params=_cparams("parallel", "arbitrary"),
        name="rwkv_rec",
    )(r, w, k, kk, kka, vt, s0, _block_ones(LANE, RWKV_HEAD), vsel, ysel)


def _rwkv_lanes_kernel(r_ref, w_ref, k_ref, kk_ref, kka_ref, v_ref, s0_ref, y_ref, sout_ref, vt_scr, yt_scr,
                       *, n_steps, n_seq):
    hd = RWKV_HEAD
    for t in range(n_steps):
        rows = slice(t * n_seq, (t + 1) * n_seq)
        r_t, w_t, k_t = r_ref[rows, :].T, w_ref[rows, :].T, k_ref[rows, :].T
        kk_t, kka_t = kk_ref[rows, :].T, kka_ref[rows, :].T
        vt_scr[...] = v_ref[rows, :].T
        src = s0_ref if t == 0 else sout_ref
        for h2 in range(2):
            hs = slice(h2 * hd, (h2 + 1) * hd)
            r_h, w_h, k_h, kk_h, kka_h = r_t[hs], w_t[hs], k_t[hs], kk_t[hs], kka_t[hs]

            def value_row(i, carry):
                s = src[h2, i]
                sab = jnp.sum(s * kk_h, axis=0, keepdims=True)
                v_i = vt_scr[pl.ds(h2 * hd + i, 1), :]
                s2 = s * w_h - sab * kka_h + v_i * k_h
                sout_ref[h2, i] = s2
                yt_scr[pl.ds(h2 * hd + i, 1), :] = jnp.sum(s2 * r_h, axis=0, keepdims=True)
                return carry

            lax.fori_loop(0, hd, value_row, 0, unroll=8)
        y_ref[rows, :] = yt_scr[...].T


def _rwkv_lanes(r, w, k, kk, kka, v, s0, *, n_steps):
    n_tok = r.shape[0]
    n_seq = n_tok // n_steps
    assert n_seq == LANE, "one lane per sequence"
    row_spec = pl.BlockSpec((n_tok, LANE), lambda hp: (0, hp))
    state_spec = pl.BlockSpec((2, RWKV_HEAD, RWKV_HEAD, n_seq), lambda hp: (hp, 0, 0, 0))
    return pl.pallas_call(
        functools.partial(_rwkv_lanes_kernel, n_steps=n_steps, n_seq=n_seq),
        grid=(RWKV_HEADS // 2,),
        in_specs=[row_spec] * 6 + [state_spec],
        out_specs=[row_spec, state_spec],
        out_shape=[jax.ShapeDtypeStruct((n_tok, RWKV_WIDTH), F32),
                   jax.ShapeDtypeStruct((RWKV_HEADS, RWKV_HEAD, RWKV_HEAD, n_seq), F32)],
        scratch_shapes=[pltpu.VMEM((LANE, n_seq), F32), pltpu.VMEM((LANE, n_seq), F32)],
        compiler_params=_cparams("parallel"),
        name="rwkv_lanes",
    )(r, w, k, kk, kka, v, s0)


def _mix_router_body(x_ref, og_ref, y_ref, bv_ref, gate_ref, lnw_ref, lnb_ref, wo_ref, gffn_ref, wr_hi_ref, wr_lo_ref,
                     br_ref, bo_ref, tril_ref, h_ref, n2_ref, route_ref, route_t_ref, cnt_scr):
    bo = bo_ref[...]
    y = y_ref[...]
    inv_n = 1.0 / RWKV_HEAD
    d = y - _group_sum(y, bo) * inv_n
    var = _group_sum(d * d, bo) * inv_n
    yn = d * lax.rsqrt(var + RWKV_GN_EPS) * lnw_ref[...] + lnb_ref[...] + bv_ref[...]
    o_rwkv = yn * gate_ref[...]
    mix = (_dot(og_ref[...].astype(BF16), wo_ref[0:GLA_WIDTH, :])
           + _dot(o_rwkv.astype(BF16), wo_ref[GLA_WIDTH:, :]))
    h = x_ref[...] + mix
    h_ref[...] = h
    n2 = h * lax.rsqrt(jnp.mean(h * h, axis=-1, keepdims=True) + NORM_EPS) * gffn_ref[...]
    n2_ref[...] = n2
    n2_hi, n2_lo = _split2(n2)
    lg = (_dot(n2_hi, wr_hi_ref[...]) + _dot(n2_hi, wr_lo_ref[...]) + _dot(n2_lo, wr_hi_ref[...])) + br_ref[...]
    neg = jnp.float32(-3.0e38)
    big = jnp.float32(1.0e9)
    lane = lax.broadcasted_iota(jnp.int32, lg.shape, 1).astype(F32)
    gmask = lane < N_GROUPS
    gmax = jnp.max(jnp.where(gmask, lg, neg), axis=1, keepdims=True)
    p_top = 1.0 / jnp.sum(jnp.where(gmask, jnp.exp(jnp.minimum(lg - gmax, 0.0)), 0.0), axis=1, keepdims=True)
    gidx = jnp.min(jnp.where(gmask & (lg == gmax), lane, big), axis=1, keepdims=True)
    e_lo = EXPERT_LANE0 + gidx * EXPERTS_PER_GROUP
    emask = (lane >= e_lo) & (lane < e_lo + EXPERTS_PER_GROUP)
    m1 = jnp.max(jnp.where(emask, lg, neg), axis=1, keepdims=True)
    e1 = jnp.min(jnp.where(emask & (lg == m1), lane, big), axis=1, keepdims=True)
    emask2 = emask & (lane != e1)
    m2 = jnp.max(jnp.where(emask2, lg, neg), axis=1, keepdims=True)
    e2 = jnp.min(jnp.where(emask2 & (lg == m2), lane, big), axis=1, keepdims=True)
    r21 = jnp.exp(m2 - m1)
    w1 = p_top / (1.0 + r21)
    w2 = p_top * r21 / (1.0 + r21)
    o1 = lane == e1
    o2 = lane == e2
    onehot = jnp.where(o1 | o2, 1.0, 0.0)
    rank = _dot(tril_ref[...], onehot.astype(BF16)) + cnt_scr[...]
    pos1 = jnp.sum(jnp.where(o1, rank, 0.0), axis=1, keepdims=True)
    pos2 = jnp.sum(jnp.where(o2, rank, 0.0), axis=1, keepdims=True)
    cnt_scr[...] += jnp.sum(onehot, axis=0, keepdims=True)
    route = jnp.where(lane == ROUTE_E1, e1 - EXPERT_LANE0, 0.0)
    route = jnp.where(lane == ROUTE_E2, e2 - EXPERT_LANE0, route)
    route = jnp.where(lane == ROUTE_W1, w1, route)
    route = jnp.where(lane == ROUTE_W2, w2, route)
    route = jnp.where(lane == ROUTE_P1, pos1, route)
    route = jnp.where(lane == ROUTE_P2, pos2, route)
    route_ref[...] = route
    route_t_ref[0] = route.T[0:8, :]


def _mix_router_kernel(*refs, n_prompt_tiles):
    prompt_rows, sample_rows, rest = refs[0:5], refs[5:10], refs[10:]
    consts, (h_ref, n2_ref, route_ref, route_t_ref, cnt_ref, cnt_scr) = rest[:9], rest[9:]
    i = pl.program_id(0)

    @pl.when(i == 0)
    def _():
        cnt_scr[...] = jnp.zeros(cnt_scr.shape, F32)

    @pl.when(i < n_prompt_tiles)
    def _():
        _mix_router_body(*prompt_rows, *consts, h_ref, n2_ref, route_ref, route_t_ref, cnt_scr)

    @pl.when(i >= n_prompt_tiles)
    def _():
        _mix_router_body(*sample_rows, *consts, h_ref, n2_ref, route_ref, route_t_ref, cnt_scr)

    cnt_ref[...] = cnt_scr[...]


def _mix_router(prompt_rows, sample_rows, lnw, lnb, wo, gffn, wr, br, *, seq_tiles):
    tm = MOE_TM
    n_p = prompt_rows[0].shape[0] // tm
    assert sample_rows[0].shape[0] == tm
    t = (n_p + 1) * tm
    widths = (D_MODEL, GLA_WIDTH, RWKV_WIDTH, RWKV_WIDTH, RWKV_WIDTH)
    p_specs = [pl.BlockSpec((tm, n), lambda i: (jnp.minimum(i, n_p - 1), 0)) for n in widths]
    p_specs[2] = pl.BlockSpec(
        (tm, RWKV_WIDTH), lambda i: (jnp.minimum(i, n_p - 1) % seq_tiles, jnp.minimum(i, n_p - 1) // seq_tiles))
    s_specs = [pl.BlockSpec((tm, n), lambda i: (0, 0)) for n in widths]
    const = lambda shape: pl.BlockSpec(shape, lambda i: (0,) * len(shape))
    row = lambda n: pl.BlockSpec((tm, n), lambda i: (i, 0))
    tril = jnp.tril(jnp.ones((tm, tm), F32), -1).astype(BF16)
    return pl.pallas_call(
        functools.partial(_mix_router_kernel, n_prompt_tiles=n_p),
        grid=(n_p + 1,),
        in_specs=p_specs + s_specs + [
            const((1, RWKV_WIDTH)), const((1, RWKV_WIDTH)), const((D_MODEL, D_MODEL)), const((1, D_MODEL)),
            const((D_MODEL, ROUTER_LANES)), const((D_MODEL, ROUTER_LANES)), const((1, ROUTER_LANES)),
            const((RWKV_WIDTH, LANE)),
            const((tm, tm))],
        out_specs=[row(D_MODEL), row(D_MODEL), row(ROUTER_LANES), pl.BlockSpec((1, 8, tm), lambda i: (i, 0, 0)),
                   const((1, ROUTER_LANES))],
        out_shape=[jax.ShapeDtypeStruct((t, D_MODEL), F32), jax.ShapeDtypeStruct((t, D_MODEL), F32),
                   jax.ShapeDtypeStruct((t, ROUTER_LANES), F32), jax.ShapeDtypeStruct((t // tm, 8, tm), F32),
                   jax.ShapeDtypeStruct((1, ROUTER_LANES), F32)],
        scratch_shapes=[pltpu.VMEM((1, ROUTER_LANES), F32)],
        compiler_params=_cparams("arbitrary"),
        name="mix_router",
    )(*prompt_rows, *sample_rows, lnw, lnb, wo, gffn, *_split2(wr), br, _head_selector(), tril)


def _dispatch_kernel(slots_ref, x_ref, xs_hbm, sem):
    tm = x_ref.shape[0]

    for r in range(tm):
        src = x_ref.at[pl.ds(r, 1)]
        pltpu.make_async_copy(src, xs_hbm.at[pl.ds(slots_ref[0, 0, r], 1)], sem).start(priority=0)
        pltpu.make_async_copy(src, xs_hbm.at[pl.ds(slots_ref[0, 0, tm + r], 1)], sem).start(priority=1)
    for _ in range(2):
        pltpu.make_async_copy(x_ref, xs_hbm.at[pl.ds(0, tm)], sem).wait()


def _dispatch(n2, slots):
    t = n2.shape[0]
    tm = MOE_TM
    return pl.pallas_call(
        _dispatch_kernel,
        grid_spec=pltpu.PrefetchScalarGridSpec(
            num_scalar_prefetch=0,
            grid=(t // tm,),
            in_specs=[pl.BlockSpec((1, 1, 2 * tm), lambda i: (i, 0, 0), memory_space=pltpu.SMEM),
                      pl.BlockSpec((tm, D_MODEL), lambda i: (i, 0))],
            out_specs=pl.BlockSpec(memory_space=pl.ANY),
            scratch_shapes=[pltpu.SemaphoreType.DMA(())],
        ),
        out_shape=jax.ShapeDtypeStruct((2 * t, D_MODEL), F32),
        compiler_params=_cparams("arbitrary"),
        name="moe_dispatch",
    )(slots, n2)


def _experts_kernel(wt_ref, we_ref, wlo_ref, whi_ref, wfirst_ref, nw_ref,
                    xs_ref, w1_ref, w3_ref, w2_ref, ys_ref, wb1, wb3, wb2):
    w = pl.program_id(0)

    @pl.when(w < nw_ref[0])
    def _():
        new_expert = jnp.logical_or(w == 0, we_ref[w] != we_ref[jnp.maximum(w - 1, 0)])

        @pl.when(new_expert)
        def _():
            wb1[...] = w1_ref[0].astype(BF16)
            wb3[...] = w3_ref[0].astype(BF16)
            wb2[...] = w2_ref[0].astype(BF16)

        x = xs_ref[...].astype(BF16)
        a = _dot(x, wb1[...])
        b = _dot(x, wb3[...])
        o = _dot(((a * _sigmoid(a)) * b).astype(BF16), wb2[...])

        @pl.when(wfirst_ref[w] == 1)
        def _():
            ys_ref[...] = o

        @pl.when(wfirst_ref[w] == 0)
        def _():
            rows = lax.broadcasted_iota(jnp.int32, o.shape, 0)
            ys_ref[...] = jnp.where((rows >= wlo_ref[w]) & (rows < whi_ref[w]), o, ys_ref[...])


def _experts(xs, work, w1, w3, w2):
    s = xs.shape[0]
    ts = MOE_TS
    n_work = work[0].shape[0]
    return pl.pallas_call(
        _experts_kernel,
        grid_spec=pltpu.PrefetchScalarGridSpec(
            num_scalar_prefetch=6,
            grid=(n_work,),
            in_specs=[
                pl.BlockSpec((ts, D_MODEL), lambda w, wt, we, *_: (wt[w], 0)),
                pl.BlockSpec((1, D_MODEL, D_EXPERT), lambda w, wt, we, *_: (we[w], 0, 0)),
                pl.BlockSpec((1, D_MODEL, D_EXPERT), lambda w, wt, we, *_: (we[w], 0, 0)),
                pl.BlockSpec((1, D_EXPERT, D_MODEL), lambda w, wt, we, *_: (we[w], 0, 0)),
            ],
            out_specs=pl.BlockSpec((ts, D_MODEL), lambda w, wt, we, *_: (wt[w], 0)),
            scratch_shapes=[pltpu.VMEM((D_MODEL, D_EXPERT), BF16), pltpu.VMEM((D_MODEL, D_EXPERT), BF16),
                            pltpu.VMEM((D_EXPERT, D_MODEL), BF16)],
        ),
        out_shape=jax.ShapeDtypeStruct((s, D_MODEL), F32),
        compiler_params=_cparams("arbitrary"),
        name="moe_experts",
    )(*work, xs, w1, w3, w2)


def _expert_work_items(counts, total):
    ts = MOE_TS
    n_tiles = total // ts
    n_work = n_tiles + N_EXPERTS - 1
    offs = jnp.cumsum(counts) - counts
    t0 = (jnp.arange(n_tiles, dtype=jnp.int32) * ts)[:, None]
    lo = jnp.maximum(t0, offs[None, :])
    hi = jnp.minimum(t0 + ts, (offs + counts)[None, :])
    nonempty = (hi > lo).reshape(-1)
    nw = jnp.sum(nonempty.astype(jnp.int32))
    idx = jnp.nonzero(nonempty, size=n_work, fill_value=0)[0].astype(jnp.int32)
    idx = jnp.where(jnp.arange(n_work) < nw, idx, idx[jnp.maximum(nw - 1, 0)])
    wt = idx // N_EXPERTS
    we = idx % N_EXPERTS
    wlo = lo.reshape(-1)[idx] - wt * ts
    whi = hi.reshape(-1)[idx] - wt * ts
    wfirst = jnp.concatenate([jnp.ones((1,), jnp.int32), (wt[1:] != wt[:-1]).astype(jnp.int32)])
    return wt, we, wlo, whi, wfirst, nw.reshape(1)


def _combine_kernel(slots_ref, slots_next_ref, h_ref, route_ref, gfin_ref, ys_hbm, yp_ref, ysm_ref, gbuf, sems,
                    *, n_prompt_tiles):
    i = pl.program_id(0)
    n = pl.num_programs(0)
    tm = h_ref.shape[0]

    def gather(s_ref, buf):
        for r in range(tm):
            pltpu.make_async_copy(ys_hbm.at[pl.ds(s_ref[0, 0, r], 1)], gbuf.at[buf, 0, pl.ds(r, 1)],
                                  sems.at[buf]).start(priority=0)
            pltpu.make_async_copy(ys_hbm.at[pl.ds(s_ref[0, 0, tm + r], 1)], gbuf.at[buf, 1, pl.ds(r, 1)],
                                  sems.at[buf]).start(priority=1)

    cur = i % 2

    @pl.when(i == 0)
    def _():
        gather(slots_ref, 0)

    @pl.when(i + 1 < n)
    def _():
        gather(slots_next_ref, 1 - cur)

    for k in range(2):
        pltpu.make_async_copy(ys_hbm.at[pl.ds(0, tm)], gbuf.at[cur, k], sems.at[cur]).wait()
    route = route_ref[...]
    lane = lax.broadcasted_iota(jnp.int32, route.shape, 1)
    w1 = jnp.sum(jnp.where(lane == ROUTE_W1, route, 0.0), axis=1, keepdims=True)
    w2 = jnp.sum(jnp.where(lane == ROUTE_W2, route, 0.0), axis=1, keepdims=True)
    hf = h_ref[...] + (w1 * gbuf[cur, 0] + w2 * gbuf[cur, 1])
    y = hf * lax.rsqrt(jnp.mean(hf * hf, axis=-1, keepdims=True) + NORM_EPS) * gfin_ref[...]

    @pl.when(i < n_prompt_tiles)
    def _():
        yp_ref[...] = y

    @pl.when(i >= n_prompt_tiles)
    def _():
        ysm_ref[...] = y


def _combine(h, route, slots, ys, gfin, n_prompt_tiles):
    t = h.shape[0]
    tm = MOE_TM
    n_p = n_prompt_tiles
    return pl.pallas_call(
        functools.partial(_combine_kernel, n_prompt_tiles=n_p),
        grid_spec=pltpu.PrefetchScalarGridSpec(
            num_scalar_prefetch=0,
            grid=(t // tm,),
            in_specs=[pl.BlockSpec((1, 1, 2 * tm), lambda i: (i, 0, 0), memory_space=pltpu.SMEM),
                      pl.BlockSpec((1, 1, 2 * tm), lambda i: (jnp.minimum(i + 1, t // tm - 1), 0, 0),
                                   memory_space=pltpu.SMEM),
                      pl.BlockSpec((tm, D_MODEL), lambda i: (i, 0)),
                      pl.BlockSpec((tm, ROUTER_LANES), lambda i: (i, 0)),
                      pl.BlockSpec((1, D_MODEL), lambda i: (0, 0)),
                      pl.BlockSpec(memory_space=pl.ANY)],
            out_specs=[pl.BlockSpec((tm, D_MODEL), lambda i: (jnp.minimum(i, n_p - 1), 0)),
                       pl.BlockSpec((tm, D_MODEL), lambda i: (0, 0))],
            scratch_shapes=[pltpu.VMEM((2, 2, tm, D_MODEL), F32), pltpu.SemaphoreType.DMA((2,))],
        ),
        out_shape=[jax.ShapeDtypeStruct((n_p * tm, D_MODEL), F32), jax.ShapeDtypeStruct((tm, D_MODEL), F32)],
        compiler_params=_cparams("arbitrary"),
        name="moe_combine",
    )(slots, slots, h, route, gfin, ys)


def _v_tiles(v, tblk):
    b, t, _ = v.shape
    x = v.reshape(b, t // tblk, tblk, 2, RWKV_PAIRS, RWKV_HEAD).transpose(0, 1, 4, 5, 3, 2)
    x = jnp.pad(x, ((0, 0),) * 5 + ((0, RWKV_HEAD - tblk),))
    return x.reshape(b, t // tblk, RWKV_PAIRS, RWKV_HEAD, LANE).astype(BF16)


def kernel(x_prompt, x_sample, state_gla, state_rwkv, state_shift, meta_tokens, norm_mix, w_in, gla_gate_w2,
           gla_gate_b, gla_norm, rwkv_mu, rwkv_w0, rwkv_w2, rwkv_a0, rwkv_a2, rwkv_g2, rwkv_kk, rwkv_ka, rwkv_rk,
           rwkv_ln_w, rwkv_ln_b, w_out, norm_ffn, router_group_w, router_group_b, router_expert_w,
           router_expert_b, moe_w1, moe_w3, moe_w2, norm_final):
    bp, tp, _ = x_prompt.shape
    bs, ts, _ = x_sample.shape
    assert state_gla.shape[0] == 1, "one layer"
    lyr = 0

    w_in_l = w_in[lyr]
    wg = jnp.pad(w_in_l[:, :GLA_COLS], ((0, 0), (0, GLA_PCOLS - GLA_COLS))).astype(BF16)
    wr = w_in_l[:, GLA_COLS:].astype(BF16)
    g_mix = norm_mix[lyr][None, :]
    gw2p = jnp.pad(gla_gate_w2[lyr], ((0, LANE - GLA_GATE_RANK), (0, 0)))
    gb = gla_gate_b[lyr][None, :]
    gn = gla_norm[lyr][None, :]
    w2p = jnp.pad(rwkv_w2[lyr], ((0, 64), (0, 0))).astype(BF16)
    a2p = jnp.pad(rwkv_a2[lyr], ((64, 0), (0, 0))).astype(BF16)
    pre_params = (rwkv_mu[lyr][None, :], rwkv_w0[lyr][None, :], w2p, rwkv_a0[lyr][None, :], a2p,
                  rwkv_g2[lyr].astype(BF16), rwkv_kk[lyr][None, :], rwkv_ka[lyr][None, :],
                  rwkv_rk[lyr].reshape(1, RWKV_WIDTH), _head_selector())
    lnw = rwkv_ln_w[lyr][None, :]
    lnb = rwkv_ln_b[lyr][None, :]
    wo = w_out[lyr].astype(BF16)
    gffn = norm_ffn[lyr][None, :]
    n_used = N_GROUPS + N_EXPERTS
    w_router = jnp.pad(
        jnp.concatenate([router_group_w[lyr],
                         router_expert_w[lyr].transpose(1, 0, 2).reshape(D_MODEL, N_EXPERTS)], axis=1),
        ((0, 0), (0, ROUTER_LANES - n_used)))
    b_router = jnp.pad(jnp.concatenate([router_group_b[lyr], router_expert_b[lyr].reshape(N_EXPERTS)]),
                       (0, ROUTER_LANES - n_used))[None, :]
    gfin = norm_final[None, :]

    pg_m, pr_m = _inproj(meta_tokens, g_mix, wg, wr, N_META)
    _, sg_m = _gla(pg_m[None], jnp.zeros((1, GLA_HEADS, GLA_DK, GLA_DV), F32), gw2p, gb, gn,
                   bb=1, chunk=N_META, sub=N_META, t_valid=N_META)
    r, w, k, kk, kka, v_m, _, _ = _rwkv_pre(pr_m[None], jnp.zeros((1, 1, RWKV_COLS), F32), pre_params,
                                            tm=N_META, explicit_prev=False, emit_vt=False)
    _, sr_m = _rwkv_rec(r, w, k, kk, kka, _v_tiles(v_m, N_META),
                        jnp.zeros((1, RWKV_HEADS, RWKV_HEAD, RWKV_HEAD), F32), bb=1, n_steps=N_META)

    xp = x_prompt.reshape(bp * tp, D_MODEL)
    pg_p, pr_p = _inproj(xp, g_mix, wg, wr, INPROJ_TM)
    og_p, sg_p = _gla(pg_p.reshape(bp, tp, GLA_PCOLS), jnp.broadcast_to(sg_m, (bp,) + sg_m.shape[1:]), gw2p, gb, gn,
                      bb=SEQ_BLOCK, chunk=GLA_CHUNK, sub=GLA_SUB, t_valid=GLA_CHUNK)
    pr_p3 = pr_p.reshape(bp, tp, RWKV_COLS)
    first_prev = jnp.broadcast_to(pr_m[N_META - 1][None, None, :], (bp, 1, RWKV_COLS))
    r, w, k, kk, kka, vt_p, bv_p, gate_p = _rwkv_pre(pr_p3, first_prev, pre_params, tm=PRE_TM, explicit_prev=False,
                                                     emit_vt=True)
    y_p, sr_p = _rwkv_rec(r, w, k, kk, kka, vt_p, jnp.broadcast_to(sr_m, (bp,) + sr_m.shape[1:]),
                          bb=bp, n_steps=REC_TB)
    prompt_rows = (xp, og_p.reshape(bp * tp, GLA_WIDTH), y_p.reshape(tp, bp * RWKV_WIDTH),
                   bv_p.reshape(bp * tp, RWKV_WIDTH), gate_p.reshape(bp * tp, RWKV_WIDTH))

    xs = x_sample.transpose(1, 0, 2).reshape(ts * bs, D_MODEL)
    pg_s, pr_s = _inproj(xs, g_mix, wg, wr, bs * ts)
    ts_pad = 8
    pg_s3 = jnp.pad(pg_s.reshape(ts, bs, GLA_PCOLS).transpose(1, 0, 2), ((0, 0), (0, ts_pad - ts), (0, 0)))
    og_s, sg_s = _gla(pg_s3, state_gla[lyr], gw2p, gb, gn, bb=SEQ_BLOCK, chunk=ts_pad, sub=ts_pad, t_valid=ts)
    og_s = og_s[:, :ts].transpose(1, 0, 2).reshape(ts * bs, GLA_WIDTH)
    pr_s3 = pr_s.reshape(ts, bs, RWKV_COLS)
    prev_s = jnp.concatenate([state_shift[lyr][None], pr_s3[:-1]], axis=0)
    r, w, k, kk, kka, v_s, bv_s, gate_s = _rwkv_pre(pr_s3.reshape(1, bs * ts, RWKV_COLS),
                                                     prev_s.reshape(1, bs * ts, RWKV_COLS), pre_params,
                                                     tm=bs * ts, explicit_prev=True, emit_vt=False, pair_out=False)
    y_s, sr_s = _rwkv_lanes(r[0], w[0], k[0], kk[0], kka[0], v_s[0], state_rwkv[lyr].transpose(1, 2, 3, 0),
                            n_steps=ts)
    sr_s = sr_s.transpose(3, 0, 1, 2)
    sample_rows = (xs, og_s, y_s, bv_s[0], gate_s[0])

    h_all, n2_all, route, route_t, counts = _mix_router(prompt_rows, sample_rows, lnw, lnb, wo, gffn, w_router,
                                                        b_router, seq_tiles=tp // MOE_TM)
    n_tok = h_all.shape[0]
    n_p_tiles = (bp * tp) // MOE_TM
    cnt = counts[0, EXPERT_LANE0:EXPERT_LANE0 + N_EXPERTS].astype(jnp.int32)
    offs = jnp.cumsum(cnt) - cnt
    eid = route_t[:, ROUTE_E1:ROUTE_E2 + 1, :].astype(jnp.int32)
    pos = route_t[:, ROUTE_P1:ROUTE_P2 + 1, :].astype(jnp.int32)
    off = sum(jnp.where(eid == e, offs[e], 0) for e in range(N_EXPERTS))
    slots = (off + pos).reshape(n_tok // MOE_TM, 1, 2 * MOE_TM)
    xs_sorted = _dispatch(n2_all, slots)
    ys_sorted = _experts(xs_sorted, _expert_work_items(cnt, 2 * n_tok), moe_w1[lyr], moe_w3[lyr], moe_w2[lyr])
    y_prompt, y_sample = _combine(h_all, route, slots, ys_sorted, gfin, n_p_tiles)
    y_prompt = y_prompt.reshape(bp, tp, D_MODEL)
    y_sample = y_sample.reshape(ts, bs, D_MODEL).transpose(1, 0, 2)

    return (y_prompt, y_sample,
            sg_p[None], sr_p[None], pr_p3[:, -1][None],
            sg_s[None], sr_s[None], pr_s3[ts - 1][None])
```

```python
import functools

import jax
import jax.numpy as jnp
from jax import lax
from jax.experimental import pallas as pl
from jax.experimental.pallas import tpu as pltpu

F32 = jnp.float32
BF16 = jnp.bfloat16
HIGHEST = lax.Precision.HIGHEST

D_MODEL = 1024
N_META = 16
NORM_EPS = 1e-6
LOG2E = 1.4426950408889634
GLA_HEADS = 4
GLA_DK = 64
GLA_DV = 128
GLA_QK = GLA_HEADS * GLA_DK
GLA_WIDTH = GLA_HEADS * GLA_DV
GLA_GATE_RANK = 16
GLA_GATE_NORM = 16.0
GLA_CHUNK = 64
GLA_SUB = 8
GLA_COLS = 2 * GLA_QK + 2 * GLA_WIDTH + GLA_GATE_RANK
GLA_PCOLS = 2 * GLA_QK + 2 * GLA_WIDTH + 128
RWKV_WIDTH = 512
RWKV_HEAD = 64
RWKV_HEADS = 8
RWKV_PAIRS = RWKV_HEADS // 2
RWKV_DECAY_SCALE = 0.606531
RWKV_GN_EPS = 64e-5
RWKV_COLS = 3 * RWKV_WIDTH + 64 + 64 + 128
REC_TB = 64
REC_UNROLL = 8
INPROJ_TM = 512
PRE_TM = 256
SEQ_BLOCK = 8
N_GROUPS = 4
EXPERTS_PER_GROUP = 8
N_EXPERTS = 32
D_EXPERT = 512
ROUTER_LANES = 128
EXPERT_LANE0 = N_GROUPS
ROUTE_E1, ROUTE_E2, ROUTE_W1, ROUTE_W2, ROUTE_P1, ROUTE_P2 = range(6)
MOE_TM = 512
MOE_TS = 512
MOE_SUB = 256

LANE = 128
VMEM_LIMIT = 56 * 1024 * 1024


def _cparams(*sem):
    return pltpu.CompilerParams(dimension_semantics=sem, vmem_limit_bytes=VMEM_LIMIT)


def _block_ones(n, blk):
    i = jnp.arange(n)
    return (i[:, None] // blk == i[None, :] // blk).astype(BF16)


def _sigmoid(x):
    return 1.0 / (1.0 + jnp.exp(-x))


def _dot(a, b):
    return jnp.dot(a, b, preferred_element_type=F32)


def _dot_nt(a, b):
    return lax.dot_general(a, b, (((1,), (1,)), ((), ())), preferred_element_type=F32)


def _dot_tn(a, b):
    return lax.dot_general(a, b, (((0,), (0,)), ((), ())), preferred_element_type=F32)


def _split2(x):
    hi = x.astype(BF16)
    lo = (x - hi.astype(F32)).astype(BF16)
    return hi, lo


def _head_selector():
    return (jnp.arange(RWKV_WIDTH)[:, None] // RWKV_HEAD == jnp.arange(LANE)[None, :]).astype(BF16)


def _group_sum(x, sel):
    hi, lo = _split2(x)
    s_hi, s_lo = _split2(_dot(hi, sel) + _dot(lo, sel))
    return _dot_nt(s_hi, sel) + _dot_nt(s_lo, sel)


def _inproj_kernel(x_ref, g_ref, wg_ref, wr_ref, pg_ref, pr_ref):
    x = x_ref[...]
    n = x * lax.rsqrt(jnp.mean(x * x, axis=-1, keepdims=True) + NORM_EPS) * g_ref[...]
    nb = n.astype(BF16)
    pg_ref[...] = _dot(nb, wg_ref[...])
    pr_ref[...] = _dot(nb, wr_ref[...])


def _inproj(x, g, wg, wr, tm):
    t = x.shape[0]
    return pl.pallas_call(
        _inproj_kernel,
        grid=(t // tm,),
        in_specs=[
            pl.BlockSpec((tm, D_MODEL), lambda i: (i, 0)),
            pl.BlockSpec((1, D_MODEL), lambda i: (0, 0)),
            pl.BlockSpec((D_MODEL, GLA_PCOLS), lambda i: (0, 0)),
            pl.BlockSpec((D_MODEL, RWKV_COLS), lambda i: (0, 0)),
        ],
        out_specs=[
            pl.BlockSpec((tm, GLA_PCOLS), lambda i: (i, 0)),
            pl.BlockSpec((tm, RWKV_COLS), lambda i: (i, 0)),
        ],
        out_shape=[jax.ShapeDtypeStruct((t, GLA_PCOLS), F32), jax.ShapeDtypeStruct((t, RWKV_COLS), F32)],
        compiler_params=_cparams("parallel"),
        name="inproj",
    )(x, g, wg, wr)


def _gla_kernel(pg_ref, s0_ref, gw2_ref, gb_ref, gn_ref, bo_ref, tril_ref, o_ref, sout_ref, s_scr,
                *, bb, chunk, sub, t_valid):
    ci = pl.program_id(1)

    @pl.when(ci == 0)
    def _():
        s_scr[...] = s0_ref[...]

    bo = bo_ref[...]
    tril = tril_ref[...]
    lane = lax.broadcasted_iota(jnp.int32, (sub, LANE), 1) & (GLA_DK - 1)
    rowi = lax.broadcasted_iota(jnp.int32, (sub, LANE), 0)
    head0_s = lax.broadcasted_iota(jnp.int32, (sub, LANE), 1) < GLA_DK
    head0_c = lax.broadcasted_iota(jnp.int32, (chunk, LANE), 1) < GLA_DK

    gl_all = pg_ref[:, :, 2 * GLA_QK + 2 * GLA_WIDTH:].reshape(bb * chunk, LANE)
    z_all = jnp.dot(gl_all, gw2_ref[...], precision=HIGHEST, preferred_element_type=F32) + gb_ref[...]
    lg_all = (jnp.minimum(z_all, 0.0) - jnp.log1p(jnp.exp(-jnp.abs(z_all)))) * (LOG2E / GLA_GATE_NORM)
    if t_valid < chunk:
        rows = lax.broadcasted_iota(jnp.int32, lg_all.shape, 0) & (chunk - 1)
        lg_all = jnp.where(rows < t_valid, lg_all, 0.0)
    if tril.shape[0] == bb * chunk:
        b_all = jnp.dot(tril, lg_all, precision=HIGHEST, preferred_element_type=F32)

    for bi in range(bb):
        pg = pg_ref[bi]
        q = pg[:, 0:GLA_QK] * (GLA_DK ** -0.5)
        k = pg[:, GLA_QK:2 * GLA_QK]
        v = pg[:, 2 * GLA_QK:2 * GLA_QK + GLA_WIDTH]
        g = pg[:, 2 * GLA_QK + GLA_WIDTH:2 * GLA_QK + 2 * GLA_WIDTH]
        if tril.shape[0] == bb * chunk:
            b = b_all[bi * chunk:(bi + 1) * chunk]
        else:
            b = jnp.dot(tril, lg_all[bi * chunk:(bi + 1) * chunk], precision=HIGHEST, preferred_element_type=F32)
        eb = jnp.exp2(b)
        blast = b[chunk - 1:chunk, :]
        kl = k * jnp.exp2(blast - b)
        qe = q * eb

        n_blk = chunk // sub
        n_pairs = GLA_HEADS // 2
        ps = []
        for hp in range(n_pairs):
            sl = slice(hp * LANE, (hp + 1) * LANE)
            for blk in range(n_blk):
                rs = slice(blk * sub, (blk + 1) * sub)
                qb, kb, bbk = q[rs, sl], k[rs, sl], b[rs, sl]
                for j in range(sub):
                    ps.append(qb * (kb[j:j + 1] * jnp.exp2(jnp.minimum(bbk - bbk[j:j + 1], 0.0))))
        red = _dot(jnp.concatenate(ps, axis=0).astype(BF16), bo)

        o_heads = []
        for hp in range(n_pairs):
            sl = slice(hp * LANE, (hp + 1) * LANE)
            kp, bp = k[:, sl], b[:, sl]
            row_blocks = []
            for blk in range(n_blk):
                rs = slice(blk * sub, (blk + 1) * sub)
                base = (hp * n_blk + blk) * sub * sub
                a = jnp.zeros((sub, LANE), F32)
                for j in range(sub):
                    a = jnp.where((lane == blk * sub + j) & (rowi >= j), red[base + j * sub:base + (j + 1) * sub], a)
                if blk > 0:
                    bref = bp[blk * sub - 1:blk * sub]
                    qt = q[rs, sl] * jnp.exp2(bp[rs] - bref)
                    kt = (kp * jnp.exp2(jnp.minimum(bref - bp, 0.0))).astype(BF16)
                    qt2 = jnp.concatenate([jnp.where(head0_s, qt, 0.0), jnp.where(head0_s, 0.0, qt)], axis=0)
                    off2 = _dot_nt(qt2.astype(BF16), kt)
                    a = jnp.where(lane < blk * sub, jnp.concatenate([off2[:sub], off2[sub:]], axis=1), a)
                row_blocks.append(a)
            a_pair = row_blocks[0] if n_blk == 1 else jnp.concatenate(row_blocks, axis=0)
            v0 = v[:, 2 * hp * GLA_DV:(2 * hp + 1) * GLA_DV]
            v1 = v[:, (2 * hp + 1) * GLA_DV:(2 * hp + 2) * GLA_DV]
            s_pair = s_scr[bi, 2 * hp:2 * hp + 2].reshape(2 * GLA_DK, GLA_DV)
            qe_p, kl_p = qe[:, sl], kl[:, sl]

            def by_head(x):
                return jnp.concatenate([jnp.where(head0_c, x, 0.0), jnp.where(head0_c, 0.0, x)], axis=0)

            if chunk == GLA_DK:
                v_rows = jnp.concatenate([v0, v1], axis=0)
            else:
                zpad = jnp.zeros((GLA_DK - chunk, GLA_DV), F32)
                v_rows = jnp.concatenate([v0, zpad, v1, zpad], axis=0)
            lhs = jnp.concatenate([by_head(a_pair), by_head(qe_p)], axis=1).astype(BF16)
            rhs = jnp.concatenate([v_rows, s_pair], axis=0).astype(BF16)
            o2 = _dot(lhs, rhs)
            upd = _dot_tn(by_head(kl_p).astype(BF16), jnp.concatenate([v0, v1], axis=0).astype(BF16))
            dcol = jnp.broadcast_to(jnp.exp2(blast[:, sl]), (8, LANE)).T[:, 0:1]
            s_new = dcol * s_pair + upd
            s_scr[bi, 2 * hp] = s_new[:GLA_DK]
            s_scr[bi, 2 * hp + 1] = s_new[GLA_DK:]
            for h2 in range(2):
                o_h = o2[h2 * chunk:(h2 + 1) * chunk]
                o_heads.append(o_h * lax.rsqrt(jnp.mean(o_h * o_h, axis=-1, keepdims=True) + NORM_EPS) * gn_ref[...])
        o = jnp.concatenate(o_heads, axis=1)
        o_ref[bi] = o * (g * _sigmoid(g))

    @pl.when(ci == pl.num_programs(1) - 1)
    def _():
        sout_ref[...] = s_scr[...]


def _gla(pg, s0, gw2p, gb, gn, *, bb, chunk, sub, t_valid):
    b, t, _ = pg.shape
    assert chunk & (chunk - 1) == 0
    tril = jnp.tril(jnp.ones((chunk, chunk), F32))
    if bb * chunk <= LANE:
        tril = jnp.kron(jnp.eye(bb, dtype=F32), tril)
    kern = functools.partial(_gla_kernel, bb=bb, chunk=chunk, sub=sub, t_valid=t_valid)
    return pl.pallas_call(
        kern,
        grid=(b // bb, t // chunk),
        in_specs=[
            pl.BlockSpec((bb, chunk, GLA_PCOLS), lambda i, j: (i, j, 0)),
            pl.BlockSpec((bb, GLA_HEADS, GLA_DK, GLA_DV), lambda i, j: (i, 0, 0, 0)),
            pl.BlockSpec((LANE, GLA_QK), lambda i, j: (0, 0)),
            pl.BlockSpec((1, GLA_QK), lambda i, j: (0, 0)),
            pl.BlockSpec((1, GLA_DV), lambda i, j: (0, 0)),
            pl.BlockSpec((LANE, LANE), lambda i, j: (0, 0)),
            pl.BlockSpec(tril.shape, lambda i, j: (0, 0)),
        ],
        out_specs=[
            pl.BlockSpec((bb, chunk, GLA_WIDTH), lambda i, j: (i, j, 0)),
            pl.BlockSpec((bb, GLA_HEADS, GLA_DK, GLA_DV), lambda i, j: (i, 0, 0, 0)),
        ],
        out_shape=[jax.ShapeDtypeStruct((b, t, GLA_WIDTH), F32),
                   jax.ShapeDtypeStruct((b, GLA_HEADS, GLA_DK, GLA_DV), F32)],
        scratch_shapes=[pltpu.VMEM((bb, GLA_HEADS, GLA_DK, GLA_DV), F32)],
        compiler_params=_cparams("parallel", "arbitrary"),
        name="gla_chunk",
    )(pg, s0, gw2p, gb, gn, _block_ones(LANE, GLA_DK), tril)


def _rwkv_pre_kernel(pr_ref, aux_ref, mu_ref, w0_ref, w2_ref, a0_ref, a2_ref, g2_ref, kk_ref, ka_ref, rk_ref, bo_ref,
                     r_out, w_out, k_out, kkn_out, kka_out, v_out, bv_out, gate_out, carry_scr,
                     *, tm, explicit_prev, emit_vt, pair_out):
    pr = pr_ref[0]
    if explicit_prev:
        prev = aux_ref[0]
    else:
        j = pl.program_id(1)
        row0 = jnp.where(j == 0, aux_ref[0], carry_scr[...])
        rows = lax.broadcasted_iota(jnp.int32, pr.shape, 0)
        prev = jnp.where(rows == 0, row0, pltpu.roll(pr, 1, 0))
        carry_scr[...] = pr[tm - 1:tm, :]
    xm = pr + (prev - pr) * mu_ref[...]
    wd = RWKV_WIDTH
    rr, rk, rv = xm[:, 0:wd], xm[:, wd:2 * wd], xm[:, 2 * wd:3 * wd]
    wa = xm[:, 3 * wd:3 * wd + LANE]
    gl2 = xm[:, 3 * wd + LANE:3 * wd + 2 * LANE]
    logw = -RWKV_DECAY_SCALE * _sigmoid(w0_ref[...] + _dot(jnp.tanh(wa).astype(BF16), w2_ref[...]))
    aa = _sigmoid(a0_ref[...] + _dot(wa.astype(BF16), a2_ref[...]))
    gate = _dot(_sigmoid(gl2).astype(BF16), g2_ref[...])
    bo = bo_ref[...]
    kk = rk * kk_ref[...]
    kk = kk / jnp.maximum(jnp.sqrt(_group_sum(kk * kk, bo)), 1e-12)
    k = rk * (1.0 + (aa - 1.0) * ka_ref[...])
    bv = _group_sum(rr * k * rk_ref[...], bo) * rv
    w = jnp.exp(logw)
    kka = kk * aa
    hd = RWKV_HEAD

    def pair(x, hp):
        return jnp.concatenate([x[:, hp * hd:(hp + 1) * hd], x[:, (hp + RWKV_PAIRS) * hd:(hp + RWKV_PAIRS + 1) * hd]],
                               axis=1)

    if pair_out:
        for hp in range(RWKV_PAIRS):
            r_out[0, hp] = pair(rr, hp)
            w_out[0, hp] = pair(w, hp)
            k_out[0, hp] = pair(k, hp)
            kkn_out[0, hp] = pair(kk, hp)
            kka_out[0, hp] = pair(kka, hp)
    else:
        r_out[0], w_out[0], k_out[0], kkn_out[0], kka_out[0] = rr, w, k, kk, kka
    if emit_vt:
        vt = rv.T
        for tb in range(tm // REC_TB):
            ts = slice(tb * REC_TB, (tb + 1) * REC_TB)
            for hp in range(RWKV_PAIRS):
                lo, hi = hp * hd, (hp + RWKV_PAIRS) * hd
                v_out[0, tb, hp] = jnp.concatenate([vt[lo:lo + hd, ts], vt[hi:hi + hd, ts]], axis=1).astype(BF16)
    else:
        v_out[0] = rv
    bv_out[0] = bv
    gate_out[0] = gate


def _rwkv_pre(pr, aux, params, *, tm, explicit_prev, emit_vt, pair_out=True):
    b, t, _ = pr.shape
    kern = functools.partial(_rwkv_pre_kernel, tm=tm, explicit_prev=explicit_prev, emit_vt=emit_vt,
                             pair_out=pair_out)
    aux_spec = (pl.BlockSpec((1, tm, RWKV_COLS), lambda i, j: (i, j, 0)) if explicit_prev
                else pl.BlockSpec((1, 1, RWKV_COLS), lambda i, j: (i, 0, 0)))
    const = lambda shape: pl.BlockSpec(shape, lambda i, j: (0,) * len(shape))
    pair_spec = pl.BlockSpec((1, RWKV_PAIRS, tm, LANE), lambda i, j: (i, 0, j, 0))
    row_spec = pl.BlockSpec((1, tm, RWKV_WIDTH), lambda i, j: (i, j, 0))
    pair_shape = jax.ShapeDtypeStruct((b, RWKV_PAIRS, t, LANE), F32)
    row_shape = jax.ShapeDtypeStruct((b, t, RWKV_WIDTH), F32)
    if emit_vt:
        v_spec = pl.BlockSpec((1, tm // REC_TB, RWKV_PAIRS, RWKV_HEAD, LANE), lambda i, j: (i, j, 0, 0, 0))
        v_shape = jax.ShapeDtypeStruct((b, t // REC_TB, RWKV_PAIRS, RWKV_HEAD, LANE), BF16)
    else:
        v_spec, v_shape = row_spec, row_shape
    if not pair_out:
        pair_spec, pair_shape = row_spec, row_shape
    return pl.pallas_call(
        kern,
        grid=(b, t // tm),
        in_specs=[
            pl.BlockSpec((1, tm, RWKV_COLS), lambda i, j: (i, j, 0)),
            aux_spec,
            const((1, RWKV_COLS)), const((1, RWKV_WIDTH)), const((LANE, RWKV_WIDTH)), const((1, RWKV_WIDTH)),
            const((LANE, RWKV_WIDTH)), const((LANE, RWKV_WIDTH)), const((1, RWKV_WIDTH)), const((1, RWKV_WIDTH)),
            const((1, RWKV_WIDTH)), const((RWKV_WIDTH, LANE)),
        ],
        out_specs=[pair_spec] * 5 + [v_spec, row_spec, row_spec],
        out_shape=[pair_shape] * 5 + [v_shape, row_shape, row_shape],
        scratch_shapes=[pltpu.VMEM((1, RWKV_COLS), F32)],
        compiler_params=_cparams("parallel", "arbitrary"),
        name="rwkv_pre",
    )(pr, aux, *params)


def _rwkv_rec_kernel(r_ref, w_ref, k_ref, kk_ref, kka_ref, vt_ref, s0_ref, bo_ref, vsel_ref, ysel_ref,
                     y_ref, sout_ref, s_scr, t1_scr, t3_scr, yt_scr, *, bb, n_steps):
    tb = pl.program_id(1)
    nc = bb * RWKV_PAIRS
    hd = RWKV_HEAD

    @pl.when(tb == 0)
    def _():
        for c in range(nc):
            bi, hp = divmod(c, RWKV_PAIRS)
            s_scr[c] = jnp.concatenate([s0_ref[bi, hp], s0_ref[bi, hp + RWKV_PAIRS]], axis=1)

    bo = bo_ref[...]

    def step(t, u):
        row = pl.ds(t, 1)
        for c in range(nc):
            bi, hp = divmod(c, RWKV_PAIRS)
            t1_scr[c * hd:(c + 1) * hd, :] = (s_scr[c] * kk_ref[bi, hp, row, :]).astype(BF16)
        sab = _dot(t1_scr[...], bo)
        vb = _dot(vt_ref[...].reshape(nc * hd, LANE), vsel_ref[t])
        for c in range(nc):
            bi, hp = divmod(c, RWKV_PAIRS)
            rs = slice(c * hd, (c + 1) * hd)
            s2 = (s_scr[c] * w_ref[bi, hp, row, :] - sab[rs] * kka_ref[bi, hp, row, :]
                  + vb[rs] * k_ref[bi, hp, row, :])
            s_scr[c] = s2
            t3_scr[rs, :] = (s2 * r_ref[bi, hp, row, :]).astype(BF16)
        yt_scr[...] += _dot_nt(ysel_ref[u], t3_scr[...])

    n_inner = min(8, n_steps)

    def block8(t8, carry):
        yt_scr[...] = jnp.zeros(yt_scr.shape, F32)

        def inner(u, c2):
            step(t8 * 8 + u, u)
            return c2

        lax.fori_loop(0, n_inner, inner, 0, unroll=REC_UNROLL)
        t0 = pl.multiple_of(t8 * 8, 8)
        blk = RWKV_PAIRS * hd
        for bi in range(bb):
            for h2 in range(2):
                y_ref[0, pl.ds(t0, 8), bi * RWKV_WIDTH + h2 * blk:bi * RWKV_WIDTH + (h2 + 1) * blk] = (
                    yt_scr[h2 * 8:(h2 + 1) * 8, bi * blk:(bi + 1) * blk])
        return carry

    lax.fori_loop(0, (n_steps + 7) // 8, block8, 0)

    @pl.when(tb == pl.num_programs(1) - 1)
    def _():
        for c in range(nc):
            bi, hp = divmod(c, RWKV_PAIRS)
            s_c = s_scr[c]
            sout_ref[bi, hp] = s_c[:, :RWKV_HEAD]
            sout_ref[bi, hp + RWKV_PAIRS] = s_c[:, RWKV_HEAD:]


def _rwkv_rec(r, w, k, kk, kka, vt, s0, *, bb, n_steps):
    b, _, t, _ = r.shape
    tblk = min(REC_TB, t)
    ntb = t // tblk
    nc = bb * RWKV_PAIRS
    lane = jnp.arange(LANE)
    vsel = ((lane[None, :, None] // RWKV_HEAD == lane[None, None, :] // RWKV_HEAD)
            & (lane[None, :, None] % RWKV_HEAD == jnp.arange(RWKV_HEAD)[:, None, None])).astype(BF16)
    ysel = (jnp.arange(16)[None, :, None]
            == 8 * (lane[None, None, :] // RWKV_HEAD) + jnp.arange(8)[:, None, None]).astype(BF16)
    kern = functools.partial(_rwkv_rec_kernel, bb=bb, n_steps=n_steps)
    pair_spec = pl.BlockSpec((bb, RWKV_PAIRS, tblk, LANE), lambda i, j: (i, 0, j, 0))
    state_spec = pl.BlockSpec((bb, RWKV_HEADS, RWKV_HEAD, RWKV_HEAD), lambda i, j: (i, 0, 0, 0))
    ytb = max(tblk, 8)
    return pl.pallas_call(
        kern,
        grid=(b // bb, ntb),
        in_specs=[pair_spec] * 5 + [
            pl.BlockSpec((bb, 1, RWKV_PAIRS, RWKV_HEAD, LANE), lambda i, j: (i, j, 0, 0, 0)),
            state_spec,
            pl.BlockSpec((LANE, LANE), lambda i, j: (0, 0)),
            pl.BlockSpec((RWKV_HEAD, LANE, LANE), lambda i, j: (0, 0, 0)),
            pl.BlockSpec((8, 16, LANE), lambda i, j: (0, 0, 0)),
        ],
        out_specs=[
            pl.BlockSpec((1, ytb, bb * RWKV_WIDTH), lambda i, j: (i, j, 0)),
            state_spec,
        ],
        out_shape=[jax.ShapeDtypeStruct((b // bb, ntb * ytb, bb * RWKV_WIDTH), F32),
                   jax.ShapeDtypeStruct((b, RWKV_HEADS, RWKV_HEAD, RWKV_HEAD), F32)],
        scratch_shapes=[pltpu.VMEM((nc, RWKV_HEAD, LANE), F32),
                        pltpu.VMEM((nc * RWKV_HEAD, LANE), BF16),
                        pltpu.VMEM((nc * RWKV_HEAD, LANE), BF16),
                        pltpu.VMEM((16, nc * RWKV_HEAD), F32)],
        compiler_params=_cparams("parallel", "arbitrary"),
        name="rwkv_rec",
    )(r, w, k, kk, kka, vt, s0, _block_ones(LANE, RWKV_HEAD), vsel, ysel)


def _rwkv_lanes_kernel(r_ref, w_ref, k_ref, kk_ref, kka_ref, v_ref, s0_ref, y_ref, sout_ref, vt_scr, yt_scr,
                       *, n_steps, n_seq):
    hd = RWKV_HEAD
    for t in range(n_steps):
        rows = slice(t * n_seq, (t + 1) * n_seq)
        r_t, w_t, k_t = r_ref[rows, :].T, w_ref[rows, :].T, k_ref[rows, :].T
        kk_t, kka_t = kk_ref[rows, :].T, kka_ref[rows, :].T
        vt_scr[...] = v_ref[rows, :].T
        src = s0_ref if t == 0 else sout_ref
        for h2 in range(2):
            hs = slice(h2 * hd, (h2 + 1) * hd)
            r_h, w_h, k_h, kk_h, kka_h = r_t[hs], w_t[hs], k_t[hs], kk_t[hs], kka_t[hs]

            def value_row(i, carry):
                s = src[h2, i]
                sab = jnp.sum(s * kk_h, axis=0, keepdims=True)
                v_i = vt_scr[pl.ds(h2 * hd + i, 1), :]
                s2 = s * w_h - sab * kka_h + v_i * k_h
                sout_ref[h2, i] = s2
                yt_scr[pl.ds(h2 * hd + i, 1), :] = jnp.sum(s2 * r_h, axis=0, keepdims=True)
                return carry

            lax.fori_loop(0, hd, value_row, 0, unroll=8)
        y_ref[rows, :] = yt_scr[...].T


def _rwkv_lanes(r, w, k, kk, kka, v, s0, *, n_steps):
    n_tok = r.shape[0]
    n_seq = n_tok // n_steps
    assert n_seq == LANE, "one lane per sequence"
    row_spec = pl.BlockSpec((n_tok, LANE), lambda hp: (0, hp))
    state_spec = pl.BlockSpec((2, RWKV_HEAD, RWKV_HEAD, n_seq), lambda hp: (hp, 0, 0, 0))
    return pl.pallas_call(
        functools.partial(_rwkv_lanes_kernel, n_steps=n_steps, n_seq=n_seq),
        grid=(RWKV_HEADS // 2,),
        in_specs=[row_spec] * 6 + [state_spec],
        out_specs=[row_spec, state_spec],
        out_shape=[jax.ShapeDtypeStruct((n_tok, RWKV_WIDTH), F32),
                   jax.ShapeDtypeStruct((RWKV_HEADS, RWKV_HEAD, RWKV_HEAD, n_seq), F32)],
        scratch_shapes=[pltpu.VMEM((LANE, n_seq), F32), pltpu.VMEM((LANE, n_seq), F32)],
        compiler_params=_cparams("parallel"),
        name="rwkv_lanes",
    )(r, w, k, kk, kka, v, s0)


def _mix_router_body(x_ref, og_ref, y_ref, bv_ref, gate_ref, lnw_ref, lnb_ref, wo_ref, gffn_ref, wr_hi_ref, wr_lo_ref,
                     br_ref, bo_ref, tril_ref, h_ref, n2_ref, route_ref, route_t_ref, cnt_scr):
    bo = bo_ref[...]
    y = y_ref[...]
    inv_n = 1.0 / RWKV_HEAD
    d = y - _group_sum(y, bo) * inv_n
    var = _group_sum(d * d, bo) * inv_n
    yn = d * lax.rsqrt(var + RWKV_GN_EPS) * lnw_ref[...] + lnb_ref[...] + bv_ref[...]
    o_rwkv = yn * gate_ref[...]
    mix = (_dot(og_ref[...].astype(BF16), wo_ref[0:GLA_WIDTH, :])
           + _dot(o_rwkv.astype(BF16), wo_ref[GLA_WIDTH:, :]))
    h = x_ref[...] + mix
    h_ref[...] = h
    n2 = h * lax.rsqrt(jnp.mean(h * h, axis=-1, keepdims=True) + NORM_EPS) * gffn_ref[...]
    n2_ref[...] = n2
    n2_hi, n2_lo = _split2(n2)
    lg = (_dot(n2_hi, wr_hi_ref[...]) + _dot(n2_hi, wr_lo_ref[...]) + _dot(n2_lo, wr_hi_ref[...])) + br_ref[...]
    neg = jnp.float32(-3.0e38)
    big = jnp.float32(1.0e9)
    lane = lax.broadcasted_iota(jnp.int32, lg.shape, 1).astype(F32)
    gmask = lane < N_GROUPS
    gmax = jnp.max(jnp.where(gmask, lg, neg), axis=1, keepdims=True)
    p_top = 1.0 / jnp.sum(jnp.where(gmask, jnp.exp(jnp.minimum(lg - gmax, 0.0)), 0.0), axis=1, keepdims=True)
    gidx = jnp.min(jnp.where(gmask & (lg == gmax), lane, big), axis=1, keepdims=True)
    e_lo = EXPERT_LANE0 + gidx * EXPERTS_PER_GROUP
    emask = (lane >= e_lo) & (lane < e_lo + EXPERTS_PER_GROUP)
    m1 = jnp.max(jnp.where(emask, lg, neg), axis=1, keepdims=True)
    e1 = jnp.min(jnp.where(emask & (lg == m1), lane, big), axis=1, keepdims=True)
    emask2 = emask & (lane != e1)
    m2 = jnp.max(jnp.where(emask2, lg, neg), axis=1, keepdims=True)
    e2 = jnp.min(jnp.where(emask2 & (lg == m2), lane, big), axis=1, keepdims=True)
    r21 = jnp.exp(m2 - m1)
    w1 = p_top / (1.0 + r21)
    w2 = p_top * r21 / (1.0 + r21)
    o1 = lane == e1
    o2 = lane == e2
    onehot = jnp.where(o1 | o2, 1.0, 0.0)
    rank = _dot(tril_ref[...], onehot.astype(BF16)) + cnt_scr[...]
    pos1 = jnp.sum(jnp.where(o1, rank, 0.0), axis=1, keepdims=True)
    pos2 = jnp.sum(jnp.where(o2, rank, 0.0), axis=1, keepdims=True)
    cnt_scr[...] += jnp.sum(onehot, axis=0, keepdims=True)
    route = jnp.where(lane == ROUTE_E1, e1 - EXPERT_LANE0, 0.0)
    route = jnp.where(lane == ROUTE_E2, e2 - EXPERT_LANE0, route)
    route = jnp.where(lane == ROUTE_W1, w1, route)
    route = jnp.where(lane == ROUTE_W2, w2, route)
    route = jnp.where(lane == ROUTE_P1, pos1, route)
    route = jnp.where(lane == ROUTE_P2, pos2, route)
    route_ref[...] = route
    route_t_ref[0] = route.T[0:8, :]


def _mix_router_kernel(*refs, n_prompt_tiles):
    prompt_rows, sample_rows, rest = refs[0:5], refs[5:10], refs[10:]
    consts, (h_ref, n2_ref, route_ref, route_t_ref, cnt_ref, cnt_scr) = rest[:9], rest[9:]
    i = pl.program_id(0)

    @pl.when(i == 0)
    def _():
        cnt_scr[...] = jnp.zeros(cnt_scr.shape, F32)

    @pl.when(i < n_prompt_tiles)
    def _():
        _mix_router_body(*prompt_rows, *consts, h_ref, n2_ref, route_ref, route_t_ref, cnt_scr)

    @pl.when(i >= n_prompt_tiles)
    def _():
        _mix_router_body(*sample_rows, *consts, h_ref, n2_ref, route_ref, route_t_ref, cnt_scr)

    cnt_ref[...] = cnt_scr[...]


def _mix_router(prompt_rows, sample_rows, lnw, lnb, wo, gffn, wr, br, *, seq_tiles):
    tm = MOE_TM
    n_p = prompt_rows[0].shape[0] // tm
    assert sample_rows[0].shape[0] == tm
    t = (n_p + 1) * tm
    widths = (D_MODEL, GLA_WIDTH, RWKV_WIDTH, RWKV_WIDTH, RWKV_WIDTH)
    p_specs = [pl.BlockSpec((tm, n), lambda i: (jnp.minimum(i, n_p - 1), 0)) for n in widths]
    p_specs[2] = pl.BlockSpec(
        (tm, RWKV_WIDTH), lambda i: (jnp.minimum(i, n_p - 1) % seq_tiles, jnp.minimum(i, n_p - 1) // seq_tiles))
    s_specs = [pl.BlockSpec((tm, n), lambda i: (0, 0)) for n in widths]
    const = lambda shape: pl.BlockSpec(shape, lambda i: (0,) * len(shape))
    row = lambda n: pl.BlockSpec((tm, n), lambda i: (i, 0))
    tril = jnp.tril(jnp.ones((tm, tm), F32), -1).astype(BF16)
    return pl.pallas_call(
        functools.partial(_mix_router_kernel, n_prompt_tiles=n_p),
        grid=(n_p + 1,),
        in_specs=p_specs + s_specs + [
            const((1, RWKV_WIDTH)), const((1, RWKV_WIDTH)), const((D_MODEL, D_MODEL)), const((1, D_MODEL)),
            const((D_MODEL, ROUTER_LANES)), const((D_MODEL, ROUTER_LANES)), const((1, ROUTER_LANES)),
            const((RWKV_WIDTH, LANE)),
            const((tm, tm))],
        out_specs=[row(D_MODEL), row(D_MODEL), row(ROUTER_LANES), pl.BlockSpec((1, 8, tm), lambda i: (i, 0, 0)),
                   const((1, ROUTER_LANES))],
        out_shape=[jax.ShapeDtypeStruct((t, D_MODEL), F32), jax.ShapeDtypeStruct((t, D_MODEL), F32),
                   jax.ShapeDtypeStruct((t, ROUTER_LANES), F32), jax.ShapeDtypeStruct((t // tm, 8, tm), F32),
                   jax.ShapeDtypeStruct((1, ROUTER_LANES), F32)],
        scratch_shapes=[pltpu.VMEM((1, ROUTER_LANES), F32)],
        compiler_params=_cparams("arbitrary"),
        name="mix_router",
    )(*prompt_rows, *sample_rows, lnw, lnb, wo, gffn, *_split2(wr), br, _head_selector(), tril)


def _dispatch_kernel(slots_ref, x_ref, xs_hbm, sem):
    tm = x_ref.shape[0]

    for r in range(tm):
        src = x_ref.at[pl.ds(r, 1)]
        pltpu.make_async_copy(src, xs_hbm.at[pl.ds(slots_ref[0, 0, r], 1)], sem).start(priority=0)
        pltpu.make_async_copy(src, xs_hbm.at[pl.ds(slots_ref[0, 0, tm + r], 1)], sem).start(priority=1)
    for _ in range(2):
        pltpu.make_async_copy(x_ref, xs_hbm.at[pl.ds(0, tm)], sem).wait()


def _dispatch(n2, slots):
    t = n2.shape[0]
    tm = MOE_TM
    return pl.pallas_call(
        _dispatch_kernel,
        grid_spec=pltpu.PrefetchScalarGridSpec(
            num_scalar_prefetch=0,
            grid=(t // tm,),
            in_specs=[pl.BlockSpec((1, 1, 2 * tm), lambda i: (i, 0, 0), memory_space=pltpu.SMEM),
                      pl.BlockSpec((tm, D_MODEL), lambda i: (i, 0))],
            out_specs=pl.BlockSpec(memory_space=pl.ANY),
            scratch_shapes=[pltpu.SemaphoreType.DMA(())],
        ),
        out_shape=jax.ShapeDtypeStruct((2 * t, D_MODEL), F32),
        compiler_params=_cparams("arbitrary"),
        name="moe_dispatch",
    )(slots, n2)


def _experts_kernel(wt_ref, we_ref, wlo_ref, whi_ref, wfirst_ref, nw_ref,
                    xs_ref, w1_ref, w3_ref, w2_ref, ys_ref, wb1, wb3, wb2):
    w = pl.program_id(0)

    @pl.when(w < nw_ref[0])
    def _():
        new_expert = jnp.logical_or(w == 0, we_ref[w] != we_ref[jnp.maximum(w - 1, 0)])

        @pl.when(new_expert)
        def _():
            wb1[...] = w1_ref[0].astype(BF16)
            wb3[...] = w3_ref[0].astype(BF16)
            wb2[...] = w2_ref[0].astype(BF16)

        lo, hi, first = wlo_ref[w], whi_ref[w], wfirst_ref[w] == 1
        for r0 in range(0, MOE_TS, MOE_SUB):
            rs = pl.ds(r0, MOE_SUB)
            touched = (lo < r0 + MOE_SUB) & (hi > r0)

            @pl.when(touched)
            def _():
                x = xs_ref[rs, :].astype(BF16)
                a = _dot(x, wb1[...])
                b = _dot(x, wb3[...])
                o = _dot(((a * _sigmoid(a)) * b).astype(BF16), wb2[...])

                @pl.when(first)
                def _():
                    ys_ref[rs, :] = o

                @pl.when(jnp.logical_not(first))
                def _():
                    rows = lax.broadcasted_iota(jnp.int32, o.shape, 0) + r0
                    ys_ref[rs, :] = jnp.where((rows >= lo) & (rows < hi), o, ys_ref[rs, :])

            @pl.when(first & jnp.logical_not(touched))
            def _():
                ys_ref[rs, :] = jnp.zeros((MOE_SUB, D_MODEL), F32)


def _experts(xs, work, w1, w3, w2):
    s = xs.shape[0]
    ts = MOE_TS
    n_work = work[0].shape[0]
    return pl.pallas_call(
        _experts_kernel,
        grid_spec=pltpu.PrefetchScalarGridSpec(
            num_scalar_prefetch=6,
            grid=(n_work,),
            in_specs=[
                pl.BlockSpec((ts, D_MODEL), lambda w, wt, we, *_: (wt[w], 0)),
                pl.BlockSpec((1, D_MODEL, D_EXPERT), lambda w, wt, we, *_: (we[w], 0, 0)),
                pl.BlockSpec((1, D_MODEL, D_EXPERT), lambda w, wt, we, *_: (we[w], 0, 0)),
                pl.BlockSpec((1, D_EXPERT, D_MODEL), lambda w, wt, we, *_: (we[w], 0, 0)),
            ],
            out_specs=pl.BlockSpec((ts, D_MODEL), lambda w, wt, we, *_: (wt[w], 0)),
            scratch_shapes=[pltpu.VMEM((D_MODEL, D_EXPERT), BF16), pltpu.VMEM((D_MODEL, D_EXPERT), BF16),
                            pltpu.VMEM((D_EXPERT, D_MODEL), BF16)],
        ),
        out_shape=jax.ShapeDtypeStruct((s, D_MODEL), F32),
        compiler_params=_cparams("arbitrary"),
        name="moe_experts",
    )(*work, xs, w1, w3, w2)


def _expert_work_items(counts, total):
    ts = MOE_TS
    n_tiles = total // ts
    n_work = n_tiles + N_EXPERTS - 1
    offs = jnp.cumsum(counts) - counts
    t0 = (jnp.arange(n_tiles, dtype=jnp.int32) * ts)[:, None]
    lo = jnp.maximum(t0, offs[None, :])
    hi = jnp.minimum(t0 + ts, (offs + counts)[None, :])
    nonempty = (hi > lo).reshape(-1)
    nw = jnp.sum(nonempty.astype(jnp.int32))
    idx = jnp.nonzero(nonempty, size=n_work, fill_value=0)[0].astype(jnp.int32)
    idx = jnp.where(jnp.arange(n_work) < nw, idx, idx[jnp.maximum(nw - 1, 0)])
    wt = idx // N_EXPERTS
    we = idx % N_EXPERTS
    wlo = lo.reshape(-1)[idx] - wt * ts
    whi = hi.reshape(-1)[idx] - wt * ts
    wfirst = jnp.concatenate([jnp.ones((1,), jnp.int32), (wt[1:] != wt[:-1]).astype(jnp.int32)])
    return wt, we, wlo, whi, wfirst, nw.reshape(1)


def _combine_kernel(slots_ref, slots_next_ref, h_ref, route_ref, gfin_ref, ys_hbm, yp_ref, ysm_ref, gbuf, sems,
                    *, n_prompt_tiles):
    i = pl.program_id(0)
    n = pl.num_programs(0)
    tm = h_ref.shape[0]

    def gather(s_ref, buf):
        for r in range(tm):
            pltpu.make_async_copy(ys_hbm.at[pl.ds(s_ref[0, 0, r], 1)], gbuf.at[buf, 0, pl.ds(r, 1)],
                                  sems.at[buf]).start(priority=0)
            pltpu.make_async_copy(ys_hbm.at[pl.ds(s_ref[0, 0, tm + r], 1)], gbuf.at[buf, 1, pl.ds(r, 1)],
                                  sems.at[buf]).start(priority=1)

    cur = i % 2

    @pl.when(i == 0)
    def _():
        gather(slots_ref, 0)

    @pl.when(i + 1 < n)
    def _():
        gather(slots_next_ref, 1 - cur)

    for k in range(2):
        pltpu.make_async_copy(ys_hbm.at[pl.ds(0, tm)], gbuf.at[cur, k], sems.at[cur]).wait()
    route = route_ref[...]
    lane = lax.broadcasted_iota(jnp.int32, route.shape, 1)
    w1 = jnp.sum(jnp.where(lane == ROUTE_W1, route, 0.0), axis=1, keepdims=True)
    w2 = jnp.sum(jnp.where(lane == ROUTE_W2, route, 0.0), axis=1, keepdims=True)
    hf = h_ref[...] + (w1 * gbuf[cur, 0] + w2 * gbuf[cur, 1])
    y = hf * lax.rsqrt(jnp.mean(hf * hf, axis=-1, keepdims=True) + NORM_EPS) * gfin_ref[...]

    @pl.when(i < n_prompt_tiles)
    def _():
        yp_ref[...] = y

    @pl.when(i >= n_prompt_tiles)
    def _():
        ysm_ref[...] = y


def _combine(h, route, slots, ys, gfin, n_prompt_tiles):
    t = h.shape[0]
    tm = MOE_TM
    n_p = n_prompt_tiles
    return pl.pallas_call(
        functools.partial(_combine_kernel, n_prompt_tiles=n_p),
        grid_spec=pltpu.PrefetchScalarGridSpec(
            num_scalar_prefetch=0,
            grid=(t // tm,),
            in_specs=[pl.BlockSpec((1, 1, 2 * tm), lambda i: (i, 0, 0), memory_space=pltpu.SMEM),
                      pl.BlockSpec((1, 1, 2 * tm), lambda i: (jnp.minimum(i + 1, t // tm - 1), 0, 0),
                                   memory_space=pltpu.SMEM),
                      pl.BlockSpec((tm, D_MODEL), lambda i: (i, 0)),
                      pl.BlockSpec((tm, ROUTER_LANES), lambda i: (i, 0)),
                      pl.BlockSpec((1, D_MODEL), lambda i: (0, 0)),
                      pl.BlockSpec(memory_space=pl.ANY)],
            out_specs=[pl.BlockSpec((tm, D_MODEL), lambda i: (jnp.minimum(i, n_p - 1), 0)),
                       pl.BlockSpec((tm, D_MODEL), lambda i: (0, 0))],
            scratch_shapes=[pltpu.VMEM((2, 2, tm, D_MODEL), F32), pltpu.SemaphoreType.DMA((2,))],
        ),
        out_shape=[jax.ShapeDtypeStruct((n_p * tm, D_MODEL), F32), jax.ShapeDtypeStruct((tm, D_MODEL), F32)],
        compiler_params=_cparams("arbitrary"),
        name="moe_combine",
    )(slots, slots, h, route, gfin, ys)


def _v_tiles(v, tblk):
    b, t, _ = v.shape
    x = v.reshape(b, t // tblk, tblk, 2, RWKV_PAIRS, RWKV_HEAD).transpose(0, 1, 4, 5, 3, 2)
    x = jnp.pad(x, ((0, 0),) * 5 + ((0, RWKV_HEAD - tblk),))
    return x.reshape(b, t // tblk, RWKV_PAIRS, RWKV_HEAD, LANE).astype(BF16)


def kernel(x_prompt, x_sample, state_gla, state_rwkv, state_shift, meta_tokens, norm_mix, w_in, gla_gate_w2,
           gla_gate_b, gla_norm, rwkv_mu, rwkv_w0, rwkv_w2, rwkv_a0, rwkv_a2, rwkv_g2, rwkv_kk, rwkv_ka, rwkv_rk,
           rwkv_ln_w, rwkv_ln_b, w_out, norm_ffn, router_group_w, router_group_b, router_expert_w,
           router_expert_b, moe_w1, moe_w3, moe_w2, norm_final):
    bp, tp, _ = x_prompt.shape
    bs, ts, _ = x_sample.shape
    assert state_gla.shape[0] == 1, "one layer"
    lyr = 0

    w_in_l = w_in[lyr]
    wg = jnp.pad(w_in_l[:, :GLA_COLS], ((0, 0), (0, GLA_PCOLS - GLA_COLS))).astype(BF16)
    wr = w_in_l[:, GLA_COLS:].astype(BF16)
    g_mix = norm_mix[lyr][None, :]
    gw2p = jnp.pad(gla_gate_w2[lyr], ((0, LANE - GLA_GATE_RANK), (0, 0)))
    gb = gla_gate_b[lyr][None, :]
    gn = gla_norm[lyr][None, :]
    w2p = jnp.pad(rwkv_w2[lyr], ((0, 64), (0, 0))).astype(BF16)
    a2p = jnp.pad(rwkv_a2[lyr], ((64, 0), (0, 0))).astype(BF16)
    pre_params = (rwkv_mu[lyr][None, :], rwkv_w0[lyr][None, :], w2p, rwkv_a0[lyr][None, :], a2p,
                  rwkv_g2[lyr].astype(BF16), rwkv_kk[lyr][None, :], rwkv_ka[lyr][None, :],
                  rwkv_rk[lyr].reshape(1, RWKV_WIDTH), _head_selector())
    lnw = rwkv_ln_w[lyr][None, :]
    lnb = rwkv_ln_b[lyr][None, :]
    wo = w_out[lyr].astype(BF16)
    gffn = norm_ffn[lyr][None, :]
    n_used = N_GROUPS + N_EXPERTS
    w_router = jnp.pad(
        jnp.concatenate([router_group_w[lyr],
                         router_expert_w[lyr].transpose(1, 0, 2).reshape(D_MODEL, N_EXPERTS)], axis=1),
        ((0, 0), (0, ROUTER_LANES - n_used)))
    b_router = jnp.pad(jnp.concatenate([router_group_b[lyr], router_expert_b[lyr].reshape(N_EXPERTS)]),
                       (0, ROUTER_LANES - n_used))[None, :]
    gfin = norm_final[None, :]

    pg_m, pr_m = _inproj(meta_tokens, g_mix, wg, wr, N_META)
    _, sg_m = _gla(pg_m[None], jnp.zeros((1, GLA_HEADS, GLA_DK, GLA_DV), F32), gw2p, gb, gn,
                   bb=1, chunk=N_META, sub=N_META, t_valid=N_META)
    r, w, k, kk, kka, v_m, _, _ = _rwkv_pre(pr_m[None], jnp.zeros((1, 1, RWKV_COLS), F32), pre_params,
                                            tm=N_META, explicit_prev=False, emit_vt=False)
    _, sr_m = _rwkv_rec(r, w, k, kk, kka, _v_tiles(v_m, N_META),
                        jnp.zeros((1, RWKV_HEADS, RWKV_HEAD, RWKV_HEAD), F32), bb=1, n_steps=N_META)

    xp = x_prompt.reshape(bp * tp, D_MODEL)
    pg_p, pr_p = _inproj(xp, g_mix, wg, wr, INPROJ_TM)
    og_p, sg_p = _gla(pg_p.reshape(bp, tp, GLA_PCOLS), jnp.broadcast_to(sg_m, (bp,) + sg_m.shape[1:]), gw2p, gb, gn,
                      bb=SEQ_BLOCK, chunk=GLA_CHUNK, sub=GLA_SUB, t_valid=GLA_CHUNK)
    pr_p3 = pr_p.reshape(bp, tp, RWKV_COLS)
    first_prev = jnp.broadcast_to(pr_m[N_META - 1][None, None, :], (bp, 1, RWKV_COLS))
    r, w, k, kk, kka, vt_p, bv_p, gate_p = _rwkv_pre(pr_p3, first_prev, pre_params, tm=PRE_TM, explicit_prev=False,
                                                     emit_vt=True)
    y_p, sr_p = _rwkv_rec(r, w, k, kk, kka, vt_p, jnp.broadcast_to(sr_m, (bp,) + sr_m.shape[1:]),
                          bb=bp, n_steps=REC_TB)
    prompt_rows = (xp, og_p.reshape(bp * tp, GLA_WIDTH), y_p.reshape(tp, bp * RWKV_WIDTH),
                   bv_p.reshape(bp * tp, RWKV_WIDTH), gate_p.reshape(bp * tp, RWKV_WIDTH))

    xs = x_sample.transpose(1, 0, 2).reshape(ts * bs, D_MODEL)
    pg_s, pr_s = _inproj(xs, g_mix, wg, wr, bs * ts)
    ts_pad = 8
    pg_s3 = jnp.pad(pg_s.reshape(ts, bs, GLA_PCOLS).transpose(1, 0, 2), ((0, 0), (0, ts_pad - ts), (0, 0)))
    og_s, sg_s = _gla(pg_s3, state_gla[lyr], gw2p, gb, gn, bb=SEQ_BLOCK, chunk=ts_pad, sub=ts_pad, t_valid=ts)
    og_s = og_s[:, :ts].transpose(1, 0, 2).reshape(ts * bs, GLA_WIDTH)
    pr_s3 = pr_s.reshape(ts, bs, RWKV_COLS)
    prev_s = jnp.concatenate([state_shift[lyr][None], pr_s3[:-1]], axis=0)
    r, w, k, kk, kka, v_s, bv_s, gate_s = _rwkv_pre(pr_s3.reshape(1, bs * ts, RWKV_COLS),
                                                     prev_s.reshape(1, bs * ts, RWKV_COLS), pre_params,
                                                     tm=bs * ts, explicit_prev=True, emit_vt=False, pair_out=False)
    y_s, sr_s = _rwkv_lanes(r[0], w[0], k[0], kk[0], kka[0], v_s[0], state_rwkv[lyr].transpose(1, 2, 3, 0),
                            n_steps=ts)
    sr_s = sr_s.transpose(3, 0, 1, 2)
    sample_rows = (xs, og_s, y_s, bv_s[0], gate_s[0])

    h_all, n2_all, route, route_t, counts = _mix_router(prompt_rows, sample_rows, lnw, lnb, wo, gffn, w_router,
                                                        b_router, seq_tiles=tp // MOE_TM)
    n_tok = h_all.shape[0]
    n_p_tiles = (bp * tp) // MOE_TM
    cnt = counts[0, EXPERT_LANE0:EXPERT_LANE0 + N_EXPERTS].astype(jnp.int32)
    offs = jnp.cumsum(cnt) - cnt
    eid = route_t[:, ROUTE_E1:ROUTE_E2 + 1, :].astype(jnp.int32)
    pos = route_t[:, ROUTE_P1:ROUTE_P2 + 1, :].astype(jnp.int32)
    off = sum(jnp.where(eid == e, offs[e], 0) for e in range(N_EXPERTS))
    slots = (off + pos).reshape(n_tok // MOE_TM, 1, 2 * MOE_TM)
    xs_sorted = _dispatch(n2_all, slots)
    ys_sorted = _experts(xs_sorted, _expert_work_items(cnt, 2 * n_tok), moe_w1[lyr], moe_w3[lyr], moe_w2[lyr])
    y_prompt, y_sample = _combine(h_all, route, slots, ys_sorted, gfin, n_p_tiles)
    y_prompt = y_prompt.reshape(bp, tp, D_MODEL)
    y_sample = y_sample.reshape(ts, bs, D_MODEL).transpose(1, 0, 2)

    return (y_prompt, y_sample,
            sg_p[None], sr_p[None], pr_p3[:, -1][None],
            sg_s[None], sr_s[None], pr_s3[ts - 1][None])
```

```python
import functools

import jax
import jax.numpy as jnp
from jax import lax
from jax.experimental import pallas as pl
from jax.experimental.pallas import tpu as pltpu

F32 = jnp.float32
BF16 = jnp.bfloat16
HIGHEST = lax.Precision.HIGHEST

D_MODEL = 1024
N_META = 16
NORM_EPS = 1e-6
LOG2E = 1.4426950408889634
GLA_HEADS = 4
GLA_DK = 64
GLA_DV = 128
GLA_QK = GLA_HEADS * GLA_DK
GLA_WIDTH = GLA_HEADS * GLA_DV
GLA_GATE_RANK = 16
GLA_GATE_NORM = 16.0
GLA_CHUNK = 64
GLA_SUB = 8
GLA_COLS = 2 * GLA_QK + 2 * GLA_WIDTH + GLA_GATE_RANK
GLA_PCOLS = 2 * GLA_QK + 2 * GLA_WIDTH + 128
RWKV_WIDTH = 512
RWKV_HEAD = 64
RWKV_HEADS = 8
RWKV_PAIRS = RWKV_HEADS // 2
RWKV_DECAY_SCALE = 0.606531
RWKV_GN_EPS = 64e-5
RWKV_COLS = 3 * RWKV_WIDTH + 64 + 64 + 128
REC_TB = 64
REC_UNROLL = 8
INPROJ_TM = 512
PRE_TM = 512
SEQ_BLOCK = 8
N_GROUPS = 4
EXPERTS_PER_GROUP = 8
N_EXPERTS = 32
D_EXPERT = 512
ROUTER_LANES = 128
EXPERT_LANE0 = N_GROUPS
ROUTE_E1, ROUTE_E2, ROUTE_W1, ROUTE_W2, ROUTE_P1, ROUTE_P2 = range(6)
MOE_TM = 512
MOE_TS = 512

LANE = 128
VMEM_LIMIT = 56 * 1024 * 1024


def _cparams(*sem):
    return pltpu.CompilerParams(dimension_semantics=sem, vmem_limit_bytes=VMEM_LIMIT)


def _block_ones(n, blk):
    i = jnp.arange(n)
    return (i[:, None] // blk == i[None, :] // blk).astype(BF16)


def _sigmoid(x):
    return 1.0 / (1.0 + jnp.exp(-x))


def _dot(a, b):
    return jnp.dot(a, b, preferred_element_type=F32)


def _dot_nt(a, b):
    return lax.dot_general(a, b, (((1,), (1,)), ((), ())), preferred_element_type=F32)


def _dot_tn(a, b):
    return lax.dot_general(a, b, (((0,), (0,)), ((), ())), preferred_element_type=F32)


def _split2(x):
    hi = x.astype(BF16)
    lo = (x - hi.astype(F32)).astype(BF16)
    return hi, lo


def _head_selector():
    return (jnp.arange(RWKV_WIDTH)[:, None] // RWKV_HEAD == jnp.arange(LANE)[None, :]).astype(BF16)


def _group_sum(x, sel):
    hi, lo = _split2(x)
    s_hi, s_lo = _split2(_dot(hi, sel) + _dot(lo, sel))
    return _dot_nt(s_hi, sel) + _dot_nt(s_lo, sel)


def _inproj_kernel(x_ref, g_ref, wg_ref, wr_ref, pg_ref, pr_ref):
    x = x_ref[...]
    n = x * lax.rsqrt(jnp.mean(x * x, axis=-1, keepdims=True) + NORM_EPS) * g_ref[...]
    nb = n.astype(BF16)
    pg_ref[...] = _dot(nb, wg_ref[...])
    pr_ref[...] = _dot(nb, wr_ref[...])


def _inproj(x, g, wg, wr, tm):
    t = x.shape[0]
    return pl.pallas_call(
        _inproj_kernel,
        grid=(t // tm,),
        in_specs=[
            pl.BlockSpec((tm, D_MODEL), lambda i: (i, 0)),
            pl.BlockSpec((1, D_MODEL), lambda i: (0, 0)),
            pl.BlockSpec((D_MODEL, GLA_PCOLS), lambda i: (0, 0)),
            pl.BlockSpec((D_MODEL, RWKV_COLS), lambda i: (0, 0)),
        ],
        out_specs=[
            pl.BlockSpec((tm, GLA_PCOLS), lambda i: (i, 0)),
            pl.BlockSpec((tm, RWKV_COLS), lambda i: (i, 0)),
        ],
        out_shape=[jax.ShapeDtypeStruct((t, GLA_PCOLS), F32), jax.ShapeDtypeStruct((t, RWKV_COLS), F32)],
        compiler_params=_cparams("parallel"),
        name="inproj",
    )(x, g, wg, wr)


def _gla_kernel(pg_ref, s0_ref, gw2_ref, gb_ref, gn_ref, bo_ref, tril_ref, o_ref, sout_ref, s_scr,
                *, bb, chunk, sub, t_valid):
    ci = pl.program_id(1)

    @pl.when(ci == 0)
    def _():
        s_scr[...] = s0_ref[...]

    bo = bo_ref[...]
    tril = tril_ref[...]
    lane = lax.broadcasted_iota(jnp.int32, (sub, LANE), 1) & (GLA_DK - 1)
    rowi = lax.broadcasted_iota(jnp.int32, (sub, LANE), 0)
    head0_s = lax.broadcasted_iota(jnp.int32, (sub, LANE), 1) < GLA_DK
    head0_c = lax.broadcasted_iota(jnp.int32, (chunk, LANE), 1) < GLA_DK

    gl_all = pg_ref[:, :, 2 * GLA_QK + 2 * GLA_WIDTH:].reshape(bb * chunk, LANE)
    z_all = jnp.dot(gl_all, gw2_ref[...], precision=HIGHEST, preferred_element_type=F32) + gb_ref[...]
    lg_all = (jnp.minimum(z_all, 0.0) - jnp.log1p(jnp.exp(-jnp.abs(z_all)))) * (LOG2E / GLA_GATE_NORM)
    if t_valid < chunk:
        rows = lax.broadcasted_iota(jnp.int32, lg_all.shape, 0) & (chunk - 1)
        lg_all = jnp.where(rows < t_valid, lg_all, 0.0)
    if tril.shape[0] == bb * chunk:
        b_all = jnp.dot(tril, lg_all, precision=HIGHEST, preferred_element_type=F32)

    for bi in range(bb):
        pg = pg_ref[bi]
        q = pg[:, 0:GLA_QK] * (GLA_DK ** -0.5)
        k = pg[:, GLA_QK:2 * GLA_QK]
        v = pg[:, 2 * GLA_QK:2 * GLA_QK + GLA_WIDTH]
        g = pg[:, 2 * GLA_QK + GLA_WIDTH:2 * GLA_QK + 2 * GLA_WIDTH]
        if tril.shape[0] == bb * chunk:
            b = b_all[bi * chunk:(bi + 1) * chunk]
        else:
            b = jnp.dot(tril, lg_all[bi * chunk:(bi + 1) * chunk], precision=HIGHEST, preferred_element_type=F32)
        eb = jnp.exp2(b)
        blast = b[chunk - 1:chunk, :]
        kl = k * jnp.exp2(blast - b)
        qe = q * eb

        n_blk = chunk // sub
        n_pairs = GLA_HEADS // 2
        ps = []
        for hp in range(n_pairs):
            sl = slice(hp * LANE, (hp + 1) * LANE)
            for blk in range(n_blk):
                rs = slice(blk * sub, (blk + 1) * sub)
                qb, kb, bbk = q[rs, sl], k[rs, sl], b[rs, sl]
                for j in range(sub):
                    ps.append(qb * (kb[j:j + 1] * jnp.exp2(jnp.minimum(bbk - bbk[j:j + 1], 0.0))))
        red = _dot(jnp.concatenate(ps, axis=0).astype(BF16), bo)

        o_heads = []
        for hp in range(n_pairs):
            sl = slice(hp * LANE, (hp + 1) * LANE)
            kp, bp = k[:, sl], b[:, sl]
            row_blocks = []
            for blk in range(n_blk):
                rs = slice(blk * sub, (blk + 1) * sub)
                base = (hp * n_blk + blk) * sub * sub
                a = jnp.zeros((sub, LANE), F32)
                for j in range(sub):
                    a = jnp.where((lane == blk * sub + j) & (rowi >= j), red[base + j * sub:base + (j + 1) * sub], a)
                if blk > 0:
                    bref = bp[blk * sub - 1:blk * sub]
                    qt = q[rs, sl] * jnp.exp2(bp[rs] - bref)
                    kt = (kp * jnp.exp2(jnp.minimum(bref - bp, 0.0))).astype(BF16)
                    qt2 = jnp.concatenate([jnp.where(head0_s, qt, 0.0), jnp.where(head0_s, 0.0, qt)], axis=0)
                    off2 = _dot_nt(qt2.astype(BF16), kt)
                    a = jnp.where(lane < blk * sub, jnp.concatenate([off2[:sub], off2[sub:]], axis=1), a)
                row_blocks.append(a)
            a_pair = row_blocks[0] if n_blk == 1 else jnp.concatenate(row_blocks, axis=0)
            v0 = v[:, 2 * hp * GLA_DV:(2 * hp + 1) * GLA_DV]
            v1 = v[:, (2 * hp + 1) * GLA_DV:(2 * hp + 2) * GLA_DV]
            s_pair = s_scr[bi, 2 * hp:2 * hp + 2].reshape(2 * GLA_DK, GLA_DV)
            qe_p, kl_p = qe[:, sl], kl[:, sl]

            def by_head(x):
                return jnp.concatenate([jnp.where(head0_c, x, 0.0), jnp.where(head0_c, 0.0, x)], axis=0)

            if chunk == GLA_DK:
                v_rows = jnp.concatenate([v0, v1], axis=0)
            else:
                zpad = jnp.zeros((GLA_DK - chunk, GLA_DV), F32)
                v_rows = jnp.concatenate([v0, zpad, v1, zpad], axis=0)
            lhs = jnp.concatenate([by_head(a_pair), by_head(qe_p)], axis=1).astype(BF16)
            rhs = jnp.concatenate([v_rows, s_pair], axis=0).astype(BF16)
            o2 = _dot(lhs, rhs)
            upd = _dot_tn(by_head(kl_p).astype(BF16), jnp.concatenate([v0, v1], axis=0).astype(BF16))
            dcol = jnp.broadcast_to(jnp.exp2(blast[:, sl]), (8, LANE)).T[:, 0:1]
            s_new = dcol * s_pair + upd
            s_scr[bi, 2 * hp] = s_new[:GLA_DK]
            s_scr[bi, 2 * hp + 1] = s_new[GLA_DK:]
            for h2 in range(2):
                o_h = o2[h2 * chunk:(h2 + 1) * chunk]
                o_heads.append(o_h * lax.rsqrt(jnp.mean(o_h * o_h, axis=-1, keepdims=True) + NORM_EPS) * gn_ref[...])
        o = jnp.concatenate(o_heads, axis=1)
        o_ref[bi] = o * (g * _sigmoid(g))

    @pl.when(ci == pl.num_programs(1) - 1)
    def _():
        sout_ref[...] = s_scr[...]


def _gla(pg, s0, gw2p, gb, gn, *, bb, chunk, sub, t_valid):
    b, t, _ = pg.shape
    assert chunk & (chunk - 1) == 0
    tril = jnp.tril(jnp.ones((chunk, chunk), F32))
    if bb * chunk <= LANE:
        tril = jnp.kron(jnp.eye(bb, dtype=F32), tril)
    kern = functools.partial(_gla_kernel, bb=bb, chunk=chunk, sub=sub, t_valid=t_valid)
    return pl.pallas_call(
        kern,
        grid=(b // bb, t // chunk),
        in_specs=[
            pl.BlockSpec((bb, chunk, GLA_PCOLS), lambda i, j: (i, j, 0)),
            pl.BlockSpec((bb, GLA_HEADS, GLA_DK, GLA_DV), lambda i, j: (i, 0, 0, 0)),
            pl.BlockSpec((LANE, GLA_QK), lambda i, j: (0, 0)),
            pl.BlockSpec((1, GLA_QK), lambda i, j: (0, 0)),
            pl.BlockSpec((1, GLA_DV), lambda i, j: (0, 0)),
            pl.BlockSpec((LANE, LANE), lambda i, j: (0, 0)),
            pl.BlockSpec(tril.shape, lambda i, j: (0, 0)),
        ],
        out_specs=[
            pl.BlockSpec((bb, chunk, GLA_WIDTH), lambda i, j: (i, j, 0)),
            pl.BlockSpec((bb, GLA_HEADS, GLA_DK, GLA_DV), lambda i, j: (i, 0, 0, 0)),
        ],
        out_shape=[jax.ShapeDtypeStruct((b, t, GLA_WIDTH), F32),
                   jax.ShapeDtypeStruct((b, GLA_HEADS, GLA_DK, GLA_DV), F32)],
        scratch_shapes=[pltpu.VMEM((bb, GLA_HEADS, GLA_DK, GLA_DV), F32)],
        compiler_params=_cparams("parallel", "arbitrary"),
        name="gla_chunk",
    )(pg, s0, gw2p, gb, gn, _block_ones(LANE, GLA_DK), tril)


def _rwkv_pre_kernel(pr_ref, aux_ref, mu_ref, w0_ref, w2_ref, a0_ref, a2_ref, g2_ref, kk_ref, ka_ref, rk_ref, bo_ref,
                     r_out, w_out, k_out, kkn_out, kka_out, v_out, bv_out, gate_out, carry_scr,
                     *, tm, explicit_prev, emit_vt, pair_out):
    pr = pr_ref[0]
    if explicit_prev:
        prev = aux_ref[0]
    else:
        j = pl.program_id(1)
        row0 = jnp.where(j == 0, aux_ref[0], carry_scr[...])
        rows = lax.broadcasted_iota(jnp.int32, pr.shape, 0)
        prev = jnp.where(rows == 0, row0, pltpu.roll(pr, 1, 0))
        carry_scr[...] = pr[tm - 1:tm, :]
    xm = pr + (prev - pr) * mu_ref[...]
    wd = RWKV_WIDTH
    rr, rk, rv = xm[:, 0:wd], xm[:, wd:2 * wd], xm[:, 2 * wd:3 * wd]
    wa = xm[:, 3 * wd:3 * wd + LANE]
    gl2 = xm[:, 3 * wd + LANE:3 * wd + 2 * LANE]
    logw = -RWKV_DECAY_SCALE * _sigmoid(w0_ref[...] + _dot(jnp.tanh(wa).astype(BF16), w2_ref[...]))
    aa = _sigmoid(a0_ref[...] + _dot(wa.astype(BF16), a2_ref[...]))
    gate = _dot(_sigmoid(gl2).astype(BF16), g2_ref[...])
    bo = bo_ref[...]
    kk = rk * kk_ref[...]
    kk = kk / jnp.maximum(jnp.sqrt(_group_sum(kk * kk, bo)), 1e-12)
    k = rk * (1.0 + (aa - 1.0) * ka_ref[...])
    bv = _group_sum(rr * k * rk_ref[...], bo) * rv
    w = jnp.exp(logw)
    kka = kk * aa
    hd = RWKV_HEAD

    def pair(x, hp):
        return jnp.concatenate([x[:, hp * hd:(hp + 1) * hd], x[:, (hp + RWKV_PAIRS) * hd:(hp + RWKV_PAIRS + 1) * hd]],
                               axis=1)

    if pair_out:
        for hp in range(RWKV_PAIRS):
            r_out[0, hp] = pair(rr, hp)
            w_out[0, hp] = pair(w, hp)
            k_out[0, hp] = pair(k, hp)
            kkn_out[0, hp] = pair(kk, hp)
            kka_out[0, hp] = pair(kka, hp)
    else:
        r_out[0], w_out[0], k_out[0], kkn_out[0], kka_out[0] = rr, w, k, kk, kka
    if emit_vt:
        vt = rv.T
        for tb in range(tm // REC_TB):
            ts = slice(tb * REC_TB, (tb + 1) * REC_TB)
            for hp in range(RWKV_PAIRS):
                lo, hi = hp * hd, (hp + RWKV_PAIRS) * hd
                v_out[0, tb, hp] = jnp.concatenate([vt[lo:lo + hd, ts], vt[hi:hi + hd, ts]], axis=1).astype(BF16)
    else:
        v_out[0] = rv
    bv_out[0] = bv
    gate_out[0] = gate


def _rwkv_pre(pr, aux, params, *, tm, explicit_prev, emit_vt, pair_out=True):
    b, t, _ = pr.shape
    kern = functools.partial(_rwkv_pre_kernel, tm=tm, explicit_prev=explicit_prev, emit_vt=emit_vt,
                             pair_out=pair_out)
    aux_spec = (pl.BlockSpec((1, tm, RWKV_COLS), lambda i, j: (i, j, 0)) if explicit_prev
                else pl.BlockSpec((1, 1, RWKV_COLS), lambda i, j: (i, 0, 0)))
    const = lambda shape: pl.BlockSpec(shape, lambda i, j: (0,) * len(shape))
    pair_spec = pl.BlockSpec((1, RWKV_PAIRS, tm, LANE), lambda i, j: (i, 0, j, 0))
    row_spec = pl.BlockSpec((1, tm, RWKV_WIDTH), lambda i, j: (i, j, 0))
    pair_shape = jax.ShapeDtypeStruct((b, RWKV_PAIRS, t, LANE), F32)
    row_shape = jax.ShapeDtypeStruct((b, t, RWKV_WIDTH), F32)
    if emit_vt:
        v_spec = pl.BlockSpec((1, tm // REC_TB, RWKV_PAIRS, RWKV_HEAD, LANE), lambda i, j: (i, j, 0, 0, 0))
        v_shape = jax.ShapeDtypeStruct((b, t // REC_TB, RWKV_PAIRS, RWKV_HEAD, LANE), BF16)
    else:
        v_spec, v_shape = row_spec, row_shape
    if not pair_out:
        pair_spec, pair_shape = row_spec, row_shape
    return pl.pallas_call(
        kern,
        grid=(b, t // tm),
        in_specs=[
            pl.BlockSpec((1, tm, RWKV_COLS), lambda i, j: (i, j, 0)),
            aux_spec,
            const((1, RWKV_COLS)), const((1, RWKV_WIDTH)), const((LANE, RWKV_WIDTH)), const((1, RWKV_WIDTH)),
            const((LANE, RWKV_WIDTH)), const((LANE, RWKV_WIDTH)), const((1, RWKV_WIDTH)), const((1, RWKV_WIDTH)),
            const((1, RWKV_WIDTH)), const((RWKV_WIDTH, LANE)),
        ],
        out_specs=[pair_spec] * 5 + [v_spec, row_spec, row_spec],
        out_shape=[pair_shape] * 5 + [v_shape, row_shape, row_shape],
        scratch_shapes=[pltpu.VMEM((1, RWKV_COLS), F32)],
        compiler_params=_cparams("parallel", "arbitrary"),
        name="rwkv_pre",
    )(pr, aux, *params)


def _rwkv_rec_kernel(r_ref, w_ref, k_ref, kk_ref, kka_ref, vt_ref, s0_ref, bo_ref, vsel_ref, ysel_ref,
                     y_ref, sout_ref, s_scr, t1_scr, t3_scr, yt_scr, *, bb, n_steps):
    tb = pl.program_id(1)
    nc = bb * RWKV_PAIRS
    hd = RWKV_HEAD

    @pl.when(tb == 0)
    def _():
        for c in range(nc):
            bi, hp = divmod(c, RWKV_PAIRS)
            s_scr[c] = jnp.concatenate([s0_ref[bi, hp], s0_ref[bi, hp + RWKV_PAIRS]], axis=1)

    bo = bo_ref[...]

    def step(t, u):
        row = pl.ds(t, 1)
        for c in range(nc):
            bi, hp = divmod(c, RWKV_PAIRS)
            t1_scr[c * hd:(c + 1) * hd, :] = (s_scr[c] * kk_ref[bi, hp, row, :]).astype(BF16)
        sab = _dot(t1_scr[...], bo)
        vb = _dot(vt_ref[...].reshape(nc * hd, LANE), vsel_ref[t])
        for c in range(nc):
            bi, hp = divmod(c, RWKV_PAIRS)
            rs = slice(c * hd, (c + 1) * hd)
            s2 = (s_scr[c] * w_ref[bi, hp, row, :] - sab[rs] * kka_ref[bi, hp, row, :]
                  + vb[rs] * k_ref[bi, hp, row, :])
            s_scr[c] = s2
            t3_scr[rs, :] = (s2 * r_ref[bi, hp, row, :]).astype(BF16)
        yt_scr[...] += _dot_nt(ysel_ref[u], t3_scr[...])

    n_inner = min(8, n_steps)

    def block8(t8, carry):
        yt_scr[...] = jnp.zeros(yt_scr.shape, F32)

        def inner(u, c2):
            step(t8 * 8 + u, u)
            return c2

        lax.fori_loop(0, n_inner, inner, 0, unroll=REC_UNROLL)
        t0 = pl.multiple_of(t8 * 8, 8)
        blk = RWKV_PAIRS * hd
        for bi in range(bb):
            for h2 in range(2):
                y_ref[0, pl.ds(t0, 8), bi * RWKV_WIDTH + h2 * blk:bi * RWKV_WIDTH + (h2 + 1) * blk] = (
                    yt_scr[h2 * 8:(h2 + 1) * 8, bi * blk:(bi + 1) * blk])
        return carry

    lax.fori_loop(0, (n_steps + 7) // 8, block8, 0)

    @pl.when(tb == pl.num_programs(1) - 1)
    def _():
        for c in range(nc):
            bi, hp = divmod(c, RWKV_PAIRS)
            s_c = s_scr[c]
            sout_ref[bi, hp] = s_c[:, :RWKV_HEAD]
            sout_ref[bi, hp + RWKV_PAIRS] = s_c[:, RWKV_HEAD:]


def _rwkv_rec(r, w, k, kk, kka, vt, s0, *, bb, n_steps):
    b, _, t, _ = r.shape
    tblk = min(REC_TB, t)
    ntb = t // tblk
    nc = bb * RWKV_PAIRS
    lane = jnp.arange(LANE)
    vsel = ((lane[None, :, None] // RWKV_HEAD == lane[None, None, :] // RWKV_HEAD)
            & (lane[None, :, None] % RWKV_HEAD == jnp.arange(RWKV_HEAD)[:, None, None])).astype(BF16)
    ysel = (jnp.arange(16)[None, :, None]
            == 8 * (lane[None, None, :] // RWKV_HEAD) + jnp.arange(8)[:, None, None]).astype(BF16)
    kern = functools.partial(_rwkv_rec_kernel, bb=bb, n_steps=n_steps)
    pair_spec = pl.BlockSpec((bb, RWKV_PAIRS, tblk, LANE), lambda i, j: (i, 0, j, 0))
    state_spec = pl.BlockSpec((bb, RWKV_HEADS, RWKV_HEAD, RWKV_HEAD), lambda i, j: (i, 0, 0, 0))
    ytb = max(tblk, 8)
    return pl.pallas_call(
        kern,
        grid=(b // bb, ntb),
        in_specs=[pair_spec] * 5 + [
            pl.BlockSpec((bb, 1, RWKV_PAIRS, RWKV_HEAD, LANE), lambda i, j: (i, j, 0, 0, 0)),
            state_spec,
            pl.BlockSpec((LANE, LANE), lambda i, j: (0, 0)),
            pl.BlockSpec((RWKV_HEAD, LANE, LANE), lambda i, j: (0, 0, 0)),
            pl.BlockSpec((8, 16, LANE), lambda i, j: (0, 0, 0)),
        ],
        out_specs=[
            pl.BlockSpec((1, ytb, bb * RWKV_WIDTH), lambda i, j: (i, j, 0)),
            state_spec,
        ],
        out_shape=[jax.ShapeDtypeStruct((b // bb, ntb * ytb, bb * RWKV_WIDTH), F32),
                   jax.ShapeDtypeStruct((b, RWKV_HEADS, RWKV_HEAD, RWKV_HEAD), F32)],
        scratch_shapes=[pltpu.VMEM((nc, RWKV_HEAD, LANE), F32),
                        pltpu.VMEM((nc * RWKV_HEAD, LANE), BF16),
                        pltpu.VMEM((nc * RWKV_HEAD, LANE), BF16),
                        pltpu.VMEM((16, nc * RWKV_HEAD), F32)],
        compiler_params=_cparams("parallel", "arbitrary"),
        name="rwkv_rec",
    )(r, w, k, kk, kka, vt, s0, _block_ones(LANE, RWKV_HEAD), vsel, ysel)


def _rwkv_lanes_kernel(r_ref, w_ref, k_ref, kk_ref, kka_ref, v_ref, s0_ref, y_ref, sout_ref, vt_scr, yt_scr,
                       *, n_steps, n_seq):
    hd = RWKV_HEAD
    for t in range(n_steps):
        rows = slice(t * n_seq, (t + 1) * n_seq)
        r_t, w_t, k_t = r_ref[rows, :].T, w_ref[rows, :].T, k_ref[rows, :].T
        kk_t, kka_t = kk_ref[rows, :].T, kka_ref[rows, :].T
        vt_scr[...] = v_ref[rows, :].T
        src = s0_ref if t == 0 else sout_ref
        for h2 in range(2):
            hs = slice(h2 * hd, (h2 + 1) * hd)
            r_h, w_h, k_h, kk_h, kka_h = r_t[hs], w_t[hs], k_t[hs], kk_t[hs], kka_t[hs]

            def value_row(i, carry):
                s = src[h2, i]
                sab = jnp.sum(s * kk_h, axis=0, keepdims=True)
                v_i = vt_scr[pl.ds(h2 * hd + i, 1), :]
                s2 = s * w_h - sab * kka_h + v_i * k_h
                sout_ref[h2, i] = s2
                yt_scr[pl.ds(h2 * hd + i, 1), :] = jnp.sum(s2 * r_h, axis=0, keepdims=True)
                return carry

            lax.fori_loop(0, hd, value_row, 0, unroll=8)
        y_ref[rows, :] = yt_scr[...].T


def _rwkv_lanes(r, w, k, kk, kka, v, s0, *, n_steps):
    n_tok = r.shape[0]
    n_seq = n_tok // n_steps
    assert n_seq == LANE, "one lane per sequence"
    row_spec = pl.BlockSpec((n_tok, LANE), lambda hp: (0, hp))
    state_spec = pl.BlockSpec((2, RWKV_HEAD, RWKV_HEAD, n_seq), lambda hp: (hp, 0, 0, 0))
    return pl.pallas_call(
        functools.partial(_rwkv_lanes_kernel, n_steps=n_steps, n_seq=n_seq),
        grid=(RWKV_HEADS // 2,),
        in_specs=[row_spec] * 6 + [state_spec],
        out_specs=[row_spec, state_spec],
        out_shape=[jax.ShapeDtypeStruct((n_tok, RWKV_WIDTH), F32),
                   jax.ShapeDtypeStruct((RWKV_HEADS, RWKV_HEAD, RWKV_HEAD, n_seq), F32)],
        scratch_shapes=[pltpu.VMEM((LANE, n_seq), F32), pltpu.VMEM((LANE, n_seq), F32)],
        compiler_params=_cparams("parallel"),
        name="rwkv_lanes",
    )(r, w, k, kk, kka, v, s0)


def _mix_router_body(x_ref, og_ref, y_ref, bv_ref, gate_ref, lnw_ref, lnb_ref, wo_ref, gffn_ref, wr_hi_ref, wr_lo_ref,
                     br_ref, bo_ref, tril_ref, h_ref, n2_ref, route_ref, route_t_ref, cnt_scr):
    bo = bo_ref[...]
    y = y_ref[...]
    inv_n = 1.0 / RWKV_HEAD
    d = y - _group_sum(y, bo) * inv_n
    var = _group_sum(d * d, bo) * inv_n
    yn = d * lax.rsqrt(var + RWKV_GN_EPS) * lnw_ref[...] + lnb_ref[...] + bv_ref[...]
    o_rwkv = yn * gate_ref[...]
    mix = (_dot(og_ref[...].astype(BF16), wo_ref[0:GLA_WIDTH, :])
           + _dot(o_rwkv.astype(BF16), wo_ref[GLA_WIDTH:, :]))
    h = x_ref[...] + mix
    h_ref[...] = h
    n2 = h * lax.rsqrt(jnp.mean(h * h, axis=-1, keepdims=True) + NORM_EPS) * gffn_ref[...]
    n2_ref[...] = n2
    n2_hi, n2_lo = _split2(n2)
    lg = (_dot(n2_hi, wr_hi_ref[...]) + _dot(n2_hi, wr_lo_ref[...]) + _dot(n2_lo, wr_hi_ref[...])) + br_ref[...]
    neg = jnp.float32(-3.0e38)
    big = jnp.float32(1.0e9)
    lane = lax.broadcasted_iota(jnp.int32, lg.shape, 1).astype(F32)
    gmask = lane < N_GROUPS
    gmax = jnp.max(jnp.where(gmask, lg, neg), axis=1, keepdims=True)
    p_top = 1.0 / jnp.sum(jnp.where(gmask, jnp.exp(jnp.minimum(lg - gmax, 0.0)), 0.0), axis=1, keepdims=True)
    gidx = jnp.min(jnp.where(gmask & (lg == gmax), lane, big), axis=1, keepdims=True)
    e_lo = EXPERT_LANE0 + gidx * EXPERTS_PER_GROUP
    emask = (lane >= e_lo) & (lane < e_lo + EXPERTS_PER_GROUP)
    m1 = jnp.max(jnp.where(emask, lg, neg), axis=1, keepdims=True)
    e1 = jnp.min(jnp.where(emask & (lg == m1), lane, big), axis=1, keepdims=True)
    emask2 = emask & (lane != e1)
    m2 = jnp.max(jnp.where(emask2, lg, neg), axis=1, keepdims=True)
    e2 = jnp.min(jnp.where(emask2 & (lg == m2), lane, big), axis=1, keepdims=True)
    r21 = jnp.exp(m2 - m1)
    w1 = p_top / (1.0 + r21)
    w2 = p_top * r21 / (1.0 + r21)
    o1 = lane == e1
    o2 = lane == e2
    onehot = jnp.where(o1 | o2, 1.0, 0.0)
    rank = _dot(tril_ref[...], onehot.astype(BF16)) + cnt_scr[...]
    pos1 = jnp.sum(jnp.where(o1, rank, 0.0), axis=1, keepdims=True)
    pos2 = jnp.sum(jnp.where(o2, rank, 0.0), axis=1, keepdims=True)
    cnt_scr[...] += jnp.sum(onehot, axis=0, keepdims=True)
    route = jnp.where(lane == ROUTE_E1, e1 - EXPERT_LANE0, 0.0)
    route = jnp.where(lane == ROUTE_E2, e2 - EXPERT_LANE0, route)
    route = jnp.where(lane == ROUTE_W1, w1, route)
    route = jnp.where(lane == ROUTE_W2, w2, route)
    route = jnp.where(lane == ROUTE_P1, pos1, route)
    route = jnp.where(lane == ROUTE_P2, pos2, route)
    route_ref[...] = route
    route_t_ref[0] = route.T[0:8, :]


def _mix_router_kernel(*refs, n_prompt_tiles):
    prompt_rows, sample_rows, rest = refs[0:5], refs[5:10], refs[10:]
    consts, (h_ref, n2_ref, route_ref, route_t_ref, cnt_ref, cnt_scr) = rest[:9], rest[9:]
    i = pl.program_id(0)

    @pl.when(i == 0)
    def _():
        cnt_scr[...] = jnp.zeros(cnt_scr.shape, F32)

    @pl.when(i < n_prompt_tiles)
    def _():
        _mix_router_body(*prompt_rows, *consts, h_ref, n2_ref, route_ref, route_t_ref, cnt_scr)

    @pl.when(i >= n_prompt_tiles)
    def _():
        _mix_router_body(*sample_rows, *consts, h_ref, n2_ref, route_ref, route_t_ref, cnt_scr)

    cnt_ref[...] = cnt_scr[...]


def _mix_router(prompt_rows, sample_rows, lnw, lnb, wo, gffn, wr, br, *, seq_tiles):
    tm = MOE_TM
    n_p = prompt_rows[0].shape[0] // tm
    assert sample_rows[0].shape[0] == tm
    t = (n_p + 1) * tm
    widths = (D_MODEL, GLA_WIDTH, RWKV_WIDTH, RWKV_WIDTH, RWKV_WIDTH)
    p_specs = [pl.BlockSpec((tm, n), lambda i: (jnp.minimum(i, n_p - 1), 0)) for n in widths]
    p_specs[2] = pl.BlockSpec(
        (tm, RWKV_WIDTH), lambda i: (jnp.minimum(i, n_p - 1) % seq_tiles, jnp.minimum(i, n_p - 1) // seq_tiles))
    s_specs = [pl.BlockSpec((tm, n), lambda i: (0, 0)) for n in widths]
    const = lambda shape: pl.BlockSpec(shape, lambda i: (0,) * len(shape))
    row = lambda n: pl.BlockSpec((tm, n), lambda i: (i, 0))
    tril = jnp.tril(jnp.ones((tm, tm), F32), -1).astype(BF16)
    return pl.pallas_call(
        functools.partial(_mix_router_kernel, n_prompt_tiles=n_p),
        grid=(n_p + 1,),
        in_specs=p_specs + s_specs + [
            const((1, RWKV_WIDTH)), const((1, RWKV_WIDTH)), const((D_MODEL, D_MODEL)), const((1, D_MODEL)),
            const((D_MODEL, ROUTER_LANES)), const((D_MODEL, ROUTER_LANES)), const((1, ROUTER_LANES)),
            const((RWKV_WIDTH, LANE)),
            const((tm, tm))],
        out_specs=[row(D_MODEL), row(D_MODEL), row(ROUTER_LANES), pl.BlockSpec((1, 8, tm), lambda i: (i, 0, 0)),
                   const((1, ROUTER_LANES))],
        out_shape=[jax.ShapeDtypeStruct((t, D_MODEL), F32), jax.ShapeDtypeStruct((t, D_MODEL), F32),
                   jax.ShapeDtypeStruct((t, ROUTER_LANES), F32), jax.ShapeDtypeStruct((t // tm, 8, tm), F32),
                   jax.ShapeDtypeStruct((1, ROUTER_LANES), F32)],
        scratch_shapes=[pltpu.VMEM((1, ROUTER_LANES), F32)],
        compiler_params=_cparams("arbitrary"),
        name="mix_router",
    )(*prompt_rows, *sample_rows, lnw, lnb, wo, gffn, *_split2(wr), br, _head_selector(), tril)


def _dispatch_kernel(slots_ref, x_ref, xs_hbm, sem):
    tm = x_ref.shape[0]

    for r in range(tm):
        src = x_ref.at[pl.ds(r, 1)]
        pltpu.make_async_copy(src, xs_hbm.at[pl.ds(slots_ref[0, 0, r], 1)], sem).start(priority=0)
        pltpu.make_async_copy(src, xs_hbm.at[pl.ds(slots_ref[0, 0, tm + r], 1)], sem).start(priority=1)
    for _ in range(2):
        pltpu.make_async_copy(x_ref, xs_hbm.at[pl.ds(0, tm)], sem).wait()


def _dispatch(n2, slots):
    t = n2.shape[0]
    tm = MOE_TM
    return pl.pallas_call(
        _dispatch_kernel,
        grid_spec=pltpu.PrefetchScalarGridSpec(
            num_scalar_prefetch=0,
            grid=(t // tm,),
            in_specs=[pl.BlockSpec((1, 1, 2 * tm), lambda i: (i, 0, 0), memory_space=pltpu.SMEM),
                      pl.BlockSpec((tm, D_MODEL), lambda i: (i, 0))],
            out_specs=pl.BlockSpec(memory_space=pl.ANY),
            scratch_shapes=[pltpu.SemaphoreType.DMA(())],
        ),
        out_shape=jax.ShapeDtypeStruct((2 * t, D_MODEL), F32),
        compiler_params=_cparams("arbitrary"),
        name="moe_dispatch",
    )(slots, n2)


def _experts_kernel(wt_ref, we_ref, wlo_ref, whi_ref, wfirst_ref, nw_ref,
                    xs_ref, w1_ref, w3_ref, w2_ref, ys_ref, wb1, wb3, wb2):
    w = pl.program_id(0)

    @pl.when(w < nw_ref[0])
    def _():
        new_expert = jnp.logical_or(w == 0, we_ref[w] != we_ref[jnp.maximum(w - 1, 0)])

        @pl.when(new_expert)
        def _():
            wb1[...] = w1_ref[0].astype(BF16)
            wb3[...] = w3_ref[0].astype(BF16)
            wb2[...] = w2_ref[0].astype(BF16)

        x = xs_ref[...].astype(BF16)
        a = _dot(x, wb1[...])
        b = _dot(x, wb3[...])
        o = _dot(((a * _sigmoid(a)) * b).astype(BF16), wb2[...])

        @pl.when(wfirst_ref[w] == 1)
        def _():
            ys_ref[...] = o

        @pl.when(wfirst_ref[w] == 0)
        def _():
            rows = lax.broadcasted_iota(jnp.int32, o.shape, 0)
            ys_ref[...] = jnp.where((rows >= wlo_ref[w]) & (rows < whi_ref[w]), o, ys_ref[...])


def _experts(xs, work, w1, w3, w2):
    s = xs.shape[0]
    ts = MOE_TS
    n_work = work[0].shape[0]
    return pl.pallas_call(
        _experts_kernel,
        grid_spec=pltpu.PrefetchScalarGridSpec(
            num_scalar_prefetch=6,
            grid=(n_work,),
            in_specs=[
                pl.BlockSpec((ts, D_MODEL), lambda w, wt, we, *_: (wt[w], 0)),
                pl.BlockSpec((1, D_MODEL, D_EXPERT), lambda w, wt, we, *_: (we[w], 0, 0)),
                pl.BlockSpec((1, D_MODEL, D_EXPERT), lambda w, wt, we, *_: (we[w], 0, 0)),
                pl.BlockSpec((1, D_EXPERT, D_MODEL), lambda w, wt, we, *_: (we[w], 0, 0)),
            ],
            out_specs=pl.BlockSpec((ts, D_MODEL), lambda w, wt, we, *_: (wt[w], 0)),
            scratch_shapes=[pltpu.VMEM((D_MODEL, D_EXPERT), BF16), pltpu.VMEM((D_MODEL, D_EXPERT), BF16),
                            pltpu.VMEM((D_EXPERT, D_MODEL), BF16)],
        ),
        out_shape=jax.ShapeDtypeStruct((s, D_MODEL), F32),
        compiler_params=_cparams("arbitrary"),
        name="moe_experts",
    )(*work, xs, w1, w3, w2)


def _expert_work_items(counts, total):
    ts = MOE_TS
    n_tiles = total // ts
    n_work = n_tiles + N_EXPERTS - 1
    offs = jnp.cumsum(counts) - counts
    t0 = (jnp.arange(n_tiles, dtype=jnp.int32) * ts)[:, None]
    lo = jnp.maximum(t0, offs[None, :])
    hi = jnp.minimum(t0 + ts, (offs + counts)[None, :])
    nonempty = (hi > lo).reshape(-1)
    nw = jnp.sum(nonempty.astype(jnp.int32))
    idx = jnp.nonzero(nonempty, size=n_work, fill_value=0)[0].astype(jnp.int32)
    idx = jnp.where(jnp.arange(n_work) < nw, idx, idx[jnp.maximum(nw - 1, 0)])
    wt = idx // N_EXPERTS
    we = idx % N_EXPERTS
    wlo = lo.reshape(-1)[idx] - wt * ts
    whi = hi.reshape(-1)[idx] - wt * ts
    wfirst = jnp.concatenate([jnp.ones((1,), jnp.int32), (wt[1:] != wt[:-1]).astype(jnp.int32)])
    return wt, we, wlo, whi, wfirst, nw.reshape(1)


def _combine_kernel(slots_ref, slots_next_ref, h_ref, route_ref, gfin_ref, ys_hbm, yp_ref, ysm_ref, gbuf, sems,
                    *, n_prompt_tiles):
    i = pl.program_id(0)
    n = pl.num_programs(0)
    tm = h_ref.shape[0]

    def gather(s_ref, buf):
        for r in range(tm):
            pltpu.make_async_copy(ys_hbm.at[pl.ds(s_ref[0, 0, r], 1)], gbuf.at[buf, 0, pl.ds(r, 1)],
                                  sems.at[buf]).start(priority=0)
            pltpu.make_async_copy(ys_hbm.at[pl.ds(s_ref[0, 0, tm + r], 1)], gbuf.at[buf, 1, pl.ds(r, 1)],
                                  sems.at[buf]).start(priority=1)

    cur = i % 2

    @pl.when(i == 0)
    def _():
        gather(slots_ref, 0)

    @pl.when(i + 1 < n)
    def _():
        gather(slots_next_ref, 1 - cur)

    for k in range(2):
        pltpu.make_async_copy(ys_hbm.at[pl.ds(0, tm)], gbuf.at[cur, k], sems.at[cur]).wait()
    route = route_ref[...]
    lane = lax.broadcasted_iota(jnp.int32, route.shape, 1)
    w1 = jnp.sum(jnp.where(lane == ROUTE_W1, route, 0.0), axis=1, keepdims=True)
    w2 = jnp.sum(jnp.where(lane == ROUTE_W2, route, 0.0), axis=1, keepdims=True)
    hf = h_ref[...] + (w1 * gbuf[cur, 0] + w2 * gbuf[cur, 1])
    y = hf * lax.rsqrt(jnp.mean(hf * hf, axis=-1, keepdims=True) + NORM_EPS) * gfin_ref[...]

    @pl.when(i < n_prompt_tiles)
    def _():
        yp_ref[...] = y

    @pl.when(i >= n_prompt_tiles)
    def _():
        ysm_ref[...] = y


def _combine(h, route, slots, ys, gfin, n_prompt_tiles):
    t = h.shape[0]
    tm = MOE_TM
    n_p = n_prompt_tiles
    return pl.pallas_call(
        functools.partial(_combine_kernel, n_prompt_tiles=n_p),
        grid_spec=pltpu.PrefetchScalarGridSpec(
            num_scalar_prefetch=0,
            grid=(t // tm,),
            in_specs=[pl.BlockSpec((1, 1, 2 * tm), lambda i: (i, 0, 0), memory_space=pltpu.SMEM),
                      pl.BlockSpec((1, 1, 2 * tm), lambda i: (jnp.minimum(i + 1, t // tm - 1), 0, 0),
                                   memory_space=pltpu.SMEM),
                      pl.BlockSpec((tm, D_MODEL), lambda i: (i, 0)),
                      pl.BlockSpec((tm, ROUTER_LANES), lambda i: (i, 0)),
                      pl.BlockSpec((1, D_MODEL), lambda i: (0, 0)),
                      pl.BlockSpec(memory_space=pl.ANY)],
            out_specs=[pl.BlockSpec((tm, D_MODEL), lambda i: (jnp.minimum(i, n_p - 1), 0)),
                       pl.BlockSpec((tm, D_MODEL), lambda i: (0, 0))],
            scratch_shapes=[pltpu.VMEM((2, 2, tm, D_MODEL), F32), pltpu.SemaphoreType.DMA((2,))],
        ),
        out_shape=[jax.ShapeDtypeStruct((n_p * tm, D_MODEL), F32), jax.ShapeDtypeStruct((tm, D_MODEL), F32)],
        compiler_params=_cparams("arbitrary"),
        name="moe_combine",
    )(slots, slots, h, route, gfin, ys)


def _v_tiles(v, tblk):
    b, t, _ = v.shape
    x = v.reshape(b, t // tblk, tblk, 2, RWKV_PAIRS, RWKV_HEAD).transpose(0, 1, 4, 5, 3, 2)
    x = jnp.pad(x, ((0, 0),) * 5 + ((0, RWKV_HEAD - tblk),))
    return x.reshape(b, t // tblk, RWKV_PAIRS, RWKV_HEAD, LANE).astype(BF16)


def kernel(x_prompt, x_sample, state_gla, state_rwkv, state_shift, meta_tokens, norm_mix, w_in, gla_gate_w2,
           gla_gate_b, gla_norm, rwkv_mu, rwkv_w0, rwkv_w2, rwkv_a0, rwkv_a2, rwkv_g2, rwkv_kk, rwkv_ka, rwkv_rk,
           rwkv_ln_w, rwkv_ln_b, w_out, norm_ffn, router_group_w, router_group_b, router_expert_w,
           router_expert_b, moe_w1, moe_w3, moe_w2, norm_final):
    bp, tp, _ = x_prompt.shape
    bs, ts, _ = x_sample.shape
    assert state_gla.shape[0] == 1, "one layer"
    lyr = 0

    w_in_l = w_in[lyr]
    wg = jnp.pad(w_in_l[:, :GLA_COLS], ((0, 0), (0, GLA_PCOLS - GLA_COLS))).astype(BF16)
    wr = w_in_l[:, GLA_COLS:].astype(BF16)
    g_mix = norm_mix[lyr][None, :]
    gw2p = jnp.pad(gla_gate_w2[lyr], ((0, LANE - GLA_GATE_RANK), (0, 0)))
    gb = gla_gate_b[lyr][None, :]
    gn = gla_norm[lyr][None, :]
    w2p = jnp.pad(rwkv_w2[lyr], ((0, 64), (0, 0))).astype(BF16)
    a2p = jnp.pad(rwkv_a2[lyr], ((64, 0), (0, 0))).astype(BF16)
    pre_params = (rwkv_mu[lyr][None, :], rwkv_w0[lyr][None, :], w2p, rwkv_a0[lyr][None, :], a2p,
                  rwkv_g2[lyr].astype(BF16), rwkv_kk[lyr][None, :], rwkv_ka[lyr][None, :],
                  rwkv_rk[lyr].reshape(1, RWKV_WIDTH), _head_selector())
    lnw = rwkv_ln_w[lyr][None, :]
    lnb = rwkv_ln_b[lyr][None, :]
    wo = w_out[lyr].astype(BF16)
    gffn = norm_ffn[lyr][None, :]
    n_used = N_GROUPS + N_EXPERTS
    w_router = jnp.pad(
        jnp.concatenate([router_group_w[lyr],
                         router_expert_w[lyr].transpose(1, 0, 2).reshape(D_MODEL, N_EXPERTS)], axis=1),
        ((0, 0), (0, ROUTER_LANES - n_used)))
    b_router = jnp.pad(jnp.concatenate([router_group_b[lyr], router_expert_b[lyr].reshape(N_EXPERTS)]),
                       (0, ROUTER_LANES - n_used))[None, :]
    gfin = norm_final[None, :]

    pg_m, pr_m = _inproj(meta_tokens, g_mix, wg, wr, N_META)
    _, sg_m = _gla(pg_m[None], jnp.zeros((1, GLA_HEADS, GLA_DK, GLA_DV), F32), gw2p, gb, gn,
                   bb=1, chunk=N_META, sub=N_META, t_valid=N_META)
    r, w, k, kk, kka, v_m, _, _ = _rwkv_pre(pr_m[None], jnp.zeros((1, 1, RWKV_COLS), F32), pre_params,
                                            tm=N_META, explicit_prev=False, emit_vt=False)
    _, sr_m = _rwkv_rec(r, w, k, kk, kka, _v_tiles(v_m, N_META),
                        jnp.zeros((1, RWKV_HEADS, RWKV_HEAD, RWKV_HEAD), F32), bb=1, n_steps=N_META)

    xp = x_prompt.reshape(bp * tp, D_MODEL)
    pg_p, pr_p = _inproj(xp, g_mix, wg, wr, INPROJ_TM)
    og_p, sg_p = _gla(pg_p.reshape(bp, tp, GLA_PCOLS), jnp.broadcast_to(sg_m, (bp,) + sg_m.shape[1:]), gw2p, gb, gn,
                      bb=SEQ_BLOCK, chunk=GLA_CHUNK, sub=GLA_SUB, t_valid=GLA_CHUNK)
    pr_p3 = pr_p.reshape(bp, tp, RWKV_COLS)
    first_prev = jnp.broadcast_to(pr_m[N_META - 1][None, None, :], (bp, 1, RWKV_COLS))
    r, w, k, kk, kka, vt_p, bv_p, gate_p = _rwkv_pre(pr_p3, first_prev, pre_params, tm=PRE_TM, explicit_prev=False,
                                                     emit_vt=True)
    y_p, sr_p = _rwkv_rec(r, w, k, kk, kka, vt_p, jnp.broadcast_to(sr_m, (bp,) + sr_m.shape[1:]),
                          bb=bp, n_steps=REC_TB)
    prompt_rows = (xp, og_p.reshape(bp * tp, GLA_WIDTH), y_p.reshape(tp, bp * RWKV_WIDTH),
                   bv_p.reshape(bp * tp, RWKV_WIDTH), gate_p.reshape(bp * tp, RWKV_WIDTH))

    xs = x_sample.transpose(1, 0, 2).reshape(ts * bs, D_MODEL)
    pg_s, pr_s = _inproj(xs, g_mix, wg, wr, bs * ts)
    ts_pad = 8
    pg_s3 = jnp.pad(pg_s.reshape(ts, bs, GLA_PCOLS).transpose(1, 0, 2), ((0, 0), (0, ts_pad - ts), (0, 0)))
    og_s, sg_s = _gla(pg_s3, state_gla[lyr], gw2p, gb, gn, bb=SEQ_BLOCK, chunk=ts_pad, sub=ts_pad, t_valid=ts)
    og_s = og_s[:, :ts].transpose(1, 0, 2).reshape(ts * bs, GLA_WIDTH)
    pr_s3 = pr_s.reshape(ts, bs, RWKV_COLS)
    prev_s = jnp.concatenate([state_shift[lyr][None], pr_s3[:-1]], axis=0)
    r, w, k, kk, kka, v_s, bv_s, gate_s = _rwkv_pre(pr_s3.reshape(1, bs * ts, RWKV_COLS),
                                                     prev_s.reshape(1, bs * ts, RWKV_COLS), pre_params,
                                                     tm=bs * ts, explicit_prev=True, emit_vt=False, pair_out=False)
    y_s, sr_s = _rwkv_lanes(r[0], w[0], k[0], kk[0], kka[0], v_s[0], state_rwkv[lyr].transpose(1, 2, 3, 0),
                            n_steps=ts)
    sr_s = sr_s.transpose(3, 0, 1, 2)
    sample_rows = (xs, og_s, y_s, bv_s[0], gate_s[0])

    h_all, n2_all, route, route_t, counts = _mix_router(prompt_rows, sample_rows, lnw, lnb, wo, gffn, w_router,
                                                        b_router, seq_tiles=tp // MOE_TM)
    n_tok = h_all.shape[0]
    n_p_tiles = (bp * tp) // MOE_TM
    cnt = counts[0, EXPERT_LANE0:EXPERT_LANE0 + N_EXPERTS].astype(jnp.int32)
    offs = jnp.cumsum(cnt) - cnt
    eid = route_t[:, ROUTE_E1:ROUTE_E2 + 1, :].astype(jnp.int32)
    pos = route_t[:, ROUTE_P1:ROUTE_P2 + 1, :].astype(jnp.int32)
    off = sum(jnp.where(eid == e, offs[e], 0) for e in range(N_EXPERTS))
    slots = (off + pos).reshape(n_tok // MOE_TM, 1, 2 * MOE_TM)
    xs_sorted = _dispatch(n2_all, slots)
    ys_sorted = _experts(xs_sorted, _expert_work_items(cnt, 2 * n_tok), moe_w1[lyr], moe_w3[lyr], moe_w2[lyr])
    y_prompt, y_sample = _combine(h_all, route, slots, ys_sorted, gfin, n_p_tiles)
    y_prompt = y_prompt.reshape(bp, tp, D_MODEL)
    y_sample = y_sample.reshape(ts, bs, D_MODEL).transpose(1, 0, 2)

    return (y_prompt, y_sample,
            sg_p[None], sr_p[None], pr_p3[:, -1][None],
            sg_s[None], sr_s[None], pr_s3[ts - 1][None])
```
